```python
import math
import jax, jax.numpy as jnp
from jax import lax
import numpy as np

D_MODEL = 1024
BATCH = 16
SEQ = 2048
DEPTH = 1

EPS = 1e-6
NEG_INF = -1e30

A_HEADS = 8
A_HEAD_DIM = 64
A_WIDTH = A_HEADS * A_HEAD_DIM
DILATED_PATTERNS = ((128, 1), (512, 4), (2048, 16))
A_BLOCK = 64

REL_BUCKETS = 32
REL_MAX_DISTANCE = 1024

B_HEADS = 8
B_Q_LORA = 256
B_KV_LORA = 128
B_QK_NOPE = 64
B_QK_ROPE = 32
B_V_DIM = 64
B_WIDTH = B_HEADS * B_V_DIM
ROPE_THETA = 10000.0
B_Q_BLOCK = 128

MIX_WIDTH = A_WIDTH + B_WIDTH
IN_PROJ_WIDTH = 3 * A_WIDTH + B_Q_LORA + B_KV_LORA + B_QK_ROPE

N_GROUPS = 4
EXPERTS_PER_GROUP = 8
N_EXPERTS = N_GROUPS * EXPERTS_PER_GROUP
TOP_K_IN_GROUP = 2
EXPERT_FF = 256
MOE_BLOCK = 128

kernel_name = "hymba_dilated_mla_hmoe_encoder"


def rmsnorm(x, g):
    xf = x.astype(jnp.float32)
    y = xf * lax.rsqrt(jnp.mean(xf * xf, axis=-1, keepdims=True) + EPS)
    return (y * g.astype(jnp.float32)).astype(x.dtype)


def t5_relative_bucket(rel):
    half = REL_BUCKETS // 2
    max_exact = half // 2
    n = np.abs(rel)
    large = max_exact + (np.log(np.maximum(n, 1) / max_exact)
                         / math.log(REL_MAX_DISTANCE / max_exact) * (half - max_exact)).astype(np.int32)
    large = np.minimum(large, half - 1)
    return (np.where(rel > 0, half, 0) + np.where(n < max_exact, n, large)).astype(np.int32)


def dilated_window_attention(q, k, v, rel_bias, dilation, half_steps):
    b, h, s, dh = q.shape
    L = s // dilation
    qb = math.gcd(L, A_BLOCK)
    nb = L // qb
    span = qb + 2 * half_steps

    def to_residue(t):
        return t.reshape(b, h, L, dilation, dh).transpose(0, 1, 3, 2, 4)

    qr, kr, vr = to_residue(q), to_residue(k), to_residue(v)
    pad = ((0, 0), (0, 0), (0, 0), (half_steps, half_steps), (0, 0))
    kp, vp = jnp.pad(kr, pad), jnp.pad(vr, pad)
    key_idx = np.arange(nb)[:, None] * qb + np.arange(span)[None, :]
    kb = jnp.take(kp, key_idx, axis=3)
    vb = jnp.take(vp, key_idx, axis=3)
    qblk = qr.reshape(b, h, dilation, nb, qb, dh)
    scores = jnp.einsum('bhrnqd,bhrnkd->bhrnqk', qblk, kb,
                        preferred_element_type=jnp.float32) * (dh ** -0.5)
    rel_steps = np.arange(span)[None, :] - half_steps - np.arange(qb)[:, None]
    buckets = t5_relative_bucket(rel_steps * dilation)
    bias = rel_bias[buckets].astype(jnp.float32).transpose(2, 0, 1)
    key_pos = key_idx - half_steps
    valid = ((np.abs(rel_steps) <= half_steps)[None]
             & (key_pos >= 0)[:, None, :] & (key_pos < L)[:, None, :])
    scores = jnp.where(valid[None, None, None], scores + bias[None, :, None, None], NEG_INF)
    m = jnp.max(scores, axis=-1, keepdims=True)
    p = jnp.exp(scores - m)
    l = jnp.sum(p, axis=-1, keepdims=True)
    o = jnp.einsum('bhrnqk,bhrnkd->bhrnqd', p, vb, preferred_element_type=jnp.float32) / l
    lse = (m + jnp.log(l))[..., 0]
    o = o.reshape(b, h, dilation, L, dh).transpose(0, 1, 3, 2, 4).reshape(b, h, s, dh)
    lse = lse.reshape(b, h, dilation, L).transpose(0, 1, 3, 2).reshape(b, h, s)
    return o, lse


def mixer_dilated(q, k, v, rel_bias):
    outs, lses = [], []
    for window, dilation in DILATED_PATTERNS:
        o, lse = dilated_window_attention(q, k, v, rel_bias, dilation, window // (2 * dilation))
        outs.append(o)
        lses.append(lse)
    w = jax.nn.softmax(jnp.stack(lses, 0), axis=0)
    return jnp.einsum('pbhs,pbhsd->bhsd', w, jnp.stack(outs, 0))


def apply_rope(x):
    s, d = x.shape[1], x.shape[-1]
    half = d // 2
    inv_freq = ROPE_THETA ** (-(jnp.arange(half, dtype=jnp.float32) / half))
    ang = jnp.arange(s, dtype=jnp.float32)[:, None] * inv_freq[None, :]
    cos, sin = jnp.cos(ang)[None, :, None, :], jnp.sin(ang)[None, :, None, :]
    x1, x2 = x[..., :half].astype(jnp.float32), x[..., half:].astype(jnp.float32)
    return jnp.concatenate([x1 * cos - x2 * sin, x1 * sin + x2 * cos], axis=-1).astype(x.dtype)


def mixer_mla(c_q, c_kv, k_rope, g_q_latent, w_q_up, g_kv_latent, w_kv_up):
    b, s, _ = c_q.shape
    q = (rmsnorm(c_q, g_q_latent) @ w_q_up).reshape(b, s, B_HEADS, B_QK_NOPE + B_QK_ROPE)
    q_nope, q_rope = q[..., :B_QK_NOPE], apply_rope(q[..., B_QK_NOPE:])
    kv = (rmsnorm(c_kv, g_kv_latent) @ w_kv_up).reshape(b, s, B_HEADS, B_QK_NOPE + B_V_DIM)
    k_nope, v = kv[..., :B_QK_NOPE], kv[..., B_QK_NOPE:].astype(jnp.float32)
    k_r = apply_rope(k_rope[:, :, None, :])[:, :, 0]
    scale = (B_QK_NOPE + B_QK_ROPE) ** -0.5
    nb = s // B_Q_BLOCK

    def block(args):
        qn, qr = args
        sc = (jnp.einsum('bqhd,bkhd->bhqk', qn, k_nope, preferred_element_type=jnp.float32)
              + jnp.einsum('bqhd,bkd->bhqk', qr, k_r, preferred_element_type=jnp.float32)) * scale
        p = jax.nn.softmax(sc, axis=-1)
        return jnp.einsum('bhqk,bkhd->bqhd', p, v)

    qn_b = q_nope.reshape(b, nb, B_Q_BLOCK, B_HEADS, B_QK_NOPE).transpose(1, 0, 2, 3, 4)
    qr_b = q_rope.reshape(b, nb, B_Q_BLOCK, B_HEADS, B_QK_ROPE).transpose(1, 0, 2, 3, 4)
    o = lax.map(block, (qn_b, qr_b))
    return o.transpose(1, 0, 2, 3, 4).reshape(b, s, B_WIDTH)


def hierarchical_moe(t, w_router_group, b_router_group, w_router_expert, b_router_expert,
                     w_gate, w_up, w_down):
    n_tok, d = t.shape
    group_logits = jnp.dot(t, w_router_group, preferred_element_type=jnp.float32) + b_router_group.astype(jnp.float32)
    group_probs = jax.nn.softmax(group_logits, axis=-1)
    group_idx = jnp.argmax(group_probs, axis=-1)
    group_gate = jnp.take_along_axis(group_probs, group_idx[:, None], axis=-1)
    expert_logits = (jnp.dot(t, w_router_expert, preferred_element_type=jnp.float32)
                     + b_router_expert.astype(jnp.float32)).reshape(n_tok, N_GROUPS, EXPERTS_PER_GROUP)
    in_group = jnp.take_along_axis(expert_logits, group_idx[:, None, None], axis=1)[:, 0]
    top_p, top_i = lax.top_k(jax.nn.softmax(in_group, axis=-1), TOP_K_IN_GROUP)
    gates = group_gate * top_p / jnp.sum(top_p, axis=-1, keepdims=True)
    expert_ids = group_idx[:, None] * EXPERTS_PER_GROUP + top_i

    n_assign = n_tok * TOP_K_IN_GROUP
    flat_expert = expert_ids.reshape(-1)
    flat_token = jnp.repeat(jnp.arange(n_tok), TOP_K_IN_GROUP)
    flat_gate = gates.reshape(-1)
    order = jnp.argsort(flat_expert)
    s_expert, s_token, s_gate = flat_expert[order], flat_token[order], flat_gate[order]
    counts = jnp.bincount(flat_expert, length=N_EXPERTS)
    starts = jnp.cumsum(counts) - counts
    padded = (counts + MOE_BLOCK - 1) // MOE_BLOCK * MOE_BLOCK
    padded_ends = jnp.cumsum(padded)
    padded_starts = padded_ends - padded
    dest = padded_starts[s_expert] + jnp.arange(n_assign) - starts[s_expert]
    n_blocks = -(-n_assign // MOE_BLOCK) + N_EXPERTS
    buf = jnp.zeros((n_blocks * MOE_BLOCK, d), t.dtype).at[dest].set(t[s_token])
    block_expert = jnp.minimum(
        jnp.searchsorted(padded_ends, jnp.arange(n_blocks) * MOE_BLOCK, side='right'), N_EXPERTS - 1)

    def expert_block(args):
        xb, e = args
        hb = jax.nn.silu(xb @ w_gate[e]) * (xb @ w_up[e])
        return hb @ w_down[e]

    out = lax.map(expert_block, (buf.reshape(n_blocks, MOE_BLOCK, d), block_expert))
    contrib = out.reshape(n_blocks * MOE_BLOCK, d)[dest].astype(jnp.float32) * s_gate[:, None]
    return jax.ops.segment_sum(contrib, s_token, num_segments=n_tok)


def setup_inputs(seed: int = 0) -> dict:
    key = jax.random.key(seed)
    ks = jax.random.split(key, 24)
    f32 = jnp.float32
    nrm = lambda k, shape, fan_in: jax.random.normal(k, shape, f32) * (fan_in ** -0.5)
    gain = lambda k, shape: 1.0 + 0.02 * jax.random.normal(k, shape, f32)
    return {
        "x": jax.random.normal(ks[0], (BATCH, SEQ, D_MODEL), f32),
        "g_attn_norm": gain(ks[1], (DEPTH, D_MODEL)),
        "w_in": nrm(ks[2], (DEPTH, D_MODEL, IN_PROJ_WIDTH), D_MODEL),
        "rel_bias": 0.5 * jax.random.normal(ks[3], (REL_BUCKETS, A_HEADS), f32),
        "g_q_latent": gain(ks[4], (DEPTH, B_Q_LORA)),
        "w_q_up": nrm(ks[5], (DEPTH, B_Q_LORA, B_HEADS * (B_QK_NOPE + B_QK_ROPE)), B_Q_LORA),
        "g_kv_latent": gain(ks[6], (DEPTH, B_KV_LORA)),
        "w_kv_up": nrm(ks[7], (DEPTH, B_KV_LORA, B_HEADS * (B_QK_NOPE + B_V_DIM)), B_KV_LORA),
        "g_out_a": gain(ks[8], (DEPTH, A_WIDTH)),
        "g_out_b": gain(ks[9], (DEPTH, B_WIDTH)),
        "w_out": nrm(ks[10], (DEPTH, MIX_WIDTH, D_MODEL), MIX_WIDTH),
        "g_ffn_norm": gain(ks[11], (DEPTH, D_MODEL)),
        "w_router_group": nrm(ks[12], (DEPTH, D_MODEL, N_GROUPS), D_MODEL),
        "b_router_group": 0.01 * jax.random.normal(ks[13], (DEPTH, N_GROUPS), f32),
        "w_router_expert": nrm(ks[14], (DEPTH, D_MODEL, N_EXPERTS), D_MODEL),
        "b_router_expert": 0.01 * jax.random.normal(ks[15], (DEPTH, N_EXPERTS), f32),
        "w_gate": nrm(ks[16], (DEPTH, N_EXPERTS, D_MODEL, EXPERT_FF), D_MODEL),
        "w_up": nrm(ks[17], (DEPTH, N_EXPERTS, D_MODEL, EXPERT_FF), D_MODEL),
        "w_down": nrm(ks[18], (DEPTH, N_EXPERTS, EXPERT_FF, D_MODEL), EXPERT_FF),
        "g_final": gain(ks[19], (D_MODEL,)),
    }


def reference(x, g_attn_norm, w_in, rel_bias, g_q_latent, w_q_up, g_kv_latent, w_kv_up,
              g_out_a, g_out_b, w_out, g_ffn_norm, w_router_group, b_router_group,
              w_router_expert, b_router_expert, w_gate, w_up, w_down, g_final):
    b, s, d = x.shape
    splits = list(np.cumsum([A_WIDTH, A_WIDTH, A_WIDTH, B_Q_LORA, B_KV_LORA]))
    for l in range(DEPTH):
        h = rmsnorm(x, g_attn_norm[l])
        proj = h @ w_in[l]
        q_a, k_a, v_a, c_q, c_kv, k_rope = jnp.split(proj, splits, axis=-1)
        heads = lambda t: t.reshape(b, s, A_HEADS, A_HEAD_DIM).transpose(0, 2, 1, 3)
        out_a = mixer_dilated(heads(q_a), heads(k_a), heads(v_a), rel_bias)
        out_a = out_a.transpose(0, 2, 1, 3).reshape(b, s, A_WIDTH)
        out_b = mixer_mla(c_q, c_kv, k_rope, g_q_latent[l], w_q_up[l], g_kv_latent[l], w_kv_up[l])
        mixed = jnp.concatenate([rmsnorm(out_a, g_out_a[l]), rmsnorm(out_b, g_out_b[l])], axis=-1)
        x = x + (mixed.astype(x.dtype) @ w_out[l])
        h2 = rmsnorm(x, g_ffn_norm[l]).reshape(b * s, d)
        y = hierarchical_moe(h2, w_router_group[l], b_router_group[l], w_router_expert[l],
                             b_router_expert[l], w_gate[l], w_up[l], w_down[l])
        x = x + y.reshape(b, s, d).astype(x.dtype)
    return rmsnorm(x, g_final)
```

```python
import functools
import math

import numpy as np
import jax
import jax.numpy as jnp
from jax import lax
from jax.experimental import pallas as pl
from jax.experimental.pallas import tpu as pltpu

F32 = jnp.float32
BF16 = jnp.bfloat16
I32 = jnp.int32
U32 = jnp.uint32

D_MODEL = 1024
EPS = 1e-6
NEG_INF = -1e30
LANES = 128

A_HEADS = 8
A_HEAD_DIM = 64
A_WIDTH = 512
DILATED_PATTERNS = ((128, 1), (512, 4), (2048, 16))
REL_BUCKETS = 32
REL_MAX_DISTANCE = 1024
A_QB = 128

B_HEADS = 8
B_Q_LORA = 256
B_KV_LORA = 128
B_QK_NOPE = 64
B_QK_ROPE = 32
B_V_DIM = 64
B_WIDTH = 512
ROPE_THETA = 10000.0
B_SCALE = (B_QK_NOPE + B_QK_ROPE) ** -0.5
B_QB = 256

N_GROUPS = 4
EXPERTS_PER_GROUP = 8
N_EXPERTS = 32
TOP_K = 2
EXPERT_FF = 256
MOE_BLK = 256

ROW_TILE = 512
PROJ_COLS = 2048

_NT = (((1,), (1,)), ((), ()))


def _cparams(semantics, vmem_mb=48, **kw):
    return pltpu.CompilerParams(dimension_semantics=semantics,
                                vmem_limit_bytes=vmem_mb * 1024 * 1024, **kw)


def _rms(x, g):
    return x * lax.rsqrt(jnp.mean(x * x, axis=-1, keepdims=True) + EPS) * g


def _lane_iota(rows=1):
    return lax.broadcasted_iota(I32, (rows, LANES), 1)


def _proj_kernel(x_ref, g_ref, win_ref, gq_ref, wq_ref, gkv_ref, wkb_ref, wvb_ref, cos_ref, sin_ref,
                 qa_ref, ka_ref, va_ref, qb_ref, kb_ref, vb_ref):
    h = _rms(x_ref[...], g_ref[...]).astype(BF16)
    proj = jnp.dot(h, win_ref[...], preferred_element_type=F32)
    lo = _lane_iota() < A_HEAD_DIM
    qa_ref[...] = (proj[:, 0:512] * (A_HEAD_DIM ** -0.5)).astype(BF16)
    for p in range(A_HEADS // 2):
        kp = proj[:, 512 + LANES * p:512 + LANES * (p + 1)]
        ka_ref[:, 2 * LANES * p:2 * LANES * p + LANES] = jnp.where(lo, kp, 0.0).astype(BF16)
        ka_ref[:, 2 * LANES * p + LANES:2 * LANES * (p + 1)] = jnp.where(lo, 0.0, kp).astype(BF16)
    va_ref[...] = proj[:, 1024:1536].astype(BF16)

    cos = cos_ref[...]
    sin = sin_ref[...]
    cq = _rms(proj[:, 1536:1792], gq_ref[...]).astype(BF16)
    q = jnp.dot(cq, wq_ref[...], preferred_element_type=F32)
    q_mul = (cos + jnp.where(lo, 1.0, 0.0)) * B_SCALE
    q_rot = sin * B_SCALE
    for hd in range(B_HEADS):
        t = q[:, LANES * hd:LANES * (hd + 1)]
        qb_ref[:, LANES * hd:LANES * (hd + 1)] = (t * q_mul + pltpu.roll(t, 96, 1) * q_rot).astype(BF16)

    ckv = _rms(proj[:, 1792:1920], gkv_ref[...]).astype(BF16)
    kr = proj[:, 1920:2048]
    kr = kr * cos + pltpu.roll(kr, 96, 1) * sin
    kn = jnp.dot(ckv, wkb_ref[...], preferred_element_type=F32)
    for hd in range(B_HEADS):
        kb_ref[:, LANES * hd:LANES * (hd + 1)] = (kn[:, LANES * hd:LANES * (hd + 1)] + kr).astype(BF16)
    vb_ref[...] = jnp.dot(ckv, wvb_ref[...], preferred_element_type=F32).astype(BF16)


def _proj_call(x2, g_attn, w_in, g_q, w_q, g_kv, w_kb, w_vb, cos_t, sin_t, seq):
    t = x2.shape[0]
    tm = ROW_TILE
    nseq = seq // tm
    row = lambda i: (i, 0)
    const = lambda i: (0, 0)
    pos = lambda i: (i % nseq, 0)
    out = lambda w: jax.ShapeDtypeStruct((t, w), BF16)
    return pl.pallas_call(
        _proj_kernel,
        grid=(t // tm,),
        in_specs=[
            pl.BlockSpec((tm, D_MODEL), row),
            pl.BlockSpec((1, D_MODEL), const),
            pl.BlockSpec((D_MODEL, PROJ_COLS), const),
            pl.BlockSpec((1, B_Q_LORA), const),
            pl.BlockSpec((B_Q_LORA, B_HEADS * LANES), const),
            pl.BlockSpec((1, B_KV_LORA), const),
            pl.BlockSpec((B_KV_LORA, B_HEADS * LANES), const),
            pl.BlockSpec((B_KV_LORA, B_WIDTH), const),
            pl.BlockSpec((tm, LANES), pos),
            pl.BlockSpec((tm, LANES), pos),
        ],
        out_specs=[
            pl.BlockSpec((tm, A_WIDTH), row),
            pl.BlockSpec((tm, 2 * A_WIDTH), row),
            pl.BlockSpec((tm, A_WIDTH), row),
            pl.BlockSpec((tm, B_HEADS * LANES), row),
            pl.BlockSpec((tm, B_HEADS * LANES), row),
            pl.BlockSpec((tm, B_WIDTH), row),
        ],
        out_shape=[out(A_WIDTH), out(2 * A_WIDTH), out(A_WIDTH),
                   out(B_HEADS * LANES), out(B_HEADS * LANES), out(B_WIDTH)],
        compiler_params=_cparams(("parallel",)),
        name="proj",
    )(x2, g_attn, w_in, g_q, w_q, g_kv, w_kb, w_vb, cos_t, sin_t)


def _dilated_kernel(*refs, seq_len, key_width, first, last):
    q_ref, k_ref, v_ref, bias_ref = refs[:4]
    refs = refs[4:]
    if not first:
        acc_in, st_in = refs[:2]
        refs = refs[2:]
    if last:
        (o_ref,) = refs
    else:
        acc_out, st_out = refs
    nblk = seq_len // A_QB
    lane = _lane_iota()
    lo = lane < A_HEAD_DIM

    def block(n, carry):
        if nblk == 1:
            q0, ks, var = 0, 0, 0
        else:
            q0 = pl.multiple_of(n * A_QB, A_QB)
            ks = pl.multiple_of(jnp.clip(q0 - 64, 0, seq_len - key_width), 64)
            var = jnp.where(n == 0, 0, jnp.where(n == nblk - 1, 2, 1))
        rows = pl.ds(q0, A_QB)
        keys = pl.ds(ks, key_width)
        if not first:
            st = st_in[0, rows, :]
        st_new = jnp.zeros((A_QB, LANES), F32)
        for p in range(A_HEADS // 2):
            tile = slice(LANES * p, LANES * (p + 1))
            qp = q_ref[0, rows, tile]
            vp = v_ref[0, keys, tile]
            pvs, alphas, dens = [], [], []
            for half in range(2):
                hd = 2 * p + half
                kh = k_ref[0, keys, LANES * hd:LANES * (hd + 1)]
                s = lax.dot_general(qp, kh, _NT, preferred_element_type=F32) + bias_ref[var, hd]
                m_new = jnp.max(s, axis=-1, keepdims=True)
                if not first:
                    m_old = st[:, hd:hd + 1]
                    m_new = jnp.maximum(m_old, m_new)
                pr = jnp.exp(s - m_new)
                den = jnp.sum(pr, axis=-1, keepdims=True)
                if not first:
                    alpha = jnp.exp(m_old - m_new)
                    den = den + alpha * st[:, 8 + hd:9 + hd]
                    alphas.append(alpha)
                pvs.append(jnp.dot(pr.astype(BF16), vp, preferred_element_type=F32))
                dens.append(den)
                st_new = jnp.where(lane == hd, m_new, st_new)
                st_new = jnp.where(lane == 8 + hd, den, st_new)
            acc = jnp.where(lo, pvs[0], pvs[1])
            if not first:
                acc = acc + acc_in[0, rows, tile] * jnp.where(lo, alphas[0], alphas[1])
            if last:
                o_ref[0, rows, tile] = (acc * jnp.where(lo, 1.0 / dens[0], 1.0 / dens[1])).astype(BF16)
            else:
                acc_out[0, rows, tile] = acc
        if not last:
            st_out[0, rows, :] = st_new
        return carry

    if nblk == 1:
        block(0, 0)
    else:
        lax.fori_loop(0, nblk, block, 0)


def _dilated_call(qa, ka, va, bias, state, batch, seq, dilation, first, last):
    r = dilation
    sl = seq // r
    kw = min(2 * A_QB, sl)
    view = lambda a, w: a.reshape(batch, sl, r * w)
    blk = lambda w: pl.BlockSpec((1, sl, w), lambda b, c: (b, 0, c))
    args = [view(qa, A_WIDTH), view(ka, 2 * A_WIDTH), view(va, A_WIDTH), bias]
    in_specs = [blk(A_WIDTH), blk(2 * A_WIDTH), blk(A_WIDTH),
                pl.BlockSpec(bias.shape, lambda b, c: (0, 0, 0, 0))]
    if not first:
        acc, st = state
        args += [view(acc, A_WIDTH), view(st, LANES)]
        in_specs += [blk(A_WIDTH), blk(LANES)]
    if last:
        out_specs = [blk(A_WIDTH)]
        out_shape = [jax.ShapeDtypeStruct((batch, sl, r * A_WIDTH), BF16)]
    else:
        out_specs = [blk(A_WIDTH), blk(LANES)]
        out_shape = [jax.ShapeDtypeStruct((batch, sl, r * A_WIDTH), F32),
                     jax.ShapeDtypeStruct((batch, sl, r * LANES), F32)]
    outs = pl.pallas_call(
        functools.partial(_dilated_kernel, seq_len=sl, key_width=kw, first=first, last=last),
        grid=(batch, r),
        in_specs=in_specs,
        out_specs=out_specs,
        out_shape=out_shape,
        compiler_params=_cparams(("parallel", "parallel")),
        name=f"dilated_r{r}",
    )(*args)
    return [o.reshape(batch * seq, -1) for o in outs]


def _t5_bucket(rel):
    half = REL_BUCKETS // 2
    max_exact = half // 2
    n = np.abs(rel)
    large = max_exact + (np.log(np.maximum(n, 1) / max_exact)
                         / math.log(REL_MAX_DISTANCE / max_exact) * (half - max_exact)).astype(np.int32)
    large = np.minimum(large, half - 1)
    return (np.where(rel > 0, half, 0) + np.where(n < max_exact, n, large)).astype(np.int32)


def _dilated_bias(rel_bias, seq, dilation, half_steps):
    sl = seq // dilation
    kw = min(2 * A_QB, sl)
    offsets = [0] if sl == kw else [0, -half_steps, A_QB - kw]
    tiles = []
    for off in offsets:
        rel = np.arange(kw)[None, :] + off - np.arange(A_QB)[:, None]
        valid = np.abs(rel) <= half_steps
        b = rel_bias[_t5_bucket(rel * dilation)].astype(F32)
        tiles.append(jnp.where(valid[:, :, None], b, NEG_INF).transpose(2, 0, 1))
    return jnp.stack(tiles, 0)


def _mla_kernel(q_ref, k_ref, v_ref, o_ref):
    lo = _lane_iota() < B_V_DIM
    v = v_ref[0]
    outs = []
    for half in range(2):
        tile = slice(LANES * half, LANES * (half + 1))
        s = lax.dot_general(q_ref[0, :, tile], k_ref[0, :, tile], _NT, preferred_element_type=F32)
        m = jnp.max(s, axis=-1, keepdims=True)
        pr = jnp.exp(s - m)
        den = jnp.sum(pr, axis=-1, keepdims=True)
        outs.append(jnp.dot(pr.astype(BF16), v, preferred_element_type=F32) * (1.0 / den))
    o_ref[0] = jnp.where(lo, outs[0], outs[1]).astype(BF16)


def _mla_call(qb, kb, vb, batch, seq):
    qb = qb.reshape(batch, seq, B_HEADS * LANES)
    kb = kb.reshape(batch, seq, B_HEADS * LANES)
    vb = vb.reshape(batch, seq, B_WIDTH)
    out = pl.pallas_call(
        _mla_kernel,
        grid=(batch, B_HEADS // 2, seq // B_QB),
        in_specs=[
            pl.BlockSpec((1, B_QB, 2 * LANES), lambda b, p, i: (b, i, p)),
            pl.BlockSpec((1, seq, 2 * LANES), lambda b, p, i: (b, 0, p)),
            pl.BlockSpec((1, seq, LANES), lambda b, p, i: (b, 0, p)),
        ],
        out_specs=pl.BlockSpec((1, B_QB, LANES), lambda b, p, i: (b, i, p)),
        out_shape=jax.ShapeDtypeStruct((batch, seq, B_WIDTH), BF16),
        compiler_params=_cparams(("parallel", "parallel", "parallel")),
        name="mla",
    )(qb, kb, vb)
    return out.reshape(batch * seq, B_WIDTH)


def _mix_kernel(oa_ref, ob_ref, x_ref, ga_ref, gb_ref, wo_ref, gf_ref, wr_ref, br_ref, tri_ref,
                x1_ref, hp_ref, idx_ref, gate_ref, cnt_ref, carry_ref):
    i = pl.program_id(0)

    @pl.when(i == 0)
    def _():
        carry_ref[...] = jnp.zeros_like(carry_ref)

    a = _rms(oa_ref[...].astype(F32), ga_ref[...]).astype(BF16)
    b = _rms(ob_ref[...].astype(F32), gb_ref[...]).astype(BF16)
    mix = (jnp.dot(a, wo_ref[0:A_WIDTH, :], preferred_element_type=F32)
           + jnp.dot(b, wo_ref[A_WIDTH:, :], preferred_element_type=F32))
    x1 = x_ref[...] + mix
    x1_ref[...] = x1
    h2 = _rms(x1, gf_ref[...])
    hb = h2.astype(BF16).astype(F32)
    half = D_MODEL // 2
    hp_ref[...] = ((lax.bitcast_convert_type(hb[:, :half], U32) >> 16)
                   | (lax.bitcast_convert_type(hb[:, half:], U32) & jnp.uint32(0xFFFF0000)))

    lg = jnp.dot(h2, wr_ref[...], preferred_element_type=F32, precision=lax.Precision.HIGHEST) + br_ref[...]
    tm = lg.shape[0]
    lane = _lane_iota(tm)
    is_g = (lane >= N_EXPERTS) & (lane < N_EXPERTS + N_GROUPS)
    gl = jnp.where(is_g, lg, NEG_INF)
    ge = jnp.exp(gl - jnp.max(gl, axis=-1, keepdims=True))
    gp = ge / jnp.sum(ge, axis=-1, keepdims=True)
    g_gate = jnp.max(gp, axis=-1, keepdims=True)
    g_idx = jnp.min(jnp.where(is_g & (gp == g_gate), lane - N_EXPERTS, LANES), axis=-1, keepdims=True)
    sel = (lane >> 3) == g_idx
    el = jnp.where(sel, lg, NEG_INF)
    ee = jnp.exp(el - jnp.max(el, axis=-1, keepdims=True))
    ep = jnp.where(sel, ee / jnp.sum(ee, axis=-1, keepdims=True), -1.0)
    p1 = jnp.max(ep, axis=-1, keepdims=True)
    i1 = jnp.min(jnp.where(ep == p1, lane, LANES), axis=-1, keepdims=True)
    ep2 = jnp.where(lane == i1, -1.0, ep)
    p2 = jnp.max(ep2, axis=-1, keepdims=True)
    i2 = jnp.min(jnp.where(sel & (ep2 == p2) & (lane != i1), lane, LANES), axis=-1, keepdims=True)
    den = p1 + p2
    g1 = g_gate * p1 / den
    g2 = g_gate * p2 / den

    hit1 = lane == i1
    hit2 = lane == i2
    onehot = jnp.where(hit1 | hit2, 1.0, 0.0)
    before = jnp.dot(tri_ref[...], onehot.astype(BF16), preferred_element_type=F32) + carry_ref[...]
    r1 = jnp.sum(jnp.where(hit1, before, 0.0), axis=-1, keepdims=True).astype(I32)
    r2 = jnp.sum(jnp.where(hit2, before, 0.0), axis=-1, keepdims=True).astype(I32)
    carry_ref[...] += jnp.sum(onehot, axis=0, keepdims=True)

    idx_ref[...] = jnp.where(lane == 0, i1, jnp.where(lane == 1, i2,
                             jnp.where(lane == 2, r1, jnp.where(lane == 3, r2, 0))))
    gate_ref[...] = jnp.where(lane == 0, g1, jnp.where(lane == 1, g2, 0.0))

    @pl.when(i == pl.num_programs(0) - 1)
    def _():
        cnt_ref[...] = jnp.broadcast_to(carry_ref[...], cnt_ref.shape).astype(I32)


def _mix_call(oa, ob, x2, g_a, g_b, w_out, g_ffn, w_router, b_router, tri):
    t = x2.shape[0]
    tm = ROW_TILE
    row = lambda i: (i, 0)
    const = lambda i: (0, 0)
    return pl.pallas_call(
        _mix_kernel,
        grid=(t // tm,),
        in_specs=[
            pl.BlockSpec((tm, A_WIDTH), row),
            pl.BlockSpec((tm, B_WIDTH), row),
            pl.BlockSpec((tm, D_MODEL), row),
            pl.BlockSpec((1, A_WIDTH), const),
            pl.BlockSpec((1, B_WIDTH), const),
            pl.BlockSpec((D_MODEL, D_MODEL), const),
            pl.BlockSpec((1, D_MODEL), const),
            pl.BlockSpec((D_MODEL, LANES), const),
            pl.BlockSpec((1, LANES), const),
            pl.BlockSpec((tm, tm), const),
        ],
        out_specs=[
            pl.BlockSpec((tm, D_MODEL), row),
            pl.BlockSpec((tm, D_MODEL // 2), row),
            pl.BlockSpec((tm, LANES), row),
            pl.BlockSpec((tm, LANES), row),
            pl.BlockSpec((8, LANES), const),
        ],
        out_shape=[
            jax.ShapeDtypeStruct((t, D_MODEL), F32),
            jax.ShapeDtypeStruct((t, D_MODEL // 2), U32),
            jax.ShapeDtypeStruct((t, LANES), I32),
            jax.ShapeDtypeStruct((t, LANES), F32),
            jax.ShapeDtypeStruct((8, LANES), I32),
        ],
        scratch_shapes=[pltpu.VMEM((1, LANES), F32)],
        compiler_params=_cparams(("arbitrary",)),
        name="mix_router",
    )(oa, ob, x2, g_a, g_b, w_out, g_ffn, w_router, b_router, tri)


def _dest_kernel(idx_ref, pstart_ref, dest_ref):
    idx = idx_ref[...]
    lane = _lane_iota(idx.shape[0])
    ps = pstart_ref[...]

    def slot(k):
        e = jnp.sum(jnp.where(lane == k, idx, 0), axis=-1, keepdims=True)
        rank = jnp.sum(jnp.where(lane == 2 + k, idx, 0), axis=-1, keepdims=True)
        return jnp.sum(jnp.where(lane == e, ps, 0), axis=-1, keepdims=True) + rank

    dest_ref[...] = jnp.where(lane == 0, slot(0), jnp.where(lane == 1, slot(1), 0))


def _dest_call(idx, pstart):
    t = idx.shape[0]
    tm = ROW_TILE
    return pl.pallas_call(
        _dest_kernel,
        grid=(t // tm,),
        in_specs=[pl.BlockSpec((tm, LANES), lambda i: (i, 0)), pl.BlockSpec((1, LANES), lambda i: (0, 0))],
        out_specs=pl.BlockSpec((tm, LANES), lambda i: (i, 0)),
        out_shape=jax.ShapeDtypeStruct((t, LANES), I32),
        compiler_params=_cparams(("parallel",)),
        name="dest_rows",
    )(idx, pstart)


def _dispatch_kernel(dest_ref, h_ref, buf_in, buf_out, sem):
    del buf_in
    tt = h_ref.shape[0]

    def row_copy(i, d):
        return pltpu.make_async_copy(h_ref.at[pl.ds(i, 1)], buf_out.at[pl.ds(d, 1)], sem)

    def issue(i, c):
        for k in range(TOP_K):
            row_copy(i, dest_ref[TOP_K * i + k]).start()
        return c

    def drain(i, c):
        for k in range(TOP_K):
            row_copy(i, dest_ref[TOP_K * i + k]).wait()
        return c

    lax.fori_loop(0, tt, issue, 0)
    lax.fori_loop(0, tt, drain, 0)


def _dispatch_call(dest_flat, hp, n_rows):
    t = hp.shape[0]
    tt = ROW_TILE
    buf0 = jnp.zeros((n_rows, D_MODEL // 2), U32)
    return pl.pallas_call(
        _dispatch_kernel,
        grid=(t // tt,),
        in_specs=[
            pl.BlockSpec((TOP_K * tt,), lambda i: (i,), memory_space=pltpu.SMEM),
            pl.BlockSpec((tt, D_MODEL // 2), lambda i: (i, 0)),
            pl.BlockSpec(memory_space=pl.ANY),
        ],
        out_specs=pl.BlockSpec(memory_space=pl.ANY),
        out_shape=jax.ShapeDtypeStruct((n_rows, D_MODEL // 2), U32),
        scratch_shapes=[pltpu.SemaphoreType.DMA(())],
        input_output_aliases={2: 0},
        compiler_params=_cparams(("arbitrary",), disable_bounds_checks=True, has_side_effects=True),
        name="dispatch",
    )(dest_flat, hp, buf0)


def _expert_kernel(be_ref, new_ref, valid_ref, buf_ref, wg_ref, wu_ref, wd_ref, out_ref, wg_s, wu_s, wd_s):
    j = pl.program_id(0)
    del be_ref

    @pl.when(new_ref[j] == 1)
    def _():
        wg_s[...] = wg_ref[0].astype(BF16)
        wu_s[...] = wu_ref[0].astype(BF16)
        wd_s[...] = wd_ref[0].astype(BF16)

    @pl.when(valid_ref[j] == 1)
    def _():
        w = buf_ref[...]
        half = D_MODEL // 2
        x_lo = lax.bitcast_convert_type(w << 16, F32).astype(BF16)
        x_hi = lax.bitcast_convert_type(w & jnp.uint32(0xFFFF0000), F32).astype(BF16)
        g = (jnp.dot(x_lo, wg_s[0:half, :], preferred_element_type=F32)
             + jnp.dot(x_hi, wg_s[half:, :], preferred_element_type=F32))
        u = (jnp.dot(x_lo, wu_s[0:half, :], preferred_element_type=F32)
             + jnp.dot(x_hi, wu_s[half:, :], preferred_element_type=F32))
        hb = (g * jax.nn.sigmoid(g)) * u
        out_ref[...] = jnp.dot(hb.astype(BF16), wd_s[...], preferred_element_type=F32)

    @pl.when(valid_ref[j] == 0)
    def _():
        out_ref[...] = jnp.zeros_like(out_ref)


def _expert_call(block_expert, block_new, block_valid, buf, w_gate, w_up, w_down):
    n_rows = buf.shape[0]
    nb = n_rows // MOE_BLK
    wsel = lambda j, be, nw, va: (be[j], 0, 0)
    rows = lambda j, be, nw, va: (j, 0)
    return pl.pallas_call(
        _expert_kernel,
        grid_spec=pltpu.PrefetchScalarGridSpec(
            num_scalar_prefetch=3,
            grid=(nb,),
            in_specs=[
                pl.BlockSpec((MOE_BLK, D_MODEL // 2), rows),
                pl.BlockSpec((1, D_MODEL, EXPERT_FF), wsel),
                pl.BlockSpec((1, D_MODEL, EXPERT_FF), wsel),
                pl.BlockSpec((1, EXPERT_FF, D_MODEL), wsel),
            ],
            out_specs=pl.BlockSpec((MOE_BLK, D_MODEL), rows),
            scratch_shapes=[pltpu.VMEM((D_MODEL, EXPERT_FF), BF16),
                            pltpu.VMEM((D_MODEL, EXPERT_FF), BF16),
                            pltpu.VMEM((EXPERT_FF, D_MODEL), BF16)],
        ),
        out_shape=jax.ShapeDtypeStruct((n_rows, D_MODEL), F32),
        compiler_params=_cparams(("arbitrary",)),
        name="experts",
    )(block_expert, block_new, block_valid, buf, w_gate, w_up, w_down)


def _combine_kernel(dest_ref, x1_ref, gate_ref, gf_ref, eo_ref, o_ref, rows_ref, sem):
    tt = x1_ref.shape[0]

    def row_copy(i, k, d):
        return pltpu.make_async_copy(eo_ref.at[pl.ds(d, 1)], rows_ref.at[k, pl.ds(i, 1)], sem)

    def issue(i, c):
        for k in range(TOP_K):
            row_copy(i, k, dest_ref[TOP_K * i + k]).start()
        return c

    def drain(i, c):
        for k in range(TOP_K):
            row_copy(i, k, dest_ref[TOP_K * i + k]).wait()
        return c

    lax.fori_loop(0, tt, issue, 0)
    lax.fori_loop(0, tt, drain, 0)
    gate = gate_ref[...]
    y = rows_ref[0] * gate[:, 0:1] + rows_ref[1] * gate[:, 1:2]
    o_ref[...] = _rms(x1_ref[...] + y, gf_ref[...])


def _combine_call(dest_flat, x1, gates, g_final, expert_out):
    t = x1.shape[0]
    tt = ROW_TILE // 2
    return pl.pallas_call(
        _combine_kernel,
        grid=(t // tt,),
        in_specs=[
            pl.BlockSpec((TOP_K * tt,), lambda i: (i,), memory_space=pltpu.SMEM),
            pl.BlockSpec((tt, D_MODEL), lambda i: (i, 0)),
            pl.BlockSpec((tt, LANES), lambda i: (i, 0)),
            pl.BlockSpec((1, D_MODEL), lambda i: (0, 0)),
            pl.BlockSpec(memory_space=pl.ANY),
        ],
        out_specs=pl.BlockSpec((tt, D_MODEL), lambda i: (i, 0)),
        out_shape=jax.ShapeDtypeStruct((t, D_MODEL), F32),
        scratch_shapes=[pltpu.VMEM((TOP_K, tt, D_MODEL), F32), pltpu.SemaphoreType.DMA(())],
        compiler_params=_cparams(("arbitrary",), disable_bounds_checks=True),
        name="combine",
    )(dest_flat, x1, gates, g_final, expert_out)


def _rope_tables(seq):
    half = B_QK_ROPE // 2
    inv_freq = ROPE_THETA ** (-(jnp.arange(half, dtype=F32) / half))
    ang = jnp.arange(seq, dtype=F32)[:, None] * inv_freq[None, :]
    cos, sin = jnp.cos(ang), jnp.sin(ang)
    z = jnp.zeros((seq, B_QK_NOPE), F32)
    z2 = jnp.zeros((seq, B_QK_ROPE), F32)
    return (jnp.concatenate([z, cos, cos, z2], axis=1), jnp.concatenate([z, -sin, sin, z2], axis=1))


def _swap_halves(w):
    half = w.shape[-1] // 2
    return jnp.concatenate([w[..., half:], w[..., :half]], axis=-1)


def _layout_weights(w_in, w_q_up, w_kv_up):
    d = w_in.shape[0]
    w_kr = w_in[:, 3 * A_WIDTH + B_Q_LORA + B_KV_LORA:]
    w_in_l = jnp.concatenate(
        [w_in[:, :3 * A_WIDTH + B_Q_LORA + B_KV_LORA], jnp.zeros((d, B_QK_NOPE), F32), w_kr, _swap_halves(w_kr)],
        axis=1).astype(BF16)
    wq = w_q_up.reshape(B_Q_LORA, B_HEADS, B_QK_NOPE + B_QK_ROPE)
    wq_l = jnp.concatenate([wq, _swap_halves(wq[..., B_QK_NOPE:])], axis=-1)
    wq_l = wq_l.reshape(B_Q_LORA, B_HEADS * LANES).astype(BF16)
    wkv = w_kv_up.reshape(B_KV_LORA, B_HEADS, B_QK_NOPE + B_V_DIM)
    wkb = jnp.concatenate([wkv[..., :B_QK_NOPE], jnp.zeros_like(wkv[..., :B_QK_NOPE])], axis=-1)
    wkb = wkb.reshape(B_KV_LORA, B_HEADS * LANES).astype(BF16)
    wvb = wkv[..., B_QK_NOPE:].reshape(B_KV_LORA, B_WIDTH).astype(BF16)
    return w_in_l, wq_l, wkb, wvb


def _block_plan(counts, n_blocks):
    padded = (counts + MOE_BLK - 1) // MOE_BLK * MOE_BLK
    ends = jnp.cumsum(padded)
    starts = ends - padded
    first_row = jnp.arange(n_blocks, dtype=I32) * MOE_BLK
    expert = jnp.minimum(jnp.searchsorted(ends, first_row, side="right"), N_EXPERTS - 1).astype(I32)
    new = jnp.concatenate([jnp.ones((1,), I32), (expert[1:] != expert[:-1]).astype(I32)])
    valid = (first_row < ends[-1]).astype(I32)
    return starts.astype(I32), expert, new, valid


def kernel(x, g_attn_norm, w_in, rel_bias, g_q_latent, w_q_up, g_kv_latent, w_kv_up, g_out_a, g_out_b, w_out,
           g_ffn_norm, w_router_group, b_router_group, w_router_expert, b_router_expert, w_gate, w_up, w_down,
           g_final):
    batch, seq, d = x.shape
    t = batch * seq
    assert g_attn_norm.shape[0] == 1 and d == D_MODEL and seq % ROW_TILE == 0
    cos_t, sin_t = _rope_tables(seq)
    tri = jnp.tril(jnp.ones((ROW_TILE, ROW_TILE), F32), -1).astype(BF16)
    n_blocks = t * TOP_K // MOE_BLK + N_EXPERTS
    x2 = x.reshape(t, d)
    row = lambda v: v.reshape(1, -1)

    w_in_l, wq_l, wkb_l, wvb_l = _layout_weights(w_in[0], w_q_up[0], w_kv_up[0])
    qa, ka, va, qb, kb, vb = _proj_call(x2, row(g_attn_norm[0]), w_in_l, row(g_q_latent[0]), wq_l,
                                        row(g_kv_latent[0]), wkb_l, wvb_l, cos_t, sin_t, seq)
    state = None
    for pi, (window, dilation) in enumerate(DILATED_PATTERNS):
        bias = _dilated_bias(rel_bias, seq, dilation, window // (2 * dilation))
        state = _dilated_call(qa, ka, va, bias, state, batch, seq, dilation,
                              first=pi == 0, last=pi == len(DILATED_PATTERNS) - 1)
    (oa,) = state
    ob = _mla_call(qb, kb, vb, batch, seq)

    pad = LANES - N_EXPERTS - N_GROUPS
    w_router = jnp.concatenate([w_router_expert[0], w_router_group[0], jnp.zeros((d, pad), F32)], axis=1)
    b_router = jnp.concatenate([b_router_expert[0], b_router_group[0], jnp.zeros((pad,), F32)])
    x1, hp, idx, gates, cnt = _mix_call(oa, ob, x2, row(g_out_a[0]), row(g_out_b[0]), w_out[0].astype(BF16),
                                        row(g_ffn_norm[0]), w_router, row(b_router), tri)
    pstart, block_expert, block_new, block_valid = _block_plan(cnt[0, :N_EXPERTS], n_blocks)
    pstart_row = jnp.concatenate([pstart, jnp.zeros((LANES - N_EXPERTS,), I32)]).reshape(1, LANES)
    dest = _dest_call(idx, pstart_row)
    dest_flat = dest[:, :TOP_K].reshape(-1)
    buf = _dispatch_call(dest_flat, hp, n_blocks * MOE_BLK)
    expert_out = _expert_call(block_expert, block_new, block_valid, buf, w_gate[0], w_up[0], w_down[0])
    return _combine_call(dest_flat, x1, gates, row(g_final), expert_out).reshape(batch, seq, d)
```

```python
import functools
import math

import numpy as np
import jax
import jax.numpy as jnp
from jax import lax
from jax.experimental import pallas as pl
from jax.experimental.pallas import tpu as pltpu

F32 = jnp.float32
BF16 = jnp.bfloat16
I32 = jnp.int32
U32 = jnp.uint32

D_MODEL = 1024
EPS = 1e-6
NEG_INF = -1e30
LANES = 128

A_HEADS = 8
A_HEAD_DIM = 64
A_WIDTH = 512
DILATED_PATTERNS = ((128, 1), (512, 4), (2048, 16))
REL_BUCKETS = 32
REL_MAX_DISTANCE = 1024
A_QB = 128

B_HEADS = 8
B_Q_LORA = 256
B_KV_LORA = 128
B_QK_NOPE = 64
B_QK_ROPE = 32
B_V_DIM = 64
B_WIDTH = 512
ROPE_THETA = 10000.0
B_SCALE = (B_QK_NOPE + B_QK_ROPE) ** -0.5
B_QB = 256

N_GROUPS = 4
EXPERTS_PER_GROUP = 8
N_EXPERTS = 32
TOP_K = 2
EXPERT_FF = 256
MOE_BLK = 256

ROW_TILE = 512
PROJ_COLS = 2048

_NT = (((1,), (1,)), ((), ()))


def _cparams(semantics, vmem_mb=48, **kw):
    return pltpu.CompilerParams(dimension_semantics=semantics,
                                vmem_limit_bytes=vmem_mb * 1024 * 1024, **kw)


def _rms(x, g):
    return x * lax.rsqrt(jnp.mean(x * x, axis=-1, keepdims=True) + EPS) * g


def _lane_iota(rows=1):
    return lax.broadcasted_iota(I32, (rows, LANES), 1)


def _proj_kernel(x_ref, g_ref, win_ref, gq_ref, wq_ref, gkv_ref, wkb_ref, wvb_ref, cos_ref, sin_ref,
                 qa_ref, ka_ref, va_ref, qb_ref, kb_ref, vb_ref):
    h = _rms(x_ref[...], g_ref[...]).astype(BF16)
    proj = jnp.dot(h, win_ref[...], preferred_element_type=F32)
    lo = _lane_iota() < A_HEAD_DIM
    qa_ref[...] = (proj[:, 0:512] * (A_HEAD_DIM ** -0.5)).astype(BF16)
    for p in range(A_HEADS // 2):
        kp = proj[:, 512 + LANES * p:512 + LANES * (p + 1)]
        ka_ref[:, 2 * LANES * p:2 * LANES * p + LANES] = jnp.where(lo, kp, 0.0).astype(BF16)
        ka_ref[:, 2 * LANES * p + LANES:2 * LANES * (p + 1)] = jnp.where(lo, 0.0, kp).astype(BF16)
    va_ref[...] = proj[:, 1024:1536].astype(BF16)

    cos = cos_ref[...]
    sin = sin_ref[...]
    cq = _rms(proj[:, 1536:1792], gq_ref[...]).astype(BF16)
    q = jnp.dot(cq, wq_ref[...], preferred_element_type=F32)
    q_mul = (cos + jnp.where(lo, 1.0, 0.0)) * B_SCALE
    q_rot = sin * B_SCALE
    for hd in range(B_HEADS):
        t = q[:, LANES * hd:LANES * (hd + 1)]
        qb_ref[:, LANES * hd:LANES * (hd + 1)] = (t * q_mul + pltpu.roll(t, 96, 1) * q_rot).astype(BF16)

    ckv = _rms(proj[:, 1792:1920], gkv_ref[...]).astype(BF16)
    kr = proj[:, 1920:2048]
    kr = kr * cos + pltpu.roll(kr, 96, 1) * sin
    kn = jnp.dot(ckv, wkb_ref[...], preferred_element_type=F32)
    for hd in range(B_HEADS):
        kb_ref[:, LANES * hd:LANES * (hd + 1)] = (kn[:, LANES * hd:LANES * (hd + 1)] + kr).astype(BF16)
    vb_ref[...] = jnp.dot(ckv, wvb_ref[...], preferred_element_type=F32).astype(BF16)


def _proj_call(x2, g_attn, w_in, g_q, w_q, g_kv, w_kb, w_vb, cos_t, sin_t, seq):
    t = x2.shape[0]
    tm = ROW_TILE
    nseq = seq // tm
    row = lambda i: (i, 0)
    const = lambda i: (0, 0)
    pos = lambda i: (i % nseq, 0)
    out = lambda w: jax.ShapeDtypeStruct((t, w), BF16)
    return pl.pallas_call(
        _proj_kernel,
        grid=(t // tm,),
        in_specs=[
            pl.BlockSpec((tm, D_MODEL), row),
            pl.BlockSpec((1, D_MODEL), const),
            pl.BlockSpec((D_MODEL, PROJ_COLS), const),
            pl.BlockSpec((1, B_Q_LORA), const),
            pl.BlockSpec((B_Q_LORA, B_HEADS * LANES), const),
            pl.BlockSpec((1, B_KV_LORA), const),
            pl.BlockSpec((B_KV_LORA, B_HEADS * LANES), const),
            pl.BlockSpec((B_KV_LORA, B_WIDTH), const),
            pl.BlockSpec((tm, LANES), pos),
            pl.BlockSpec((tm, LANES), pos),
        ],
        out_specs=[
            pl.BlockSpec((tm, A_WIDTH), row),
            pl.BlockSpec((tm, 2 * A_WIDTH), row),
            pl.BlockSpec((tm, A_WIDTH), row),
            pl.BlockSpec((tm, B_HEADS * LANES), row),
            pl.BlockSpec((tm, B_HEADS * LANES), row),
            pl.BlockSpec((tm, B_WIDTH), row),
        ],
        out_shape=[out(A_WIDTH), out(2 * A_WIDTH), out(A_WIDTH),
                   out(B_HEADS * LANES), out(B_HEADS * LANES), out(B_WIDTH)],
        compiler_params=_cparams(("parallel",)),
        name="proj",
    )(x2, g_attn, w_in, g_q, w_q, g_kv, w_kb, w_vb, cos_t, sin_t)


def _dilated_kernel(q_ref, k_ref, v_ref, bias_ref, o_ref, lse_ref, *, seq_len, key_width):
    nblk = seq_len // A_QB
    lane = _lane_iota()
    lo = lane < A_HEAD_DIM

    def block(n, carry):
        if nblk == 1:
            q0, ks, var = 0, 0, 0
        else:
            q0 = pl.multiple_of(n * A_QB, A_QB)
            ks = pl.multiple_of(jnp.clip(q0 - 64, 0, seq_len - key_width), 64)
            var = jnp.where(n == 0, 0, jnp.where(n == nblk - 1, 2, 1))
        rows = pl.ds(q0, A_QB)
        keys = pl.ds(ks, key_width)
        lse_tile = jnp.zeros((A_QB, LANES), F32)
        for p in range(A_HEADS // 2):
            tile = slice(LANES * p, LANES * (p + 1))
            qp = q_ref[0, 0, rows, tile]
            vp = v_ref[0, 0, keys, tile]
            outs = []
            for half in range(2):
                hd = 2 * p + half
                kh = k_ref[0, 0, keys, LANES * hd:LANES * (hd + 1)]
                s = lax.dot_general(qp, kh, _NT, preferred_element_type=F32) + bias_ref[var, hd]
                m = jnp.max(s, axis=-1, keepdims=True)
                pr = jnp.exp(s - m)
                den = jnp.sum(pr, axis=-1, keepdims=True)
                outs.append(jnp.dot(pr.astype(BF16), vp, preferred_element_type=F32) * (1.0 / den))
                lse_tile = jnp.where(lane == hd, m + jnp.log(den), lse_tile)
            o_ref[0, 0, rows, tile] = jnp.where(lo, outs[0], outs[1]).astype(BF16)
        lse_ref[0, 0, rows, :] = lse_tile
        return carry

    if nblk == 1:
        block(0, 0)
    else:
        lax.fori_loop(0, nblk, block, 0)


def _dilated_call(qa, ka, va, bias, batch, seq, dilation):
    r = dilation
    sl = seq // r
    kw = min(2 * A_QB, sl)
    blk = lambda w: pl.BlockSpec((1, 1, sl, w), lambda b, c: (b, c, 0, 0))
    return pl.pallas_call(
        functools.partial(_dilated_kernel, seq_len=sl, key_width=kw),
        grid=(batch, r),
        in_specs=[blk(A_WIDTH), blk(2 * A_WIDTH), blk(A_WIDTH),
                  pl.BlockSpec(bias.shape, lambda b, c: (0, 0, 0, 0))],
        out_specs=[blk(A_WIDTH), blk(LANES)],
        out_shape=[jax.ShapeDtypeStruct((batch, r, sl, A_WIDTH), BF16),
                   jax.ShapeDtypeStruct((batch, r, sl, LANES), F32)],
        compiler_params=_cparams(("parallel", "parallel")),
        name=f"dilated_r{r}",
    )(qa, ka, va, bias)


def _t5_bucket(rel):
    half = REL_BUCKETS // 2
    max_exact = half // 2
    n = np.abs(rel)
    large = max_exact + (np.log(np.maximum(n, 1) / max_exact)
                         / math.log(REL_MAX_DISTANCE / max_exact) * (half - max_exact)).astype(np.int32)
    large = np.minimum(large, half - 1)
    return (np.where(rel > 0, half, 0) + np.where(n < max_exact, n, large)).astype(np.int32)


def _dilated_bias(rel_bias, seq, dilation, half_steps):
    sl = seq // dilation
    kw = min(2 * A_QB, sl)
    offsets = [0] if sl == kw else [0, -half_steps, A_QB - kw]
    rel = np.stack([np.arange(kw)[None, :] + off - np.arange(A_QB)[:, None] for off in offsets])
    valid = np.abs(rel) <= half_steps
    bucket = np.where(valid, _t5_bucket(rel * dilation), REL_BUCKETS).astype(np.int32)
    onehot = (jnp.asarray(bucket)[..., None] == jnp.arange(REL_BUCKETS + 1, dtype=I32)).astype(F32)
    table = jnp.concatenate([rel_bias.astype(F32), jnp.full((1, A_HEADS), NEG_INF, F32)], axis=0)
    return jnp.einsum("vqkb,bh->vhqk", onehot, table, precision=lax.Precision.HIGHEST)


def _mla_kernel(q_ref, k_ref, v_ref, o_ref):
    lo = _lane_iota() < B_V_DIM
    v = v_ref[0]
    outs = []
    for half in range(2):
        tile = slice(LANES * half, LANES * (half + 1))
        s = lax.dot_general(q_ref[0, :, tile], k_ref[0, :, tile], _NT, preferred_element_type=F32)
        m = jnp.max(s, axis=-1, keepdims=True)
        pr = jnp.exp(s - m)
        den = jnp.sum(pr, axis=-1, keepdims=True)
        outs.append(jnp.dot(pr.astype(BF16), v, preferred_element_type=F32) * (1.0 / den))
    o_ref[0] = jnp.where(lo, outs[0], outs[1]).astype(BF16)


def _mla_call(qb, kb, vb, batch, seq):
    qb = qb.reshape(batch, seq, B_HEADS * LANES)
    kb = kb.reshape(batch, seq, B_HEADS * LANES)
    vb = vb.reshape(batch, seq, B_WIDTH)
    out = pl.pallas_call(
        _mla_kernel,
        grid=(batch, B_HEADS // 2, seq // B_QB),
        in_specs=[
            pl.BlockSpec((1, B_QB, 2 * LANES), lambda b, p, i: (b, i, p)),
            pl.BlockSpec((1, seq, 2 * LANES), lambda b, p, i: (b, 0, p)),
            pl.BlockSpec((1, seq, LANES), lambda b, p, i: (b, 0, p)),
        ],
        out_specs=pl.BlockSpec((1, B_QB, LANES), lambda b, p, i: (b, i, p)),
        out_shape=jax.ShapeDtypeStruct((batch, seq, B_WIDTH), BF16),
        compiler_params=_cparams(("parallel", "parallel", "parallel")),
        name="mla",
    )(qb, kb, vb)
    return out.reshape(batch * seq, B_WIDTH)


def _merge_patterns(o_refs, lse_refs):
    lses = [r[...] for r in lse_refs]
    top = functools.reduce(jnp.maximum, lses)
    es = [jnp.exp(l - top) for l in lses]
    inv = 1.0 / functools.reduce(jnp.add, es)
    ws = [e * inv for e in es]
    lo = _lane_iota() < A_HEAD_DIM
    tiles = []
    for p in range(A_HEADS // 2):
        tile = slice(LANES * p, LANES * (p + 1))
        acc = None
        for w, o_ref in zip(ws, o_refs):
            term = jnp.where(lo, w[:, 2 * p:2 * p + 1], w[:, 2 * p + 1:2 * p + 2]) * o_ref[:, tile].astype(F32)
            acc = term if acc is None else acc + term
        tiles.append(acc)
    return jnp.concatenate(tiles, axis=1)


def _mix_kernel(o1_ref, o4_ref, o16_ref, l1_ref, l4_ref, l16_ref, ob_ref, x_ref, ga_ref, gb_ref, wo_ref,
                gf_ref, wr_ref, br_ref, tri_ref, x1_ref, hp_ref, idx_ref, gate_ref, cnt_ref, carry_ref):
    i = pl.program_id(0)

    @pl.when(i == 0)
    def _():
        carry_ref[...] = jnp.zeros_like(carry_ref)

    oa = _merge_patterns((o1_ref, o4_ref, o16_ref), (l1_ref, l4_ref, l16_ref))
    a = _rms(oa, ga_ref[...]).astype(BF16)
    b = _rms(ob_ref[...].astype(F32), gb_ref[...]).astype(BF16)
    mix = (jnp.dot(a, wo_ref[0:A_WIDTH, :], preferred_element_type=F32)
           + jnp.dot(b, wo_ref[A_WIDTH:, :], preferred_element_type=F32))
    x1 = x_ref[...] + mix
    x1_ref[...] = x1
    h2 = _rms(x1, gf_ref[...])
    hb = h2.astype(BF16).astype(F32)
    half = D_MODEL // 2
    hp_ref[...] = ((lax.bitcast_convert_type(hb[:, :half], U32) >> 16)
                   | (lax.bitcast_convert_type(hb[:, half:], U32) & jnp.uint32(0xFFFF0000)))

    lg = jnp.dot(h2, wr_ref[...], preferred_element_type=F32, precision=lax.Precision.HIGHEST) + br_ref[...]
    tm = lg.shape[0]
    lane = _lane_iota(tm)
    is_g = (lane >= N_EXPERTS) & (lane < N_EXPERTS + N_GROUPS)
    gl = jnp.where(is_g, lg, NEG_INF)
    ge = jnp.exp(gl - jnp.max(gl, axis=-1, keepdims=True))
    gp = ge / jnp.sum(ge, axis=-1, keepdims=True)
    g_gate = jnp.max(gp, axis=-1, keepdims=True)
    g_idx = jnp.min(jnp.where(is_g & (gp == g_gate), lane - N_EXPERTS, LANES), axis=-1, keepdims=True)
    sel = (lane >> 3) == g_idx
    el = jnp.where(sel, lg, NEG_INF)
    ee = jnp.exp(el - jnp.max(el, axis=-1, keepdims=True))
    ep = jnp.where(sel, ee / jnp.sum(ee, axis=-1, keepdims=True), -1.0)
    p1 = jnp.max(ep, axis=-1, keepdims=True)
    i1 = jnp.min(jnp.where(ep == p1, lane, LANES), axis=-1, keepdims=True)
    ep2 = jnp.where(lane == i1, -1.0, ep)
    p2 = jnp.max(ep2, axis=-1, keepdims=True)
    i2 = jnp.min(jnp.where(sel & (ep2 == p2) & (lane != i1), lane, LANES), axis=-1, keepdims=True)
    den = p1 + p2
    g1 = g_gate * p1 / den
    g2 = g_gate * p2 / den

    hit1 = lane == i1
    hit2 = lane == i2
    onehot = jnp.where(hit1 | hit2, 1.0, 0.0)
    before = jnp.dot(tri_ref[...], onehot.astype(BF16), preferred_element_type=F32) + carry_ref[...]
    r1 = jnp.sum(jnp.where(hit1, before, 0.0), axis=-1, keepdims=True).astype(I32)
    r2 = jnp.sum(jnp.where(hit2, before, 0.0), axis=-1, keepdims=True).astype(I32)
    carry_ref[...] += jnp.sum(onehot, axis=0, keepdims=True)

    idx_ref[...] = jnp.where(lane == 0, i1, jnp.where(lane == 1, i2,
                             jnp.where(lane == 2, r1, jnp.where(lane == 3, r2, 0))))
    gate_ref[...] = jnp.where(lane == 0, g1, jnp.where(lane == 1, g2, 0.0))

    @pl.when(i == pl.num_programs(0) - 1)
    def _():
        cnt_ref[...] = jnp.broadcast_to(carry_ref[...], cnt_ref.shape).astype(I32)


def _mix_call(oas, lses, ob, x2, g_a, g_b, w_out, g_ffn, w_router, b_router, tri):
    t = x2.shape[0]
    tm = ROW_TILE
    row = lambda i: (i, 0)
    const = lambda i: (0, 0)
    return pl.pallas_call(
        _mix_kernel,
        grid=(t // tm,),
        in_specs=[
            pl.BlockSpec((tm, A_WIDTH), row),
            pl.BlockSpec((tm, A_WIDTH), row),
            pl.BlockSpec((tm, A_WIDTH), row),
            pl.BlockSpec((tm, LANES), row),
            pl.BlockSpec((tm, LANES), row),
            pl.BlockSpec((tm, LANES), row),
            pl.BlockSpec((tm, B_WIDTH), row),
            pl.BlockSpec((tm, D_MODEL), row),
            pl.BlockSpec((1, A_WIDTH), const),
            pl.BlockSpec((1, B_WIDTH), const),
            pl.BlockSpec((D_MODEL, D_MODEL), const),
            pl.BlockSpec((1, D_MODEL), const),
            pl.BlockSpec((D_MODEL, LANES), const),
            pl.BlockSpec((1, LANES), const),
            pl.BlockSpec((tm, tm), const),
        ],
        out_specs=[
            pl.BlockSpec((tm, D_MODEL), row),
            pl.BlockSpec((tm, D_MODEL // 2), row),
            pl.BlockSpec((tm, LANES), row),
            pl.BlockSpec((tm, LANES), row),
            pl.BlockSpec((8, LANES), const),
        ],
        out_shape=[
            jax.ShapeDtypeStruct((t, D_MODEL), F32),
            jax.ShapeDtypeStruct((t, D_MODEL // 2), U32),
            jax.ShapeDtypeStruct((t, LANES), I32),
            jax.ShapeDtypeStruct((t, LANES), F32),
            jax.ShapeDtypeStruct((8, LANES), I32),
        ],
        scratch_shapes=[pltpu.VMEM((1, LANES), F32)],
        compiler_params=_cparams(("arbitrary",)),
        name="mix_router",
    )(*oas, *lses, ob, x2, g_a, g_b, w_out, g_ffn, w_router, b_router, tri)


def _dest_kernel(idx_ref, pstart_ref, dest_ref):
    idx = idx_ref[...]
    lane = _lane_iota(idx.shape[0])
    ps = pstart_ref[...]

    def slot(k):
        e = jnp.sum(jnp.where(lane == k, idx, 0), axis=-1, keepdims=True)
        rank = jnp.sum(jnp.where(lane == 2 + k, idx, 0), axis=-1, keepdims=True)
        return jnp.sum(jnp.where(lane == e, ps, 0), axis=-1, keepdims=True) + rank

    dest_ref[...] = jnp.where(lane == 0, slot(0), jnp.where(lane == 1, slot(1), 0))


def _dest_call(idx, pstart):
    t = idx.shape[0]
    tm = ROW_TILE
    return pl.pallas_call(
        _dest_kernel,
        grid=(t // tm,),
        in_specs=[pl.BlockSpec((tm, LANES), lambda i: (i, 0)), pl.BlockSpec((1, LANES), lambda i: (0, 0))],
        out_specs=pl.BlockSpec((tm, LANES), lambda i: (i, 0)),
        out_shape=jax.ShapeDtypeStruct((t, LANES), I32),
        compiler_params=_cparams(("parallel",)),
        name="dest_rows",
    )(idx, pstart)


def _dispatch_kernel(dest_ref, h_ref, buf_in, buf_out, sem):
    del buf_in
    tt = h_ref.shape[0]

    def row_copy(i, d):
        return pltpu.make_async_copy(h_ref.at[pl.ds(i, 1)], buf_out.at[pl.ds(d, 1)], sem)

    def issue(i, c):
        for k in range(TOP_K):
            row_copy(i, dest_ref[TOP_K * i + k]).start()
        return c

    def drain(i, c):
        for k in range(TOP_K):
            row_copy(i, dest_ref[TOP_K * i + k]).wait()
        return c

    lax.fori_loop(0, tt, issue, 0)
    lax.fori_loop(0, tt, drain, 0)


def _dispatch_call(dest_flat, hp, n_rows):
    t = hp.shape[0]
    tt = ROW_TILE
    buf0 = jnp.zeros((n_rows, D_MODEL // 2), U32)
    return pl.pallas_call(
        _dispatch_kernel,
        grid=(t // tt,),
        in_specs=[
            pl.BlockSpec((TOP_K * tt,), lambda i: (i,), memory_space=pltpu.SMEM),
            pl.BlockSpec((tt, D_MODEL // 2), lambda i: (i, 0)),
            pl.BlockSpec(memory_space=pl.ANY),
        ],
        out_specs=pl.BlockSpec(memory_space=pl.ANY),
        out_shape=jax.ShapeDtypeStruct((n_rows, D_MODEL // 2), U32),
        scratch_shapes=[pltpu.SemaphoreType.DMA(())],
        input_output_aliases={2: 0},
        compiler_params=_cparams(("arbitrary",), disable_bounds_checks=True, has_side_effects=True),
        name="dispatch",
    )(dest_flat, hp, buf0)


def _expert_kernel(be_ref, new_ref, valid_ref, buf_ref, wg_ref, wu_ref, wd_ref, out_ref, wg_s, wu_s, wd_s):
    j = pl.program_id(0)
    del be_ref

    @pl.when(new_ref[j] == 1)
    def _():
        wg_s[...] = wg_ref[0].astype(BF16)
        wu_s[...] = wu_ref[0].astype(BF16)
        wd_s[...] = wd_ref[0].astype(BF16)

    @pl.when(valid_ref[j] == 1)
    def _():
        w = buf_ref[...]
        half = D_MODEL // 2
        x_lo = lax.bitcast_convert_type(w << 16, F32).astype(BF16)
        x_hi = lax.bitcast_convert_type(w & jnp.uint32(0xFFFF0000), F32).astype(BF16)
        g = (jnp.dot(x_lo, wg_s[0:half, :], preferred_element_type=F32)
             + jnp.dot(x_hi, wg_s[half:, :], preferred_element_type=F32))
        u = (jnp.dot(x_lo, wu_s[0:half, :], preferred_element_type=F32)
             + jnp.dot(x_hi, wu_s[half:, :], preferred_element_type=F32))
        hb = (g * jax.nn.sigmoid(g)) * u
        out_ref[...] = jnp.dot(hb.astype(BF16), wd_s[...], preferred_element_type=F32)

    @pl.when(valid_ref[j] == 0)
    def _():
        out_ref[...] = jnp.zeros_like(out_ref)


def _expert_call(block_expert, block_new, block_valid, buf, w_gate, w_up, w_down):
    n_rows = buf.shape[0]
    nb = n_rows // MOE_BLK
    wsel = lambda j, be, nw, va: (be[j], 0, 0)
    rows = lambda j, be, nw, va: (j, 0)
    return pl.pallas_call(
        _expert_kernel,
        grid_spec=pltpu.PrefetchScalarGridSpec(
            num_scalar_prefetch=3,
            grid=(nb,),
            in_specs=[
                pl.BlockSpec((MOE_BLK, D_MODEL // 2), rows),
                pl.BlockSpec((1, D_MODEL, EXPERT_FF), wsel),
                pl.BlockSpec((1, D_MODEL, EXPERT_FF), wsel),
                pl.BlockSpec((1, EXPERT_FF, D_MODEL), wsel),
            ],
            out_specs=pl.BlockSpec((MOE_BLK, D_MODEL), rows),
            scratch_shapes=[pltpu.VMEM((D_MODEL, EXPERT_FF), BF16),
                            pltpu.VMEM((D_MODEL, EXPERT_FF), BF16),
                            pltpu.VMEM((EXPERT_FF, D_MODEL), BF16)],
        ),
        out_shape=jax.ShapeDtypeStruct((n_rows, D_MODEL), F32),
        compiler_params=_cparams(("arbitrary",)),
        name="experts",
    )(block_expert, block_new, block_valid, buf, w_gate, w_up, w_down)


def _combine_kernel(dest_ref, x1_ref, gate_ref, gf_ref, eo_ref, o_ref, rows_ref, sem):
    tt = x1_ref.shape[0]

    def row_copy(i, k, d):
        return pltpu.make_async_copy(eo_ref.at[pl.ds(d, 1)], rows_ref.at[k, pl.ds(i, 1)], sem)

    def issue(i, c):
        for k in range(TOP_K):
            row_copy(i, k, dest_ref[TOP_K * i + k]).start()
        return c

    def drain(i, c):
        for k in range(TOP_K):
            row_copy(i, k, dest_ref[TOP_K * i + k]).wait()
        return c

    lax.fori_loop(0, tt, issue, 0)
    lax.fori_loop(0, tt, drain, 0)
    gate = gate_ref[...]
    y = rows_ref[0] * gate[:, 0:1] + rows_ref[1] * gate[:, 1:2]
    o_ref[...] = _rms(x1_ref[...] + y, gf_ref[...])


def _combine_call(dest_flat, x1, gates, g_final, expert_out):
    t = x1.shape[0]
    tt = ROW_TILE // 2
    return pl.pallas_call(
        _combine_kernel,
        grid=(t // tt,),
        in_specs=[
            pl.BlockSpec((TOP_K * tt,), lambda i: (i,), memory_space=pltpu.SMEM),
            pl.BlockSpec((tt, D_MODEL), lambda i: (i, 0)),
            pl.BlockSpec((tt, LANES), lambda i: (i, 0)),
            pl.BlockSpec((1, D_MODEL), lambda i: (0, 0)),
            pl.BlockSpec(memory_space=pl.ANY),
        ],
        out_specs=pl.BlockSpec((tt, D_MODEL), lambda i: (i, 0)),
        out_shape=jax.ShapeDtypeStruct((t, D_MODEL), F32),
        scratch_shapes=[pltpu.VMEM((TOP_K, tt, D_MODEL), F32), pltpu.SemaphoreType.DMA(())],
        compiler_params=_cparams(("arbitrary",), disable_bounds_checks=True),
        name="combine",
    )(dest_flat, x1, gates, g_final, expert_out)


def _rope_tables(seq):
    half = B_QK_ROPE // 2
    inv_freq = ROPE_THETA ** (-(jnp.arange(half, dtype=F32) / half))
    ang = jnp.arange(seq, dtype=F32)[:, None] * inv_freq[None, :]
    cos, sin = jnp.cos(ang), jnp.sin(ang)
    z = jnp.zeros((seq, B_QK_NOPE), F32)
    z2 = jnp.zeros((seq, B_QK_ROPE), F32)
    return (jnp.concatenate([z, cos, cos, z2], axis=1), jnp.concatenate([z, -sin, sin, z2], axis=1))


def _swap_halves(w):
    half = w.shape[-1] // 2
    return jnp.concatenate([w[..., half:], w[..., :half]], axis=-1)


def _layout_weights(w_in, w_q_up, w_kv_up):
    d = w_in.shape[0]
    w_kr = w_in[:, 3 * A_WIDTH + B_Q_LORA + B_KV_LORA:]
    w_in_l = jnp.concatenate(
        [w_in[:, :3 * A_WIDTH + B_Q_LORA + B_KV_LORA], jnp.zeros((d, B_QK_NOPE), F32), w_kr, _swap_halves(w_kr)],
        axis=1).astype(BF16)
    wq = w_q_up.reshape(B_Q_LORA, B_HEADS, B_QK_NOPE + B_QK_ROPE)
    wq_l = jnp.concatenate([wq, _swap_halves(wq[..., B_QK_NOPE:])], axis=-1)
    wq_l = wq_l.reshape(B_Q_LORA, B_HEADS * LANES).astype(BF16)
    wkv = w_kv_up.reshape(B_KV_LORA, B_HEADS, B_QK_NOPE + B_V_DIM)
    wkb = jnp.concatenate([wkv[..., :B_QK_NOPE], jnp.zeros_like(wkv[..., :B_QK_NOPE])], axis=-1)
    wkb = wkb.reshape(B_KV_LORA, B_HEADS * LANES).astype(BF16)
    wvb = wkv[..., B_QK_NOPE:].reshape(B_KV_LORA, B_WIDTH).astype(BF16)
    return w_in_l, wq_l, wkb, wvb


def _block_plan(counts, n_blocks):
    padded = (counts + MOE_BLK - 1) // MOE_BLK * MOE_BLK
    ends = jnp.cumsum(padded)
    starts = ends - padded
    first_row = jnp.arange(n_blocks, dtype=I32) * MOE_BLK
    expert = jnp.minimum(jnp.sum(ends[None, :] <= first_row[:, None], axis=1), N_EXPERTS - 1).astype(I32)
    new = jnp.concatenate([jnp.ones((1,), I32), (expert[1:] != expert[:-1]).astype(I32)])
    valid = (first_row < ends[-1]).astype(I32)
    return starts.astype(I32), expert, new, valid


def kernel(x, g_attn_norm, w_in, rel_bias, g_q_latent, w_q_up, g_kv_latent, w_kv_up, g_out_a, g_out_b, w_out,
           g_ffn_norm, w_router_group, b_router_group, w_router_expert, b_router_expert, w_gate, w_up, w_down,
           g_final):
    batch, seq, d = x.shape
    t = batch * seq
    assert g_attn_norm.shape[0] == 1 and d == D_MODEL and seq % ROW_TILE == 0
    cos_t, sin_t = _rope_tables(seq)
    tri = jnp.tril(jnp.ones((ROW_TILE, ROW_TILE), F32), -1).astype(BF16)
    n_blocks = t * TOP_K // MOE_BLK + N_EXPERTS
    x2 = x.reshape(t, d)
    row = lambda v: v.reshape(1, -1)

    w_in_l, wq_l, wkb_l, wvb_l = _layout_weights(w_in[0], w_q_up[0], w_kv_up[0])
    qa, ka, va, qb, kb, vb = _proj_call(x2, row(g_attn_norm[0]), w_in_l, row(g_q_latent[0]), wq_l,
                                        row(g_kv_latent[0]), wkb_l, wvb_l, cos_t, sin_t, seq)
    oas, lses = [], []
    for window, dilation in DILATED_PATTERNS:
        bias = _dilated_bias(rel_bias, seq, dilation, window // (2 * dilation))
        to_res = lambda a: a.reshape(batch, seq // dilation, dilation, -1).transpose(0, 2, 1, 3)
        from_res = lambda a: a.transpose(0, 2, 1, 3).reshape(t, -1)
        o_p, lse_p = _dilated_call(to_res(qa), to_res(ka), to_res(va), bias, batch, seq, dilation)
        oas.append(from_res(o_p))
        lses.append(from_res(lse_p))
    ob = _mla_call(qb, kb, vb, batch, seq)

    pad = LANES - N_EXPERTS - N_GROUPS
    w_router = jnp.concatenate([w_router_expert[0], w_router_group[0], jnp.zeros((d, pad), F32)], axis=1)
    b_router = jnp.concatenate([b_router_expert[0], b_router_group[0], jnp.zeros((pad,), F32)])
    x1, hp, idx, gates, cnt = _mix_call(oas, lses, ob, x2, row(g_out_a[0]), row(g_out_b[0]), w_out[0].astype(BF16),
                                        row(g_ffn_norm[0]), w_router, row(b_router), tri)
    pstart, block_expert, block_new, block_valid = _block_plan(cnt[0, :N_EXPERTS], n_blocks)
    pstart_row = jnp.concatenate([pstart, jnp.zeros((LANES - N_EXPERTS,), I32)]).reshape(1, LANES)
    dest = _dest_call(idx, pstart_row)
    dest_flat = dest[:, :TOP_K].reshape(-1)
    buf = _dispatch_call(dest_flat, hp, n_blocks * MOE_BLK)
    expert_out = _expert_call(block_expert, block_new, block_valid, buf, w_gate[0], w_up[0], w_down[0])
    return _combine_call(dest_flat, x1, gates, row(g_final), expert_out).reshape(batch, seq, d)
```

```python
import functools
import math

import numpy as np
import jax
import jax.numpy as jnp
from jax import lax
from jax.experimental import pallas as pl
from jax.experimental.pallas import tpu as pltpu

F32 = jnp.float32
BF16 = jnp.bfloat16
I32 = jnp.int32
U32 = jnp.uint32

D_MODEL = 1024
EPS = 1e-6
NEG_INF = -1e30
LANES = 128

A_HEADS = 8
A_HEAD_DIM = 64
A_WIDTH = 512
DILATED_PATTERNS = ((128, 1), (512, 4), (2048, 16))
REL_BUCKETS = 32
REL_MAX_DISTANCE = 1024
A_QB = 128

B_HEADS = 8
B_Q_LORA = 256
B_KV_LORA = 128
B_QK_NOPE = 64
B_QK_ROPE = 32
B_V_DIM = 64
B_WIDTH = 512
ROPE_THETA = 10000.0
B_SCALE = (B_QK_NOPE + B_QK_ROPE) ** -0.5
B_QB = 256

N_GROUPS = 4
EXPERTS_PER_GROUP = 8
N_EXPERTS = 32
TOP_K = 2
EXPERT_FF = 256
MOE_BLK = 256
ROUTER_ROWS = 40

ROW_TILE = 512
ISSUE_UNROLL = 8
PROJ_COLS = 2048

_NT = (((1,), (1,)), ((), ()))


def _cparams(semantics, vmem_mb=48, **kw):
    return pltpu.CompilerParams(dimension_semantics=semantics,
                                vmem_limit_bytes=vmem_mb * 1024 * 1024, **kw)


def _rms(x, g):
    return x * lax.rsqrt(jnp.mean(x * x, axis=-1, keepdims=True) + EPS) * g


def _lane_iota(rows=1):
    return lax.broadcasted_iota(I32, (rows, LANES), 1)


def _proj_kernel(x_ref, g_ref, win_ref, gq_ref, wq_ref, gkv_ref, wkb_ref, wvb_ref, cos_ref, sin_ref,
                 qa_ref, ka_ref, va_ref, qb_ref, kb_ref, vb_ref):
    h = _rms(x_ref[...], g_ref[...]).astype(BF16)
    proj = jnp.dot(h, win_ref[...], preferred_element_type=F32)
    lo = _lane_iota() < A_HEAD_DIM
    qa_ref[...] = (proj[:, 0:512] * (A_HEAD_DIM ** -0.5)).astype(BF16)
    for p in range(A_HEADS // 2):
        kp = proj[:, 512 + LANES * p:512 + LANES * (p + 1)]
        ka_ref[:, 2 * LANES * p:2 * LANES * p + LANES] = jnp.where(lo, kp, 0.0).astype(BF16)
        ka_ref[:, 2 * LANES * p + LANES:2 * LANES * (p + 1)] = jnp.where(lo, 0.0, kp).astype(BF16)
    va_ref[...] = proj[:, 1024:1536].astype(BF16)

    cos = cos_ref[...]
    sin = sin_ref[...]
    cq = _rms(proj[:, 1536:1792], gq_ref[...]).astype(BF16)
    q = jnp.dot(cq, wq_ref[...], preferred_element_type=F32)
    q_mul = (cos + jnp.where(lo, 1.0, 0.0)) * B_SCALE
    q_rot = sin * B_SCALE
    for hd in range(B_HEADS):
        t = q[:, LANES * hd:LANES * (hd + 1)]
        qb_ref[:, LANES * hd:LANES * (hd + 1)] = (t * q_mul + pltpu.roll(t, 96, 1) * q_rot).astype(BF16)

    ckv = _rms(proj[:, 1792:1920], gkv_ref[...]).astype(BF16)
    kr = proj[:, 1920:2048]
    kr = kr * cos + pltpu.roll(kr, 96, 1) * sin
    kn = jnp.dot(ckv, wkb_ref[...], preferred_element_type=F32)
    for hd in range(B_HEADS):
        kb_ref[:, LANES * hd:LANES * (hd + 1)] = (kn[:, LANES * hd:LANES * (hd + 1)] + kr).astype(BF16)
    vb_ref[...] = jnp.dot(ckv, wvb_ref[...], preferred_element_type=F32).astype(BF16)


def _proj_call(x2, g_attn, w_in, g_q, w_q, g_kv, w_kb, w_vb, cos_t, sin_t, seq):
    t = x2.shape[0]
    tm = ROW_TILE
    nseq = seq // tm
    row = lambda i: (i, 0)
    const = lambda i: (0, 0)
    pos = lambda i: (i % nseq, 0)
    out = lambda w: jax.ShapeDtypeStruct((t, w), BF16)
    return pl.pallas_call(
        _proj_kernel,
        grid=(t // tm,),
        in_specs=[
            pl.BlockSpec((tm, D_MODEL), row),
            pl.BlockSpec((1, D_MODEL), const),
            pl.BlockSpec((D_MODEL, PROJ_COLS), const),
            pl.BlockSpec((1, B_Q_LORA), const),
            pl.BlockSpec((B_Q_LORA, B_HEADS * LANES), const),
            pl.BlockSpec((1, B_KV_LORA), const),
            pl.BlockSpec((B_KV_LORA, B_HEADS * LANES), const),
            pl.BlockSpec((B_KV_LORA, B_WIDTH), const),
            pl.BlockSpec((tm, LANES), pos),
            pl.BlockSpec((tm, LANES), pos),
        ],
        out_specs=[
            pl.BlockSpec((tm, A_WIDTH), row),
            pl.BlockSpec((tm, 2 * A_WIDTH), row),
            pl.BlockSpec((tm, A_WIDTH), row),
            pl.BlockSpec((tm, B_HEADS * LANES), row),
            pl.BlockSpec((tm, B_HEADS * LANES), row),
            pl.BlockSpec((tm, B_WIDTH), row),
        ],
        out_shape=[out(A_WIDTH), out(2 * A_WIDTH), out(A_WIDTH),
                   out(B_HEADS * LANES), out(B_HEADS * LANES), out(B_WIDTH)],
        compiler_params=_cparams(("parallel",)),
        name="proj",
    )(x2, g_attn, w_in, g_q, w_q, g_kv, w_kb, w_vb, cos_t, sin_t)


def _dilated_kernel(q_ref, k_ref, v_ref, bias_ref, o_ref, lse_ref, *, seq_len, key_width, group):
    nblk = seq_len // A_QB
    lo = _lane_iota() < A_HEAD_DIM

    def block(it, carry):
        c = it // nblk
        if nblk == 1:
            q0, ks, var = 0, 0, 0
        else:
            n = it % nblk
            q0 = pl.multiple_of(n * A_QB, A_QB)
            ks = pl.multiple_of(jnp.clip(q0 - 64, 0, seq_len - key_width), 64)
            var = jnp.where(n == 0, 0, jnp.where(n == nblk - 1, 2, 1))
        rows = pl.ds(q0, A_QB)
        keys = pl.ds(ks, key_width)
        for p in range(A_HEADS // 2):
            tile = slice(LANES * p, LANES * (p + 1))
            qp = q_ref[0, c, rows, tile]
            vp = v_ref[0, c, keys, tile]
            outs, lses = [], []
            for half in range(2):
                hd = 2 * p + half
                kh = k_ref[0, c, keys, LANES * hd:LANES * (hd + 1)]
                s = lax.dot_general(qp, kh, _NT, preferred_element_type=F32) + bias_ref[var, hd]
                m = jnp.max(s, axis=-1, keepdims=True)
                pr = jnp.exp(s - m)
                den = jnp.sum(pr, axis=-1, keepdims=True)
                outs.append(jnp.dot(pr.astype(BF16), vp, preferred_element_type=F32) * (1.0 / den))
                lses.append(m + jnp.log(den))
            o_ref[0, c, rows, tile] = jnp.where(lo, outs[0], outs[1]).astype(BF16)
            lse_ref[0, c, rows, tile] = jnp.where(lo, lses[0], lses[1])
        return carry

    lax.fori_loop(0, group * nblk, block, 0)


def _dilated_call(qa, ka, va, bias, batch, seq, dilation):
    r = dilation
    sl = seq // r
    kw = min(2 * A_QB, sl)
    group = max(1, min(r, (4 * A_QB) // sl))
    blk = lambda w: pl.BlockSpec((1, group, sl, w), lambda b, c: (b, c, 0, 0))
    return pl.pallas_call(
        functools.partial(_dilated_kernel, seq_len=sl, key_width=kw, group=group),
        grid=(batch, r // group),
        in_specs=[blk(A_WIDTH), blk(2 * A_WIDTH), blk(A_WIDTH),
                  pl.BlockSpec(bias.shape, lambda b, c: (0, 0, 0, 0))],
        out_specs=[blk(A_WIDTH), blk(A_WIDTH)],
        out_shape=[jax.ShapeDtypeStruct((batch, r, sl, A_WIDTH), BF16),
                   jax.ShapeDtypeStruct((batch, r, sl, A_WIDTH), F32)],
        compiler_params=_cparams(("parallel", "parallel")),
        name=f"dilated_r{r}",
    )(qa, ka, va, bias)


def _t5_bucket(rel):
    half = REL_BUCKETS // 2
    max_exact = half // 2
    n = np.abs(rel)
    large = max_exact + (np.log(np.maximum(n, 1) / max_exact)
                         / math.log(REL_MAX_DISTANCE / max_exact) * (half - max_exact)).astype(np.int32)
    large = np.minimum(large, half - 1)
    return (np.where(rel > 0, half, 0) + np.where(n < max_exact, n, large)).astype(np.int32)


def _dilated_bias(rel_bias, seq, dilation, half_steps):
    sl = seq // dilation
    kw = min(2 * A_QB, sl)
    offsets = [0] if sl == kw else [0, -half_steps, A_QB - kw]
    rel = np.stack([np.arange(kw)[None, :] + off - np.arange(A_QB)[:, None] for off in offsets])
    valid = np.abs(rel) <= half_steps
    bucket = np.where(valid, _t5_bucket(rel * dilation), REL_BUCKETS).astype(np.int32)
    onehot = (jnp.asarray(bucket)[..., None] == jnp.arange(REL_BUCKETS + 1, dtype=I32)).astype(F32)
    table = jnp.concatenate([rel_bias.astype(F32), jnp.full((1, A_HEADS), NEG_INF, F32)], axis=0)
    return jnp.einsum("vqkb,bh->vhqk", onehot, table, precision=lax.Precision.HIGHEST)


def _mla_kernel(q_ref, k_ref, v_ref, o_ref):
    lo = _lane_iota() < B_V_DIM
    v = v_ref[0]
    outs = []
    for half in range(2):
        tile = slice(LANES * half, LANES * (half + 1))
        s = lax.dot_general(q_ref[0, :, tile], k_ref[0, :, tile], _NT, preferred_element_type=F32)
        m = jnp.max(s, axis=-1, keepdims=True)
        pr = jnp.exp(s - m)
        den = jnp.sum(pr, axis=-1, keepdims=True)
        outs.append(jnp.dot(pr.astype(BF16), v, preferred_element_type=F32) * (1.0 / den))
    o_ref[0] = jnp.where(lo, outs[0], outs[1]).astype(BF16)


def _mla_call(qb, kb, vb, batch, seq):
    qb = qb.reshape(batch, seq, B_HEADS * LANES)
    kb = kb.reshape(batch, seq, B_HEADS * LANES)
    vb = vb.reshape(batch, seq, B_WIDTH)
    out = pl.pallas_call(
        _mla_kernel,
        grid=(batch, B_HEADS // 2, seq // B_QB),
        in_specs=[
            pl.BlockSpec((1, B_QB, 2 * LANES), lambda b, p, i: (b, i, p)),
            pl.BlockSpec((1, seq, 2 * LANES), lambda b, p, i: (b, 0, p)),
            pl.BlockSpec((1, seq, LANES), lambda b, p, i: (b, 0, p)),
        ],
        out_specs=pl.BlockSpec((1, B_QB, LANES), lambda b, p, i: (b, i, p)),
        out_shape=jax.ShapeDtypeStruct((batch, seq, B_WIDTH), BF16),
        compiler_params=_cparams(("parallel", "parallel", "parallel")),
        name="mla",
    )(qb, kb, vb)
    return out.reshape(batch * seq, B_WIDTH)


def _merge_patterns(o_refs, lse_refs):
    lses = [r[...] for r in lse_refs]
    top = functools.reduce(jnp.maximum, lses)
    es = [jnp.exp(l - top) for l in lses]
    inv = 1.0 / functools.reduce(jnp.add, es)
    return functools.reduce(jnp.add, [e * inv * r[...].astype(F32) for e, r in zip(es, o_refs)])


def _mix_kernel(o1_ref, o4_ref, o16_ref, l1_ref, l4_ref, l16_ref, ob_ref, x_ref, ga_ref, gb_ref, wo_ref,
                gf_ref, wr_ref, br_ref, tri_ref, x1_ref, hp_ref, idx_ref, gate_ref, cnt_ref, carry_ref):
    i = pl.program_id(0)

    @pl.when(i == 0)
    def _():
        carry_ref[...] = jnp.zeros_like(carry_ref)

    oa = _merge_patterns((o1_ref, o4_ref, o16_ref), (l1_ref, l4_ref, l16_ref))
    a = _rms(oa, ga_ref[...]).astype(BF16)
    b = _rms(ob_ref[...].astype(F32), gb_ref[...]).astype(BF16)
    mix = (jnp.dot(a, wo_ref[0:A_WIDTH, :], preferred_element_type=F32)
           + jnp.dot(b, wo_ref[A_WIDTH:, :], preferred_element_type=F32))
    x1 = x_ref[...] + mix
    x1_ref[...] = x1
    h2 = _rms(x1, gf_ref[...])
    hb = h2.astype(BF16).astype(F32)
    half = D_MODEL // 2
    hp_ref[...] = ((lax.bitcast_convert_type(hb[:, :half], U32) >> 16)
                   | (lax.bitcast_convert_type(hb[:, half:], U32) & jnp.uint32(0xFFFF0000)))

    lg = lax.dot_general(wr_ref[...], h2, _NT, preferred_element_type=F32,
                         precision=lax.Precision.HIGHEST) + br_ref[...]
    row = lax.broadcasted_iota(I32, lg.shape, 0)
    is_g = (row >= N_EXPERTS) & (row < N_EXPERTS + N_GROUPS)
    gl = jnp.where(is_g, lg, NEG_INF)
    ge = jnp.exp(gl - jnp.max(gl, axis=0, keepdims=True))
    gp = ge / jnp.sum(ge, axis=0, keepdims=True)
    g_gate = jnp.max(gp, axis=0, keepdims=True)
    g_idx = jnp.min(jnp.where(is_g & (gp == g_gate), row - N_EXPERTS, LANES), axis=0, keepdims=True)
    sel = (row >> 3) == g_idx
    el = jnp.where(sel, lg, NEG_INF)
    ee = jnp.exp(el - jnp.max(el, axis=0, keepdims=True))
    ep = jnp.where(sel, ee / jnp.sum(ee, axis=0, keepdims=True), -1.0)
    p1 = jnp.max(ep, axis=0, keepdims=True)
    i1 = jnp.min(jnp.where(ep == p1, row, LANES), axis=0, keepdims=True)
    ep2 = jnp.where(row == i1, -1.0, ep)
    p2 = jnp.max(ep2, axis=0, keepdims=True)
    i2 = jnp.min(jnp.where(sel & (ep2 == p2) & (row != i1), row, LANES), axis=0, keepdims=True)
    den = p1 + p2
    g1 = g_gate * p1 / den
    g2 = g_gate * p2 / den

    hit1 = row == i1
    hit2 = row == i2
    onehot = jnp.where(hit1 | hit2, 1.0, 0.0)
    before = jnp.dot(onehot.astype(BF16), tri_ref[...], preferred_element_type=F32) + carry_ref[...]
    r1 = jnp.sum(jnp.where(hit1, before, 0.0), axis=0, keepdims=True).astype(I32)
    r2 = jnp.sum(jnp.where(hit2, before, 0.0), axis=0, keepdims=True).astype(I32)
    carry_ref[...] += jnp.sum(onehot, axis=1, keepdims=True)

    row8 = lax.broadcasted_iota(I32, idx_ref.shape, 0)
    idx_ref[...] = jnp.where(row8 == 0, i1, jnp.where(row8 == 1, i2,
                             jnp.where(row8 == 2, r1, jnp.where(row8 == 3, r2, 0))))
    gate_ref[...] = jnp.where(row8 == 0, g1, jnp.where(row8 == 1, g2, 0.0))

    @pl.when(i == pl.num_programs(0) - 1)
    def _():
        cnt_ref[...] = jnp.broadcast_to(carry_ref[...], cnt_ref.shape).astype(I32)


def _mix_call(oas, lses, ob, x2, g_a, g_b, w_out, g_ffn, w_router, b_router, tri):
    t = x2.shape[0]
    tm = ROW_TILE
    row = lambda i: (i, 0)
    const = lambda i: (0, 0)
    return pl.pallas_call(
        _mix_kernel,
        grid=(t // tm,),
        in_specs=[
            pl.BlockSpec((tm, A_WIDTH), row),
            pl.BlockSpec((tm, A_WIDTH), row),
            pl.BlockSpec((tm, A_WIDTH), row),
            pl.BlockSpec((tm, A_WIDTH), row),
            pl.BlockSpec((tm, A_WIDTH), row),
            pl.BlockSpec((tm, A_WIDTH), row),
            pl.BlockSpec((tm, B_WIDTH), row),
            pl.BlockSpec((tm, D_MODEL), row),
            pl.BlockSpec((1, A_WIDTH), const),
            pl.BlockSpec((1, B_WIDTH), const),
            pl.BlockSpec((D_MODEL, D_MODEL), const),
            pl.BlockSpec((1, D_MODEL), const),
            pl.BlockSpec((ROUTER_ROWS, D_MODEL), const),
            pl.BlockSpec((ROUTER_ROWS, 1), const),
            pl.BlockSpec((tm, tm), const),
        ],
        out_specs=[
            pl.BlockSpec((tm, D_MODEL), row),
            pl.BlockSpec((tm, D_MODEL // 2), row),
            pl.BlockSpec((8, tm), lambda i: (0, i)),
            pl.BlockSpec((8, tm), lambda i: (0, i)),
            pl.BlockSpec((ROUTER_ROWS, LANES), const),
        ],
        out_shape=[
            jax.ShapeDtypeStruct((t, D_MODEL), F32),
            jax.ShapeDtypeStruct((t, D_MODEL // 2), U32),
            jax.ShapeDtypeStruct((8, t), I32),
            jax.ShapeDtypeStruct((8, t), F32),
            jax.ShapeDtypeStruct((ROUTER_ROWS, LANES), I32),
        ],
        scratch_shapes=[pltpu.VMEM((ROUTER_ROWS, 1), F32)],
        compiler_params=_cparams(("arbitrary",)),
        name="mix_router",
    )(*oas, *lses, ob, x2, g_a, g_b, w_out, g_ffn, w_router, b_router, tri)


def _dest_kernel(idx_ref, pstart_ref, dest_ref):
    idx = idx_ref[...]
    row = lax.broadcasted_iota(I32, (ROUTER_ROWS, idx.shape[1]), 0)
    ps = pstart_ref[...]

    def slot(k):
        return jnp.sum(jnp.where(row == idx[k:k + 1, :], ps, 0), axis=0, keepdims=True) + idx[2 + k:3 + k, :]

    row8 = lax.broadcasted_iota(I32, idx.shape, 0)
    dest_ref[...] = jnp.where(row8 == 0, slot(0), jnp.where(row8 == 1, slot(1), 0))


def _dest_call(idx, pstart):
    t = idx.shape[1]
    tm = 4 * ROW_TILE
    return pl.pallas_call(
        _dest_kernel,
        grid=(t // tm,),
        in_specs=[pl.BlockSpec((8, tm), lambda i: (0, i)), pl.BlockSpec((ROUTER_ROWS, 1), lambda i: (0, 0))],
        out_specs=pl.BlockSpec((8, tm), lambda i: (0, i)),
        out_shape=jax.ShapeDtypeStruct((8, t), I32),
        compiler_params=_cparams(("parallel",)),
        name="dest_rows",
    )(idx, pstart)


def _dispatch_kernel(d0_ref, d1_ref, h_ref, buf_in, buf_out, sem):
    del buf_in
    tt = h_ref.shape[0]

    def issue(i, c):
        for d_ref in (d0_ref, d1_ref):
            pltpu.make_async_copy(h_ref.at[pl.ds(i, 1)], buf_out.at[pl.ds(d_ref[i], 1)], sem).start()
        return c

    lax.fori_loop(0, tt, issue, 0, unroll=ISSUE_UNROLL)
    for k in range(TOP_K):
        pltpu.make_async_copy(h_ref, buf_out.at[pl.ds(0, tt)], sem).wait()


def _dispatch_call(dests, hp, n_rows):
    t = hp.shape[0]
    tt = 2 * ROW_TILE
    buf0 = jnp.zeros((n_rows, D_MODEL // 2), U32)
    return pl.pallas_call(
        _dispatch_kernel,
        grid=(t // tt,),
        in_specs=[
            pl.BlockSpec((tt,), lambda i: (i,), memory_space=pltpu.SMEM),
            pl.BlockSpec((tt,), lambda i: (i,), memory_space=pltpu.SMEM),
            pl.BlockSpec((tt, D_MODEL // 2), lambda i: (i, 0)),
            pl.BlockSpec(memory_space=pl.ANY),
        ],
        out_specs=pl.BlockSpec(memory_space=pl.ANY),
        out_shape=jax.ShapeDtypeStruct((n_rows, D_MODEL // 2), U32),
        scratch_shapes=[pltpu.SemaphoreType.DMA(())],
        input_output_aliases={3: 0},
        compiler_params=_cparams(("arbitrary",), disable_bounds_checks=True, has_side_effects=True),
        name="dispatch",
    )(*dests, hp, buf0)


def _expert_kernel(be_ref, new_ref, valid_ref, buf_ref, wg_ref, wu_ref, wd_ref, out_ref, wg_s, wu_s, wd_s):
    j = pl.program_id(0)
    del be_ref

    @pl.when(new_ref[j] == 1)
    def _():
        wg_s[...] = wg_ref[0].astype(BF16)
        wu_s[...] = wu_ref[0].astype(BF16)
        wd_s[...] = wd_ref[0].astype(BF16)

    @pl.when(valid_ref[j] == 1)
    def _():
        w = buf_ref[...]
        half = D_MODEL // 2
        x_lo = lax.bitcast_convert_type(w << 16, F32).astype(BF16)
        x_hi = lax.bitcast_convert_type(w & jnp.uint32(0xFFFF0000), F32).astype(BF16)
        g = (jnp.dot(x_lo, wg_s[0:half, :], preferred_element_type=F32)
             + jnp.dot(x_hi, wg_s[half:, :], preferred_element_type=F32))
        u = (jnp.dot(x_lo, wu_s[0:half, :], preferred_element_type=F32)
             + jnp.dot(x_hi, wu_s[half:, :], preferred_element_type=F32))
        hb = (g * jax.nn.sigmoid(g)) * u
        out_ref[...] = jnp.dot(hb.astype(BF16), wd_s[...], preferred_element_type=F32)

    @pl.when(valid_ref[j] == 0)
    def _():
        out_ref[...] = jnp.zeros_like(out_ref)


def _expert_call(block_expert, block_new, block_valid, buf, w_gate, w_up, w_down):
    n_rows = buf.shape[0]
    nb = n_rows // MOE_BLK
    wsel = lambda j, be, nw, va: (be[j], 0, 0)
    rows = lambda j, be, nw, va: (j, 0)
    return pl.pallas_call(
        _expert_kernel,
        grid_spec=pltpu.PrefetchScalarGridSpec(
            num_scalar_prefetch=3,
            grid=(nb,),
            in_specs=[
                pl.BlockSpec((MOE_BLK, D_MODEL // 2), rows),
                pl.BlockSpec((1, D_MODEL, EXPERT_FF), wsel),
                pl.BlockSpec((1, D_MODEL, EXPERT_FF), wsel),
                pl.BlockSpec((1, EXPERT_FF, D_MODEL), wsel),
            ],
            out_specs=pl.BlockSpec((MOE_BLK, D_MODEL), rows),
            scratch_shapes=[pltpu.VMEM((D_MODEL, EXPERT_FF), BF16),
                            pltpu.VMEM((D_MODEL, EXPERT_FF), BF16),
                            pltpu.VMEM((EXPERT_FF, D_MODEL), BF16)],
        ),
        out_shape=jax.ShapeDtypeStruct((n_rows, D_MODEL), F32),
        compiler_params=_cparams(("arbitrary",)),
        name="experts",
    )(block_expert, block_new, block_valid, buf, w_gate, w_up, w_down)


def _combine_kernel(d0_ref, d1_ref, d0_next_ref, d1_next_ref, x1_ref, gate_ref, gf_ref, eo_ref, o_ref,
                    rows_ref, sems):
    i = pl.program_id(0)
    tt = x1_ref.shape[0]

    def gather(d_refs, slot):
        def issue(r, c):
            for k, d_ref in enumerate(d_refs):
                pltpu.make_async_copy(eo_ref.at[pl.ds(d_ref[r], 1)], rows_ref.at[slot, k, pl.ds(r, 1)],
                                      sems.at[slot]).start()
            return c

        lax.fori_loop(0, tt, issue, 0, unroll=ISSUE_UNROLL)

    @pl.when(i == 0)
    def _():
        gather((d0_ref, d1_ref), 0)

    @pl.when(i + 1 < pl.num_programs(0))
    def _():
        gather((d0_next_ref, d1_next_ref), (i + 1) % 2)

    slot = i % 2
    for k in range(TOP_K):
        pltpu.make_async_copy(eo_ref.at[pl.ds(0, tt)], rows_ref.at[slot, k], sems.at[slot]).wait()
    gate = gate_ref[...]
    y = rows_ref[slot, 0] * gate[:, 0:1] + rows_ref[slot, 1] * gate[:, 1:2]
    o_ref[...] = _rms(x1_ref[...] + y, gf_ref[...])


def _combine_call(dests, x1, gates, g_final, expert_out):
    t = x1.shape[0]
    tt = ROW_TILE // 2
    last = t // tt - 1
    cur = pl.BlockSpec((tt,), lambda i: (i,), memory_space=pltpu.SMEM)
    nxt = pl.BlockSpec((tt,), lambda i: (jnp.minimum(i + 1, last),), memory_space=pltpu.SMEM)
    return pl.pallas_call(
        _combine_kernel,
        grid=(t // tt,),
        in_specs=[
            cur, cur, nxt, nxt,
            pl.BlockSpec((tt, D_MODEL), lambda i: (i, 0)),
            pl.BlockSpec((tt, TOP_K), lambda i: (i, 0)),
            pl.BlockSpec((1, D_MODEL), lambda i: (0, 0)),
            pl.BlockSpec(memory_space=pl.ANY),
        ],
        out_specs=pl.BlockSpec((tt, D_MODEL), lambda i: (i, 0)),
        out_shape=jax.ShapeDtypeStruct((t, D_MODEL), F32),
        scratch_shapes=[pltpu.VMEM((2, TOP_K, tt, D_MODEL), F32), pltpu.SemaphoreType.DMA((2,))],
        compiler_params=_cparams(("arbitrary",), disable_bounds_checks=True),
        name="combine",
    )(*dests, *dests, x1, gates, g_final, expert_out)


def _rope_tables(seq):
    half = B_QK_ROPE // 2
    inv_freq = ROPE_THETA ** (-(jnp.arange(half, dtype=F32) / half))
    ang = jnp.arange(seq, dtype=F32)[:, None] * inv_freq[None, :]
    cos, sin = jnp.cos(ang), jnp.sin(ang)
    z = jnp.zeros((seq, B_QK_NOPE), F32)
    z2 = jnp.zeros((seq, B_QK_ROPE), F32)
    return (jnp.concatenate([z, cos, cos, z2], axis=1), jnp.concatenate([z, -sin, sin, z2], axis=1))


def _swap_halves(w):
    half = w.shape[-1] // 2
    return jnp.concatenate([w[..., half:], w[..., :half]], axis=-1)


def _layout_weights(w_in, w_q_up, w_kv_up):
    d = w_in.shape[0]
    w_kr = w_in[:, 3 * A_WIDTH + B_Q_LORA + B_KV_LORA:]
    w_in_l = jnp.concatenate(
        [w_in[:, :3 * A_WIDTH + B_Q_LORA + B_KV_LORA], jnp.zeros((d, B_QK_NOPE), F32), w_kr, _swap_halves(w_kr)],
        axis=1).astype(BF16)
    wq = w_q_up.reshape(B_Q_LORA, B_HEADS, B_QK_NOPE + B_QK_ROPE)
    wq_l = jnp.concatenate([wq, _swap_halves(wq[..., B_QK_NOPE:])], axis=-1)
    wq_l = wq_l.reshape(B_Q_LORA, B_HEADS * LANES).astype(BF16)
    wkv = w_kv_up.reshape(B_KV_LORA, B_HEADS, B_QK_NOPE + B_V_DIM)
    wkb = jnp.concatenate([wkv[..., :B_QK_NOPE], jnp.zeros_like(wkv[..., :B_QK_NOPE])], axis=-1)
    wkb = wkb.reshape(B_KV_LORA, B_HEADS * LANES).astype(BF16)
    wvb = wkv[..., B_QK_NOPE:].reshape(B_KV_LORA, B_WIDTH).astype(BF16)
    return w_in_l, wq_l, wkb, wvb


def _block_plan(counts, n_blocks):
    padded = (counts + MOE_BLK - 1) // MOE_BLK * MOE_BLK
    ends = jnp.cumsum(padded)
    starts = ends - padded
    first_row = jnp.arange(n_blocks, dtype=I32) * MOE_BLK
    expert = jnp.minimum(jnp.sum(ends[None, :] <= first_row[:, None], axis=1), N_EXPERTS - 1).astype(I32)
    new = jnp.concatenate([jnp.ones((1,), I32), (expert[1:] != expert[:-1]).astype(I32)])
    valid = (first_row < ends[-1]).astype(I32)
    return starts.astype(I32), expert, new, valid


def kernel(x, g_attn_norm, w_in, rel_bias, g_q_latent, w_q_up, g_kv_latent, w_kv_up, g_out_a, g_out_b, w_out,
           g_ffn_norm, w_router_group, b_router_group, w_router_expert, b_router_expert, w_gate, w_up, w_down,
           g_final):
    batch, seq, d = x.shape
    t = batch * seq
    assert g_attn_norm.shape[0] == 1 and d == D_MODEL and seq % ROW_TILE == 0
    cos_t, sin_t = _rope_tables(seq)
    tri = jnp.triu(jnp.ones((ROW_TILE, ROW_TILE), F32), 1).astype(BF16)
    n_blocks = t * TOP_K // MOE_BLK + N_EXPERTS
    x2 = x.reshape(t, d)
    row = lambda v: v.reshape(1, -1)

    w_in_l, wq_l, wkb_l, wvb_l = _layout_weights(w_in[0], w_q_up[0], w_kv_up[0])
    qa, ka, va, qb, kb, vb = _proj_call(x2, row(g_attn_norm[0]), w_in_l, row(g_q_latent[0]), wq_l,
                                        row(g_kv_latent[0]), wkb_l, wvb_l, cos_t, sin_t, seq)
    oas, lses = [], []
    for window, dilation in DILATED_PATTERNS:
        bias = _dilated_bias(rel_bias, seq, dilation, window // (2 * dilation))
        to_res = lambda a: a.reshape(batch, seq // dilation, dilation, -1).transpose(0, 2, 1, 3)
        from_res = lambda a: a.transpose(0, 2, 1, 3).reshape(t, -1)
        o_p, lse_p = _dilated_call(to_res(qa), to_res(ka), to_res(va), bias, batch, seq, dilation)
        oas.append(from_res(o_p))
        lses.append(from_res(lse_p))
    ob = _mla_call(qb, kb, vb, batch, seq)

    pad = ROUTER_ROWS - N_EXPERTS - N_GROUPS
    w_router = jnp.concatenate([w_router_expert[0], w_router_group[0], jnp.zeros((d, pad), F32)], axis=1).T
    b_router = jnp.concatenate([b_router_expert[0], b_router_group[0], jnp.zeros((pad,), F32)])
    x1, hp, idx, gates, cnt = _mix_call(oas, lses, ob, x2, row(g_out_a[0]), row(g_out_b[0]), w_out[0].astype(BF16),
                                        row(g_ffn_norm[0]), w_router, b_router.reshape(-1, 1), tri)
    pstart, block_expert, block_new, block_valid = _block_plan(cnt[:N_EXPERTS, 0], n_blocks)
    pstart_col = jnp.concatenate([pstart, jnp.zeros((ROUTER_ROWS - N_EXPERTS,), I32)]).reshape(-1, 1)
    dest = _dest_call(idx, pstart_col)
    dests = (dest[0], dest[1])
    buf = _dispatch_call(dests, hp, n_blocks * MOE_BLK)
    expert_out = _expert_call(block_expert, block_new, block_valid, buf, w_gate[0], w_up[0], w_down[0])
    return _combine_call(dests, x1, gates[:TOP_K].T, row(g_final), expert_out).reshape(batch, seq, d)
```

```python
import functools
import math

import numpy as np
import jax
import jax.numpy as jnp
from jax import lax
from jax.experimental import pallas as pl
from jax.experimental.pallas import tpu as pltpu

F32 = jnp.float32
BF16 = jnp.bfloat16
I32 = jnp.int32
U32 = jnp.uint32

D_MODEL = 1024
EPS = 1e-6
NEG_INF = -1e30
LANES = 128

A_HEADS = 8
A_HEAD_DIM = 64
A_WIDTH = 512
DILATED_PATTERNS = ((128, 1), (512, 4), (2048, 16))
REL_BUCKETS = 32
REL_MAX_DISTANCE = 1024
A_QB = 128

B_HEADS = 8
B_Q_LORA = 256
B_KV_LORA = 128
B_QK_NOPE = 64
B_QK_ROPE = 32
B_V_DIM = 64
B_WIDTH = 512
ROPE_THETA = 10000.0
B_SCALE = (B_QK_NOPE + B_QK_ROPE) ** -0.5
B_QB = 512
B_SUB = 256
LOG2E = math.log2(math.e)

N_GROUPS = 4
EXPERTS_PER_GROUP = 8
N_EXPERTS = 32
TOP_K = 2
EXPERT_FF = 256
MOE_BLK = 256
ROUTER_ROWS = 40

ROW_TILE = 512
ISSUE_UNROLL = 8
PROJ_COLS = 2048

_NT = (((1,), (1,)), ((), ()))


def _cparams(semantics, vmem_mb=48, **kw):
    return pltpu.CompilerParams(dimension_semantics=semantics,
                                vmem_limit_bytes=vmem_mb * 1024 * 1024, **kw)


def _rms(x, g):
    return x * lax.rsqrt(jnp.mean(x * x, axis=-1, keepdims=True) + EPS) * g


def _lane_iota(rows=1):
    return lax.broadcasted_iota(I32, (rows, LANES), 1)


def _proj_kernel(x_ref, g_ref, win_ref, gq_ref, wq_ref, gkv_ref, wkb_ref, wvb_ref, cos_ref, sin_ref,
                 qa_ref, ka_ref, va_ref, qb_ref, kb_ref, vb_ref):
    h = _rms(x_ref[...], g_ref[...]).astype(BF16)
    proj = jnp.dot(h, win_ref[...], preferred_element_type=F32)
    lo = _lane_iota() < A_HEAD_DIM
    qa_ref[...] = (proj[:, 0:512] * (A_HEAD_DIM ** -0.5)).astype(BF16)
    for p in range(A_HEADS // 2):
        kp = proj[:, 512 + LANES * p:512 + LANES * (p + 1)]
        ka_ref[:, 2 * LANES * p:2 * LANES * p + LANES] = jnp.where(lo, kp, 0.0).astype(BF16)
        ka_ref[:, 2 * LANES * p + LANES:2 * LANES * (p + 1)] = jnp.where(lo, 0.0, kp).astype(BF16)
    va_ref[...] = proj[:, 1024:1536].astype(BF16)

    cos = cos_ref[...]
    sin = sin_ref[...]
    cq = _rms(proj[:, 1536:1792], gq_ref[...]).astype(BF16)
    q = jnp.dot(cq, wq_ref[...], preferred_element_type=F32)
    q_mul = (cos + jnp.where(lo, 1.0, 0.0)) * (B_SCALE * LOG2E)
    q_rot = sin * (B_SCALE * LOG2E)
    for hd in range(B_HEADS):
        t = q[:, LANES * hd:LANES * (hd + 1)]
        qb_ref[:, LANES * hd:LANES * (hd + 1)] = (t * q_mul + pltpu.roll(t, 96, 1) * q_rot).astype(BF16)

    ckv = _rms(proj[:, 1792:1920], gkv_ref[...]).astype(BF16)
    kr = proj[:, 1920:2048]
    kr = kr * cos + pltpu.roll(kr, 96, 1) * sin
    kn = jnp.dot(ckv, wkb_ref[...], preferred_element_type=F32)
    for hd in range(B_HEADS):
        kb_ref[:, LANES * hd:LANES * (hd + 1)] = (kn[:, LANES * hd:LANES * (hd + 1)] + kr).astype(BF16)
    vb_ref[...] = jnp.dot(ckv, wvb_ref[...], preferred_element_type=F32).astype(BF16)


def _proj_call(x2, g_attn, w_in, g_q, w_q, g_kv, w_kb, w_vb, cos_t, sin_t, seq):
    t = x2.shape[0]
    tm = ROW_TILE
    nseq = seq // tm
    row = lambda i: (i, 0)
    const = lambda i: (0, 0)
    pos = lambda i: (i % nseq, 0)
    out = lambda w: jax.ShapeDtypeStruct((t, w), BF16)
    return pl.pallas_call(
        _proj_kernel,
        grid=(t // tm,),
        in_specs=[
            pl.BlockSpec((tm, D_MODEL), row),
            pl.BlockSpec((1, D_MODEL), const),
            pl.BlockSpec((D_MODEL, PROJ_COLS), const),
            pl.BlockSpec((1, B_Q_LORA), const),
            pl.BlockSpec((B_Q_LORA, B_HEADS * LANES), const),
            pl.BlockSpec((1, B_KV_LORA), const),
            pl.BlockSpec((B_KV_LORA, B_HEADS * LANES), const),
            pl.BlockSpec((B_KV_LORA, B_WIDTH), const),
            pl.BlockSpec((tm, LANES), pos),
            pl.BlockSpec((tm, LANES), pos),
        ],
        out_specs=[
            pl.BlockSpec((tm, A_WIDTH), row),
            pl.BlockSpec((tm, 2 * A_WIDTH), row),
            pl.BlockSpec((tm, A_WIDTH), row),
            pl.BlockSpec((tm, B_HEADS * LANES), row),
            pl.BlockSpec((tm, B_HEADS * LANES), row),
            pl.BlockSpec((tm, B_WIDTH), row),
        ],
        out_shape=[out(A_WIDTH), out(2 * A_WIDTH), out(A_WIDTH),
                   out(B_HEADS * LANES), out(B_HEADS * LANES), out(B_WIDTH)],
        compiler_params=_cparams(("parallel",)),
        name="proj",
    )(x2, g_attn, w_in, g_q, w_q, g_kv, w_kb, w_vb, cos_t, sin_t)


def _dilated_kernel(q_ref, k_ref, v_ref, bias_ref, o_ref, lse_ref, *, seq_len, key_width, group):
    nblk = seq_len // A_QB
    lo = _lane_iota() < A_HEAD_DIM

    def block(it, carry):
        c = it // nblk
        if nblk == 1:
            q0, ks, var = 0, 0, 0
        else:
            n = it % nblk
            q0 = pl.multiple_of(n * A_QB, A_QB)
            ks = pl.multiple_of(jnp.clip(q0 - 64, 0, seq_len - key_width), 64)
            var = jnp.where(n == 0, 0, jnp.where(n == nblk - 1, 2, 1))
        rows = pl.ds(q0, A_QB)
        keys = pl.ds(ks, key_width)
        tiles = [slice(LANES * p, LANES * (p + 1)) for p in range(A_HEADS // 2)]
        scores = [lax.dot_general(q_ref[0, c, rows, tiles[hd // 2]], k_ref[0, c, keys, LANES * hd:LANES * (hd + 1)],
                                  _NT, preferred_element_type=F32) + bias_ref[var, hd] for hd in range(A_HEADS)]
        maxes = [jnp.max(s, axis=-1, keepdims=True) for s in scores]
        probs = [jnp.exp(s - m) for s, m in zip(scores, maxes)]
        dens = [jnp.sum(pr, axis=-1, keepdims=True) for pr in probs]
        pvs = [jnp.dot(pr.astype(BF16), v_ref[0, c, keys, tiles[hd // 2]], preferred_element_type=F32)
               for hd, pr in enumerate(probs)]
        for p, tile in enumerate(tiles):
            h0, h1 = 2 * p, 2 * p + 1
            o_ref[0, c, rows, tile] = jnp.where(lo, pvs[h0] * (1.0 / dens[h0]), pvs[h1] * (1.0 / dens[h1])).astype(BF16)
            lse_ref[0, c, rows, tile] = jnp.where(lo, maxes[h0] + jnp.log(dens[h0]), maxes[h1] + jnp.log(dens[h1]))
        return carry

    lax.fori_loop(0, group * nblk, block, 0)


def _dilated_call(qa, ka, va, bias, batch, seq, dilation):
    r = dilation
    sl = seq // r
    kw = min(2 * A_QB, sl)
    group = max(1, min(r, (4 * A_QB) // sl))
    blk = lambda w: pl.BlockSpec((1, group, sl, w), lambda b, c: (b, c, 0, 0))
    return pl.pallas_call(
        functools.partial(_dilated_kernel, seq_len=sl, key_width=kw, group=group),
        grid=(batch, r // group),
        in_specs=[blk(A_WIDTH), blk(2 * A_WIDTH), blk(A_WIDTH),
                  pl.BlockSpec(bias.shape, lambda b, c: (0, 0, 0, 0))],
        out_specs=[blk(A_WIDTH), blk(A_WIDTH)],
        out_shape=[jax.ShapeDtypeStruct((batch, r, sl, A_WIDTH), BF16),
                   jax.ShapeDtypeStruct((batch, r, sl, A_WIDTH), F32)],
        compiler_params=_cparams(("parallel", "parallel")),
        name=f"dilated_r{r}",
    )(qa, ka, va, bias)


def _t5_bucket(rel):
    half = REL_BUCKETS // 2
    max_exact = half // 2
    n = np.abs(rel)
    large = max_exact + (np.log(np.maximum(n, 1) / max_exact)
                         / math.log(REL_MAX_DISTANCE / max_exact) * (half - max_exact)).astype(np.int32)
    large = np.minimum(large, half - 1)
    return (np.where(rel > 0, half, 0) + np.where(n < max_exact, n, large)).astype(np.int32)


def _dilated_bias(rel_bias, seq, dilation, half_steps):
    sl = seq // dilation
    kw = min(2 * A_QB, sl)
    offsets = [0] if sl == kw else [0, -half_steps, A_QB - kw]
    rel = np.stack([np.arange(kw)[None, :] + off - np.arange(A_QB)[:, None] for off in offsets])
    valid = np.abs(rel) <= half_steps
    bucket = np.where(valid, _t5_bucket(rel * dilation), REL_BUCKETS).astype(np.int32)
    onehot = (jnp.asarray(bucket)[..., None] == jnp.arange(REL_BUCKETS + 1, dtype=I32)).astype(F32)
    table = jnp.concatenate([rel_bias.astype(F32), jnp.full((1, A_HEADS), NEG_INF, F32)], axis=0)
    return jnp.einsum("vqkb,bh->vhqk", onehot, table, precision=lax.Precision.HIGHEST)


def _mla_kernel(q_ref, k_ref, v_ref, o_ref):
    lo = _lane_iota() < B_V_DIM
    sub = B_QB // B_SUB
    tiles = [slice(0, LANES), slice(LANES, 2 * LANES)]

    def block(i, carry):
        r0 = pl.multiple_of(i * B_QB, B_QB)
        rows = [pl.ds(r0 + B_SUB * j, B_SUB) for j in range(sub)]
        units = [(j, half) for j in range(sub) for half in range(2)]
        scores = [lax.dot_general(q_ref[0, rows[j], tiles[half]], k_ref[0, :, tiles[half]], _NT,
                                  preferred_element_type=F32) for j, half in units]
        maxes = [jnp.max(s, axis=-1, keepdims=True) for s in scores]
        probs = [jnp.exp2(s - m) for s, m in zip(scores, maxes)]
        dens = [jnp.sum(pr, axis=-1, keepdims=True) for pr in probs]
        outs = [jnp.dot(pr.astype(BF16), v_ref[0], preferred_element_type=F32) * (1.0 / den)
                for pr, den in zip(probs, dens)]
        for j in range(sub):
            o_ref[0, rows[j], :] = jnp.where(lo, outs[2 * j], outs[2 * j + 1]).astype(BF16)
        return carry

    lax.fori_loop(0, q_ref.shape[1] // B_QB, block, 0)


def _mla_call(qb, kb, vb, batch, seq):
    qb = qb.reshape(batch, seq, B_HEADS * LANES)
    kb = kb.reshape(batch, seq, B_HEADS * LANES)
    vb = vb.reshape(batch, seq, B_WIDTH)
    pair = lambda w: pl.BlockSpec((1, seq, w), lambda b, p: (b, 0, p))
    out = pl.pallas_call(
        _mla_kernel,
        grid=(batch, B_HEADS // 2),
        in_specs=[pair(2 * LANES), pair(2 * LANES), pair(LANES)],
        out_specs=pair(LANES),
        out_shape=jax.ShapeDtypeStruct((batch, seq, B_WIDTH), BF16),
        compiler_params=_cparams(("parallel", "parallel")),
        name="mla",
    )(qb, kb, vb)
    return out.reshape(batch * seq, B_WIDTH)


def _merge_patterns(o_refs, lse_refs):
    lses = [r[...] for r in lse_refs]
    top = functools.reduce(jnp.maximum, lses)
    es = [jnp.exp(l - top) for l in lses]
    inv = 1.0 / functools.reduce(jnp.add, es)
    return functools.reduce(jnp.add, [e * inv * r[...].astype(F32) for e, r in zip(es, o_refs)])


def _mix_kernel(o1_ref, o4_ref, o16_ref, l1_ref, l4_ref, l16_ref, ob_ref, x_ref, ga_ref, gb_ref, wo_ref,
                gf_ref, wr_ref, br_ref, tri_ref, x1_ref, hp_ref, idx_ref, gate_ref, cnt_ref, carry_ref):
    i = pl.program_id(0)

    @pl.when(i == 0)
    def _():
        carry_ref[...] = jnp.zeros_like(carry_ref)

    oa = _merge_patterns((o1_ref, o4_ref, o16_ref), (l1_ref, l4_ref, l16_ref))
    a = _rms(oa, ga_ref[...]).astype(BF16)
    b = _rms(ob_ref[...].astype(F32), gb_ref[...]).astype(BF16)
    mix = (jnp.dot(a, wo_ref[0:A_WIDTH, :], preferred_element_type=F32)
           + jnp.dot(b, wo_ref[A_WIDTH:, :], preferred_element_type=F32))
    x1 = x_ref[...] + mix
    x1_ref[...] = x1
    h2 = _rms(x1, gf_ref[...])
    hb = h2.astype(BF16).astype(F32)
    half = D_MODEL // 2
    hp_ref[...] = ((lax.bitcast_convert_type(hb[:, :half], U32) >> 16)
                   | (lax.bitcast_convert_type(hb[:, half:], U32) & jnp.uint32(0xFFFF0000)))

    lg = lax.dot_general(wr_ref[...], h2, _NT, preferred_element_type=F32,
                         precision=lax.Precision.HIGHEST) + br_ref[...]
    row = lax.broadcasted_iota(I32, lg.shape, 0)
    is_g = (row >= N_EXPERTS) & (row < N_EXPERTS + N_GROUPS)
    gl = jnp.where(is_g, lg, NEG_INF)
    ge = jnp.exp(gl - jnp.max(gl, axis=0, keepdims=True))
    gp = ge / jnp.sum(ge, axis=0, keepdims=True)
    g_gate = jnp.max(gp, axis=0, keepdims=True)
    g_idx = jnp.min(jnp.where(is_g & (gp == g_gate), row - N_EXPERTS, LANES), axis=0, keepdims=True)
    sel = (row >> 3) == g_idx
    el = jnp.where(sel, lg, NEG_INF)
    ee = jnp.exp(el - jnp.max(el, axis=0, keepdims=True))
    ep = jnp.where(sel, ee / jnp.sum(ee, axis=0, keepdims=True), -1.0)
    p1 = jnp.max(ep, axis=0, keepdims=True)
    i1 = jnp.min(jnp.where(ep == p1, row, LANES), axis=0, keepdims=True)
    ep2 = jnp.where(row == i1, -1.0, ep)
    p2 = jnp.max(ep2, axis=0, keepdims=True)
    i2 = jnp.min(jnp.where(sel & (ep2 == p2) & (row != i1), row, LANES), axis=0, keepdims=True)
    den = p1 + p2
    g1 = g_gate * p1 / den
    g2 = g_gate * p2 / den

    hit1 = row == i1
    hit2 = row == i2
    onehot = jnp.where(hit1 | hit2, 1.0, 0.0)
    before = jnp.dot(onehot.astype(BF16), tri_ref[...], preferred_element_type=F32) + carry_ref[...]
    r1 = jnp.sum(jnp.where(hit1, before, 0.0), axis=0, keepdims=True).astype(I32)
    r2 = jnp.sum(jnp.where(hit2, before, 0.0), axis=0, keepdims=True).astype(I32)
    carry_ref[...] += jnp.sum(onehot, axis=1, keepdims=True)

    row8 = lax.broadcasted_iota(I32, idx_ref.shape, 0)
    idx_ref[...] = jnp.where(row8 == 0, i1, jnp.where(row8 == 1, i2,
                             jnp.where(row8 == 2, r1, jnp.where(row8 == 3, r2, 0))))
    gate_ref[...] = jnp.where(row8 == 0, g1, jnp.where(row8 == 1, g2, 0.0))

    @pl.when(i == pl.num_programs(0) - 1)
    def _():
        cnt_ref[...] = jnp.broadcast_to(carry_ref[...], cnt_ref.shape).astype(I32)


def _mix_call(oas, lses, ob, x2, g_a, g_b, w_out, g_ffn, w_router, b_router, tri):
    t = x2.shape[0]
    tm = ROW_TILE
    row = lambda i: (i, 0)
    const = lambda i: (0, 0)
    return pl.pallas_call(
        _mix_kernel,
        grid=(t // tm,),
        in_specs=[
            pl.BlockSpec((tm, A_WIDTH), row),
            pl.BlockSpec((tm, A_WIDTH), row),
            pl.BlockSpec((tm, A_WIDTH), row),
            pl.BlockSpec((tm, A_WIDTH), row),
            pl.BlockSpec((tm, A_WIDTH), row),
            pl.BlockSpec((tm, A_WIDTH), row),
            pl.BlockSpec((tm, B_WIDTH), row),
            pl.BlockSpec((tm, D_MODEL), row),
            pl.BlockSpec((1, A_WIDTH), const),
            pl.BlockSpec((1, B_WIDTH), const),
            pl.BlockSpec((D_MODEL, D_MODEL), const),
            pl.BlockSpec((1, D_MODEL), const),
            pl.BlockSpec((ROUTER_ROWS, D_MODEL), const),
            pl.BlockSpec((ROUTER_ROWS, 1), const),
            pl.BlockSpec((tm, tm), const),
        ],
        out_specs=[
            pl.BlockSpec((tm, D_MODEL), row),
            pl.BlockSpec((tm, D_MODEL // 2), row),
            pl.BlockSpec((8, tm), lambda i: (0, i)),
            pl.BlockSpec((8, tm), lambda i: (0, i)),
            pl.BlockSpec((ROUTER_ROWS, LANES), const),
        ],
        out_shape=[
            jax.ShapeDtypeStruct((t, D_MODEL), F32),
            jax.ShapeDtypeStruct((t, D_MODEL // 2), U32),
            jax.ShapeDtypeStruct((8, t), I32),
            jax.ShapeDtypeStruct((8, t), F32),
            jax.ShapeDtypeStruct((ROUTER_ROWS, LANES), I32),
        ],
        scratch_shapes=[pltpu.VMEM((ROUTER_ROWS, 1), F32)],
        compiler_params=_cparams(("arbitrary",)),
        name="mix_router",
    )(*oas, *lses, ob, x2, g_a, g_b, w_out, g_ffn, w_router, b_router, tri)


def _dest_kernel(idx_ref, pstart_ref, dest_ref):
    idx = idx_ref[...]
    row = lax.broadcasted_iota(I32, (ROUTER_ROWS, idx.shape[1]), 0)
    ps = pstart_ref[...]

    def slot(k):
        return jnp.sum(jnp.where(row == idx[k:k + 1, :], ps, 0), axis=0, keepdims=True) + idx[2 + k:3 + k, :]

    row8 = lax.broadcasted_iota(I32, idx.shape, 0)
    dest_ref[...] = jnp.where(row8 == 0, slot(0), jnp.where(row8 == 1, slot(1), 0))


def _dest_call(idx, pstart):
    t = idx.shape[1]
    tm = 4 * ROW_TILE
    return pl.pallas_call(
        _dest_kernel,
        grid=(t // tm,),
        in_specs=[pl.BlockSpec((8, tm), lambda i: (0, i)), pl.BlockSpec((ROUTER_ROWS, 1), lambda i: (0, 0))],
        out_specs=pl.BlockSpec((8, tm), lambda i: (0, i)),
        out_shape=jax.ShapeDtypeStruct((8, t), I32),
        compiler_params=_cparams(("parallel",)),
        name="dest_rows",
    )(idx, pstart)


def _dispatch_kernel(d0_ref, d1_ref, h_ref, buf_in, buf_out, sem):
    del buf_in
    tt = h_ref.shape[0]

    def issue(i, c):
        for d_ref in (d0_ref, d1_ref):
            pltpu.make_async_copy(h_ref.at[pl.ds(i, 1)], buf_out.at[pl.ds(d_ref[i], 1)], sem).start()
        return c

    lax.fori_loop(0, tt, issue, 0, unroll=ISSUE_UNROLL)
    for k in range(TOP_K):
        pltpu.make_async_copy(h_ref, buf_out.at[pl.ds(0, tt)], sem).wait()


def _dispatch_call(dests, hp, n_rows):
    t = hp.shape[0]
    tt = 2 * ROW_TILE
    buf0 = jnp.zeros((n_rows, D_MODEL // 2), U32)
    return pl.pallas_call(
        _dispatch_kernel,
        grid=(t // tt,),
        in_specs=[
            pl.BlockSpec((tt,), lambda i: (i,), memory_space=pltpu.SMEM),
            pl.BlockSpec((tt,), lambda i: (i,), memory_space=pltpu.SMEM),
            pl.BlockSpec((tt, D_MODEL // 2), lambda i: (i, 0)),
            pl.BlockSpec(memory_space=pl.ANY),
        ],
        out_specs=pl.BlockSpec(memory_space=pl.ANY),
        out_shape=jax.ShapeDtypeStruct((n_rows, D_MODEL // 2), U32),
        scratch_shapes=[pltpu.SemaphoreType.DMA(())],
        input_output_aliases={3: 0},
        compiler_params=_cparams(("arbitrary",), disable_bounds_checks=True, has_side_effects=True),
        name="dispatch",
    )(*dests, hp, buf0)


def _expert_kernel(be_ref, new_ref, valid_ref, buf_ref, wg_ref, wu_ref, wd_ref, out_ref, wg_s, wu_s, wd_s):
    j = pl.program_id(0)
    del be_ref

    @pl.when(new_ref[j] == 1)
    def _():
        wg_s[...] = wg_ref[0].astype(BF16)
        wu_s[...] = wu_ref[0].astype(BF16)
        wd_s[...] = wd_ref[0].astype(BF16)

    @pl.when(valid_ref[j] == 1)
    def _():
        w = buf_ref[...]
        half = D_MODEL // 2
        x_lo = lax.bitcast_convert_type(w << 16, F32).astype(BF16)
        x_hi = lax.bitcast_convert_type(w & jnp.uint32(0xFFFF0000), F32).astype(BF16)
        g = (jnp.dot(x_lo, wg_s[0:half, :], preferred_element_type=F32)
             + jnp.dot(x_hi, wg_s[half:, :], preferred_element_type=F32))
        u = (jnp.dot(x_lo, wu_s[0:half, :], preferred_element_type=F32)
             + jnp.dot(x_hi, wu_s[half:, :], preferred_element_type=F32))
        hb = (g * jax.nn.sigmoid(g)) * u
        out_ref[...] = jnp.dot(hb.astype(BF16), wd_s[...], preferred_element_type=F32)

    @pl.when(valid_ref[j] == 0)
    def _():
        out_ref[...] = jnp.zeros_like(out_ref)


def _expert_call(block_expert, block_new, block_valid, buf, w_gate, w_up, w_down):
    n_rows = buf.shape[0]
    nb = n_rows // MOE_BLK
    wsel = lambda j, be, nw, va: (be[j], 0, 0)
    rows = lambda j, be, nw, va: (j, 0)
    return pl.pallas_call(
        _expert_kernel,
        grid_spec=pltpu.PrefetchScalarGridSpec(
            num_scalar_prefetch=3,
            grid=(nb,),
            in_specs=[
                pl.BlockSpec((MOE_BLK, D_MODEL // 2), rows),
                pl.BlockSpec((1, D_MODEL, EXPERT_FF), wsel),
                pl.BlockSpec((1, D_MODEL, EXPERT_FF), wsel),
                pl.BlockSpec((1, EXPERT_FF, D_MODEL), wsel),
            ],
            out_specs=pl.BlockSpec((MOE_BLK, D_MODEL), rows),
            scratch_shapes=[pltpu.VMEM((D_MODEL, EXPERT_FF), BF16),
                            pltpu.VMEM((D_MODEL, EXPERT_FF), BF16),
                            pltpu.VMEM((EXPERT_FF, D_MODEL), BF16)],
        ),
        out_shape=jax.ShapeDtypeStruct((n_rows, D_MODEL), F32),
        compiler_params=_cparams(("arbitrary",)),
        name="experts",
    )(block_expert, block_new, block_valid, buf, w_gate, w_up, w_down)


def _combine_kernel(d0_ref, d1_ref, d0_next_ref, d1_next_ref, x1_ref, gate_ref, gf_ref, eo_ref, o_ref,
                    rows_ref, sems):
    i = pl.program_id(0)
    tt = x1_ref.shape[0]

    def gather(d_refs, slot):
        def issue(r, c):
            for k, d_ref in enumerate(d_refs):
                pltpu.make_async_copy(eo_ref.at[pl.ds(d_ref[r], 1)], rows_ref.at[slot, k, pl.ds(r, 1)],
                                      sems.at[slot]).start()
            return c

        lax.fori_loop(0, tt, issue, 0, unroll=ISSUE_UNROLL)

    @pl.when(i == 0)
    def _():
        gather((d0_ref, d1_ref), 0)

    @pl.when(i + 1 < pl.num_programs(0))
    def _():
        gather((d0_next_ref, d1_next_ref), (i + 1) % 2)

    slot = i % 2
    for k in range(TOP_K):
        pltpu.make_async_copy(eo_ref.at[pl.ds(0, tt)], rows_ref.at[slot, k], sems.at[slot]).wait()
    gate = gate_ref[...]
    y = rows_ref[slot, 0] * gate[:, 0:1] + rows_ref[slot, 1] * gate[:, 1:2]
    o_ref[...] = _rms(x1_ref[...] + y, gf_ref[...])


def _combine_call(dests, x1, gates, g_final, expert_out):
    t = x1.shape[0]
    tt = ROW_TILE // 2
    last = t // tt - 1
    cur = pl.BlockSpec((tt,), lambda i: (i,), memory_space=pltpu.SMEM)
    nxt = pl.BlockSpec((tt,), lambda i: (jnp.minimum(i + 1, last),), memory_space=pltpu.SMEM)
    return pl.pallas_call(
        _combine_kernel,
        grid=(t // tt,),
        in_specs=[
            cur, cur, nxt, nxt,
            pl.BlockSpec((tt, D_MODEL), lambda i: (i, 0)),
            pl.BlockSpec((tt, TOP_K), lambda i: (i, 0)),
            pl.BlockSpec((1, D_MODEL), lambda i: (0, 0)),
            pl.BlockSpec(memory_space=pl.ANY),
        ],
        out_specs=pl.BlockSpec((tt, D_MODEL), lambda i: (i, 0)),
        out_shape=jax.ShapeDtypeStruct((t, D_MODEL), F32),
        scratch_shapes=[pltpu.VMEM((2, TOP_K, tt, D_MODEL), F32), pltpu.SemaphoreType.DMA((2,))],
        compiler_params=_cparams(("arbitrary",), disable_bounds_checks=True),
        name="combine",
    )(*dests, *dests, x1, gates, g_final, expert_out)


def _rope_tables(seq):
    half = B_QK_ROPE // 2
    inv_freq = ROPE_THETA ** (-(jnp.arange(half, dtype=F32) / half))
    ang = jnp.arange(seq, dtype=F32)[:, None] * inv_freq[None, :]
    cos, sin = jnp.cos(ang), jnp.sin(ang)
    z = jnp.zeros((seq, B_QK_NOPE), F32)
    z2 = jnp.zeros((seq, B_QK_ROPE), F32)
    return (jnp.concatenate([z, cos, cos, z2], axis=1), jnp.concatenate([z, -sin, sin, z2], axis=1))


def _swap_halves(w):
    half = w.shape[-1] // 2
    return jnp.concatenate([w[..., half:], w[..., :half]], axis=-1)


def _layout_weights(w_in, w_q_up, w_kv_up):
    d = w_in.shape[0]
    w_kr = w_in[:, 3 * A_WIDTH + B_Q_LORA + B_KV_LORA:]
    w_in_l = jnp.concatenate(
        [w_in[:, :3 * A_WIDTH + B_Q_LORA + B_KV_LORA], jnp.zeros((d, B_QK_NOPE), F32), w_kr, _swap_halves(w_kr)],
        axis=1).astype(BF16)
    wq = w_q_up.reshape(B_Q_LORA, B_HEADS, B_QK_NOPE + B_QK_ROPE)
    wq_l = jnp.concatenate([wq, _swap_halves(wq[..., B_QK_NOPE:])], axis=-1)
    wq_l = wq_l.reshape(B_Q_LORA, B_HEADS * LANES).astype(BF16)
    wkv = w_kv_up.reshape(B_KV_LORA, B_HEADS, B_QK_NOPE + B_V_DIM)
    wkb = jnp.concatenate([wkv[..., :B_QK_NOPE], jnp.zeros_like(wkv[..., :B_QK_NOPE])], axis=-1)
    wkb = wkb.reshape(B_KV_LORA, B_HEADS * LANES).astype(BF16)
    wvb = wkv[..., B_QK_NOPE:].reshape(B_KV_LORA, B_WIDTH).astype(BF16)
    return w_in_l, wq_l, wkb, wvb


def _block_plan(counts, n_blocks):
    padded = (counts + MOE_BLK - 1) // MOE_BLK * MOE_BLK
    ends = jnp.cumsum(padded)
    starts = ends - padded
    first_row = jnp.arange(n_blocks, dtype=I32) * MOE_BLK
    expert = jnp.minimum(jnp.sum(ends[None, :] <= first_row[:, None], axis=1), N_EXPERTS - 1).astype(I32)
    new = jnp.concatenate([jnp.ones((1,), I32), (expert[1:] != expert[:-1]).astype(I32)])
    valid = (first_row < ends[-1]).astype(I32)
    return starts.astype(I32), expert, new, valid


def kernel(x, g_attn_norm, w_in, rel_bias, g_q_latent, w_q_up, g_kv_latent, w_kv_up, g_out_a, g_out_b, w_out,
           g_ffn_norm, w_router_group, b_router_group, w_router_expert, b_router_expert, w_gate, w_up, w_down,
           g_final):
    batch, seq, d = x.shape
    t = batch * seq
    assert g_attn_norm.shape[0] == 1 and d == D_MODEL and seq % ROW_TILE == 0
    cos_t, sin_t = _rope_tables(seq)
    tri = jnp.triu(jnp.ones((ROW_TILE, ROW_TILE), F32), 1).astype(BF16)
    n_blocks = t * TOP_K // MOE_BLK + N_EXPERTS
    x2 = x.reshape(t, d)
    row = lambda v: v.reshape(1, -1)

    w_in_l, wq_l, wkb_l, wvb_l = _layout_weights(w_in[0], w_q_up[0], w_kv_up[0])
    qa, ka, va, qb, kb, vb = _proj_call(x2, row(g_attn_norm[0]), w_in_l, row(g_q_latent[0]), wq_l,
                                        row(g_kv_latent[0]), wkb_l, wvb_l, cos_t, sin_t, seq)
    oas, lses = [], []
    for window, dilation in DILATED_PATTERNS:
        bias = _dilated_bias(rel_bias, seq, dilation, window // (2 * dilation))
        to_res = lambda a: a.reshape(batch, seq // dilation, dilation, -1).transpose(0, 2, 1, 3)
        from_res = lambda a: a.transpose(0, 2, 1, 3).reshape(t, -1)
        o_p, lse_p = _dilated_call(to_res(qa), to_res(ka), to_res(va), bias, batch, seq, dilation)
        oas.append(from_res(o_p))
        lses.append(from_res(lse_p))
    ob = _mla_call(qb, kb, vb, batch, seq)

    pad = ROUTER_ROWS - N_EXPERTS - N_GROUPS
    w_router = jnp.concatenate([w_router_expert[0], w_router_group[0], jnp.zeros((d, pad), F32)], axis=1).T
    b_router = jnp.concatenate([b_router_expert[0], b_router_group[0], jnp.zeros((pad,), F32)])
    x1, hp, idx, gates, cnt = _mix_call(oas, lses, ob, x2, row(g_out_a[0]), row(g_out_b[0]), w_out[0].astype(BF16),
                                        row(g_ffn_norm[0]), w_router, b_router.reshape(-1, 1), tri)
    pstart, block_expert, block_new, block_valid = _block_plan(cnt[:N_EXPERTS, 0], n_blocks)
    pstart_col = jnp.concatenate([pstart, jnp.zeros((ROUTER_ROWS - N_EXPERTS,), I32)]).reshape(-1, 1)
    dest = _dest_call(idx, pstart_col)
    dests = (dest[0], dest[1])
    buf = _dispatch_call(dests, hp, n_blocks * MOE_BLK)
    expert_out = _expert_call(block_expert, block_new, block_valid, buf, w_gate[0], w_up[0], w_down[0])
    return _combine_call(dests, x1, gates[:TOP_K].T, row(g_final), expert_out).reshape(batch, seq, d)
```

```python
import functools
import math

import numpy as np
import jax
import jax.numpy as jnp
from jax import lax
from jax.experimental import pallas as pl
from jax.experimental.pallas import tpu as pltpu

F32 = jnp.float32
BF16 = jnp.bfloat16
I32 = jnp.int32
U32 = jnp.uint32

D_MODEL = 1024
EPS = 1e-6
NEG_INF = -1e30
LANES = 128

A_HEADS = 8
A_HEAD_DIM = 64
A_WIDTH = 512
DILATED_PATTERNS = ((128, 1), (512, 4), (2048, 16))
REL_BUCKETS = 32
REL_MAX_DISTANCE = 1024
A_QB = 128

B_HEADS = 8
B_Q_LORA = 256
B_KV_LORA = 128
B_QK_NOPE = 64
B_QK_ROPE = 32
B_V_DIM = 64
B_WIDTH = 512
ROPE_THETA = 10000.0
B_SCALE = (B_QK_NOPE + B_QK_ROPE) ** -0.5
B_QB = 512
B_SUB = 256
LOG2E = math.log2(math.e)

N_GROUPS = 4
EXPERTS_PER_GROUP = 8
N_EXPERTS = 32
TOP_K = 2
EXPERT_FF = 256
MOE_BLK = 256
ROUTER_ROWS = 40

ROW_TILE = 512
ISSUE_UNROLL = 8
PROJ_COLS = 2048

_NT = (((1,), (1,)), ((), ()))


def _cparams(semantics, vmem_mb=48, **kw):
    return pltpu.CompilerParams(dimension_semantics=semantics,
                                vmem_limit_bytes=vmem_mb * 1024 * 1024, **kw)


def _rms(x, g):
    return x * lax.rsqrt(jnp.mean(x * x, axis=-1, keepdims=True) + EPS) * g


def _lane_iota(rows=1):
    return lax.broadcasted_iota(I32, (rows, LANES), 1)


def _proj_kernel(x_ref, g_ref, win_ref, gq_ref, wq_ref, gkv_ref, wkb_ref, wvb_ref, cos_ref, sin_ref, *refs):
    a_refs = refs[:3 * len(DILATED_PATTERNS)]
    qb_ref, kb_ref, vb_ref, slab_ref = refs[len(a_refs):]
    tm = x_ref.shape[0]
    h = _rms(x_ref[...], g_ref[...]).astype(BF16)
    proj = jnp.dot(h, win_ref[...], preferred_element_type=F32)
    lo = _lane_iota() < A_HEAD_DIM
    pairs = A_HEADS // 2
    for s in range(3 * pairs):
        col = proj[:, LANES * s:LANES * (s + 1)]
        slab_ref[s] = col * (A_HEAD_DIM ** -0.5) if s < pairs else col
    for pi, (_, r) in enumerate(DILATED_PATTERNS):
        q_out, k_out, v_out = a_refs[3 * pi:3 * pi + 3]
        for c in range(r):
            rows = pl.ds(c, tm // r, stride=r) if r > 1 else pl.ds(0, tm)
            for p in range(pairs):
                tile = slice(LANES * p, LANES * (p + 1))
                q_out[0, c, :, tile] = slab_ref[p, rows, :].astype(BF16)
                kp = slab_ref[pairs + p, rows, :]
                k_out[0, c, :, 2 * LANES * p:2 * LANES * p + LANES] = jnp.where(lo, kp, 0.0).astype(BF16)
                k_out[0, c, :, 2 * LANES * p + LANES:2 * LANES * (p + 1)] = jnp.where(lo, 0.0, kp).astype(BF16)
                v_out[0, c, :, tile] = slab_ref[2 * pairs + p, rows, :].astype(BF16)

    cos = cos_ref[...]
    sin = sin_ref[...]
    cq = _rms(proj[:, 1536:1792], gq_ref[...]).astype(BF16)
    q = jnp.dot(cq, wq_ref[...], preferred_element_type=F32)
    q_mul = (cos + jnp.where(lo, 1.0, 0.0)) * (B_SCALE * LOG2E)
    q_rot = sin * (B_SCALE * LOG2E)
    for hd in range(B_HEADS):
        t = q[:, LANES * hd:LANES * (hd + 1)]
        qb_ref[:, LANES * hd:LANES * (hd + 1)] = (t * q_mul + pltpu.roll(t, 96, 1) * q_rot).astype(BF16)

    ckv = _rms(proj[:, 1792:1920], gkv_ref[...]).astype(BF16)
    kr = proj[:, 1920:2048]
    kr = kr * cos + pltpu.roll(kr, 96, 1) * sin
    kn = jnp.dot(ckv, wkb_ref[...], preferred_element_type=F32)
    for hd in range(B_HEADS):
        kb_ref[:, LANES * hd:LANES * (hd + 1)] = (kn[:, LANES * hd:LANES * (hd + 1)] + kr).astype(BF16)
    vb_ref[...] = jnp.dot(ckv, wvb_ref[...], preferred_element_type=F32).astype(BF16)


def _proj_call(x2, g_attn, w_in, g_q, w_q, g_kv, w_kb, w_vb, cos_t, sin_t, seq):
    t = x2.shape[0]
    tm = ROW_TILE
    nseq = seq // tm
    row = lambda i: (i, 0)
    const = lambda i: (0, 0)
    pos = lambda i: (i % nseq, 0)
    out = lambda w: jax.ShapeDtypeStruct((t, w), BF16)
    a_specs, a_shapes = [], []
    for _, r in DILATED_PATTERNS:
        for w in (A_WIDTH, 2 * A_WIDTH, A_WIDTH):
            a_specs.append(pl.BlockSpec((1, r, tm // r, w), lambda i: (i // nseq, 0, i % nseq, 0)))
            a_shapes.append(jax.ShapeDtypeStruct((t // seq, r, seq // r, w), BF16))
    return pl.pallas_call(
        _proj_kernel,
        grid=(t // tm,),
        in_specs=[
            pl.BlockSpec((tm, D_MODEL), row),
            pl.BlockSpec((1, D_MODEL), const),
            pl.BlockSpec((D_MODEL, PROJ_COLS), const),
            pl.BlockSpec((1, B_Q_LORA), const),
            pl.BlockSpec((B_Q_LORA, B_HEADS * LANES), const),
            pl.BlockSpec((1, B_KV_LORA), const),
            pl.BlockSpec((B_KV_LORA, B_HEADS * LANES), const),
            pl.BlockSpec((B_KV_LORA, B_WIDTH), const),
            pl.BlockSpec((tm, LANES), pos),
            pl.BlockSpec((tm, LANES), pos),
        ],
        out_specs=a_specs + [
            pl.BlockSpec((tm, B_HEADS * LANES), row),
            pl.BlockSpec((tm, B_HEADS * LANES), row),
            pl.BlockSpec((tm, B_WIDTH), row),
        ],
        out_shape=a_shapes + [out(B_HEADS * LANES), out(B_HEADS * LANES), out(B_WIDTH)],
        scratch_shapes=[pltpu.VMEM((3 * A_HEADS // 2, tm, LANES), F32)],
        compiler_params=_cparams(("parallel",)),
        name="proj",
    )(x2, g_attn, w_in, g_q, w_q, g_kv, w_kb, w_vb, cos_t, sin_t)


def _dilated_kernel(q_ref, k_ref, v_ref, bias_ref, o_ref, lse_ref, *, seq_len, dilation, key_width, group):
    nblk = seq_len // A_QB
    lo = _lane_iota() < A_HEAD_DIM
    first_class = pl.program_id(1) * group

    def block(it, carry):
        c = it // nblk
        if nblk == 1:
            q0, ks, var = 0, 0, 0
        else:
            n = it % nblk
            q0 = pl.multiple_of(n * A_QB, A_QB)
            ks = pl.multiple_of(jnp.clip(q0 - 64, 0, seq_len - key_width), 64)
            var = jnp.where(n == 0, 0, jnp.where(n == nblk - 1, 2, 1))
        rows = pl.ds(q0, A_QB)
        keys = pl.ds(ks, key_width)
        if dilation == 1:
            out_rows = rows
        else:
            out_rows = pl.ds(first_class + c + dilation * q0, A_QB, stride=dilation)
        tiles = [slice(LANES * p, LANES * (p + 1)) for p in range(A_HEADS // 2)]
        scores = [lax.dot_general(q_ref[0, c, rows, tiles[hd // 2]], k_ref[0, c, keys, LANES * hd:LANES * (hd + 1)],
                                  _NT, preferred_element_type=F32) + bias_ref[var, hd] for hd in range(A_HEADS)]
        maxes = [jnp.max(s, axis=-1, keepdims=True) for s in scores]
        probs = [jnp.exp(s - m) for s, m in zip(scores, maxes)]
        dens = [jnp.sum(pr, axis=-1, keepdims=True) for pr in probs]
        pvs = [jnp.dot(pr.astype(BF16), v_ref[0, c, keys, tiles[hd // 2]], preferred_element_type=F32)
               for hd, pr in enumerate(probs)]
        for p in range(A_HEADS // 2):
            h0, h1 = 2 * p, 2 * p + 1
            o_ref[0, p, out_rows, :] = jnp.where(lo, pvs[h0] * (1.0 / dens[h0]), pvs[h1] * (1.0 / dens[h1]))
            lse_ref[0, p, out_rows, :] = jnp.where(lo, maxes[h0] + jnp.log(dens[h0]), maxes[h1] + jnp.log(dens[h1]))
        return carry

    lax.fori_loop(0, group * nblk, block, 0)


def _dilated_call(qa, ka, va, bias, batch, seq, dilation):
    r = dilation
    sl = seq // r
    kw = min(2 * A_QB, sl)
    group = max(1, min(r, (4 * A_QB) // sl))
    blk = lambda w: pl.BlockSpec((1, group, sl, w), lambda b, c: (b, c, 0, 0))
    nat = pl.BlockSpec((1, A_HEADS // 2, seq, LANES), lambda b, c: (b, 0, 0, 0))
    return pl.pallas_call(
        functools.partial(_dilated_kernel, seq_len=sl, dilation=r, key_width=kw, group=group),
        grid=(batch, r // group),
        in_specs=[blk(A_WIDTH), blk(2 * A_WIDTH), blk(A_WIDTH),
                  pl.BlockSpec(bias.shape, lambda b, c: (0, 0, 0, 0))],
        out_specs=[nat, nat],
        out_shape=[jax.ShapeDtypeStruct((batch, A_HEADS // 2, seq, LANES), F32)] * 2,
        compiler_params=_cparams(("parallel", "arbitrary")),
        name=f"dilated_r{r}",
    )(qa, ka, va, bias)


def _t5_bucket(rel):
    half = REL_BUCKETS // 2
    max_exact = half // 2
    n = np.abs(rel)
    large = max_exact + (np.log(np.maximum(n, 1) / max_exact)
                         / math.log(REL_MAX_DISTANCE / max_exact) * (half - max_exact)).astype(np.int32)
    large = np.minimum(large, half - 1)
    return (np.where(rel > 0, half, 0) + np.where(n < max_exact, n, large)).astype(np.int32)


def _dilated_bias(rel_bias, seq, dilation, half_steps):
    sl = seq // dilation
    kw = min(2 * A_QB, sl)
    offsets = [0] if sl == kw else [0, -half_steps, A_QB - kw]
    rel = np.stack([np.arange(kw)[None, :] + off - np.arange(A_QB)[:, None] for off in offsets])
    valid = np.abs(rel) <= half_steps
    bucket = np.where(valid, _t5_bucket(rel * dilation), REL_BUCKETS).astype(np.int32)
    onehot = (jnp.asarray(bucket)[..., None] == jnp.arange(REL_BUCKETS + 1, dtype=I32)).astype(F32)
    table = jnp.concatenate([rel_bias.astype(F32), jnp.full((1, A_HEADS), NEG_INF, F32)], axis=0)
    return jnp.einsum("vqkb,bh->vhqk", onehot, table, precision=lax.Precision.HIGHEST)


def _mla_kernel(q_ref, k_ref, v_ref, o_ref):
    lo = _lane_iota() < B_V_DIM
    sub = B_QB // B_SUB
    tiles = [slice(0, LANES), slice(LANES, 2 * LANES)]

    def block(i, carry):
        r0 = pl.multiple_of(i * B_QB, B_QB)
        rows = [pl.ds(r0 + B_SUB * j, B_SUB) for j in range(sub)]
        units = [(j, half) for j in range(sub) for half in range(2)]
        scores = [lax.dot_general(q_ref[0, rows[j], tiles[half]], k_ref[0, :, tiles[half]], _NT,
                                  preferred_element_type=F32) for j, half in units]
        maxes = [jnp.max(s, axis=-1, keepdims=True) for s in scores]
        probs = [jnp.exp2(s - m) for s, m in zip(scores, maxes)]
        dens = [jnp.sum(pr, axis=-1, keepdims=True) for pr in probs]
        outs = [jnp.dot(pr.astype(BF16), v_ref[0], preferred_element_type=F32) * (1.0 / den)
                for pr, den in zip(probs, dens)]
        for j in range(sub):
            o_ref[0, rows[j], :] = jnp.where(lo, outs[2 * j], outs[2 * j + 1]).astype(BF16)
        return carry

    lax.fori_loop(0, q_ref.shape[1] // B_QB, block, 0)


def _mla_call(qb, kb, vb, batch, seq):
    qb = qb.reshape(batch, seq, B_HEADS * LANES)
    kb = kb.reshape(batch, seq, B_HEADS * LANES)
    vb = vb.reshape(batch, seq, B_WIDTH)
    pair = lambda w: pl.BlockSpec((1, seq, w), lambda b, p: (b, 0, p))
    out = pl.pallas_call(
        _mla_kernel,
        grid=(batch, B_HEADS // 2),
        in_specs=[pair(2 * LANES), pair(2 * LANES), pair(LANES)],
        out_specs=pair(LANES),
        out_shape=jax.ShapeDtypeStruct((batch, seq, B_WIDTH), BF16),
        compiler_params=_cparams(("parallel", "parallel")),
        name="mla",
    )(qb, kb, vb)
    return out.reshape(batch * seq, B_WIDTH)


def _merge_patterns(o_refs, lse_refs):
    tiles = []
    for p in range(A_HEADS // 2):
        lses = [r[0, p] for r in lse_refs]
        top = functools.reduce(jnp.maximum, lses)
        es = [jnp.exp(l - top) for l in lses]
        inv = 1.0 / functools.reduce(jnp.add, es)
        tiles.append(functools.reduce(jnp.add, [e * inv * r[0, p] for e, r in zip(es, o_refs)]))
    return jnp.concatenate(tiles, axis=1)


def _mix_kernel(o1_ref, o4_ref, o16_ref, l1_ref, l4_ref, l16_ref, ob_ref, x_ref, ga_ref, gb_ref, wo_ref,
                gf_ref, wr_ref, br_ref, tri_ref, x1_ref, hp_ref, idx_ref, gate_ref, cnt_ref, carry_ref):
    i = pl.program_id(0)

    @pl.when(i == 0)
    def _():
        carry_ref[...] = jnp.zeros_like(carry_ref)

    oa = _merge_patterns((o1_ref, o4_ref, o16_ref), (l1_ref, l4_ref, l16_ref))
    a = _rms(oa, ga_ref[...]).astype(BF16)
    b = _rms(ob_ref[...].astype(F32), gb_ref[...]).astype(BF16)
    mix = (jnp.dot(a, wo_ref[0:A_WIDTH, :], preferred_element_type=F32)
           + jnp.dot(b, wo_ref[A_WIDTH:, :], preferred_element_type=F32))
    x1 = x_ref[...] + mix
    x1_ref[...] = x1
    h2 = _rms(x1, gf_ref[...])
    hb = h2.astype(BF16).astype(F32)
    half = D_MODEL // 2
    hp_ref[...] = ((lax.bitcast_convert_type(hb[:, :half], U32) >> 16)
                   | (lax.bitcast_convert_type(hb[:, half:], U32) & jnp.uint32(0xFFFF0000)))

    lg = lax.dot_general(wr_ref[...], h2, _NT, preferred_element_type=F32,
                         precision=lax.Precision.HIGHEST) + br_ref[...]
    row = lax.broadcasted_iota(I32, lg.shape, 0)
    is_g = (row >= N_EXPERTS) & (row < N_EXPERTS + N_GROUPS)
    gl = jnp.where(is_g, lg, NEG_INF)
    ge = jnp.exp(gl - jnp.max(gl, axis=0, keepdims=True))
    gp = ge / jnp.sum(ge, axis=0, keepdims=True)
    g_gate = jnp.max(gp, axis=0, keepdims=True)
    g_idx = jnp.min(jnp.where(is_g & (gp == g_gate), row - N_EXPERTS, LANES), axis=0, keepdims=True)
    sel = (row >> 3) == g_idx
    el = jnp.where(sel, lg, NEG_INF)
    ee = jnp.exp(el - jnp.max(el, axis=0, keepdims=True))
    ep = jnp.where(sel, ee / jnp.sum(ee, axis=0, keepdims=True), -1.0)
    p1 = jnp.max(ep, axis=0, keepdims=True)
    i1 = jnp.min(jnp.where(ep == p1, row, LANES), axis=0, keepdims=True)
    ep2 = jnp.where(row == i1, -1.0, ep)
    p2 = jnp.max(ep2, axis=0, keepdims=True)
    i2 = jnp.min(jnp.where(sel & (ep2 == p2) & (row != i1), row, LANES), axis=0, keepdims=True)
    den = p1 + p2
    g1 = g_gate * p1 / den
    g2 = g_gate * p2 / den

    hit1 = row == i1
    hit2 = row == i2
    onehot = jnp.where(hit1 | hit2, 1.0, 0.0)
    before = jnp.dot(onehot.astype(BF16), tri_ref[...], preferred_element_type=F32) + carry_ref[...]
    r1 = jnp.sum(jnp.where(hit1, before, 0.0), axis=0, keepdims=True).astype(I32)
    r2 = jnp.sum(jnp.where(hit2, before, 0.0), axis=0, keepdims=True).astype(I32)
    carry_ref[...] += jnp.sum(onehot, axis=1, keepdims=True)

    row8 = lax.broadcasted_iota(I32, idx_ref.shape, 0)
    idx_ref[...] = jnp.where(row8 == 0, i1, jnp.where(row8 == 1, i2,
                             jnp.where(row8 == 2, r1, jnp.where(row8 == 3, r2, 0))))
    gate_ref[...] = jnp.where(row8 == 0, g1, jnp.where(row8 == 1, g2, 0.0))

    @pl.when(i == pl.num_programs(0) - 1)
    def _():
        cnt_ref[...] = jnp.broadcast_to(carry_ref[...], cnt_ref.shape).astype(I32)


def _mix_call(oas, lses, ob, x2, g_a, g_b, w_out, g_ffn, w_router, b_router, tri):
    t = x2.shape[0]
    tm = ROW_TILE
    nseq = oas[0].shape[2] // tm
    row = lambda i: (i, 0)
    const = lambda i: (0, 0)
    slab = pl.BlockSpec((1, A_HEADS // 2, tm, LANES), lambda i: (i // nseq, 0, i % nseq, 0))
    return pl.pallas_call(
        _mix_kernel,
        grid=(t // tm,),
        in_specs=[
            slab, slab, slab, slab, slab, slab,
            pl.BlockSpec((tm, B_WIDTH), row),
            pl.BlockSpec((tm, D_MODEL), row),
            pl.BlockSpec((1, A_WIDTH), const),
            pl.BlockSpec((1, B_WIDTH), const),
            pl.BlockSpec((D_MODEL, D_MODEL), const),
            pl.BlockSpec((1, D_MODEL), const),
            pl.BlockSpec((ROUTER_ROWS, D_MODEL), const),
            pl.BlockSpec((ROUTER_ROWS, 1), const),
            pl.BlockSpec((tm, tm), const),
        ],
        out_specs=[
            pl.BlockSpec((tm, D_MODEL), row),
            pl.BlockSpec((tm, D_MODEL // 2), row),
            pl.BlockSpec((8, tm), lambda i: (0, i)),
            pl.BlockSpec((8, tm), lambda i: (0, i)),
            pl.BlockSpec((ROUTER_ROWS, LANES), const),
        ],
        out_shape=[
            jax.ShapeDtypeStruct((t, D_MODEL), F32),
            jax.ShapeDtypeStruct((t, D_MODEL // 2), U32),
            jax.ShapeDtypeStruct((8, t), I32),
            jax.ShapeDtypeStruct((8, t), F32),
            jax.ShapeDtypeStruct((ROUTER_ROWS, LANES), I32),
        ],
        scratch_shapes=[pltpu.VMEM((ROUTER_ROWS, 1), F32)],
        compiler_params=_cparams(("arbitrary",)),
        name="mix_router",
    )(*oas, *lses, ob, x2, g_a, g_b, w_out, g_ffn, w_router, b_router, tri)


def _dest_kernel(idx_ref, pstart_ref, dest_ref):
    idx = idx_ref[...]
    row = lax.broadcasted_iota(I32, (ROUTER_ROWS, idx.shape[1]), 0)
    ps = pstart_ref[...]

    def slot(k):
        return jnp.sum(jnp.where(row == idx[k:k + 1, :], ps, 0), axis=0, keepdims=True) + idx[2 + k:3 + k, :]

    row8 = lax.broadcasted_iota(I32, idx.shape, 0)
    dest_ref[...] = jnp.where(row8 == 0, slot(0), jnp.where(row8 == 1, slot(1), 0))


def _dest_call(idx, pstart):
    t = idx.shape[1]
    tm = 4 * ROW_TILE
    return pl.pallas_call(
        _dest_kernel,
        grid=(t // tm,),
        in_specs=[pl.BlockSpec((8, tm), lambda i: (0, i)), pl.BlockSpec((ROUTER_ROWS, 1), lambda i: (0, 0))],
        out_specs=pl.BlockSpec((8, tm), lambda i: (0, i)),
        out_shape=jax.ShapeDtypeStruct((8, t), I32),
        compiler_params=_cparams(("parallel",)),
        name="dest_rows",
    )(idx, pstart)


def _dispatch_kernel(d0_ref, d1_ref, h_ref, buf_in, buf_out, sem):
    del buf_in
    tt = h_ref.shape[0]

    def issue(i, c):
        for d_ref in (d0_ref, d1_ref):
            pltpu.make_async_copy(h_ref.at[pl.ds(i, 1)], buf_out.at[pl.ds(d_ref[i], 1)], sem).start()
        return c

    lax.fori_loop(0, tt, issue, 0, unroll=ISSUE_UNROLL)
    for k in range(TOP_K):
        pltpu.make_async_copy(h_ref, buf_out.at[pl.ds(0, tt)], sem).wait()


def _dispatch_call(dests, hp, n_rows):
    t = hp.shape[0]
    tt = 2 * ROW_TILE
    buf0 = jnp.zeros((n_rows, D_MODEL // 2), U32)
    return pl.pallas_call(
        _dispatch_kernel,
        grid=(t // tt,),
        in_specs=[
            pl.BlockSpec((tt,), lambda i: (i,), memory_space=pltpu.SMEM),
            pl.BlockSpec((tt,), lambda i: (i,), memory_space=pltpu.SMEM),
            pl.BlockSpec((tt, D_MODEL // 2), lambda i: (i, 0)),
            pl.BlockSpec(memory_space=pl.ANY),
        ],
        out_specs=pl.BlockSpec(memory_space=pl.ANY),
        out_shape=jax.ShapeDtypeStruct((n_rows, D_MODEL // 2), U32),
        scratch_shapes=[pltpu.SemaphoreType.DMA(())],
        input_output_aliases={3: 0},
        compiler_params=_cparams(("arbitrary",), disable_bounds_checks=True, has_side_effects=True),
        name="dispatch",
    )(*dests, hp, buf0)


def _expert_kernel(be_ref, new_ref, valid_ref, buf_ref, wg_ref, wu_ref, wd_ref, out_ref, wg_s, wu_s, wd_s):
    j = pl.program_id(0)
    del be_ref

    @pl.when(new_ref[j] == 1)
    def _():
        wg_s[...] = wg_ref[0].astype(BF16)
        wu_s[...] = wu_ref[0].astype(BF16)
        wd_s[...] = wd_ref[0].astype(BF16)

    @pl.when(valid_ref[j] == 1)
    def _():
        w = buf_ref[...]
        half = D_MODEL // 2
        x_lo = lax.bitcast_convert_type(w << 16, F32).astype(BF16)
        x_hi = lax.bitcast_convert_type(w & jnp.uint32(0xFFFF0000), F32).astype(BF16)
        g = (jnp.dot(x_lo, wg_s[0:half, :], preferred_element_type=F32)
             + jnp.dot(x_hi, wg_s[half:, :], preferred_element_type=F32))
        u = (jnp.dot(x_lo, wu_s[0:half, :], preferred_element_type=F32)
             + jnp.dot(x_hi, wu_s[half:, :], preferred_element_type=F32))
        hb = (g * jax.nn.sigmoid(g)) * u
        out_ref[...] = jnp.dot(hb.astype(BF16), wd_s[...], preferred_element_type=F32)

    @pl.when(valid_ref[j] == 0)
    def _():
        out_ref[...] = jnp.zeros_like(out_ref)


def _expert_call(block_expert, block_new, block_valid, buf, w_gate, w_up, w_down):
    n_rows = buf.shape[0]
    nb = n_rows // MOE_BLK
    wsel = lambda j, be, nw, va: (be[j], 0, 0)
    rows = lambda j, be, nw, va: (j, 0)
    return pl.pallas_call(
        _expert_kernel,
        grid_spec=pltpu.PrefetchScalarGridSpec(
            num_scalar_prefetch=3,
            grid=(nb,),
            in_specs=[
                pl.BlockSpec((MOE_BLK, D_MODEL // 2), rows),
                pl.BlockSpec((1, D_MODEL, EXPERT_FF), wsel),
                pl.BlockSpec((1, D_MODEL, EXPERT_FF), wsel),
                pl.BlockSpec((1, EXPERT_FF, D_MODEL), wsel),
            ],
            out_specs=pl.BlockSpec((MOE_BLK, D_MODEL), rows),
            scratch_shapes=[pltpu.VMEM((D_MODEL, EXPERT_FF), BF16),
                            pltpu.VMEM((D_MODEL, EXPERT_FF), BF16),
                            pltpu.VMEM((EXPERT_FF, D_MODEL), BF16)],
        ),
        out_shape=jax.ShapeDtypeStruct((n_rows, D_MODEL), F32),
        compiler_params=_cparams(("arbitrary",)),
        name="experts",
    )(block_expert, block_new, block_valid, buf, w_gate, w_up, w_down)


def _combine_kernel(d0_ref, d1_ref, d0_next_ref, d1_next_ref, x1_ref, gate_ref, gf_ref, eo_ref, o_ref,
                    rows_ref, sems):
    i = pl.program_id(0)
    tt = x1_ref.shape[0]

    def gather(d_refs, slot):
        def issue(r, c):
            for k, d_ref in enumerate(d_refs):
                pltpu.make_async_copy(eo_ref.at[pl.ds(d_ref[r], 1)], rows_ref.at[slot, k, pl.ds(r, 1)],
                                      sems.at[slot]).start()
            return c

        lax.fori_loop(0, tt, issue, 0, unroll=ISSUE_UNROLL)

    @pl.when(i == 0)
    def _():
        gather((d0_ref, d1_ref), 0)

    @pl.when(i + 1 < pl.num_programs(0))
    def _():
        gather((d0_next_ref, d1_next_ref), (i + 1) % 2)

    slot = i % 2
    for k in range(TOP_K):
        pltpu.make_async_copy(eo_ref.at[pl.ds(0, tt)], rows_ref.at[slot, k], sems.at[slot]).wait()
    gate = gate_ref[...]
    y = rows_ref[slot, 0] * gate[:, 0:1] + rows_ref[slot, 1] * gate[:, 1:2]
    o_ref[...] = _rms(x1_ref[...] + y, gf_ref[...])


def _combine_call(dests, x1, gates, g_final, expert_out):
    t = x1.shape[0]
    tt = ROW_TILE // 2
    last = t // tt - 1
    cur = pl.BlockSpec((tt,), lambda i: (i,), memory_space=pltpu.SMEM)
    nxt = pl.BlockSpec((tt,), lambda i: (jnp.minimum(i + 1, last),), memory_space=pltpu.SMEM)
    return pl.pallas_call(
        _combine_kernel,
        grid=(t // tt,),
        in_specs=[
            cur, cur, nxt, nxt,
            pl.BlockSpec((tt, D_MODEL), lambda i: (i, 0)),
            pl.BlockSpec((tt, TOP_K), lambda i: (i, 0)),
            pl.BlockSpec((1, D_MODEL), lambda i: (0, 0)),
            pl.BlockSpec(memory_space=pl.ANY),
        ],
        out_specs=pl.BlockSpec((tt, D_MODEL), lambda i: (i, 0)),
        out_shape=jax.ShapeDtypeStruct((t, D_MODEL), F32),
        scratch_shapes=[pltpu.VMEM((2, TOP_K, tt, D_MODEL), F32), pltpu.SemaphoreType.DMA((2,))],
        compiler_params=_cparams(("arbitrary",), disable_bounds_checks=True),
        name="combine",
    )(*dests, *dests, x1, gates, g_final, expert_out)


def _rope_tables(seq):
    half = B_QK_ROPE // 2
    inv_freq = ROPE_THETA ** (-(jnp.arange(half, dtype=F32) / half))
    ang = jnp.arange(seq, dtype=F32)[:, None] * inv_freq[None, :]
    cos, sin = jnp.cos(ang), jnp.sin(ang)
    z = jnp.zeros((seq, B_QK_NOPE), F32)
    z2 = jnp.zeros((seq, B_QK_ROPE), F32)
    return (jnp.concatenate([z, cos, cos, z2], axis=1), jnp.concatenate([z, -sin, sin, z2], axis=1))


def _swap_halves(w):
    half = w.shape[-1] // 2
    return jnp.concatenate([w[..., half:], w[..., :half]], axis=-1)


def _layout_weights(w_in, w_q_up, w_kv_up):
    d = w_in.shape[0]
    w_kr = w_in[:, 3 * A_WIDTH + B_Q_LORA + B_KV_LORA:]
    w_in_l = jnp.concatenate(
        [w_in[:, :3 * A_WIDTH + B_Q_LORA + B_KV_LORA], jnp.zeros((d, B_QK_NOPE), F32), w_kr, _swap_halves(w_kr)],
        axis=1).astype(BF16)
    wq = w_q_up.reshape(B_Q_LORA, B_HEADS, B_QK_NOPE + B_QK_ROPE)
    wq_l = jnp.concatenate([wq, _swap_halves(wq[..., B_QK_NOPE:])], axis=-1)
    wq_l = wq_l.reshape(B_Q_LORA, B_HEADS * LANES).astype(BF16)
    wkv = w_kv_up.reshape(B_KV_LORA, B_HEADS, B_QK_NOPE + B_V_DIM)
    wkb = jnp.concatenate([wkv[..., :B_QK_NOPE], jnp.zeros_like(wkv[..., :B_QK_NOPE])], axis=-1)
    wkb = wkb.reshape(B_KV_LORA, B_HEADS * LANES).astype(BF16)
    wvb = wkv[..., B_QK_NOPE:].reshape(B_KV_LORA, B_WIDTH).astype(BF16)
    return w_in_l, wq_l, wkb, wvb


def _block_plan(counts, n_blocks):
    padded = (counts + MOE_BLK - 1) // MOE_BLK * MOE_BLK
    ends = jnp.cumsum(padded)
    starts = ends - padded
    first_row = jnp.arange(n_blocks, dtype=I32) * MOE_BLK
    expert = jnp.minimum(jnp.sum(ends[None, :] <= first_row[:, None], axis=1), N_EXPERTS - 1).astype(I32)
    new = jnp.concatenate([jnp.ones((1,), I32), (expert[1:] != expert[:-1]).astype(I32)])
    valid = (first_row < ends[-1]).astype(I32)
    return starts.astype(I32), expert, new, valid


def kernel(x, g_attn_norm, w_in, rel_bias, g_q_latent, w_q_up, g_kv_latent, w_kv_up, g_out_a, g_out_b, w_out,
           g_ffn_norm, w_router_group, b_router_group, w_router_expert, b_router_expert, w_gate, w_up, w_down,
           g_final):
    batch, seq, d = x.shape
    t = batch * seq
    assert g_attn_norm.shape[0] == 1 and d == D_MODEL and seq % ROW_TILE == 0
    cos_t, sin_t = _rope_tables(seq)
    tri = jnp.triu(jnp.ones((ROW_TILE, ROW_TILE), F32), 1).astype(BF16)
    n_blocks = t * TOP_K // MOE_BLK + N_EXPERTS
    x2 = x.reshape(t, d)
    row = lambda v: v.reshape(1, -1)

    w_in_l, wq_l, wkb_l, wvb_l = _layout_weights(w_in[0], w_q_up[0], w_kv_up[0])
    *qkv_a, qb, kb, vb = _proj_call(x2, row(g_attn_norm[0]), w_in_l, row(g_q_latent[0]), wq_l,
                                    row(g_kv_latent[0]), wkb_l, wvb_l, cos_t, sin_t, seq)
    oas, lses = [], []
    for pi, (window, dilation) in enumerate(DILATED_PATTERNS):
        bias = _dilated_bias(rel_bias, seq, dilation, window // (2 * dilation))
        o_p, lse_p = _dilated_call(*qkv_a[3 * pi:3 * pi + 3], bias, batch, seq, dilation)
        oas.append(o_p)
        lses.append(lse_p)
    ob = _mla_call(qb, kb, vb, batch, seq)

    pad = ROUTER_ROWS - N_EXPERTS - N_GROUPS
    w_router = jnp.concatenate([w_router_expert[0], w_router_group[0], jnp.zeros((d, pad), F32)], axis=1).T
    b_router = jnp.concatenate([b_router_expert[0], b_router_group[0], jnp.zeros((pad,), F32)])
    x1, hp, idx, gates, cnt = _mix_call(oas, lses, ob, x2, row(g_out_a[0]), row(g_out_b[0]), w_out[0].astype(BF16),
                                        row(g_ffn_norm[0]), w_router, b_router.reshape(-1, 1), tri)
    pstart, block_expert, block_new, block_valid = _block_plan(cnt[:N_EXPERTS, 0], n_blocks)
    pstart_col = jnp.concatenate([pstart, jnp.zeros((ROUTER_ROWS - N_EXPERTS,), I32)]).reshape(-1, 1)
    dest = _dest_call(idx, pstart_col)
    dests = (dest[0], dest[1])
    buf = _dispatch_call(dests, hp, n_blocks * MOE_BLK)
    expert_out = _expert_call(block_expert, block_new, block_valid, buf, w_gate[0], w_up[0], w_down[0])
    return _combine_call(dests, x1, gates[:TOP_K].T, row(g_final), expert_out).reshape(batch, seq, d)
```

```python
import functools
import math

import numpy as np
import jax
import jax.numpy as jnp
from jax import lax
from jax.experimental import pallas as pl
from jax.experimental.pallas import tpu as pltpu

F32 = jnp.float32
BF16 = jnp.bfloat16
I32 = jnp.int32
U32 = jnp.uint32

D_MODEL = 1024
EPS = 1e-6
NEG_INF = -1e30
LANES = 128

A_HEADS = 8
A_HEAD_DIM = 64
A_WIDTH = 512
DILATED_PATTERNS = ((128, 1), (512, 4), (2048, 16))
REL_BUCKETS = 32
REL_MAX_DISTANCE = 1024
A_QB = 128

B_HEADS = 8
B_Q_LORA = 256
B_KV_LORA = 128
B_QK_NOPE = 64
B_QK_ROPE = 32
B_V_DIM = 64
B_WIDTH = 512
ROPE_THETA = 10000.0
B_SCALE = (B_QK_NOPE + B_QK_ROPE) ** -0.5
B_QB = 512
B_SUB = 256
LOG2E = math.log2(math.e)

N_GROUPS = 4
EXPERTS_PER_GROUP = 8
N_EXPERTS = 32
TOP_K = 2
EXPERT_FF = 256
MOE_BLK = 512
ROUTER_ROWS = 40

ROW_TILE = 512
ISSUE_GROUP = 8
PROJ_COLS = 2048

_NT = (((1,), (1,)), ((), ()))


def _cparams(semantics, vmem_mb=48, **kw):
    return pltpu.CompilerParams(dimension_semantics=semantics,
                                vmem_limit_bytes=vmem_mb * 1024 * 1024, **kw)


def _rms(x, g):
    return x * lax.rsqrt(jnp.mean(x * x, axis=-1, keepdims=True) + EPS) * g


def _lane_iota(rows=1):
    return lax.broadcasted_iota(I32, (rows, LANES), 1)


def _proj_kernel(x_ref, g_ref, win_ref, gq_ref, wq_ref, gkv_ref, wkb_ref, wvb_ref, cos_ref, sin_ref, *refs):
    a_refs = refs[:3 * len(DILATED_PATTERNS)]
    qb_ref, kb_ref, vb_ref, slab_ref = refs[len(a_refs):]
    tm = x_ref.shape[0]
    h = _rms(x_ref[...], g_ref[...]).astype(BF16)
    proj = jnp.dot(h, win_ref[...], preferred_element_type=F32)
    lo = _lane_iota() < A_HEAD_DIM
    pairs = A_HEADS // 2
    for s in range(3 * pairs):
        col = proj[:, LANES * s:LANES * (s + 1)]
        slab_ref[s] = col * (A_HEAD_DIM ** -0.5) if s < pairs else col
    for pi, (_, r) in enumerate(DILATED_PATTERNS):
        q_out, k_out, v_out = a_refs[3 * pi:3 * pi + 3]
        for c in range(r):
            rows = pl.ds(c, tm // r, stride=r) if r > 1 else pl.ds(0, tm)
            for p in range(pairs):
                tile = slice(LANES * p, LANES * (p + 1))
                q_out[0, c, :, tile] = slab_ref[p, rows, :].astype(BF16)
                kp = slab_ref[pairs + p, rows, :]
                k_out[0, c, :, 2 * LANES * p:2 * LANES * p + LANES] = jnp.where(lo, kp, 0.0).astype(BF16)
                k_out[0, c, :, 2 * LANES * p + LANES:2 * LANES * (p + 1)] = jnp.where(lo, 0.0, kp).astype(BF16)
                v_out[0, c, :, tile] = slab_ref[2 * pairs + p, rows, :].astype(BF16)

    cos = cos_ref[...]
    sin = sin_ref[...]
    cq = _rms(proj[:, 1536:1792], gq_ref[...]).astype(BF16)
    q = jnp.dot(cq, wq_ref[...], preferred_element_type=F32)
    q_mul = (cos + jnp.where(lo, 1.0, 0.0)) * (B_SCALE * LOG2E)
    q_rot = sin * (B_SCALE * LOG2E)
    for hd in range(B_HEADS):
        t = q[:, LANES * hd:LANES * (hd + 1)]
        qb_ref[:, LANES * hd:LANES * (hd + 1)] = (t * q_mul + pltpu.roll(t, 96, 1) * q_rot).astype(BF16)

    ckv = _rms(proj[:, 1792:1920], gkv_ref[...]).astype(BF16)
    kr = proj[:, 1920:2048]
    kr = kr * cos + pltpu.roll(kr, 96, 1) * sin
    kn = jnp.dot(ckv, wkb_ref[...], preferred_element_type=F32)
    for hd in range(B_HEADS):
        kb_ref[:, LANES * hd:LANES * (hd + 1)] = (kn[:, LANES * hd:LANES * (hd + 1)] + kr).astype(BF16)
    vb_ref[...] = jnp.dot(ckv, wvb_ref[...], preferred_element_type=F32).astype(BF16)


def _proj_call(x2, g_attn, w_in, g_q, w_q, g_kv, w_kb, w_vb, cos_t, sin_t, seq):
    t = x2.shape[0]
    tm = ROW_TILE
    nseq = seq // tm
    row = lambda i: (i, 0)
    const = lambda i: (0, 0)
    pos = lambda i: (i % nseq, 0)
    out = lambda w: jax.ShapeDtypeStruct((t, w), BF16)
    a_specs, a_shapes = [], []
    for _, r in DILATED_PATTERNS:
        for w in (A_WIDTH, 2 * A_WIDTH, A_WIDTH):
            a_specs.append(pl.BlockSpec((1, r, tm // r, w), lambda i: (i // nseq, 0, i % nseq, 0)))
            a_shapes.append(jax.ShapeDtypeStruct((t // seq, r, seq // r, w), BF16))
    return pl.pallas_call(
        _proj_kernel,
        grid=(t // tm,),
        in_specs=[
            pl.BlockSpec((tm, D_MODEL), row),
            pl.BlockSpec((1, D_MODEL), const),
            pl.BlockSpec((D_MODEL, PROJ_COLS), const),
            pl.BlockSpec((1, B_Q_LORA), const),
            pl.BlockSpec((B_Q_LORA, B_HEADS * LANES), const),
            pl.BlockSpec((1, B_KV_LORA), const),
            pl.BlockSpec((B_KV_LORA, B_HEADS * LANES), const),
            pl.BlockSpec((B_KV_LORA, B_WIDTH), const),
            pl.BlockSpec((tm, LANES), pos),
            pl.BlockSpec((tm, LANES), pos),
        ],
        out_specs=a_specs + [
            pl.BlockSpec((tm, B_HEADS * LANES), row),
            pl.BlockSpec((tm, B_HEADS * LANES), row),
            pl.BlockSpec((tm, B_WIDTH), row),
        ],
        out_shape=a_shapes + [out(B_HEADS * LANES), out(B_HEADS * LANES), out(B_WIDTH)],
        scratch_shapes=[pltpu.VMEM((3 * A_HEADS // 2, tm, LANES), F32)],
        compiler_params=_cparams(("parallel",)),
        name="proj",
    )(x2, g_attn, w_in, g_q, w_q, g_kv, w_kb, w_vb, cos_t, sin_t)


def _dilated_kernel(q_ref, k_ref, v_ref, bias_ref, o_ref, lse_ref, *, seq_len, dilation, key_width, group):
    nblk = seq_len // A_QB
    lo = _lane_iota() < A_HEAD_DIM
    first_class = pl.program_id(1) * group

    def block(it, carry):
        c = it // nblk
        if nblk == 1:
            q0, ks, var = 0, 0, 0
        else:
            n = it % nblk
            q0 = pl.multiple_of(n * A_QB, A_QB)
            ks = pl.multiple_of(jnp.clip(q0 - 64, 0, seq_len - key_width), 64)
            var = jnp.where(n == 0, 0, jnp.where(n == nblk - 1, 2, 1))
        rows = pl.ds(q0, A_QB)
        keys = pl.ds(ks, key_width)
        if dilation == 1:
            out_rows = rows
        else:
            out_rows = pl.ds(first_class + c + dilation * q0, A_QB, stride=dilation)
        tiles = [slice(LANES * p, LANES * (p + 1)) for p in range(A_HEADS // 2)]
        scores = [lax.dot_general(q_ref[0, c, rows, tiles[hd // 2]], k_ref[0, c, keys, LANES * hd:LANES * (hd + 1)],
                                  _NT, preferred_element_type=F32) + bias_ref[var, hd] for hd in range(A_HEADS)]
        maxes = [jnp.max(s, axis=-1, keepdims=True) for s in scores]
        probs = [jnp.exp(s - m) for s, m in zip(scores, maxes)]
        dens = [jnp.sum(pr, axis=-1, keepdims=True) for pr in probs]
        pvs = [jnp.dot(pr.astype(BF16), v_ref[0, c, keys, tiles[hd // 2]], preferred_element_type=F32)
               for hd, pr in enumerate(probs)]
        for p in range(A_HEADS // 2):
            h0, h1 = 2 * p, 2 * p + 1
            o_ref[0, p, out_rows, :] = jnp.where(lo, pvs[h0] * (1.0 / dens[h0]), pvs[h1] * (1.0 / dens[h1]))
            lse_ref[0, p, out_rows, :] = jnp.where(lo, maxes[h0] + jnp.log(dens[h0]), maxes[h1] + jnp.log(dens[h1]))
        return carry

    lax.fori_loop(0, group * nblk, block, 0, unroll=2)


def _dilated_call(qa, ka, va, bias, batch, seq, dilation):
    r = dilation
    sl = seq // r
    kw = min(2 * A_QB, sl)
    group = max(1, min(r, (4 * A_QB) // sl))
    blk = lambda w: pl.BlockSpec((1, group, sl, w), lambda b, c: (b, c, 0, 0))
    nat = pl.BlockSpec((1, A_HEADS // 2, seq, LANES), lambda b, c: (b, 0, 0, 0))
    return pl.pallas_call(
        functools.partial(_dilated_kernel, seq_len=sl, dilation=r, key_width=kw, group=group),
        grid=(batch, r // group),
        in_specs=[blk(A_WIDTH), blk(2 * A_WIDTH), blk(A_WIDTH),
                  pl.BlockSpec(bias.shape, lambda b, c: (0, 0, 0, 0))],
        out_specs=[nat, nat],
        out_shape=[jax.ShapeDtypeStruct((batch, A_HEADS // 2, seq, LANES), F32)] * 2,
        compiler_params=_cparams(("parallel", "arbitrary")),
        name=f"dilated_r{r}",
    )(qa, ka, va, bias)


def _t5_bucket(rel):
    half = REL_BUCKETS // 2
    max_exact = half // 2
    n = np.abs(rel)
    large = max_exact + (np.log(np.maximum(n, 1) / max_exact)
                         / math.log(REL_MAX_DISTANCE / max_exact) * (half - max_exact)).astype(np.int32)
    large = np.minimum(large, half - 1)
    return (np.where(rel > 0, half, 0) + np.where(n < max_exact, n, large)).astype(np.int32)


def _dilated_bias(rel_bias, seq, dilation, half_steps):
    sl = seq // dilation
    kw = min(2 * A_QB, sl)
    offsets = [0] if sl == kw else [0, -half_steps, A_QB - kw]
    rel = np.stack([np.arange(kw)[None, :] + off - np.arange(A_QB)[:, None] for off in offsets])
    valid = np.abs(rel) <= half_steps
    bucket = np.where(valid, _t5_bucket(rel * dilation), REL_BUCKETS).astype(np.int32)
    onehot = (jnp.asarray(bucket)[..., None] == jnp.arange(REL_BUCKETS + 1, dtype=I32)).astype(F32)
    table = jnp.concatenate([rel_bias.astype(F32), jnp.full((1, A_HEADS), NEG_INF, F32)], axis=0)
    return jnp.einsum("vqkb,bh->vhqk", onehot, table, precision=lax.Precision.HIGHEST)


def _mla_kernel(q_ref, k_ref, v_ref, o_ref):
    lo = _lane_iota() < B_V_DIM
    sub = B_QB // B_SUB
    tiles = [slice(0, LANES), slice(LANES, 2 * LANES)]

    def block(i, carry):
        r0 = pl.multiple_of(i * B_QB, B_QB)
        rows = [pl.ds(r0 + B_SUB * j, B_SUB) for j in range(sub)]
        units = [(j, half) for j in range(sub) for half in range(2)]
        scores = [lax.dot_general(q_ref[0, rows[j], tiles[half]], k_ref[0, :, tiles[half]], _NT,
                                  preferred_element_type=F32) for j, half in units]
        maxes = [jnp.max(s, axis=-1, keepdims=True) for s in scores]
        probs = [jnp.exp2(s - m) for s, m in zip(scores, maxes)]
        dens = [jnp.sum(pr, axis=-1, keepdims=True) for pr in probs]
        outs = [jnp.dot(pr.astype(BF16), v_ref[0], preferred_element_type=F32) * (1.0 / den)
                for pr, den in zip(probs, dens)]
        for j in range(sub):
            o_ref[0, rows[j], :] = jnp.where(lo, outs[2 * j], outs[2 * j + 1]).astype(BF16)
        return carry

    lax.fori_loop(0, q_ref.shape[1] // B_QB, block, 0)


def _mla_call(qb, kb, vb, batch, seq):
    qb = qb.reshape(batch, seq, B_HEADS * LANES)
    kb = kb.reshape(batch, seq, B_HEADS * LANES)
    vb = vb.reshape(batch, seq, B_WIDTH)
    pair = lambda w: pl.BlockSpec((1, seq, w), lambda b, p: (b, 0, p))
    out = pl.pallas_call(
        _mla_kernel,
        grid=(batch, B_HEADS // 2),
        in_specs=[pair(2 * LANES), pair(2 * LANES), pair(LANES)],
        out_specs=pair(LANES),
        out_shape=jax.ShapeDtypeStruct((batch, seq, B_WIDTH), BF16),
        compiler_params=_cparams(("parallel", "parallel")),
        name="mla",
    )(qb, kb, vb)
    return out.reshape(batch * seq, B_WIDTH)


def _merge_patterns(o_refs, lse_refs):
    tiles = []
    for p in range(A_HEADS // 2):
        lses = [r[0, p] for r in lse_refs]
        top = functools.reduce(jnp.maximum, lses)
        es = [jnp.exp(l - top) for l in lses]
        inv = 1.0 / functools.reduce(jnp.add, es)
        tiles.append(functools.reduce(jnp.add, [e * inv * r[0, p] for e, r in zip(es, o_refs)]))
    return jnp.concatenate(tiles, axis=1)


def _mix_kernel(o1_ref, o4_ref, o16_ref, l1_ref, l4_ref, l16_ref, ob_ref, x_ref, ga_ref, gb_ref, wo_ref,
                gf_ref, wr_ref, br_ref, tri_ref, x1_ref, hp_ref, idx_ref, gate_ref, cnt_ref, carry_ref):
    i = pl.program_id(0)

    @pl.when(i == 0)
    def _():
        carry_ref[...] = jnp.zeros_like(carry_ref)

    oa = _merge_patterns((o1_ref, o4_ref, o16_ref), (l1_ref, l4_ref, l16_ref))
    a = _rms(oa, ga_ref[...]).astype(BF16)
    b = _rms(ob_ref[...].astype(F32), gb_ref[...]).astype(BF16)
    mix = (jnp.dot(a, wo_ref[0:A_WIDTH, :], preferred_element_type=F32)
           + jnp.dot(b, wo_ref[A_WIDTH:, :], preferred_element_type=F32))
    x1 = x_ref[...] + mix
    x1_ref[...] = x1
    h2 = _rms(x1, gf_ref[...])
    hb = h2.astype(BF16).astype(F32)
    half = D_MODEL // 2
    hp_ref[...] = ((lax.bitcast_convert_type(hb[:, :half], U32) >> 16)
                   | (lax.bitcast_convert_type(hb[:, half:], U32) & jnp.uint32(0xFFFF0000)))

    lg = lax.dot_general(wr_ref[...], h2, _NT, preferred_element_type=F32,
                         precision=lax.Precision.HIGHEST) + br_ref[...]
    row = lax.broadcasted_iota(I32, lg.shape, 0)
    is_g = (row >= N_EXPERTS) & (row < N_EXPERTS + N_GROUPS)
    gl = jnp.where(is_g, lg, NEG_INF)
    ge = jnp.exp(gl - jnp.max(gl, axis=0, keepdims=True))
    gp = ge / jnp.sum(ge, axis=0, keepdims=True)
    g_gate = jnp.max(gp, axis=0, keepdims=True)
    g_idx = jnp.min(jnp.where(is_g & (gp == g_gate), row - N_EXPERTS, LANES), axis=0, keepdims=True)
    sel = (row >> 3) == g_idx
    el = jnp.where(sel, lg, NEG_INF)
    ee = jnp.exp(el - jnp.max(el, axis=0, keepdims=True))
    ep = jnp.where(sel, ee / jnp.sum(ee, axis=0, keepdims=True), -1.0)
    p1 = jnp.max(ep, axis=0, keepdims=True)
    i1 = jnp.min(jnp.where(ep == p1, row, LANES), axis=0, keepdims=True)
    ep2 = jnp.where(row == i1, -1.0, ep)
    p2 = jnp.max(ep2, axis=0, keepdims=True)
    i2 = jnp.min(jnp.where(sel & (ep2 == p2) & (row != i1), row, LANES), axis=0, keepdims=True)
    den = p1 + p2
    g1 = g_gate * p1 / den
    g2 = g_gate * p2 / den

    hit1 = row == i1
    hit2 = row == i2
    onehot = jnp.where(hit1 | hit2, 1.0, 0.0)
    before = jnp.dot(onehot.astype(BF16), tri_ref[...], preferred_element_type=F32) + carry_ref[...]
    r1 = jnp.sum(jnp.where(hit1, before, 0.0), axis=0, keepdims=True).astype(I32)
    r2 = jnp.sum(jnp.where(hit2, before, 0.0), axis=0, keepdims=True).astype(I32)
    carry_ref[...] += jnp.sum(onehot, axis=1, keepdims=True)

    row8 = lax.broadcasted_iota(I32, idx_ref.shape, 0)
    idx_ref[...] = jnp.where(row8 == 0, i1, jnp.where(row8 == 1, i2,
                             jnp.where(row8 == 2, r1, jnp.where(row8 == 3, r2, 0))))
    gate_ref[...] = jnp.where(row8 == 0, g1, jnp.where(row8 == 1, g2, 0.0))

    @pl.when(i == pl.num_programs(0) - 1)
    def _():
        cnt_ref[...] = jnp.broadcast_to(carry_ref[...], cnt_ref.shape).astype(I32)


def _mix_call(oas, lses, ob, x2, g_a, g_b, w_out, g_ffn, w_router, b_router, tri):
    t = x2.shape[0]
    tm = ROW_TILE
    nseq = oas[0].shape[2] // tm
    row = lambda i: (i, 0)
    const = lambda i: (0, 0)
    slab = pl.BlockSpec((1, A_HEADS // 2, tm, LANES), lambda i: (i // nseq, 0, i % nseq, 0))
    return pl.pallas_call(
        _mix_kernel,
        grid=(t // tm,),
        in_specs=[
            slab, slab, slab, slab, slab, slab,
            pl.BlockSpec((tm, B_WIDTH), row),
            pl.BlockSpec((tm, D_MODEL), row),
            pl.BlockSpec((1, A_WIDTH), const),
            pl.BlockSpec((1, B_WIDTH), const),
            pl.BlockSpec((D_MODEL, D_MODEL), const),
            pl.BlockSpec((1, D_MODEL), const),
            pl.BlockSpec((ROUTER_ROWS, D_MODEL), const),
            pl.BlockSpec((ROUTER_ROWS, 1), const),
            pl.BlockSpec((tm, tm), const),
        ],
        out_specs=[
            pl.BlockSpec((tm, D_MODEL), row),
            pl.BlockSpec((tm, D_MODEL // 2), row),
            pl.BlockSpec((8, tm), lambda i: (0, i)),
            pl.BlockSpec((8, tm), lambda i: (0, i)),
            pl.BlockSpec((ROUTER_ROWS, LANES), const),
        ],
        out_shape=[
            jax.ShapeDtypeStruct((t, D_MODEL), F32),
            jax.ShapeDtypeStruct((t, D_MODEL // 2), U32),
            jax.ShapeDtypeStruct((8, t), I32),
            jax.ShapeDtypeStruct((8, t), F32),
            jax.ShapeDtypeStruct((ROUTER_ROWS, LANES), I32),
        ],
        scratch_shapes=[pltpu.VMEM((ROUTER_ROWS, 1), F32)],
        compiler_params=_cparams(("arbitrary",)),
        name="mix_router",
    )(*oas, *lses, ob, x2, g_a, g_b, w_out, g_ffn, w_router, b_router, tri)


def _dest_kernel(idx_ref, pstart_ref, dest_ref):
    idx = idx_ref[...]
    row = lax.broadcasted_iota(I32, (ROUTER_ROWS, idx.shape[1]), 0)
    ps = pstart_ref[...]

    def slot(k):
        return jnp.sum(jnp.where(row == idx[k:k + 1, :], ps, 0), axis=0, keepdims=True) + idx[2 + k:3 + k, :]

    row8 = lax.broadcasted_iota(I32, idx.shape, 0)
    dest_ref[...] = jnp.where(row8 == 0, slot(0), jnp.where(row8 == 1, slot(1), 0))


def _dest_call(idx, pstart):
    t = idx.shape[1]
    tm = 4 * ROW_TILE
    return pl.pallas_call(
        _dest_kernel,
        grid=(t // tm,),
        in_specs=[pl.BlockSpec((8, tm), lambda i: (0, i)), pl.BlockSpec((ROUTER_ROWS, 1), lambda i: (0, 0))],
        out_specs=pl.BlockSpec((8, tm), lambda i: (0, i)),
        out_shape=jax.ShapeDtypeStruct((8, t), I32),
        compiler_params=_cparams(("parallel",)),
        name="dest_rows",
    )(idx, pstart)


def _dispatch_kernel(d0_ref, d1_ref, h_ref, buf_in, buf_out, sem):
    del buf_in
    tt = h_ref.shape[0]

    def issue(g, c):
        base = pl.multiple_of(g * ISSUE_GROUP, ISSUE_GROUP)
        for j in range(ISSUE_GROUP):
            for prio, d_ref in enumerate((d0_ref, d1_ref)):
                pltpu.make_async_copy(h_ref.at[pl.ds(base + j, 1)], buf_out.at[pl.ds(d_ref[base + j], 1)],
                                      sem).start(priority=prio)
        return c

    lax.fori_loop(0, tt // ISSUE_GROUP, issue, 0)
    for k in range(TOP_K):
        pltpu.make_async_copy(h_ref, buf_out.at[pl.ds(0, tt)], sem).wait()


def _dispatch_call(dests, hp, n_rows):
    t = hp.shape[0]
    tt = 2 * ROW_TILE
    buf0 = jnp.zeros((n_rows, D_MODEL // 2), U32)
    return pl.pallas_call(
        _dispatch_kernel,
        grid=(t // tt,),
        in_specs=[
            pl.BlockSpec((tt,), lambda i: (i,), memory_space=pltpu.SMEM),
            pl.BlockSpec((tt,), lambda i: (i,), memory_space=pltpu.SMEM),
            pl.BlockSpec((tt, D_MODEL // 2), lambda i: (i, 0)),
            pl.BlockSpec(memory_space=pl.ANY),
        ],
        out_specs=pl.BlockSpec(memory_space=pl.ANY),
        out_shape=jax.ShapeDtypeStruct((n_rows, D_MODEL // 2), U32),
        scratch_shapes=[pltpu.SemaphoreType.DMA(())],
        input_output_aliases={3: 0},
        compiler_params=_cparams(("arbitrary",), disable_bounds_checks=True, has_side_effects=True),
        name="dispatch",
    )(*dests, hp, buf0)


def _expert_kernel(be_ref, new_ref, valid_ref, buf_ref, wg_ref, wu_ref, wd_ref, out_ref, wg_s, wu_s, wd_s):
    j = pl.program_id(0)
    del be_ref

    @pl.when(new_ref[j] == 1)
    def _():
        wg_s[...] = wg_ref[0].astype(BF16)
        wu_s[...] = wu_ref[0].astype(BF16)
        wd_s[...] = wd_ref[0].astype(BF16)

    @pl.when(valid_ref[j] == 1)
    def _():
        w = buf_ref[...]
        half = D_MODEL // 2
        x_lo = lax.bitcast_convert_type(w << 16, F32).astype(BF16)
        x_hi = lax.bitcast_convert_type(w & jnp.uint32(0xFFFF0000), F32).astype(BF16)
        g = (jnp.dot(x_lo, wg_s[0:half, :], preferred_element_type=F32)
             + jnp.dot(x_hi, wg_s[half:, :], preferred_element_type=F32))
        u = (jnp.dot(x_lo, wu_s[0:half, :], preferred_element_type=F32)
             + jnp.dot(x_hi, wu_s[half:, :], preferred_element_type=F32))
        hb = (g * jax.nn.sigmoid(g)) * u
        out_ref[...] = jnp.dot(hb.astype(BF16), wd_s[...], preferred_element_type=F32)

    @pl.when(valid_ref[j] == 0)
    def _():
        out_ref[...] = jnp.zeros_like(out_ref)


def _expert_call(block_expert, block_new, block_valid, buf, w_gate, w_up, w_down):
    n_rows = buf.shape[0]
    nb = n_rows // MOE_BLK
    wsel = lambda j, be, nw, va: (be[j], 0, 0)
    rows = lambda j, be, nw, va: (j, 0)
    return pl.pallas_call(
        _expert_kernel,
        grid_spec=pltpu.PrefetchScalarGridSpec(
            num_scalar_prefetch=3,
            grid=(nb,),
            in_specs=[
                pl.BlockSpec((MOE_BLK, D_MODEL // 2), rows),
                pl.BlockSpec((1, D_MODEL, EXPERT_FF), wsel),
                pl.BlockSpec((1, D_MODEL, EXPERT_FF), wsel),
                pl.BlockSpec((1, EXPERT_FF, D_MODEL), wsel),
            ],
            out_specs=pl.BlockSpec((MOE_BLK, D_MODEL), rows),
            scratch_shapes=[pltpu.VMEM((D_MODEL, EXPERT_FF), BF16),
                            pltpu.VMEM((D_MODEL, EXPERT_FF), BF16),
                            pltpu.VMEM((EXPERT_FF, D_MODEL), BF16)],
        ),
        out_shape=jax.ShapeDtypeStruct((n_rows, D_MODEL), F32),
        compiler_params=_cparams(("arbitrary",)),
        name="experts",
    )(block_expert, block_new, block_valid, buf, w_gate, w_up, w_down)


def _combine_kernel(d0_ref, d1_ref, d0_next_ref, d1_next_ref, x1_ref, gate_ref, gf_ref, eo_ref, o_ref,
                    rows_ref, sems):
    i = pl.program_id(0)
    tt = x1_ref.shape[0]

    def gather(d_refs, slot):
        def issue(g, c):
            base = pl.multiple_of(g * ISSUE_GROUP, ISSUE_GROUP)
            for j in range(ISSUE_GROUP):
                for k, d_ref in enumerate(d_refs):
                    pltpu.make_async_copy(eo_ref.at[pl.ds(d_ref[base + j], 1)],
                                          rows_ref.at[slot, k, pl.ds(base + j, 1)],
                                          sems.at[slot]).start(priority=k)
            return c

        lax.fori_loop(0, tt // ISSUE_GROUP, issue, 0)

    @pl.when(i == 0)
    def _():
        gather((d0_ref, d1_ref), 0)

    @pl.when(i + 1 < pl.num_programs(0))
    def _():
        gather((d0_next_ref, d1_next_ref), (i + 1) % 2)

    slot = i % 2
    for k in range(TOP_K):
        pltpu.make_async_copy(eo_ref.at[pl.ds(0, tt)], rows_ref.at[slot, k], sems.at[slot]).wait()
    gate = gate_ref[...]
    y = rows_ref[slot, 0] * gate[:, 0:1] + rows_ref[slot, 1] * gate[:, 1:2]
    o_ref[...] = _rms(x1_ref[...] + y, gf_ref[...])


def _combine_call(dests, x1, gates, g_final, expert_out):
    t = x1.shape[0]
    tt = ROW_TILE // 2
    last = t // tt - 1
    cur = pl.BlockSpec((tt,), lambda i: (i,), memory_space=pltpu.SMEM)
    nxt = pl.BlockSpec((tt,), lambda i: (jnp.minimum(i + 1, last),), memory_space=pltpu.SMEM)
    return pl.pallas_call(
        _combine_kernel,
        grid=(t // tt,),
        in_specs=[
            cur, cur, nxt, nxt,
            pl.BlockSpec((tt, D_MODEL), lambda i: (i, 0)),
            pl.BlockSpec((tt, TOP_K), lambda i: (i, 0)),
            pl.BlockSpec((1, D_MODEL), lambda i: (0, 0)),
            pl.BlockSpec(memory_space=pl.ANY),
        ],
        out_specs=pl.BlockSpec((tt, D_MODEL), lambda i: (i, 0)),
        out_shape=jax.ShapeDtypeStruct((t, D_MODEL), F32),
        scratch_shapes=[pltpu.VMEM((2, TOP_K, tt, D_MODEL), F32), pltpu.SemaphoreType.DMA((2,))],
        compiler_params=_cparams(("arbitrary",), disable_bounds_checks=True),
        name="combine",
    )(*dests, *dests, x1, gates, g_final, expert_out)


def _rope_tables(seq):
    half = B_QK_ROPE // 2
    inv_freq = ROPE_THETA ** (-(jnp.arange(half, dtype=F32) / half))
    ang = jnp.arange(seq, dtype=F32)[:, None] * inv_freq[None, :]
    cos, sin = jnp.cos(ang), jnp.sin(ang)
    z = jnp.zeros((seq, B_QK_NOPE), F32)
    z2 = jnp.zeros((seq, B_QK_ROPE), F32)
    return (jnp.concatenate([z, cos, cos, z2], axis=1), jnp.concatenate([z, -sin, sin, z2], axis=1))


def _swap_halves(w):
    half = w.shape[-1] // 2
    return jnp.concatenate([w[..., half:], w[..., :half]], axis=-1)


def _layout_weights(w_in, w_q_up, w_kv_up):
    d = w_in.shape[0]
    w_kr = w_in[:, 3 * A_WIDTH + B_Q_LORA + B_KV_LORA:]
    w_in_l = jnp.concatenate(
        [w_in[:, :3 * A_WIDTH + B_Q_LORA + B_KV_LORA], jnp.zeros((d, B_QK_NOPE), F32), w_kr, _swap_halves(w_kr)],
        axis=1).astype(BF16)
    wq = w_q_up.reshape(B_Q_LORA, B_HEADS, B_QK_NOPE + B_QK_ROPE)
    wq_l = jnp.concatenate([wq, _swap_halves(wq[..., B_QK_NOPE:])], axis=-1)
    wq_l = wq_l.reshape(B_Q_LORA, B_HEADS * LANES).astype(BF16)
    wkv = w_kv_up.reshape(B_KV_LORA, B_HEADS, B_QK_NOPE + B_V_DIM)
    wkb = jnp.concatenate([wkv[..., :B_QK_NOPE], jnp.zeros_like(wkv[..., :B_QK_NOPE])], axis=-1)
    wkb = wkb.reshape(B_KV_LORA, B_HEADS * LANES).astype(BF16)
    wvb = wkv[..., B_QK_NOPE:].reshape(B_KV_LORA, B_WIDTH).astype(BF16)
    return w_in_l, wq_l, wkb, wvb


def _block_plan(counts, n_blocks):
    padded = (counts + MOE_BLK - 1) // MOE_BLK * MOE_BLK
    ends = jnp.cumsum(padded)
    starts = ends - padded
    first_row = jnp.arange(n_blocks, dtype=I32) * MOE_BLK
    expert = jnp.minimum(jnp.sum(ends[None, :] <= first_row[:, None], axis=1), N_EXPERTS - 1).astype(I32)
    new = jnp.concatenate([jnp.ones((1,), I32), (expert[1:] != expert[:-1]).astype(I32)])
    valid = (first_row < ends[-1]).astype(I32)
    return starts.astype(I32), expert, new, valid


def kernel(x, g_attn_norm, w_in, rel_bias, g_q_latent, w_q_up, g_kv_latent, w_kv_up, g_out_a, g_out_b, w_out,
           g_ffn_norm, w_router_group, b_router_group, w_router_expert, b_router_expert, w_gate, w_up, w_down,
           g_final):
    batch, seq, d = x.shape
    t = batch * seq
    assert g_attn_norm.shape[0] == 1 and d == D_MODEL and seq % ROW_TILE == 0
    cos_t, sin_t = _rope_tables(seq)
    tri = jnp.triu(jnp.ones((ROW_TILE, ROW_TILE), F32), 1).astype(BF16)
    n_blocks = t * TOP_K // MOE_BLK + N_EXPERTS
    x2 = x.reshape(t, d)
    row = lambda v: v.reshape(1, -1)

    w_in_l, wq_l, wkb_l, wvb_l = _layout_weights(w_in[0], w_q_up[0], w_kv_up[0])
    *qkv_a, qb, kb, vb = _proj_call(x2, row(g_attn_norm[0]), w_in_l, row(g_q_latent[0]), wq_l,
                                    row(g_kv_latent[0]), wkb_l, wvb_l, cos_t, sin_t, seq)
    oas, lses = [], []
    for pi, (window, dilation) in enumerate(DILATED_PATTERNS):
        bias = _dilated_bias(rel_bias, seq, dilation, window // (2 * dilation))
        o_p, lse_p = _dilated_call(*qkv_a[3 * pi:3 * pi + 3], bias, batch, seq, dilation)
        oas.append(o_p)
        lses.append(lse_p)
    ob = _mla_call(qb, kb, vb, batch, seq)

    pad = ROUTER_ROWS - N_EXPERTS - N_GROUPS
    w_router = jnp.concatenate([w_router_expert[0], w_router_group[0], jnp.zeros((d, pad), F32)], axis=1).T
    b_router = jnp.concatenate([b_router_expert[0], b_router_group[0], jnp.zeros((pad,), F32)])
    x1, hp, idx, gates, cnt = _mix_call(oas, lses, ob, x2, row(g_out_a[0]), row(g_out_b[0]), w_out[0].astype(BF16),
                                        row(g_ffn_norm[0]), w_router, b_router.reshape(-1, 1), tri)
    pstart, block_expert, block_new, block_valid = _block_plan(cnt[:N_EXPERTS, 0], n_blocks)
    pstart_col = jnp.concatenate([pstart, jnp.zeros((ROUTER_ROWS - N_EXPERTS,), I32)]).reshape(-1, 1)
    dest = _dest_call(idx, pstart_col)
    dests = (dest[0], dest[1])
    buf = _dispatch_call(dests, hp, n_blocks * MOE_BLK)
    expert_out = _expert_call(block_expert, block_new, block_valid, buf, w_gate[0], w_up[0], w_down[0])
    return _combine_call(dests, x1, gates[:TOP_K].T, row(g_final), expert_out).reshape(batch, seq, d)
```

```python
import functools
import math

import numpy as np
import jax
import jax.numpy as jnp
from jax import lax
from jax.experimental import pallas as pl
from jax.experimental.pallas import tpu as pltpu

F32 = jnp.float32
BF16 = jnp.bfloat16
I32 = jnp.int32
U32 = jnp.uint32

D_MODEL = 1024
EPS = 1e-6
NEG_INF = -1e30
LANES = 128
ROW_SUBLANES = D_MODEL // LANES

A_HEADS = 8
A_HEAD_DIM = 64
A_WIDTH = 512
DILATED_PATTERNS = ((128, 1), (512, 4), (2048, 16))
REL_BUCKETS = 32
REL_MAX_DISTANCE = 1024
A_QB = 128

B_HEADS = 8
B_Q_LORA = 256
B_KV_LORA = 128
B_QK_NOPE = 64
B_QK_ROPE = 32
B_V_DIM = 64
B_WIDTH = 512
ROPE_THETA = 10000.0
B_SCALE = (B_QK_NOPE + B_QK_ROPE) ** -0.5
B_QB = 512
B_SUB = 256
LOG2E = math.log2(math.e)

N_GROUPS = 4
EXPERTS_PER_GROUP = 8
N_EXPERTS = 32
TOP_K = 2
EXPERT_FF = 256
MOE_BLK = 512
ROUTER_ROWS = 40

ROW_TILE = 512
ISSUE_GROUP = 8
PROJ_COLS = 2048

_NT = (((1,), (1,)), ((), ()))


def _cparams(semantics, vmem_mb=48, **kw):
    return pltpu.CompilerParams(dimension_semantics=semantics,
                                vmem_limit_bytes=vmem_mb * 1024 * 1024, **kw)


def _rms(x, g):
    return x * lax.rsqrt(jnp.mean(x * x, axis=-1, keepdims=True) + EPS) * g


def _lane_iota(rows=1):
    return lax.broadcasted_iota(I32, (rows, LANES), 1)


def _row_tile(r):
    return pl.ds(pl.multiple_of(r * ROW_SUBLANES, ROW_SUBLANES), ROW_SUBLANES)


def _proj_kernel(x_ref, g_ref, win_ref, gq_ref, wq_ref, gkv_ref, wkb_ref, wvb_ref, cos_ref, sin_ref, *refs):
    a_refs = refs[:3 * len(DILATED_PATTERNS)]
    qb_ref, kb_ref, vb_ref, slab_ref = refs[len(a_refs):]
    tm = x_ref.shape[0]
    h = _rms(x_ref[...], g_ref[...]).astype(BF16)
    proj = jnp.dot(h, win_ref[...], preferred_element_type=F32)
    lo = _lane_iota() < A_HEAD_DIM
    pairs = A_HEADS // 2
    for s in range(3 * pairs):
        col = proj[:, LANES * s:LANES * (s + 1)]
        slab_ref[s] = col * (A_HEAD_DIM ** -0.5) if s < pairs else col
    for pi, (_, r) in enumerate(DILATED_PATTERNS):
        q_out, k_out, v_out = a_refs[3 * pi:3 * pi + 3]
        for c in range(r):
            rows = pl.ds(c, tm // r, stride=r) if r > 1 else pl.ds(0, tm)
            for p in range(pairs):
                tile = slice(LANES * p, LANES * (p + 1))
                q_out[0, c, :, tile] = slab_ref[p, rows, :].astype(BF16)
                kp = slab_ref[pairs + p, rows, :]
                k_out[0, c, :, 2 * LANES * p:2 * LANES * p + LANES] = jnp.where(lo, kp, 0.0).astype(BF16)
                k_out[0, c, :, 2 * LANES * p + LANES:2 * LANES * (p + 1)] = jnp.where(lo, 0.0, kp).astype(BF16)
                v_out[0, c, :, tile] = slab_ref[2 * pairs + p, rows, :].astype(BF16)

    cos = cos_ref[...]
    sin = sin_ref[...]
    cq = _rms(proj[:, 1536:1792], gq_ref[...]).astype(BF16)
    q = jnp.dot(cq, wq_ref[...], preferred_element_type=F32)
    q_mul = (cos + jnp.where(lo, 1.0, 0.0)) * (B_SCALE * LOG2E)
    q_rot = sin * (B_SCALE * LOG2E)
    for hd in range(B_HEADS):
        t = q[:, LANES * hd:LANES * (hd + 1)]
        qb_ref[:, LANES * hd:LANES * (hd + 1)] = (t * q_mul + pltpu.roll(t, 96, 1) * q_rot).astype(BF16)

    ckv = _rms(proj[:, 1792:1920], gkv_ref[...]).astype(BF16)
    kr = proj[:, 1920:2048]
    kr = kr * cos + pltpu.roll(kr, 96, 1) * sin
    kn = jnp.dot(ckv, wkb_ref[...], preferred_element_type=F32)
    for hd in range(B_HEADS):
        kb_ref[:, LANES * hd:LANES * (hd + 1)] = (kn[:, LANES * hd:LANES * (hd + 1)] + kr).astype(BF16)
    vb_ref[...] = jnp.dot(ckv, wvb_ref[...], preferred_element_type=F32).astype(BF16)


def _proj_call(x2, g_attn, w_in, g_q, w_q, g_kv, w_kb, w_vb, cos_t, sin_t, seq):
    t = x2.shape[0]
    tm = ROW_TILE
    nseq = seq // tm
    row = lambda i: (i, 0)
    const = lambda i: (0, 0)
    pos = lambda i: (i % nseq, 0)
    out = lambda w: jax.ShapeDtypeStruct((t, w), BF16)
    a_specs, a_shapes = [], []
    for _, r in DILATED_PATTERNS:
        for w in (A_WIDTH, 2 * A_WIDTH, A_WIDTH):
            a_specs.append(pl.BlockSpec((1, r, tm // r, w), lambda i: (i // nseq, 0, i % nseq, 0)))
            a_shapes.append(jax.ShapeDtypeStruct((t // seq, r, seq // r, w), BF16))
    return pl.pallas_call(
        _proj_kernel,
        grid=(t // tm,),
        in_specs=[
            pl.BlockSpec((tm, D_MODEL), row),
            pl.BlockSpec((1, D_MODEL), const),
            pl.BlockSpec((D_MODEL, PROJ_COLS), const),
            pl.BlockSpec((1, B_Q_LORA), const),
            pl.BlockSpec((B_Q_LORA, B_HEADS * LANES), const),
            pl.BlockSpec((1, B_KV_LORA), const),
            pl.BlockSpec((B_KV_LORA, B_HEADS * LANES), const),
            pl.BlockSpec((B_KV_LORA, B_WIDTH), const),
            pl.BlockSpec((tm, LANES), pos),
            pl.BlockSpec((tm, LANES), pos),
        ],
        out_specs=a_specs + [
            pl.BlockSpec((tm, B_HEADS * LANES), row),
            pl.BlockSpec((tm, B_HEADS * LANES), row),
            pl.BlockSpec((tm, B_WIDTH), row),
        ],
        out_shape=a_shapes + [out(B_HEADS * LANES), out(B_HEADS * LANES), out(B_WIDTH)],
        scratch_shapes=[pltpu.VMEM((3 * A_HEADS // 2, tm, LANES), F32)],
        compiler_params=_cparams(("parallel",)),
        name="proj",
    )(x2, g_attn, w_in, g_q, w_q, g_kv, w_kb, w_vb, cos_t, sin_t)


def _dilated_kernel(q_ref, k_ref, v_ref, bias_ref, o_ref, lse_ref, *, seq_len, dilation, key_width, group):
    nblk = seq_len // A_QB
    lo = _lane_iota() < A_HEAD_DIM
    first_class = pl.program_id(1) * group

    def block(it, carry):
        c = it // nblk
        if nblk == 1:
            q0, ks, var = 0, 0, 0
        else:
            n = it % nblk
            q0 = pl.multiple_of(n * A_QB, A_QB)
            ks = pl.multiple_of(jnp.clip(q0 - 64, 0, seq_len - key_width), 64)
            var = jnp.where(n == 0, 0, jnp.where(n == nblk - 1, 2, 1))
        rows = pl.ds(q0, A_QB)
        keys = pl.ds(ks, key_width)
        if dilation == 1:
            out_rows = rows
        else:
            out_rows = pl.ds(first_class + c + dilation * q0, A_QB, stride=dilation)
        tiles = [slice(LANES * p, LANES * (p + 1)) for p in range(A_HEADS // 2)]
        scores = [lax.dot_general(q_ref[0, c, rows, tiles[hd // 2]], k_ref[0, c, keys, LANES * hd:LANES * (hd + 1)],
                                  _NT, preferred_element_type=F32) + bias_ref[var, hd] for hd in range(A_HEADS)]
        maxes = [jnp.max(s, axis=-1, keepdims=True) for s in scores]
        probs = [jnp.exp(s - m) for s, m in zip(scores, maxes)]
        dens = [jnp.sum(pr, axis=-1, keepdims=True) for pr in probs]
        pvs = [jnp.dot(pr.astype(BF16), v_ref[0, c, keys, tiles[hd // 2]], preferred_element_type=F32)
               for hd, pr in enumerate(probs)]
        for p in range(A_HEADS // 2):
            h0, h1 = 2 * p, 2 * p + 1
            o_ref[0, p, out_rows, :] = jnp.where(lo, pvs[h0] * (1.0 / dens[h0]), pvs[h1] * (1.0 / dens[h1]))
            lse_ref[0, p, out_rows, :] = jnp.where(lo, maxes[h0] + jnp.log(dens[h0]), maxes[h1] + jnp.log(dens[h1]))
        return carry

    lax.fori_loop(0, group * nblk, block, 0, unroll=2)


def _dilated_call(qa, ka, va, bias, batch, seq, dilation):
    r = dilation
    sl = seq // r
    kw = min(2 * A_QB, sl)
    group = max(1, min(r, (4 * A_QB) // sl))
    blk = lambda w: pl.BlockSpec((1, group, sl, w), lambda b, c: (b, c, 0, 0))
    nat = pl.BlockSpec((1, A_HEADS // 2, seq, LANES), lambda b, c: (b, 0, 0, 0))
    return pl.pallas_call(
        functools.partial(_dilated_kernel, seq_len=sl, dilation=r, key_width=kw, group=group),
        grid=(batch, r // group),
        in_specs=[blk(A_WIDTH), blk(2 * A_WIDTH), blk(A_WIDTH),
                  pl.BlockSpec(bias.shape, lambda b, c: (0, 0, 0, 0))],
        out_specs=[nat, nat],
        out_shape=[jax.ShapeDtypeStruct((batch, A_HEADS // 2, seq, LANES), F32)] * 2,
        compiler_params=_cparams(("parallel", "arbitrary")),
        name=f"dilated_r{r}",
    )(qa, ka, va, bias)


def _t5_bucket(rel):
    half = REL_BUCKETS // 2
    max_exact = half // 2
    n = np.abs(rel)
    large = max_exact + (np.log(np.maximum(n, 1) / max_exact)
                         / math.log(REL_MAX_DISTANCE / max_exact) * (half - max_exact)).astype(np.int32)
    large = np.minimum(large, half - 1)
    return (np.where(rel > 0, half, 0) + np.where(n < max_exact, n, large)).astype(np.int32)


def _dilated_bias(rel_bias, seq, dilation, half_steps):
    sl = seq // dilation
    kw = min(2 * A_QB, sl)
    offsets = [0] if sl == kw else [0, -half_steps, A_QB - kw]
    rel = np.stack([np.arange(kw)[None, :] + off - np.arange(A_QB)[:, None] for off in offsets])
    valid = np.abs(rel) <= half_steps
    bucket = np.where(valid, _t5_bucket(rel * dilation), REL_BUCKETS).astype(np.int32)
    onehot = (jnp.asarray(bucket)[..., None] == jnp.arange(REL_BUCKETS + 1, dtype=I32)).astype(F32)
    table = jnp.concatenate([rel_bias.astype(F32), jnp.full((1, A_HEADS), NEG_INF, F32)], axis=0)
    return jnp.einsum("vqkb,bh->vhqk", onehot, table, precision=lax.Precision.HIGHEST)


def _mla_kernel(q_ref, k_ref, v_ref, o_ref):
    lo = _lane_iota() < B_V_DIM
    sub = B_QB // B_SUB
    tiles = [slice(0, LANES), slice(LANES, 2 * LANES)]

    def block(i, carry):
        r0 = pl.multiple_of(i * B_QB, B_QB)
        rows = [pl.ds(r0 + B_SUB * j, B_SUB) for j in range(sub)]
        units = [(j, half) for j in range(sub) for half in range(2)]
        scores = [lax.dot_general(q_ref[0, rows[j], tiles[half]], k_ref[0, :, tiles[half]], _NT,
                                  preferred_element_type=F32) for j, half in units]
        maxes = [jnp.max(s, axis=-1, keepdims=True) for s in scores]
        probs = [jnp.exp2(s - m) for s, m in zip(scores, maxes)]
        dens = [jnp.sum(pr, axis=-1, keepdims=True) for pr in probs]
        outs = [jnp.dot(pr.astype(BF16), v_ref[0], preferred_element_type=F32) * (1.0 / den)
                for pr, den in zip(probs, dens)]
        for j in range(sub):
            o_ref[0, rows[j], :] = jnp.where(lo, outs[2 * j], outs[2 * j + 1]).astype(BF16)
        return carry

    lax.fori_loop(0, q_ref.shape[1] // B_QB, block, 0)


def _mla_call(qb, kb, vb, batch, seq):
    qb = qb.reshape(batch, seq, B_HEADS * LANES)
    kb = kb.reshape(batch, seq, B_HEADS * LANES)
    vb = vb.reshape(batch, seq, B_WIDTH)
    pair = lambda w: pl.BlockSpec((1, seq, w), lambda b, p: (b, 0, p))
    out = pl.pallas_call(
        _mla_kernel,
        grid=(batch, B_HEADS // 2),
        in_specs=[pair(2 * LANES), pair(2 * LANES), pair(LANES)],
        out_specs=pair(LANES),
        out_shape=jax.ShapeDtypeStruct((batch, seq, B_WIDTH), BF16),
        compiler_params=_cparams(("parallel", "parallel")),
        name="mla",
    )(qb, kb, vb)
    return out.reshape(batch * seq, B_WIDTH)


def _merge_patterns(o_refs, lse_refs):
    tiles = []
    for p in range(A_HEADS // 2):
        lses = [r[0, p] for r in lse_refs]
        top = functools.reduce(jnp.maximum, lses)
        es = [jnp.exp(l - top) for l in lses]
        inv = 1.0 / functools.reduce(jnp.add, es)
        tiles.append(functools.reduce(jnp.add, [e * inv * r[0, p] for e, r in zip(es, o_refs)]))
    return jnp.concatenate(tiles, axis=1)


def _mix_kernel(o1_ref, o4_ref, o16_ref, l1_ref, l4_ref, l16_ref, ob_ref, x_ref, ga_ref, gb_ref, wo_ref,
                gf_ref, wr_ref, br_ref, tri_ref, x1_ref, hp_ref, idx_ref, gate_ref, cnt_ref, carry_ref):
    i = pl.program_id(0)

    @pl.when(i == 0)
    def _():
        carry_ref[...] = jnp.zeros_like(carry_ref)

    oa = _merge_patterns((o1_ref, o4_ref, o16_ref), (l1_ref, l4_ref, l16_ref))
    a = _rms(oa, ga_ref[...]).astype(BF16)
    b = _rms(ob_ref[...].astype(F32), gb_ref[...]).astype(BF16)
    mix = (jnp.dot(a, wo_ref[0:A_WIDTH, :], preferred_element_type=F32)
           + jnp.dot(b, wo_ref[A_WIDTH:, :], preferred_element_type=F32))
    x1 = x_ref[...] + mix
    x1_ref[...] = x1
    h2 = _rms(x1, gf_ref[...])
    for c in range(ROW_SUBLANES):
        hp_ref[pl.ds(c, h2.shape[0], stride=ROW_SUBLANES), :] = h2[:, LANES * c:LANES * (c + 1)]

    lg = lax.dot_general(wr_ref[...], h2, _NT, preferred_element_type=F32,
                         precision=lax.Precision.HIGHEST) + br_ref[...]
    row = lax.broadcasted_iota(I32, lg.shape, 0)
    is_g = (row >= N_EXPERTS) & (row < N_EXPERTS + N_GROUPS)
    gl = jnp.where(is_g, lg, NEG_INF)
    ge = jnp.exp(gl - jnp.max(gl, axis=0, keepdims=True))
    gp = ge / jnp.sum(ge, axis=0, keepdims=True)
    g_gate = jnp.max(gp, axis=0, keepdims=True)
    g_idx = jnp.min(jnp.where(is_g & (gp == g_gate), row - N_EXPERTS, LANES), axis=0, keepdims=True)
    sel = (row >> 3) == g_idx
    el = jnp.where(sel, lg, NEG_INF)
    ee = jnp.exp(el - jnp.max(el, axis=0, keepdims=True))
    ep = jnp.where(sel, ee / jnp.sum(ee, axis=0, keepdims=True), -1.0)
    p1 = jnp.max(ep, axis=0, keepdims=True)
    i1 = jnp.min(jnp.where(ep == p1, row, LANES), axis=0, keepdims=True)
    ep2 = jnp.where(row == i1, -1.0, ep)
    p2 = jnp.max(ep2, axis=0, keepdims=True)
    i2 = jnp.min(jnp.where(sel & (ep2 == p2) & (row != i1), row, LANES), axis=0, keepdims=True)
    den = p1 + p2
    g1 = g_gate * p1 / den
    g2 = g_gate * p2 / den

    hit1 = row == i1
    hit2 = row == i2
    onehot = jnp.where(hit1 | hit2, 1.0, 0.0)
    before = jnp.dot(onehot.astype(BF16), tri_ref[...], preferred_element_type=F32) + carry_ref[...]
    r1 = jnp.sum(jnp.where(hit1, before, 0.0), axis=0, keepdims=True).astype(I32)
    r2 = jnp.sum(jnp.where(hit2, before, 0.0), axis=0, keepdims=True).astype(I32)
    carry_ref[...] += jnp.sum(onehot, axis=1, keepdims=True)

    row8 = lax.broadcasted_iota(I32, idx_ref.shape, 0)
    idx_ref[...] = jnp.where(row8 == 0, i1, jnp.where(row8 == 1, i2,
                             jnp.where(row8 == 2, r1, jnp.where(row8 == 3, r2, 0))))
    gate_ref[...] = jnp.where(row8 == 0, g1, jnp.where(row8 == 1, g2, 0.0))

    @pl.when(i == pl.num_programs(0) - 1)
    def _():
        cnt_ref[...] = jnp.broadcast_to(carry_ref[...], cnt_ref.shape).astype(I32)


def _mix_call(oas, lses, ob, x2, g_a, g_b, w_out, g_ffn, w_router, b_router, tri):
    t = x2.shape[0]
    tm = ROW_TILE
    nseq = oas[0].shape[2] // tm
    row = lambda i: (i, 0)
    const = lambda i: (0, 0)
    slab = pl.BlockSpec((1, A_HEADS // 2, tm, LANES), lambda i: (i // nseq, 0, i % nseq, 0))
    return pl.pallas_call(
        _mix_kernel,
        grid=(t // tm,),
        in_specs=[
            slab, slab, slab, slab, slab, slab,
            pl.BlockSpec((tm, B_WIDTH), row),
            pl.BlockSpec((tm, D_MODEL), row),
            pl.BlockSpec((1, A_WIDTH), const),
            pl.BlockSpec((1, B_WIDTH), const),
            pl.BlockSpec((D_MODEL, D_MODEL), const),
            pl.BlockSpec((1, D_MODEL), const),
            pl.BlockSpec((ROUTER_ROWS, D_MODEL), const),
            pl.BlockSpec((ROUTER_ROWS, 1), const),
            pl.BlockSpec((tm, tm), const),
        ],
        out_specs=[
            pl.BlockSpec((tm, D_MODEL), row),
            pl.BlockSpec((tm * ROW_SUBLANES, LANES), row),
            pl.BlockSpec((8, tm), lambda i: (0, i)),
            pl.BlockSpec((8, tm), lambda i: (0, i)),
            pl.BlockSpec((ROUTER_ROWS, LANES), const),
        ],
        out_shape=[
            jax.ShapeDtypeStruct((t, D_MODEL), F32),
            jax.ShapeDtypeStruct((t * ROW_SUBLANES, LANES), F32),
            jax.ShapeDtypeStruct((8, t), I32),
            jax.ShapeDtypeStruct((8, t), F32),
            jax.ShapeDtypeStruct((ROUTER_ROWS, LANES), I32),
        ],
        scratch_shapes=[pltpu.VMEM((ROUTER_ROWS, 1), F32)],
        compiler_params=_cparams(("arbitrary",)),
        name="mix_router",
    )(*oas, *lses, ob, x2, g_a, g_b, w_out, g_ffn, w_router, b_router, tri)


def _dest_kernel(idx_ref, pstart_ref, dest_ref):
    idx = idx_ref[...]
    row = lax.broadcasted_iota(I32, (ROUTER_ROWS, idx.shape[1]), 0)
    ps = pstart_ref[...]

    def slot(k):
        return jnp.sum(jnp.where(row == idx[k:k + 1, :], ps, 0), axis=0, keepdims=True) + idx[2 + k:3 + k, :]

    row8 = lax.broadcasted_iota(I32, idx.shape, 0)
    dest_ref[...] = jnp.where(row8 == 0, slot(0), jnp.where(row8 == 1, slot(1), 0))


def _dest_call(idx, pstart):
    t = idx.shape[1]
    tm = 4 * ROW_TILE
    return pl.pallas_call(
        _dest_kernel,
        grid=(t // tm,),
        in_specs=[pl.BlockSpec((8, tm), lambda i: (0, i)), pl.BlockSpec((ROUTER_ROWS, 1), lambda i: (0, 0))],
        out_specs=pl.BlockSpec((8, tm), lambda i: (0, i)),
        out_shape=jax.ShapeDtypeStruct((8, t), I32),
        compiler_params=_cparams(("parallel",)),
        name="dest_rows",
    )(idx, pstart)


def _dispatch_kernel(valid_ref, d0_ref, d1_ref, h_ref, buf_ref, zero_ref, sem, pad_sem):
    i = pl.program_id(0)
    tt = h_ref.shape[0] // ROW_SUBLANES
    n_blocks = valid_ref.shape[0]

    def pad_copy(j):
        n_pad = pl.multiple_of((MOE_BLK - valid_ref[j]) * ROW_SUBLANES, ROW_SUBLANES)
        first = pl.multiple_of((j * MOE_BLK + valid_ref[j]) * ROW_SUBLANES, ROW_SUBLANES)
        return pltpu.make_async_copy(zero_ref.at[pl.ds(0, n_pad)], buf_ref.at[pl.ds(first, n_pad)], pad_sem)

    def for_padded_blocks(fn):
        def body(j, c):
            @pl.when(valid_ref[j] < MOE_BLK)
            def _():
                fn(pad_copy(j))
            return c
        lax.fori_loop(0, n_blocks, body, 0)

    @pl.when(i == 0)
    def _():
        zero_ref[...] = jnp.zeros_like(zero_ref)
        for_padded_blocks(lambda cp: cp.start())

    def issue(g, c):
        base = pl.multiple_of(g * ISSUE_GROUP, ISSUE_GROUP)
        for j in range(ISSUE_GROUP):
            for prio, d_ref in enumerate((d0_ref, d1_ref)):
                pltpu.make_async_copy(h_ref.at[_row_tile(base + j)], buf_ref.at[_row_tile(d_ref[base + j])],
                                      sem).start(priority=prio)
        return c

    lax.fori_loop(0, tt // ISSUE_GROUP, issue, 0)
    for k in range(TOP_K):
        pltpu.make_async_copy(h_ref, buf_ref.at[pl.ds(0, tt * ROW_SUBLANES)], sem).wait()

    @pl.when(i == pl.num_programs(0) - 1)
    def _():
        for_padded_blocks(lambda cp: cp.wait())


def _dispatch_call(block_valid, dests, hp):
    t = hp.shape[0] // ROW_SUBLANES
    tt = 2 * ROW_TILE
    n_rows = block_valid.shape[0] * MOE_BLK
    return pl.pallas_call(
        _dispatch_kernel,
        grid_spec=pltpu.PrefetchScalarGridSpec(
            num_scalar_prefetch=1,
            grid=(t // tt,),
            in_specs=[
                pl.BlockSpec((tt,), lambda i, va: (i,), memory_space=pltpu.SMEM),
                pl.BlockSpec((tt,), lambda i, va: (i,), memory_space=pltpu.SMEM),
                pl.BlockSpec((tt * ROW_SUBLANES, LANES), lambda i, va: (i, 0)),
            ],
            out_specs=pl.BlockSpec(memory_space=pl.ANY),
            scratch_shapes=[pltpu.VMEM((MOE_BLK * ROW_SUBLANES, LANES), F32),
                            pltpu.SemaphoreType.DMA(()), pltpu.SemaphoreType.DMA(())],
        ),
        out_shape=jax.ShapeDtypeStruct((n_rows * ROW_SUBLANES, LANES), F32),
        compiler_params=_cparams(("arbitrary",), disable_bounds_checks=True, has_side_effects=True),
        name="dispatch",
    )(block_valid, *dests, hp)


def _expert_kernel(be_ref, new_ref, valid_ref, buf_ref, wg_ref, wu_ref, wd_ref, out_ref, wg_s, wu_s, wd_s):
    j = pl.program_id(0)
    del be_ref

    @pl.when(new_ref[j] == 1)
    def _():
        wg_s[...] = wg_ref[0].astype(BF16)
        wu_s[...] = wu_ref[0].astype(BF16)
        wd_s[...] = wd_ref[0].astype(BF16)

    n_valid = valid_ref[j]

    @pl.when(n_valid > 0)
    def _():
        blk = buf_ref.shape[0] // ROW_SUBLANES
        cols = [pl.ds(c, blk, stride=ROW_SUBLANES) for c in range(ROW_SUBLANES)]
        x = jnp.concatenate([buf_ref[rows, :].astype(BF16) for rows in cols], axis=1)
        g = jnp.dot(x, wg_s[...], preferred_element_type=F32)
        u = jnp.dot(x, wu_s[...], preferred_element_type=F32)
        hb = (g * jax.nn.sigmoid(g)) * u
        out = jnp.dot(hb.astype(BF16), wd_s[...], preferred_element_type=F32)
        for c, rows in enumerate(cols):
            out_ref[rows, :] = out[:, LANES * c:LANES * (c + 1)]

    @pl.when(n_valid == 0)
    def _():
        out_ref[...] = jnp.zeros_like(out_ref)


def _expert_call(block_expert, block_new, block_valid, buf, w_gate, w_up, w_down):
    nb = buf.shape[0] // (MOE_BLK * ROW_SUBLANES)
    wsel = lambda j, be, nw, va: (be[j], 0, 0)
    rows = pl.BlockSpec((MOE_BLK * ROW_SUBLANES, LANES), lambda j, be, nw, va: (j, 0))
    return pl.pallas_call(
        _expert_kernel,
        grid_spec=pltpu.PrefetchScalarGridSpec(
            num_scalar_prefetch=3,
            grid=(nb,),
            in_specs=[
                rows,
                pl.BlockSpec((1, D_MODEL, EXPERT_FF), wsel),
                pl.BlockSpec((1, D_MODEL, EXPERT_FF), wsel),
                pl.BlockSpec((1, EXPERT_FF, D_MODEL), wsel),
            ],
            out_specs=rows,
            scratch_shapes=[pltpu.VMEM((D_MODEL, EXPERT_FF), BF16),
                            pltpu.VMEM((D_MODEL, EXPERT_FF), BF16),
                            pltpu.VMEM((EXPERT_FF, D_MODEL), BF16)],
        ),
        out_shape=jax.ShapeDtypeStruct(buf.shape, F32),
        compiler_params=_cparams(("arbitrary",)),
        name="experts",
    )(block_expert, block_new, block_valid, buf, w_gate, w_up, w_down)


def _combine_kernel(d0_ref, d1_ref, d0_next_ref, d1_next_ref, x1_ref, gate_ref, gf_ref, eo_ref, o_ref,
                    rows_ref, sems):
    i = pl.program_id(0)
    tt = x1_ref.shape[0]

    def gather(d_refs, slot):
        def issue(g, c):
            base = pl.multiple_of(g * ISSUE_GROUP, ISSUE_GROUP)
            for j in range(ISSUE_GROUP):
                for k, d_ref in enumerate(d_refs):
                    pltpu.make_async_copy(eo_ref.at[_row_tile(d_ref[base + j])],
                                          rows_ref.at[slot, k, _row_tile(base + j)],
                                          sems.at[slot]).start(priority=k)
            return c

        lax.fori_loop(0, tt // ISSUE_GROUP, issue, 0)

    @pl.when(i == 0)
    def _():
        gather((d0_ref, d1_ref), 0)

    @pl.when(i + 1 < pl.num_programs(0))
    def _():
        gather((d0_next_ref, d1_next_ref), (i + 1) % 2)

    slot = i % 2
    for k in range(TOP_K):
        pltpu.make_async_copy(eo_ref.at[pl.ds(0, tt * ROW_SUBLANES)], rows_ref.at[slot, k], sems.at[slot]).wait()
    gate = gate_ref[...]

    def rows(k):
        return jnp.concatenate([rows_ref[slot, k, pl.ds(c, tt, stride=ROW_SUBLANES), :]
                                for c in range(ROW_SUBLANES)], axis=1)

    y = rows(0) * gate[:, 0:1] + rows(1) * gate[:, 1:2]
    o_ref[...] = _rms(x1_ref[...] + y, gf_ref[...])


def _combine_call(dests, x1, gates, g_final, expert_out):
    t = x1.shape[0]
    tt = ROW_TILE // 2
    last = t // tt - 1
    cur = pl.BlockSpec((tt,), lambda i: (i,), memory_space=pltpu.SMEM)
    nxt = pl.BlockSpec((tt,), lambda i: (jnp.minimum(i + 1, last),), memory_space=pltpu.SMEM)
    return pl.pallas_call(
        _combine_kernel,
        grid=(t // tt,),
        in_specs=[
            cur, cur, nxt, nxt,
            pl.BlockSpec((tt, D_MODEL), lambda i: (i, 0)),
            pl.BlockSpec((tt, TOP_K), lambda i: (i, 0)),
            pl.BlockSpec((1, D_MODEL), lambda i: (0, 0)),
            pl.BlockSpec(memory_space=pl.ANY),
        ],
        out_specs=pl.BlockSpec((tt, D_MODEL), lambda i: (i, 0)),
        out_shape=jax.ShapeDtypeStruct((t, D_MODEL), F32),
        scratch_shapes=[pltpu.VMEM((2, TOP_K, tt * ROW_SUBLANES, LANES), F32), pltpu.SemaphoreType.DMA((2,))],
        compiler_params=_cparams(("arbitrary",), disable_bounds_checks=True),
        name="combine",
    )(*dests, *dests, x1, gates, g_final, expert_out)


def _rope_tables(seq):
    half = B_QK_ROPE // 2
    inv_freq = ROPE_THETA ** (-(jnp.arange(half, dtype=F32) / half))
    ang = jnp.arange(seq, dtype=F32)[:, None] * inv_freq[None, :]
    cos, sin = jnp.cos(ang), jnp.sin(ang)
    z = jnp.zeros((seq, B_QK_NOPE), F32)
    z2 = jnp.zeros((seq, B_QK_ROPE), F32)
    return (jnp.concatenate([z, cos, cos, z2], axis=1), jnp.concatenate([z, -sin, sin, z2], axis=1))


def _swap_halves(w):
    half = w.shape[-1] // 2
    return jnp.concatenate([w[..., half:], w[..., :half]], axis=-1)


def _layout_weights(w_in, w_q_up, w_kv_up):
    d = w_in.shape[0]
    w_kr = w_in[:, 3 * A_WIDTH + B_Q_LORA + B_KV_LORA:]
    w_in_l = jnp.concatenate(
        [w_in[:, :3 * A_WIDTH + B_Q_LORA + B_KV_LORA], jnp.zeros((d, B_QK_NOPE), F32), w_kr, _swap_halves(w_kr)],
        axis=1).astype(BF16)
    wq = w_q_up.reshape(B_Q_LORA, B_HEADS, B_QK_NOPE + B_QK_ROPE)
    wq_l = jnp.concatenate([wq, _swap_halves(wq[..., B_QK_NOPE:])], axis=-1)
    wq_l = wq_l.reshape(B_Q_LORA, B_HEADS * LANES).astype(BF16)
    wkv = w_kv_up.reshape(B_KV_LORA, B_HEADS, B_QK_NOPE + B_V_DIM)
    wkb = jnp.concatenate([wkv[..., :B_QK_NOPE], jnp.zeros_like(wkv[..., :B_QK_NOPE])], axis=-1)
    wkb = wkb.reshape(B_KV_LORA, B_HEADS * LANES).astype(BF16)
    wvb = wkv[..., B_QK_NOPE:].reshape(B_KV_LORA, B_WIDTH).astype(BF16)
    return w_in_l, wq_l, wkb, wvb


def _block_plan(counts, n_blocks):
    padded = (counts + MOE_BLK - 1) // MOE_BLK * MOE_BLK
    ends = jnp.cumsum(padded)
    starts = ends - padded
    first_row = jnp.arange(n_blocks, dtype=I32) * MOE_BLK
    expert = jnp.minimum(jnp.sum(ends[None, :] <= first_row[:, None], axis=1), N_EXPERTS - 1).astype(I32)
    new = jnp.concatenate([jnp.ones((1,), I32), (expert[1:] != expert[:-1]).astype(I32)])
    valid = jnp.clip((starts + counts)[expert] - first_row, 0, MOE_BLK).astype(I32)
    return starts.astype(I32), expert, new, valid


def kernel(x, g_attn_norm, w_in, rel_bias, g_q_latent, w_q_up, g_kv_latent, w_kv_up, g_out_a, g_out_b, w_out,
           g_ffn_norm, w_router_group, b_router_group, w_router_expert, b_router_expert, w_gate, w_up, w_down,
           g_final):
    batch, seq, d = x.shape
    t = batch * seq
    assert g_attn_norm.shape[0] == 1 and d == D_MODEL and seq % ROW_TILE == 0
    cos_t, sin_t = _rope_tables(seq)
    tri = jnp.triu(jnp.ones((ROW_TILE, ROW_TILE), F32), 1).astype(BF16)
    n_blocks = t * TOP_K // MOE_BLK + N_EXPERTS
    x2 = x.reshape(t, d)
    row = lambda v: v.reshape(1, -1)

    w_in_l, wq_l, wkb_l, wvb_l = _layout_weights(w_in[0], w_q_up[0], w_kv_up[0])
    *qkv_a, qb, kb, vb = _proj_call(x2, row(g_attn_norm[0]), w_in_l, row(g_q_latent[0]), wq_l,
                                    row(g_kv_latent[0]), wkb_l, wvb_l, cos_t, sin_t, seq)
    oas, lses = [], []
    for pi, (window, dilation) in enumerate(DILATED_PATTERNS):
        bias = _dilated_bias(rel_bias, seq, dilation, window // (2 * dilation))
        o_p, lse_p = _dilated_call(*qkv_a[3 * pi:3 * pi + 3], bias, batch, seq, dilation)
        oas.append(o_p)
        lses.append(lse_p)
    ob = _mla_call(qb, kb, vb, batch, seq)

    pad = ROUTER_ROWS - N_EXPERTS - N_GROUPS
    w_router = jnp.concatenate([w_router_expert[0], w_router_group[0], jnp.zeros((d, pad), F32)], axis=1).T
    b_router = jnp.concatenate([b_router_expert[0], b_router_group[0], jnp.zeros((pad,), F32)])
    x1, hp, idx, gates, cnt = _mix_call(oas, lses, ob, x2, row(g_out_a[0]), row(g_out_b[0]), w_out[0].astype(BF16),
                                        row(g_ffn_norm[0]), w_router, b_router.reshape(-1, 1), tri)
    pstart, block_expert, block_new, block_valid = _block_plan(cnt[:N_EXPERTS, 0], n_blocks)
    pstart_col = jnp.concatenate([pstart, jnp.zeros((ROUTER_ROWS - N_EXPERTS,), I32)]).reshape(-1, 1)
    dest = _dest_call(idx, pstart_col)
    dests = (dest[0], dest[1])
    buf = _dispatch_call(block_valid, dests, hp)
    expert_out = _expert_call(block_expert, block_new, block_valid, buf, w_gate[0], w_up[0], w_down[0])
    return _combine_call(dests, x1, gates[:TOP_K].T, row(g_final), expert_out).reshape(batch, seq, d)
```

```python
import functools
import math

import numpy as np
import jax
import jax.numpy as jnp
from jax import lax
from jax.experimental import pallas as pl
from jax.experimental.pallas import tpu as pltpu

F32 = jnp.float32
BF16 = jnp.bfloat16
I32 = jnp.int32
U32 = jnp.uint32

D_MODEL = 1024
EPS = 1e-6
NEG_INF = -1e30
LANES = 128
ROW_SUBLANES = D_MODEL // LANES

A_HEADS = 8
A_HEAD_DIM = 64
A_WIDTH = 512
A_QKV_WIDTH = 3 * A_WIDTH
DILATED_PATTERNS = ((128, 1), (512, 4), (2048, 16))
REL_BUCKETS = 32
REL_MAX_DISTANCE = 1024
A_QB = 128

B_HEADS = 8
B_Q_LORA = 256
B_KV_LORA = 128
B_QK_NOPE = 64
B_QK_ROPE = 32
B_V_DIM = 64
B_WIDTH = 512
ROPE_THETA = 10000.0
B_SCALE = (B_QK_NOPE + B_QK_ROPE) ** -0.5
B_QB = 512
B_SUB = 256
LOG2E = math.log2(math.e)

N_GROUPS = 4
EXPERTS_PER_GROUP = 8
N_EXPERTS = 32
TOP_K = 2
EXPERT_FF = 256
MOE_BLK = 512
ROUTER_ROWS = 40

ROW_TILE = 512
ISSUE_GROUP = 8
PROJ_COLS = 2048

_NT = (((1,), (1,)), ((), ()))


def _cparams(semantics, vmem_mb=48, **kw):
    return pltpu.CompilerParams(dimension_semantics=semantics,
                                vmem_limit_bytes=vmem_mb * 1024 * 1024, **kw)


def _rms(x, g):
    return x * lax.rsqrt(jnp.mean(x * x, axis=-1, keepdims=True) + EPS) * g


def _lane_iota(rows=1):
    return lax.broadcasted_iota(I32, (rows, LANES), 1)


def _row_tile(r):
    return pl.ds(pl.multiple_of(r * ROW_SUBLANES, ROW_SUBLANES), ROW_SUBLANES)


def _proj_kernel(x_ref, g_ref, win_ref, gq_ref, wq_ref, gkv_ref, wkb_ref, wvb_ref, cos_ref, sin_ref, *refs):
    a_refs = refs[:len(DILATED_PATTERNS)]
    qb_ref, kb_ref, vb_ref, slab_ref = refs[len(a_refs):]
    tm = x_ref.shape[0]
    h = _rms(x_ref[...], g_ref[...]).astype(BF16)
    proj = jnp.dot(h, win_ref[...], preferred_element_type=F32)
    lo = _lane_iota() < A_HEAD_DIM
    n_slabs = A_QKV_WIDTH // LANES
    for s in range(n_slabs):
        slab_ref[s] = proj[:, LANES * s:LANES * (s + 1)]
    for qkv_out, (_, r) in zip(a_refs, DILATED_PATTERNS):
        for c in range(r):
            rows = pl.ds(c, tm // r, stride=r) if r > 1 else pl.ds(0, tm)
            for s in range(n_slabs):
                qkv_out[0, c, :, LANES * s:LANES * (s + 1)] = slab_ref[s, rows, :].astype(BF16)

    cos = cos_ref[...]
    sin = sin_ref[...]
    cq = _rms(proj[:, 1536:1792], gq_ref[...]).astype(BF16)
    q = jnp.dot(cq, wq_ref[...], preferred_element_type=F32)
    q_mul = (cos + jnp.where(lo, 1.0, 0.0)) * (B_SCALE * LOG2E)
    q_rot = sin * (B_SCALE * LOG2E)
    for hd in range(B_HEADS):
        t = q[:, LANES * hd:LANES * (hd + 1)]
        qb_ref[:, LANES * hd:LANES * (hd + 1)] = (t * q_mul + pltpu.roll(t, 96, 1) * q_rot).astype(BF16)

    ckv = _rms(proj[:, 1792:1920], gkv_ref[...]).astype(BF16)
    kr = proj[:, 1920:2048]
    kr = kr * cos + pltpu.roll(kr, 96, 1) * sin
    kn = jnp.dot(ckv, wkb_ref[...], preferred_element_type=F32)
    for hd in range(B_HEADS):
        kb_ref[:, LANES * hd:LANES * (hd + 1)] = (kn[:, LANES * hd:LANES * (hd + 1)] + kr).astype(BF16)
    vb_ref[...] = jnp.dot(ckv, wvb_ref[...], preferred_element_type=F32).astype(BF16)


def _proj_call(x2, g_attn, w_in, g_q, w_q, g_kv, w_kb, w_vb, cos_t, sin_t, seq):
    t = x2.shape[0]
    tm = ROW_TILE
    nseq = seq // tm
    row = lambda i: (i, 0)
    const = lambda i: (0, 0)
    pos = lambda i: (i % nseq, 0)
    out = lambda w: jax.ShapeDtypeStruct((t, w), BF16)
    a_specs, a_shapes = [], []
    for _, r in DILATED_PATTERNS:
        a_specs.append(pl.BlockSpec((1, r, tm // r, A_QKV_WIDTH), lambda i: (i // nseq, 0, i % nseq, 0)))
        a_shapes.append(jax.ShapeDtypeStruct((t // seq, r, seq // r, A_QKV_WIDTH), BF16))
    return pl.pallas_call(
        _proj_kernel,
        grid=(t // tm,),
        in_specs=[
            pl.BlockSpec((tm, D_MODEL), row),
            pl.BlockSpec((1, D_MODEL), const),
            pl.BlockSpec((D_MODEL, PROJ_COLS), const),
            pl.BlockSpec((1, B_Q_LORA), const),
            pl.BlockSpec((B_Q_LORA, B_HEADS * LANES), const),
            pl.BlockSpec((1, B_KV_LORA), const),
            pl.BlockSpec((B_KV_LORA, B_HEADS * LANES), const),
            pl.BlockSpec((B_KV_LORA, B_WIDTH), const),
            pl.BlockSpec((tm, LANES), pos),
            pl.BlockSpec((tm, LANES), pos),
        ],
        out_specs=a_specs + [
            pl.BlockSpec((tm, B_HEADS * LANES), row),
            pl.BlockSpec((tm, B_HEADS * LANES), row),
            pl.BlockSpec((tm, B_WIDTH), row),
        ],
        out_shape=a_shapes + [out(B_HEADS * LANES), out(B_HEADS * LANES), out(B_WIDTH)],
        scratch_shapes=[pltpu.VMEM((A_QKV_WIDTH // LANES, tm, LANES), F32)],
        compiler_params=_cparams(("parallel",)),
        name="proj",
    )(x2, g_attn, w_in, g_q, w_q, g_kv, w_kb, w_vb, cos_t, sin_t)


def _pack_bf16_pair(a, b):
    a_bits = lax.bitcast_convert_type(a.astype(BF16).astype(F32), U32) >> 16
    b_bits = lax.bitcast_convert_type(b.astype(BF16).astype(F32), U32) & jnp.uint32(0xFFFF0000)
    return a_bits | b_bits


def _unpack_bf16_pair(w):
    return (lax.bitcast_convert_type(w << 16, F32), lax.bitcast_convert_type(w & jnp.uint32(0xFFFF0000), F32))


def _dilated_kernel(qkv_ref, bias_ref, o_ref, lse_ref, *, seq_len, dilation, key_width, group):
    nblk = seq_len // A_QB
    lo = _lane_iota() < A_HEAD_DIM
    first_class = pl.program_id(1) * group
    pairs = A_HEADS // 2

    def block(it, carry):
        c = it // nblk
        if nblk == 1:
            q0, ks, var = 0, 0, 0
        else:
            n = it % nblk
            q0 = pl.multiple_of(n * A_QB, A_QB)
            ks = pl.multiple_of(jnp.clip(q0 - 64, 0, seq_len - key_width), 64)
            var = jnp.where(n == 0, 0, jnp.where(n == nblk - 1, 2, 1))
        rows = pl.ds(q0, A_QB)
        keys = pl.ds(ks, key_width)
        if dilation == 1:
            out_rows = rows
        else:
            out_rows = pl.ds(first_class + c + dilation * q0, A_QB, stride=dilation)
        q_tiles = [qkv_ref[0, c, rows, LANES * p:LANES * (p + 1)] for p in range(pairs)]
        k_tiles = [qkv_ref[0, c, keys, A_WIDTH + LANES * p:A_WIDTH + LANES * (p + 1)] for p in range(pairs)]
        v_tiles = [qkv_ref[0, c, keys, 2 * A_WIDTH + LANES * p:2 * A_WIDTH + LANES * (p + 1)] for p in range(pairs)]
        zero = jnp.zeros((), BF16)
        k_heads = [jnp.where(lo, k_tiles[hd // 2], zero) if hd % 2 == 0 else jnp.where(lo, zero, k_tiles[hd // 2])
                   for hd in range(A_HEADS)]
        scores = [lax.dot_general(q_tiles[hd // 2], k_heads[hd], _NT, preferred_element_type=F32) + bias_ref[var, hd]
                  for hd in range(A_HEADS)]
        maxes = [jnp.max(s, axis=-1, keepdims=True) for s in scores]
        probs = [jnp.exp(s - m) for s, m in zip(scores, maxes)]
        dens = [jnp.sum(pr, axis=-1, keepdims=True) for pr in probs]
        pvs = [jnp.dot(pr.astype(BF16), v_tiles[hd // 2], preferred_element_type=F32) for hd, pr in enumerate(probs)]
        outs = []
        for p in range(pairs):
            h0, h1 = 2 * p, 2 * p + 1
            outs.append(jnp.where(lo, pvs[h0] * (1.0 / dens[h0]), pvs[h1] * (1.0 / dens[h1])))
            lse_ref[0, p, out_rows, :] = jnp.where(lo, maxes[h0] + jnp.log(dens[h0]), maxes[h1] + jnp.log(dens[h1]))
        for j in range(pairs // 2):
            o_ref[0, j, out_rows, :] = _pack_bf16_pair(outs[2 * j], outs[2 * j + 1])
        return carry

    lax.fori_loop(0, group * nblk, block, 0, unroll=4)


def _dilated_call(qkv, bias, batch, seq, dilation):
    r = dilation
    sl = seq // r
    kw = min(2 * A_QB, sl)
    group = max(1, min(r, (4 * A_QB) // sl))
    pairs = A_HEADS // 2
    nat = lambda n: pl.BlockSpec((1, n, seq, LANES), lambda b, c: (b, 0, 0, 0))
    return pl.pallas_call(
        functools.partial(_dilated_kernel, seq_len=sl, dilation=r, key_width=kw, group=group),
        grid=(batch, r // group),
        in_specs=[pl.BlockSpec((1, group, sl, A_QKV_WIDTH), lambda b, c: (b, c, 0, 0)),
                  pl.BlockSpec(bias.shape, lambda b, c: (0, 0, 0, 0))],
        out_specs=[nat(pairs // 2), nat(pairs)],
        out_shape=[jax.ShapeDtypeStruct((batch, pairs // 2, seq, LANES), U32),
                   jax.ShapeDtypeStruct((batch, pairs, seq, LANES), F32)],
        compiler_params=_cparams(("parallel", "arbitrary")),
        name=f"dilated_r{r}",
    )(qkv, bias)


def _t5_bucket(rel):
    half = REL_BUCKETS // 2
    max_exact = half // 2
    n = np.abs(rel)
    large = max_exact + (np.log(np.maximum(n, 1) / max_exact)
                         / math.log(REL_MAX_DISTANCE / max_exact) * (half - max_exact)).astype(np.int32)
    large = np.minimum(large, half - 1)
    return (np.where(rel > 0, half, 0) + np.where(n < max_exact, n, large)).astype(np.int32)


def _dilated_bias(rel_bias, seq, dilation, half_steps):
    sl = seq // dilation
    kw = min(2 * A_QB, sl)
    offsets = [0] if sl == kw else [0, -half_steps, A_QB - kw]
    rel = np.stack([np.arange(kw)[None, :] + off - np.arange(A_QB)[:, None] for off in offsets])
    valid = np.abs(rel) <= half_steps
    bucket = np.where(valid, _t5_bucket(rel * dilation), REL_BUCKETS).astype(np.int32)
    onehot = (jnp.asarray(bucket)[..., None] == jnp.arange(REL_BUCKETS + 1, dtype=I32)).astype(F32)
    table = jnp.concatenate([rel_bias.astype(F32), jnp.full((1, A_HEADS), NEG_INF, F32)], axis=0)
    return jnp.einsum("vqkb,bh->vhqk", onehot, table, precision=lax.Precision.HIGHEST)


def _mla_kernel(q_ref, k_ref, v_ref, o_ref):
    lo = _lane_iota() < B_V_DIM
    sub = B_QB // B_SUB
    tiles = [slice(0, LANES), slice(LANES, 2 * LANES)]

    def block(i, carry):
        r0 = pl.multiple_of(i * B_QB, B_QB)
        rows = [pl.ds(r0 + B_SUB * j, B_SUB) for j in range(sub)]
        units = [(j, half) for j in range(sub) for half in range(2)]
        scores = [lax.dot_general(q_ref[0, rows[j], tiles[half]], k_ref[0, :, tiles[half]], _NT,
                                  preferred_element_type=F32) for j, half in units]
        maxes = [jnp.max(s, axis=-1, keepdims=True) for s in scores]
        probs = [jnp.exp2(s - m) for s, m in zip(scores, maxes)]
        dens = [jnp.sum(pr, axis=-1, keepdims=True) for pr in probs]
        outs = [jnp.dot(pr.astype(BF16), v_ref[0], preferred_element_type=F32) * (1.0 / den)
                for pr, den in zip(probs, dens)]
        for j in range(sub):
            o_ref[0, rows[j], :] = jnp.where(lo, outs[2 * j], outs[2 * j + 1]).astype(BF16)
        return carry

    lax.fori_loop(0, q_ref.shape[1] // B_QB, block, 0)


def _mla_call(qb, kb, vb, batch, seq):
    qb = qb.reshape(batch, seq, B_HEADS * LANES)
    kb = kb.reshape(batch, seq, B_HEADS * LANES)
    vb = vb.reshape(batch, seq, B_WIDTH)
    pair = lambda w: pl.BlockSpec((1, seq, w), lambda b, p: (b, 0, p))
    out = pl.pallas_call(
        _mla_kernel,
        grid=(batch, B_HEADS // 2),
        in_specs=[pair(2 * LANES), pair(2 * LANES), pair(LANES)],
        out_specs=pair(LANES),
        out_shape=jax.ShapeDtypeStruct((batch, seq, B_WIDTH), BF16),
        compiler_params=_cparams(("parallel", "parallel")),
        name="mla",
    )(qb, kb, vb)
    return out.reshape(batch * seq, B_WIDTH)


def _merge_patterns(o_refs, lse_refs):
    tiles = []
    for j in range(A_HEADS // 4):
        outs = [_unpack_bf16_pair(r[0, j]) for r in o_refs]
        for half in range(2):
            lses = [r[0, 2 * j + half] for r in lse_refs]
            top = functools.reduce(jnp.maximum, lses)
            es = [jnp.exp(l - top) for l in lses]
            inv = 1.0 / functools.reduce(jnp.add, es)
            tiles.append(functools.reduce(jnp.add, [e * inv * o[half] for e, o in zip(es, outs)]))
    return jnp.concatenate(tiles, axis=1)


def _mix_kernel(o1_ref, o4_ref, o16_ref, l1_ref, l4_ref, l16_ref, ob_ref, x_ref, ga_ref, gb_ref, wo_ref,
                gf_ref, wr_ref, br_ref, tri_ref, x1_ref, hp_ref, idx_ref, gate_ref, cnt_ref, carry_ref):
    i = pl.program_id(0)

    @pl.when(i == 0)
    def _():
        carry_ref[...] = jnp.zeros_like(carry_ref)

    oa = _merge_patterns((o1_ref, o4_ref, o16_ref), (l1_ref, l4_ref, l16_ref))
    a = _rms(oa, ga_ref[...]).astype(BF16)
    b = _rms(ob_ref[...].astype(F32), gb_ref[...]).astype(BF16)
    mix = (jnp.dot(a, wo_ref[0:A_WIDTH, :], preferred_element_type=F32)
           + jnp.dot(b, wo_ref[A_WIDTH:, :], preferred_element_type=F32))
    x1 = x_ref[...] + mix
    x1_ref[...] = x1
    h2 = _rms(x1, gf_ref[...])
    for c in range(ROW_SUBLANES):
        hp_ref[pl.ds(c, h2.shape[0], stride=ROW_SUBLANES), :] = h2[:, LANES * c:LANES * (c + 1)]

    lg = lax.dot_general(wr_ref[...], h2, _NT, preferred_element_type=F32,
                         precision=lax.Precision.HIGHEST) + br_ref[...]
    row = lax.broadcasted_iota(I32, lg.shape, 0)
    is_g = (row >= N_EXPERTS) & (row < N_EXPERTS + N_GROUPS)
    gl = jnp.where(is_g, lg, NEG_INF)
    ge = jnp.exp(gl - jnp.max(gl, axis=0, keepdims=True))
    gp = ge / jnp.sum(ge, axis=0, keepdims=True)
    g_gate = jnp.max(gp, axis=0, keepdims=True)
    g_idx = jnp.min(jnp.where(is_g & (gp == g_gate), row - N_EXPERTS, LANES), axis=0, keepdims=True)
    sel = (row >> 3) == g_idx
    el = jnp.where(sel, lg, NEG_INF)
    ee = jnp.exp(el - jnp.max(el, axis=0, keepdims=True))
    ep = jnp.where(sel, ee / jnp.sum(ee, axis=0, keepdims=True), -1.0)
    p1 = jnp.max(ep, axis=0, keepdims=True)
    i1 = jnp.min(jnp.where(ep == p1, row, LANES), axis=0, keepdims=True)
    ep2 = jnp.where(row == i1, -1.0, ep)
    p2 = jnp.max(ep2, axis=0, keepdims=True)
    i2 = jnp.min(jnp.where(sel & (ep2 == p2) & (row != i1), row, LANES), axis=0, keepdims=True)
    den = p1 + p2
    g1 = g_gate * p1 / den
    g2 = g_gate * p2 / den

    hit1 = row == i1
    hit2 = row == i2
    onehot = jnp.where(hit1 | hit2, 1.0, 0.0)
    before = jnp.dot(onehot.astype(BF16), tri_ref[...], preferred_element_type=F32) + carry_ref[...]
    r1 = jnp.sum(jnp.where(hit1, before, 0.0), axis=0, keepdims=True).astype(I32)
    r2 = jnp.sum(jnp.where(hit2, before, 0.0), axis=0, keepdims=True).astype(I32)
    carry_ref[...] += jnp.sum(onehot, axis=1, keepdims=True)

    row8 = lax.broadcasted_iota(I32, idx_ref.shape, 0)
    idx_ref[...] = jnp.where(row8 == 0, i1, jnp.where(row8 == 1, i2,
                             jnp.where(row8 == 2, r1, jnp.where(row8 == 3, r2, 0))))
    gate_ref[...] = jnp.where(row8 == 0, g1, jnp.where(row8 == 1, g2, 0.0))

    @pl.when(i == pl.num_programs(0) - 1)
    def _():
        cnt_ref[...] = jnp.broadcast_to(carry_ref[...], cnt_ref.shape).astype(I32)


def _mix_call(oas, lses, ob, x2, g_a, g_b, w_out, g_ffn, w_router, b_router, tri):
    t = x2.shape[0]
    tm = ROW_TILE
    nseq = oas[0].shape[2] // tm
    row = lambda i: (i, 0)
    const = lambda i: (0, 0)
    slab = lambda n: pl.BlockSpec((1, n, tm, LANES), lambda i: (i // nseq, 0, i % nseq, 0))
    o_slab, lse_slab = slab(A_HEADS // 4), slab(A_HEADS // 2)
    return pl.pallas_call(
        _mix_kernel,
        grid=(t // tm,),
        in_specs=[
            o_slab, o_slab, o_slab, lse_slab, lse_slab, lse_slab,
            pl.BlockSpec((tm, B_WIDTH), row),
            pl.BlockSpec((tm, D_MODEL), row),
            pl.BlockSpec((1, A_WIDTH), const),
            pl.BlockSpec((1, B_WIDTH), const),
            pl.BlockSpec((D_MODEL, D_MODEL), const),
            pl.BlockSpec((1, D_MODEL), const),
            pl.BlockSpec((ROUTER_ROWS, D_MODEL), const),
            pl.BlockSpec((ROUTER_ROWS, 1), const),
            pl.BlockSpec((tm, tm), const),
        ],
        out_specs=[
            pl.BlockSpec((tm, D_MODEL), row),
            pl.BlockSpec((tm * ROW_SUBLANES, LANES), row),
            pl.BlockSpec((8, tm), lambda i: (0, i)),
            pl.BlockSpec((8, tm), lambda i: (0, i)),
            pl.BlockSpec((ROUTER_ROWS, LANES), const),
        ],
        out_shape=[
            jax.ShapeDtypeStruct((t, D_MODEL), F32),
            jax.ShapeDtypeStruct((t * ROW_SUBLANES, LANES), F32),
            jax.ShapeDtypeStruct((8, t), I32),
            jax.ShapeDtypeStruct((8, t), F32),
            jax.ShapeDtypeStruct((ROUTER_ROWS, LANES), I32),
        ],
        scratch_shapes=[pltpu.VMEM((ROUTER_ROWS, 1), F32)],
        compiler_params=_cparams(("arbitrary",)),
        name="mix_router",
    )(*oas, *lses, ob, x2, g_a, g_b, w_out, g_ffn, w_router, b_router, tri)


def _dest_kernel(idx_ref, pstart_ref, dest_ref):
    idx = idx_ref[...]
    row = lax.broadcasted_iota(I32, (ROUTER_ROWS, idx.shape[1]), 0)
    ps = pstart_ref[...]

    def slot(k):
        return jnp.sum(jnp.where(row == idx[k:k + 1, :], ps, 0), axis=0, keepdims=True) + idx[2 + k:3 + k, :]

    row8 = lax.broadcasted_iota(I32, idx.shape, 0)
    dest_ref[...] = jnp.where(row8 == 0, slot(0), jnp.where(row8 == 1, slot(1), 0))


def _dest_call(idx, pstart):
    t = idx.shape[1]
    tm = 4 * ROW_TILE
    return pl.pallas_call(
        _dest_kernel,
        grid=(t // tm,),
        in_specs=[pl.BlockSpec((8, tm), lambda i: (0, i)), pl.BlockSpec((ROUTER_ROWS, 1), lambda i: (0, 0))],
        out_specs=pl.BlockSpec((8, tm), lambda i: (0, i)),
        out_shape=jax.ShapeDtypeStruct((8, t), I32),
        compiler_params=_cparams(("parallel",)),
        name="dest_rows",
    )(idx, pstart)


def _dispatch_kernel(valid_ref, d0_ref, d1_ref, h_ref, buf_ref, zero_ref, sem, pad_sem):
    i = pl.program_id(0)
    tt = h_ref.shape[0] // ROW_SUBLANES
    n_blocks = valid_ref.shape[0]

    def pad_copy(j):
        n_pad = pl.multiple_of((MOE_BLK - valid_ref[j]) * ROW_SUBLANES, ROW_SUBLANES)
        first = pl.multiple_of((j * MOE_BLK + valid_ref[j]) * ROW_SUBLANES, ROW_SUBLANES)
        return pltpu.make_async_copy(zero_ref.at[pl.ds(0, n_pad)], buf_ref.at[pl.ds(first, n_pad)], pad_sem)

    def for_padded_blocks(fn):
        def body(j, c):
            @pl.when(valid_ref[j] < MOE_BLK)
            def _():
                fn(pad_copy(j))
            return c
        lax.fori_loop(0, n_blocks, body, 0)

    @pl.when(i == 0)
    def _():
        zero_ref[...] = jnp.zeros_like(zero_ref)
        for_padded_blocks(lambda cp: cp.start())

    def issue(g, c):
        base = pl.multiple_of(g * ISSUE_GROUP, ISSUE_GROUP)
        for j in range(ISSUE_GROUP):
            for prio, d_ref in enumerate((d0_ref, d1_ref)):
                pltpu.make_async_copy(h_ref.at[_row_tile(base + j)], buf_ref.at[_row_tile(d_ref[base + j])],
                                      sem).start(priority=prio)
        return c

    lax.fori_loop(0, tt // ISSUE_GROUP, issue, 0)
    for k in range(TOP_K):
        pltpu.make_async_copy(h_ref, buf_ref.at[pl.ds(0, tt * ROW_SUBLANES)], sem).wait()

    @pl.when(i == pl.num_programs(0) - 1)
    def _():
        for_padded_blocks(lambda cp: cp.wait())


def _dispatch_call(block_valid, dests, hp):
    t = hp.shape[0] // ROW_SUBLANES
    tt = 2 * ROW_TILE
    n_rows = block_valid.shape[0] * MOE_BLK
    return pl.pallas_call(
        _dispatch_kernel,
        grid_spec=pltpu.PrefetchScalarGridSpec(
            num_scalar_prefetch=1,
            grid=(t // tt,),
            in_specs=[
                pl.BlockSpec((tt,), lambda i, va: (i,), memory_space=pltpu.SMEM),
                pl.BlockSpec((tt,), lambda i, va: (i,), memory_space=pltpu.SMEM),
                pl.BlockSpec((tt * ROW_SUBLANES, LANES), lambda i, va: (i, 0)),
            ],
            out_specs=pl.BlockSpec(memory_space=pl.ANY),
            scratch_shapes=[pltpu.VMEM((MOE_BLK * ROW_SUBLANES, LANES), F32),
                            pltpu.SemaphoreType.DMA(()), pltpu.SemaphoreType.DMA(())],
        ),
        out_shape=jax.ShapeDtypeStruct((n_rows * ROW_SUBLANES, LANES), F32),
        compiler_params=_cparams(("arbitrary",), disable_bounds_checks=True, has_side_effects=True),
        name="dispatch",
    )(block_valid, *dests, hp)


def _expert_kernel(be_ref, new_ref, valid_ref, buf_ref, wg_ref, wu_ref, wd_ref, out_ref, wg_s, wu_s, wd_s):
    j = pl.program_id(0)
    del be_ref

    @pl.when(new_ref[j] == 1)
    def _():
        wg_s[...] = wg_ref[0].astype(BF16)
        wu_s[...] = wu_ref[0].astype(BF16)
        wd_s[...] = wd_ref[0].astype(BF16)

    n_valid = valid_ref[j]

    @pl.when(n_valid > 0)
    def _():
        blk = buf_ref.shape[0] // ROW_SUBLANES
        cols = [pl.ds(c, blk, stride=ROW_SUBLANES) for c in range(ROW_SUBLANES)]
        x = jnp.concatenate([buf_ref[rows, :].astype(BF16) for rows in cols], axis=1)
        g = jnp.dot(x, wg_s[...], preferred_element_type=F32)
        u = jnp.dot(x, wu_s[...], preferred_element_type=F32)
        hb = (g * jax.nn.sigmoid(g)) * u
        out = jnp.dot(hb.astype(BF16), wd_s[...], preferred_element_type=F32)
        for c, rows in enumerate(cols):
            out_ref[rows, :] = out[:, LANES * c:LANES * (c + 1)]

    @pl.when(n_valid == 0)
    def _():
        out_ref[...] = jnp.zeros_like(out_ref)


def _expert_call(block_expert, block_new, block_valid, buf, w_gate, w_up, w_down):
    nb = buf.shape[0] // (MOE_BLK * ROW_SUBLANES)
    wsel = lambda j, be, nw, va: (be[j], 0, 0)
    rows = pl.BlockSpec((MOE_BLK * ROW_SUBLANES, LANES), lambda j, be, nw, va: (j, 0))
    return pl.pallas_call(
        _expert_kernel,
        grid_spec=pltpu.PrefetchScalarGridSpec(
            num_scalar_prefetch=3,
            grid=(nb,),
            in_specs=[
                rows,
                pl.BlockSpec((1, D_MODEL, EXPERT_FF), wsel),
                pl.BlockSpec((1, D_MODEL, EXPERT_FF), wsel),
                pl.BlockSpec((1, EXPERT_FF, D_MODEL), wsel),
            ],
            out_specs=rows,
            scratch_shapes=[pltpu.VMEM((D_MODEL, EXPERT_FF), BF16),
                            pltpu.VMEM((D_MODEL, EXPERT_FF), BF16),
                            pltpu.VMEM((EXPERT_FF, D_MODEL), BF16)],
        ),
        out_shape=jax.ShapeDtypeStruct(buf.shape, F32),
        compiler_params=_cparams(("arbitrary",)),
        name="experts",
    )(block_expert, block_new, block_valid, buf, w_gate, w_up, w_down)


def _combine_kernel(d0_ref, d1_ref, d0_next_ref, d1_next_ref, x1_ref, gate_ref, gf_ref, eo_ref, o_ref,
                    rows_ref, sems):
    i = pl.program_id(0)
    tt = x1_ref.shape[0]

    def gather(d_refs, slot):
        def issue(g, c):
            base = pl.multiple_of(g * ISSUE_GROUP, ISSUE_GROUP)
            for j in range(ISSUE_GROUP):
                for k, d_ref in enumerate(d_refs):
                    pltpu.make_async_copy(eo_ref.at[_row_tile(d_ref[base + j])],
                                          rows_ref.at[slot, k, _row_tile(base + j)],
                                          sems.at[slot]).start(priority=k)
            return c

        lax.fori_loop(0, tt // ISSUE_GROUP, issue, 0)

    @pl.when(i == 0)
    def _():
        gather((d0_ref, d1_ref), 0)

    @pl.when(i + 1 < pl.num_programs(0))
    def _():
        gather((d0_next_ref, d1_next_ref), (i + 1) % 2)

    slot = i % 2
    for k in range(TOP_K):
        pltpu.make_async_copy(eo_ref.at[pl.ds(0, tt * ROW_SUBLANES)], rows_ref.at[slot, k], sems.at[slot]).wait()
    gate = gate_ref[...]

    def rows(k):
        return jnp.concatenate([rows_ref[slot, k, pl.ds(c, tt, stride=ROW_SUBLANES), :]
                                for c in range(ROW_SUBLANES)], axis=1)

    y = rows(0) * gate[:, 0:1] + rows(1) * gate[:, 1:2]
    o_ref[...] = _rms(x1_ref[...] + y, gf_ref[...])


def _combine_call(dests, x1, gates, g_final, expert_out):
    t = x1.shape[0]
    tt = ROW_TILE // 2
    last = t // tt - 1
    cur = pl.BlockSpec((tt,), lambda i: (i,), memory_space=pltpu.SMEM)
    nxt = pl.BlockSpec((tt,), lambda i: (jnp.minimum(i + 1, last),), memory_space=pltpu.SMEM)
    return pl.pallas_call(
        _combine_kernel,
        grid=(t // tt,),
        in_specs=[
            cur, cur, nxt, nxt,
            pl.BlockSpec((tt, D_MODEL), lambda i: (i, 0)),
            pl.BlockSpec((tt, TOP_K), lambda i: (i, 0)),
            pl.BlockSpec((1, D_MODEL), lambda i: (0, 0)),
            pl.BlockSpec(memory_space=pl.ANY),
        ],
        out_specs=pl.BlockSpec((tt, D_MODEL), lambda i: (i, 0)),
        out_shape=jax.ShapeDtypeStruct((t, D_MODEL), F32),
        scratch_shapes=[pltpu.VMEM((2, TOP_K, tt * ROW_SUBLANES, LANES), F32), pltpu.SemaphoreType.DMA((2,))],
        compiler_params=_cparams(("arbitrary",), disable_bounds_checks=True),
        name="combine",
    )(*dests, *dests, x1, gates, g_final, expert_out)


def _rope_tables(seq):
    half = B_QK_ROPE // 2
    inv_freq = ROPE_THETA ** (-(jnp.arange(half, dtype=F32) / half))
    ang = jnp.arange(seq, dtype=F32)[:, None] * inv_freq[None, :]
    cos, sin = jnp.cos(ang), jnp.sin(ang)
    z = jnp.zeros((seq, B_QK_NOPE), F32)
    z2 = jnp.zeros((seq, B_QK_ROPE), F32)
    return (jnp.concatenate([z, cos, cos, z2], axis=1), jnp.concatenate([z, -sin, sin, z2], axis=1))


def _swap_halves(w):
    half = w.shape[-1] // 2
    return jnp.concatenate([w[..., half:], w[..., :half]], axis=-1)


def _layout_weights(w_in, w_q_up, w_kv_up):
    d = w_in.shape[0]
    w_kr = w_in[:, 3 * A_WIDTH + B_Q_LORA + B_KV_LORA:]
    w_in_l = jnp.concatenate(
        [w_in[:, :A_WIDTH] * (A_HEAD_DIM ** -0.5), w_in[:, A_WIDTH:3 * A_WIDTH + B_Q_LORA + B_KV_LORA],
         jnp.zeros((d, B_QK_NOPE), F32), w_kr, _swap_halves(w_kr)], axis=1).astype(BF16)
    wq = w_q_up.reshape(B_Q_LORA, B_HEADS, B_QK_NOPE + B_QK_ROPE)
    wq_l = jnp.concatenate([wq, _swap_halves(wq[..., B_QK_NOPE:])], axis=-1)
    wq_l = wq_l.reshape(B_Q_LORA, B_HEADS * LANES).astype(BF16)
    wkv = w_kv_up.reshape(B_KV_LORA, B_HEADS, B_QK_NOPE + B_V_DIM)
    wkb = jnp.concatenate([wkv[..., :B_QK_NOPE], jnp.zeros_like(wkv[..., :B_QK_NOPE])], axis=-1)
    wkb = wkb.reshape(B_KV_LORA, B_HEADS * LANES).astype(BF16)
    wvb = wkv[..., B_QK_NOPE:].reshape(B_KV_LORA, B_WIDTH).astype(BF16)
    return w_in_l, wq_l, wkb, wvb


def _block_plan(counts, n_blocks):
    padded = (counts + MOE_BLK - 1) // MOE_BLK * MOE_BLK
    ends = jnp.cumsum(padded)
    starts = ends - padded
    first_row = jnp.arange(n_blocks, dtype=I32) * MOE_BLK
    expert = jnp.minimum(jnp.sum(ends[None, :] <= first_row[:, None], axis=1), N_EXPERTS - 1).astype(I32)
    new = jnp.concatenate([jnp.ones((1,), I32), (expert[1:] != expert[:-1]).astype(I32)])
    valid = jnp.clip((starts + counts)[expert] - first_row, 0, MOE_BLK).astype(I32)
    return starts.astype(I32), expert, new, valid


def kernel(x, g_attn_norm, w_in, rel_bias, g_q_latent, w_q_up, g_kv_latent, w_kv_up, g_out_a, g_out_b, w_out,
           g_ffn_norm, w_router_group, b_router_group, w_router_expert, b_router_expert, w_gate, w_up, w_down,
           g_final):
    batch, seq, d = x.shape
    t = batch * seq
    assert g_attn_norm.shape[0] == 1 and d == D_MODEL and seq % ROW_TILE == 0
    cos_t, sin_t = _rope_tables(seq)
    tri = jnp.triu(jnp.ones((ROW_TILE, ROW_TILE), F32), 1).astype(BF16)
    n_blocks = t * TOP_K // MOE_BLK + N_EXPERTS
    x2 = x.reshape(t, d)
    row = lambda v: v.reshape(1, -1)

    w_in_l, wq_l, wkb_l, wvb_l = _layout_weights(w_in[0], w_q_up[0], w_kv_up[0])
    *qkv_a, qb, kb, vb = _proj_call(x2, row(g_attn_norm[0]), w_in_l, row(g_q_latent[0]), wq_l,
                                    row(g_kv_latent[0]), wkb_l, wvb_l, cos_t, sin_t, seq)
    oas, lses = [], []
    for pi, (window, dilation) in enumerate(DILATED_PATTERNS):
        bias = _dilated_bias(rel_bias, seq, dilation, window // (2 * dilation))
        o_p, lse_p = _dilated_call(qkv_a[pi], bias, batch, seq, dilation)
        oas.append(o_p)
        lses.append(lse_p)
    ob = _mla_call(qb, kb, vb, batch, seq)

    pad = ROUTER_ROWS - N_EXPERTS - N_GROUPS
    w_router = jnp.concatenate([w_router_expert[0], w_router_group[0], jnp.zeros((d, pad), F32)], axis=1).T
    b_router = jnp.concatenate([b_router_expert[0], b_router_group[0], jnp.zeros((pad,), F32)])
    x1, hp, idx, gates, cnt = _mix_call(oas, lses, ob, x2, row(g_out_a[0]), row(g_out_b[0]), w_out[0].astype(BF16),
                                        row(g_ffn_norm[0]), w_router, b_router.reshape(-1, 1), tri)
    pstart, block_expert, block_new, block_valid = _block_plan(cnt[:N_EXPERTS, 0], n_blocks)
    pstart_col = jnp.concatenate([pstart, jnp.zeros((ROUTER_ROWS - N_EXPERTS,), I32)]).reshape(-1, 1)
    dest = _dest_call(idx, pstart_col)
    dests = (dest[0], dest[1])
    buf = _dispatch_call(block_valid, dests, hp)
    expert_out = _expert_call(block_expert, block_new, block_valid, buf, w_gate[0], w_up[0], w_down[0])
    return _combine_call(dests, x1, gates[:TOP_K].T, row(g_final), expert_out).reshape(batch, seq, d)
```

```python
import functools
import math

import numpy as np
import jax
import jax.numpy as jnp
from jax import lax
from jax.experimental import pallas as pl
from jax.experimental.pallas import tpu as pltpu

F32 = jnp.float32
BF16 = jnp.bfloat16
I32 = jnp.int32
U32 = jnp.uint32

D_MODEL = 1024
EPS = 1e-6
NEG_INF = -1e30
LANES = 128
ROW_SUBLANES = D_MODEL // LANES

A_HEADS = 8
A_HEAD_DIM = 64
A_WIDTH = 512
A_QKV_WIDTH = 3 * A_WIDTH
DILATED_PATTERNS = ((128, 1), (512, 4), (2048, 16))
REL_BUCKETS = 32
REL_MAX_DISTANCE = 1024
A_QB = 128

B_HEADS = 8
B_Q_LORA = 256
B_KV_LORA = 128
B_QK_NOPE = 64
B_QK_ROPE = 32
B_V_DIM = 64
B_WIDTH = 512
ROPE_THETA = 10000.0
B_SCALE = (B_QK_NOPE + B_QK_ROPE) ** -0.5
B_QB = 512
B_SUB = 256
LOG2E = math.log2(math.e)

N_GROUPS = 4
EXPERTS_PER_GROUP = 8
N_EXPERTS = 32
TOP_K = 2
EXPERT_FF = 256
MOE_BLK = 512
ROUTER_ROWS = 40

ROW_TILE = 512
ISSUE_GROUP = 8
COMBINE_CHUNK = 32
PROJ_COLS = 2048

_NT = (((1,), (1,)), ((), ()))


def _cparams(semantics, vmem_mb=48, **kw):
    return pltpu.CompilerParams(dimension_semantics=semantics,
                                vmem_limit_bytes=vmem_mb * 1024 * 1024, **kw)


def _rms(x, g):
    return x * lax.rsqrt(jnp.mean(x * x, axis=-1, keepdims=True) + EPS) * g


def _lane_iota(rows=1):
    return lax.broadcasted_iota(I32, (rows, LANES), 1)


def _row_tile(r):
    return pl.ds(pl.multiple_of(r * ROW_SUBLANES, ROW_SUBLANES), ROW_SUBLANES)


def _proj_kernel(x_ref, g_ref, win_ref, gq_ref, wq_ref, gkv_ref, wkb_ref, wvb_ref, cos_ref, sin_ref, *refs):
    a_refs = refs[:len(DILATED_PATTERNS)]
    qb_ref, kb_ref, vb_ref, slab_ref = refs[len(a_refs):]
    tm = x_ref.shape[0]
    h = _rms(x_ref[...], g_ref[...]).astype(BF16)
    proj = jnp.dot(h, win_ref[...], preferred_element_type=F32)
    lo = _lane_iota() < A_HEAD_DIM
    n_slabs = A_QKV_WIDTH // LANES
    for s in range(n_slabs):
        slab_ref[s] = proj[:, LANES * s:LANES * (s + 1)]
    for qkv_out, (_, r) in zip(a_refs, DILATED_PATTERNS):
        for c in range(r):
            rows = pl.ds(c, tm // r, stride=r) if r > 1 else pl.ds(0, tm)
            for s in range(n_slabs):
                qkv_out[0, c, :, LANES * s:LANES * (s + 1)] = slab_ref[s, rows, :].astype(BF16)

    cos = cos_ref[...]
    sin = sin_ref[...]
    cq = _rms(proj[:, 1536:1792], gq_ref[...]).astype(BF16)
    q = jnp.dot(cq, wq_ref[...], preferred_element_type=F32)
    q_mul = (cos + jnp.where(lo, 1.0, 0.0)) * (B_SCALE * LOG2E)
    q_rot = sin * (B_SCALE * LOG2E)
    for hd in range(B_HEADS):
        t = q[:, LANES * hd:LANES * (hd + 1)]
        qb_ref[:, LANES * hd:LANES * (hd + 1)] = (t * q_mul + pltpu.roll(t, 96, 1) * q_rot).astype(BF16)

    ckv = _rms(proj[:, 1792:1920], gkv_ref[...]).astype(BF16)
    kr = proj[:, 1920:2048]
    kr = kr * cos + pltpu.roll(kr, 96, 1) * sin
    kn = jnp.dot(ckv, wkb_ref[...], preferred_element_type=F32)
    for hd in range(B_HEADS):
        kb_ref[:, LANES * hd:LANES * (hd + 1)] = (kn[:, LANES * hd:LANES * (hd + 1)] + kr).astype(BF16)
    vb_ref[...] = jnp.dot(ckv, wvb_ref[...], preferred_element_type=F32).astype(BF16)


def _proj_call(x2, g_attn, w_in, g_q, w_q, g_kv, w_kb, w_vb, cos_t, sin_t, seq):
    t = x2.shape[0]
    tm = ROW_TILE
    nseq = seq // tm
    row = lambda i: (i, 0)
    const = lambda i: (0, 0)
    pos = lambda i: (i % nseq, 0)
    out = lambda w: jax.ShapeDtypeStruct((t, w), BF16)
    a_specs, a_shapes = [], []
    for _, r in DILATED_PATTERNS:
        a_specs.append(pl.BlockSpec((1, r, tm // r, A_QKV_WIDTH), lambda i: (i // nseq, 0, i % nseq, 0)))
        a_shapes.append(jax.ShapeDtypeStruct((t // seq, r, seq // r, A_QKV_WIDTH), BF16))
    return pl.pallas_call(
        _proj_kernel,
        grid=(t // tm,),
        in_specs=[
            pl.BlockSpec((tm, D_MODEL), row),
            pl.BlockSpec((1, D_MODEL), const),
            pl.BlockSpec((D_MODEL, PROJ_COLS), const),
            pl.BlockSpec((1, B_Q_LORA), const),
            pl.BlockSpec((B_Q_LORA, B_HEADS * LANES), const),
            pl.BlockSpec((1, B_KV_LORA), const),
            pl.BlockSpec((B_KV_LORA, B_HEADS * LANES), const),
            pl.BlockSpec((B_KV_LORA, B_WIDTH), const),
            pl.BlockSpec((tm, LANES), pos),
            pl.BlockSpec((tm, LANES), pos),
        ],
        out_specs=a_specs + [
            pl.BlockSpec((tm, B_HEADS * LANES), row),
            pl.BlockSpec((tm, B_HEADS * LANES), row),
            pl.BlockSpec((tm, B_WIDTH), row),
        ],
        out_shape=a_shapes + [out(B_HEADS * LANES), out(B_HEADS * LANES), out(B_WIDTH)],
        scratch_shapes=[pltpu.VMEM((A_QKV_WIDTH // LANES, tm, LANES), F32)],
        compiler_params=_cparams(("parallel",)),
        name="proj",
    )(x2, g_attn, w_in, g_q, w_q, g_kv, w_kb, w_vb, cos_t, sin_t)


def _pack_bf16_pair(a, b):
    a_bits = lax.bitcast_convert_type(a.astype(BF16).astype(F32), U32) >> 16
    b_bits = lax.bitcast_convert_type(b.astype(BF16).astype(F32), U32) & jnp.uint32(0xFFFF0000)
    return a_bits | b_bits


def _unpack_bf16_pair(w):
    return (lax.bitcast_convert_type(w << 16, F32), lax.bitcast_convert_type(w & jnp.uint32(0xFFFF0000), F32))


def _dilated_kernel(qkv_ref, bias_ref, o_ref, lse_ref, *, seq_len, dilation, key_width, group):
    nblk = seq_len // A_QB
    lo = _lane_iota() < A_HEAD_DIM
    first_class = pl.program_id(1) * group
    pairs = A_HEADS // 2

    def block(it, carry):
        c = it // nblk
        if nblk == 1:
            q0, ks, var = 0, 0, 0
        else:
            n = it % nblk
            q0 = pl.multiple_of(n * A_QB, A_QB)
            ks = pl.multiple_of(jnp.clip(q0 - 64, 0, seq_len - key_width), 64)
            var = jnp.where(n == 0, 0, jnp.where(n == nblk - 1, 2, 1))
        rows = pl.ds(q0, A_QB)
        keys = pl.ds(ks, key_width)
        if dilation == 1:
            out_rows = rows
        else:
            out_rows = pl.ds(first_class + c + dilation * q0, A_QB, stride=dilation)
        q_tiles = [qkv_ref[0, c, rows, LANES * p:LANES * (p + 1)] for p in range(pairs)]
        k_tiles = [qkv_ref[0, c, keys, A_WIDTH + LANES * p:A_WIDTH + LANES * (p + 1)] for p in range(pairs)]
        v_tiles = [qkv_ref[0, c, keys, 2 * A_WIDTH + LANES * p:2 * A_WIDTH + LANES * (p + 1)] for p in range(pairs)]
        zero = jnp.zeros((), BF16)
        k_heads = [jnp.where(lo, k_tiles[hd // 2], zero) if hd % 2 == 0 else jnp.where(lo, zero, k_tiles[hd // 2])
                   for hd in range(A_HEADS)]
        scores = [lax.dot_general(q_tiles[hd // 2], k_heads[hd], _NT, preferred_element_type=F32) + bias_ref[var, hd]
                  for hd in range(A_HEADS)]
        maxes = [jnp.max(s, axis=-1, keepdims=True) for s in scores]
        probs = [jnp.exp(s - m) for s, m in zip(scores, maxes)]
        dens = [jnp.sum(pr, axis=-1, keepdims=True) for pr in probs]
        pvs = [jnp.dot(pr.astype(BF16), v_tiles[hd // 2], preferred_element_type=F32) for hd, pr in enumerate(probs)]
        outs = []
        for p in range(pairs):
            h0, h1 = 2 * p, 2 * p + 1
            outs.append(jnp.where(lo, pvs[h0] * (1.0 / dens[h0]), pvs[h1] * (1.0 / dens[h1])))
            lse_ref[0, p, out_rows, :] = jnp.where(lo, maxes[h0] + jnp.log(dens[h0]), maxes[h1] + jnp.log(dens[h1]))
        for j in range(pairs // 2):
            o_ref[0, j, out_rows, :] = _pack_bf16_pair(outs[2 * j], outs[2 * j + 1])
        return carry

    lax.fori_loop(0, group * nblk, block, 0, unroll=4)


def _dilated_call(qkv, bias, batch, seq, dilation):
    r = dilation
    sl = seq // r
    kw = min(2 * A_QB, sl)
    group = max(1, min(r, (4 * A_QB) // sl))
    pairs = A_HEADS // 2
    nat = lambda n: pl.BlockSpec((1, n, seq, LANES), lambda b, c: (b, 0, 0, 0))
    return pl.pallas_call(
        functools.partial(_dilated_kernel, seq_len=sl, dilation=r, key_width=kw, group=group),
        grid=(batch, r // group),
        in_specs=[pl.BlockSpec((1, group, sl, A_QKV_WIDTH), lambda b, c: (b, c, 0, 0)),
                  pl.BlockSpec(bias.shape, lambda b, c: (0, 0, 0, 0))],
        out_specs=[nat(pairs // 2), nat(pairs)],
        out_shape=[jax.ShapeDtypeStruct((batch, pairs // 2, seq, LANES), U32),
                   jax.ShapeDtypeStruct((batch, pairs, seq, LANES), F32)],
        compiler_params=_cparams(("parallel", "arbitrary")),
        name=f"dilated_r{r}",
    )(qkv, bias)


def _t5_bucket(rel):
    half = REL_BUCKETS // 2
    max_exact = half // 2
    n = np.abs(rel)
    large = max_exact + (np.log(np.maximum(n, 1) / max_exact)
                         / math.log(REL_MAX_DISTANCE / max_exact) * (half - max_exact)).astype(np.int32)
    large = np.minimum(large, half - 1)
    return (np.where(rel > 0, half, 0) + np.where(n < max_exact, n, large)).astype(np.int32)


def _dilated_bias(rel_bias, seq, dilation, half_steps):
    sl = seq // dilation
    kw = min(2 * A_QB, sl)
    offsets = [0] if sl == kw else [0, -half_steps, A_QB - kw]
    rel = np.stack([np.arange(kw)[None, :] + off - np.arange(A_QB)[:, None] for off in offsets])
    valid = np.abs(rel) <= half_steps
    bucket = np.where(valid, _t5_bucket(rel * dilation), REL_BUCKETS).astype(np.int32)
    onehot = (jnp.asarray(bucket)[..., None] == jnp.arange(REL_BUCKETS + 1, dtype=I32)).astype(F32)
    table = jnp.concatenate([rel_bias.astype(F32), jnp.full((1, A_HEADS), NEG_INF, F32)], axis=0)
    return jnp.einsum("vqkb,bh->vhqk", onehot, table, precision=lax.Precision.HIGHEST)


def _mla_kernel(q_ref, k_ref, v_ref, o_ref):
    lo = _lane_iota() < B_V_DIM
    sub = B_QB // B_SUB
    tiles = [slice(0, LANES), slice(LANES, 2 * LANES)]

    def block(i, carry):
        r0 = pl.multiple_of(i * B_QB, B_QB)
        rows = [pl.ds(r0 + B_SUB * j, B_SUB) for j in range(sub)]
        units = [(j, half) for j in range(sub) for half in range(2)]
        scores = [lax.dot_general(q_ref[0, rows[j], tiles[half]], k_ref[0, :, tiles[half]], _NT,
                                  preferred_element_type=F32) for j, half in units]
        maxes = [jnp.max(s, axis=-1, keepdims=True) for s in scores]
        probs = [jnp.exp2(s - m) for s, m in zip(scores, maxes)]
        dens = [jnp.sum(pr, axis=-1, keepdims=True) for pr in probs]
        outs = [jnp.dot(pr.astype(BF16), v_ref[0], preferred_element_type=F32) * (1.0 / den)
                for pr, den in zip(probs, dens)]
        for j in range(sub):
            o_ref[0, rows[j], :] = jnp.where(lo, outs[2 * j], outs[2 * j + 1]).astype(BF16)
        return carry

    lax.fori_loop(0, q_ref.shape[1] // B_QB, block, 0)


def _mla_call(qb, kb, vb, batch, seq):
    qb = qb.reshape(batch, seq, B_HEADS * LANES)
    kb = kb.reshape(batch, seq, B_HEADS * LANES)
    vb = vb.reshape(batch, seq, B_WIDTH)
    pair = lambda w: pl.BlockSpec((1, seq, w), lambda b, p: (b, 0, p))
    out = pl.pallas_call(
        _mla_kernel,
        grid=(batch, B_HEADS // 2),
        in_specs=[pair(2 * LANES), pair(2 * LANES), pair(LANES)],
        out_specs=pair(LANES),
        out_shape=jax.ShapeDtypeStruct((batch, seq, B_WIDTH), BF16),
        compiler_params=_cparams(("parallel", "parallel")),
        name="mla",
    )(qb, kb, vb)
    return out.reshape(batch * seq, B_WIDTH)


def _merge_patterns(o_refs, lse_refs):
    tiles = []
    for j in range(A_HEADS // 4):
        outs = [_unpack_bf16_pair(r[0, j]) for r in o_refs]
        for half in range(2):
            lses = [r[0, 2 * j + half] for r in lse_refs]
            top = functools.reduce(jnp.maximum, lses)
            es = [jnp.exp(l - top) for l in lses]
            inv = 1.0 / functools.reduce(jnp.add, es)
            tiles.append(functools.reduce(jnp.add, [e * inv * o[half] for e, o in zip(es, outs)]))
    return jnp.concatenate(tiles, axis=1)


def _mix_kernel(o1_ref, o4_ref, o16_ref, l1_ref, l4_ref, l16_ref, ob_ref, x_ref, ga_ref, gb_ref, wo_ref,
                gf_ref, wr_ref, br_ref, tri_ref, x1_ref, hp_ref, idx_ref, gate_ref, cnt_ref, carry_ref):
    i = pl.program_id(0)

    @pl.when(i == 0)
    def _():
        carry_ref[...] = jnp.zeros_like(carry_ref)

    oa = _merge_patterns((o1_ref, o4_ref, o16_ref), (l1_ref, l4_ref, l16_ref))
    a = _rms(oa, ga_ref[...]).astype(BF16)
    b = _rms(ob_ref[...].astype(F32), gb_ref[...]).astype(BF16)
    mix = (jnp.dot(a, wo_ref[0:A_WIDTH, :], preferred_element_type=F32)
           + jnp.dot(b, wo_ref[A_WIDTH:, :], preferred_element_type=F32))
    x1 = x_ref[...] + mix
    x1_ref[...] = x1
    h2 = _rms(x1, gf_ref[...])
    for c in range(ROW_SUBLANES):
        hp_ref[pl.ds(c, h2.shape[0], stride=ROW_SUBLANES), :] = h2[:, LANES * c:LANES * (c + 1)]

    wr = wr_ref[...]
    wr_hi = wr.astype(BF16)
    wr_lo = (wr - wr_hi.astype(F32)).astype(BF16)
    h_hi = h2.astype(BF16)
    h_lo = (h2 - h_hi.astype(F32)).astype(BF16)
    lg_hi = lax.dot_general(jnp.concatenate([wr_hi, wr_lo], axis=0), h_hi, _NT, preferred_element_type=F32)
    lg = (lg_hi[:ROUTER_ROWS] + lg_hi[ROUTER_ROWS:]
          + lax.dot_general(wr_hi, h_lo, _NT, preferred_element_type=F32) + br_ref[...])
    row = lax.broadcasted_iota(I32, lg.shape, 0)
    is_g = (row >= N_EXPERTS) & (row < N_EXPERTS + N_GROUPS)
    gl = jnp.where(is_g, lg, NEG_INF)
    ge = jnp.exp(gl - jnp.max(gl, axis=0, keepdims=True))
    gp = ge / jnp.sum(ge, axis=0, keepdims=True)
    g_gate = jnp.max(gp, axis=0, keepdims=True)
    g_idx = jnp.min(jnp.where(is_g & (gp == g_gate), row - N_EXPERTS, LANES), axis=0, keepdims=True)
    sel = (row >> 3) == g_idx
    el = jnp.where(sel, lg, NEG_INF)
    ee = jnp.exp(el - jnp.max(el, axis=0, keepdims=True))
    ep = jnp.where(sel, ee / jnp.sum(ee, axis=0, keepdims=True), -1.0)
    p1 = jnp.max(ep, axis=0, keepdims=True)
    i1 = jnp.min(jnp.where(ep == p1, row, LANES), axis=0, keepdims=True)
    ep2 = jnp.where(row == i1, -1.0, ep)
    p2 = jnp.max(ep2, axis=0, keepdims=True)
    i2 = jnp.min(jnp.where(sel & (ep2 == p2) & (row != i1), row, LANES), axis=0, keepdims=True)
    den = p1 + p2
    g1 = g_gate * p1 / den
    g2 = g_gate * p2 / den

    hit1 = row == i1
    hit2 = row == i2
    onehot = jnp.where(hit1 | hit2, 1.0, 0.0)
    before = jnp.dot(onehot.astype(BF16), tri_ref[...], preferred_element_type=F32) + carry_ref[...]
    r1 = jnp.sum(jnp.where(hit1, before, 0.0), axis=0, keepdims=True).astype(I32)
    r2 = jnp.sum(jnp.where(hit2, before, 0.0), axis=0, keepdims=True).astype(I32)
    carry_ref[...] += jnp.sum(onehot, axis=1, keepdims=True)

    row8 = lax.broadcasted_iota(I32, idx_ref.shape, 0)
    idx_ref[...] = jnp.where(row8 == 0, i1, jnp.where(row8 == 1, i2,
                             jnp.where(row8 == 2, r1, jnp.where(row8 == 3, r2, 0))))
    gate_ref[...] = jnp.where(row8 == 0, g1, jnp.where(row8 == 1, g2, 0.0))

    @pl.when(i == pl.num_programs(0) - 1)
    def _():
        cnt_ref[...] = jnp.broadcast_to(carry_ref[...], cnt_ref.shape).astype(I32)


def _mix_call(oas, lses, ob, x2, g_a, g_b, w_out, g_ffn, w_router, b_router, tri):
    t = x2.shape[0]
    tm = ROW_TILE
    nseq = oas[0].shape[2] // tm
    row = lambda i: (i, 0)
    const = lambda i: (0, 0)
    slab = lambda n: pl.BlockSpec((1, n, tm, LANES), lambda i: (i // nseq, 0, i % nseq, 0))
    o_slab, lse_slab = slab(A_HEADS // 4), slab(A_HEADS // 2)
    return pl.pallas_call(
        _mix_kernel,
        grid=(t // tm,),
        in_specs=[
            o_slab, o_slab, o_slab, lse_slab, lse_slab, lse_slab,
            pl.BlockSpec((tm, B_WIDTH), row),
            pl.BlockSpec((tm, D_MODEL), row),
            pl.BlockSpec((1, A_WIDTH), const),
            pl.BlockSpec((1, B_WIDTH), const),
            pl.BlockSpec((D_MODEL, D_MODEL), const),
            pl.BlockSpec((1, D_MODEL), const),
            pl.BlockSpec((ROUTER_ROWS, D_MODEL), const),
            pl.BlockSpec((ROUTER_ROWS, 1), const),
            pl.BlockSpec((tm, tm), const),
        ],
        out_specs=[
            pl.BlockSpec((tm, D_MODEL), row),
            pl.BlockSpec((tm * ROW_SUBLANES, LANES), row),
            pl.BlockSpec((8, tm), lambda i: (0, i)),
            pl.BlockSpec((8, tm), lambda i: (0, i)),
            pl.BlockSpec((ROUTER_ROWS, LANES), const),
        ],
        out_shape=[
            jax.ShapeDtypeStruct((t, D_MODEL), F32),
            jax.ShapeDtypeStruct((t * ROW_SUBLANES, LANES), F32),
            jax.ShapeDtypeStruct((8, t), I32),
            jax.ShapeDtypeStruct((8, t), F32),
            jax.ShapeDtypeStruct((ROUTER_ROWS, LANES), I32),
        ],
        scratch_shapes=[pltpu.VMEM((ROUTER_ROWS, 1), F32)],
        compiler_params=_cparams(("arbitrary",)),
        name="mix_router",
    )(*oas, *lses, ob, x2, g_a, g_b, w_out, g_ffn, w_router, b_router, tri)


def _dest_kernel(idx_ref, pstart_ref, dest_ref):
    idx = idx_ref[...]
    row = lax.broadcasted_iota(I32, (ROUTER_ROWS, idx.shape[1]), 0)
    ps = pstart_ref[...]

    def slot(k):
        return jnp.sum(jnp.where(row == idx[k:k + 1, :], ps, 0), axis=0, keepdims=True) + idx[2 + k:3 + k, :]

    row8 = lax.broadcasted_iota(I32, idx.shape, 0)
    dest_ref[...] = jnp.where(row8 == 0, slot(0), jnp.where(row8 == 1, slot(1), 0))


def _dest_call(idx, pstart):
    t = idx.shape[1]
    tm = 4 * ROW_TILE
    return pl.pallas_call(
        _dest_kernel,
        grid=(t // tm,),
        in_specs=[pl.BlockSpec((8, tm), lambda i: (0, i)), pl.BlockSpec((ROUTER_ROWS, 1), lambda i: (0, 0))],
        out_specs=pl.BlockSpec((8, tm), lambda i: (0, i)),
        out_shape=jax.ShapeDtypeStruct((8, t), I32),
        compiler_params=_cparams(("parallel",)),
        name="dest_rows",
    )(idx, pstart)


def _dispatch_kernel(valid_ref, d0_ref, d1_ref, h_ref, buf_ref, zero_ref, sem, pad_sem):
    i = pl.program_id(0)
    tt = h_ref.shape[0] // ROW_SUBLANES
    n_blocks = valid_ref.shape[0]

    def pad_copy(j):
        n_pad = pl.multiple_of((MOE_BLK - valid_ref[j]) * ROW_SUBLANES, ROW_SUBLANES)
        first = pl.multiple_of((j * MOE_BLK + valid_ref[j]) * ROW_SUBLANES, ROW_SUBLANES)
        return pltpu.make_async_copy(zero_ref.at[pl.ds(0, n_pad)], buf_ref.at[pl.ds(first, n_pad)], pad_sem)

    def for_padded_blocks(fn):
        def body(j, c):
            @pl.when(valid_ref[j] < MOE_BLK)
            def _():
                fn(pad_copy(j))
            return c
        lax.fori_loop(0, n_blocks, body, 0)

    @pl.when(i == 0)
    def _():
        zero_ref[...] = jnp.zeros_like(zero_ref)
        for_padded_blocks(lambda cp: cp.start())

    def issue(g, c):
        base = pl.multiple_of(g * ISSUE_GROUP, ISSUE_GROUP)
        for j in range(ISSUE_GROUP):
            for prio, d_ref in enumerate((d0_ref, d1_ref)):
                pltpu.make_async_copy(h_ref.at[_row_tile(base + j)], buf_ref.at[_row_tile(d_ref[base + j])],
                                      sem).start(priority=prio)
        return c

    lax.fori_loop(0, tt // ISSUE_GROUP, issue, 0)
    for k in range(TOP_K):
        pltpu.make_async_copy(h_ref, buf_ref.at[pl.ds(0, tt * ROW_SUBLANES)], sem).wait()

    @pl.when(i == pl.num_programs(0) - 1)
    def _():
        for_padded_blocks(lambda cp: cp.wait())


def _dispatch_call(block_valid, dests, hp):
    t = hp.shape[0] // ROW_SUBLANES
    tt = 2 * ROW_TILE
    n_rows = block_valid.shape[0] * MOE_BLK
    return pl.pallas_call(
        _dispatch_kernel,
        grid_spec=pltpu.PrefetchScalarGridSpec(
            num_scalar_prefetch=1,
            grid=(t // tt,),
            in_specs=[
                pl.BlockSpec((tt,), lambda i, va: (i,), memory_space=pltpu.SMEM),
                pl.BlockSpec((tt,), lambda i, va: (i,), memory_space=pltpu.SMEM),
                pl.BlockSpec((tt * ROW_SUBLANES, LANES), lambda i, va: (i, 0)),
            ],
            out_specs=pl.BlockSpec(memory_space=pl.ANY),
            scratch_shapes=[pltpu.VMEM((MOE_BLK * ROW_SUBLANES, LANES), F32),
                            pltpu.SemaphoreType.DMA(()), pltpu.SemaphoreType.DMA(())],
        ),
        out_shape=jax.ShapeDtypeStruct((n_rows * ROW_SUBLANES, LANES), F32),
        compiler_params=_cparams(("arbitrary",), disable_bounds_checks=True, has_side_effects=True),
        name="dispatch",
    )(block_valid, *dests, hp)


def _expert_kernel(be_ref, new_ref, valid_ref, buf_ref, wg_ref, wu_ref, wd_ref, out_ref, wg_s, wu_s, wd_s):
    j = pl.program_id(0)
    del be_ref

    @pl.when(new_ref[j] == 1)
    def _():
        wg_s[...] = wg_ref[0].astype(BF16)
        wu_s[...] = wu_ref[0].astype(BF16)
        wd_s[...] = wd_ref[0].astype(BF16)

    n_valid = valid_ref[j]

    @pl.when(n_valid > 0)
    def _():
        blk = buf_ref.shape[0] // ROW_SUBLANES
        cols = [pl.ds(c, blk, stride=ROW_SUBLANES) for c in range(ROW_SUBLANES)]
        x = jnp.concatenate([buf_ref[rows, :].astype(BF16) for rows in cols], axis=1)
        g = jnp.dot(x, wg_s[...], preferred_element_type=F32)
        u = jnp.dot(x, wu_s[...], preferred_element_type=F32)
        hb = (g * jax.nn.sigmoid(g)) * u
        out = jnp.dot(hb.astype(BF16), wd_s[...], preferred_element_type=F32)
        for c, rows in enumerate(cols):
            out_ref[rows, :] = out[:, LANES * c:LANES * (c + 1)]

    @pl.when(n_valid == 0)
    def _():
        out_ref[...] = jnp.zeros_like(out_ref)


def _expert_call(block_expert, block_new, block_valid, buf, w_gate, w_up, w_down):
    nb = buf.shape[0] // (MOE_BLK * ROW_SUBLANES)
    wsel = lambda j, be, nw, va: (be[j], 0, 0)
    rows = pl.BlockSpec((MOE_BLK * ROW_SUBLANES, LANES), lambda j, be, nw, va: (j, 0))
    return pl.pallas_call(
        _expert_kernel,
        grid_spec=pltpu.PrefetchScalarGridSpec(
            num_scalar_prefetch=3,
            grid=(nb,),
            in_specs=[
                rows,
                pl.BlockSpec((1, D_MODEL, EXPERT_FF), wsel),
                pl.BlockSpec((1, D_MODEL, EXPERT_FF), wsel),
                pl.BlockSpec((1, EXPERT_FF, D_MODEL), wsel),
            ],
            out_specs=rows,
            scratch_shapes=[pltpu.VMEM((D_MODEL, EXPERT_FF), BF16),
                            pltpu.VMEM((D_MODEL, EXPERT_FF), BF16),
                            pltpu.VMEM((EXPERT_FF, D_MODEL), BF16)],
        ),
        out_shape=jax.ShapeDtypeStruct(buf.shape, F32),
        compiler_params=_cparams(("arbitrary",)),
        name="experts",
    )(block_expert, block_new, block_valid, buf, w_gate, w_up, w_down)


def _combine_kernel(d0_ref, d1_ref, d0_next_ref, d1_next_ref, x1_ref, gate_ref, gf_ref, eo_ref, o_ref,
                    rows_a, rows_b, sems):
    i = pl.program_id(0)
    tt = x1_ref.shape[0]
    bufs = (rows_a, rows_b)

    def start_row(d_refs, slot, r):
        for k, d_ref in enumerate(d_refs):
            pltpu.make_async_copy(eo_ref.at[_row_tile(d_ref[r])], bufs[slot].at[k, _row_tile(r)],
                                  sems.at[slot]).start(priority=k)

    def drain(slot):
        for k in range(TOP_K):
            pltpu.make_async_copy(eo_ref.at[pl.ds(0, tt * ROW_SUBLANES)], bufs[slot].at[k], sems.at[slot]).wait()

    @pl.when(i == 0)
    def _():
        def issue(g, c):
            base = pl.multiple_of(g * ISSUE_GROUP, ISSUE_GROUP)
            for j in range(ISSUE_GROUP):
                start_row((d0_ref, d1_ref), 0, base + j)
            return c
        lax.fori_loop(0, tt // ISSUE_GROUP, issue, 0)

    def step(slot):
        drain(slot)
        gf = gf_ref[...]
        for ch in range(tt // COMBINE_CHUNK):
            r0 = ch * COMBINE_CHUNK
            for j in range(COMBINE_CHUNK):
                start_row((d0_next_ref, d1_next_ref), 1 - slot, r0 + j)
            gate = gate_ref[r0:r0 + COMBINE_CHUNK, :]

            def rows(k):
                return jnp.concatenate(
                    [bufs[slot][k, pl.ds(ROW_SUBLANES * r0 + c, COMBINE_CHUNK, stride=ROW_SUBLANES), :]
                     for c in range(ROW_SUBLANES)], axis=1)

            y = rows(0) * gate[:, 0:1] + rows(1) * gate[:, 1:2]
            o_ref[r0:r0 + COMBINE_CHUNK, :] = _rms(x1_ref[r0:r0 + COMBINE_CHUNK, :] + y, gf)

        @pl.when(i == pl.num_programs(0) - 1)
        def _():
            drain(1 - slot)

    for parity in range(2):
        pl.when(i % 2 == parity)(functools.partial(step, parity))


def _combine_call(dests, x1, gates, g_final, expert_out):
    t = x1.shape[0]
    tt = ROW_TILE // 2
    last = t // tt - 1
    cur = pl.BlockSpec((tt,), lambda i: (i,), memory_space=pltpu.SMEM)
    nxt = pl.BlockSpec((tt,), lambda i: (jnp.minimum(i + 1, last),), memory_space=pltpu.SMEM)
    return pl.pallas_call(
        _combine_kernel,
        grid=(t // tt,),
        in_specs=[
            cur, cur, nxt, nxt,
            pl.BlockSpec((tt, D_MODEL), lambda i: (i, 0)),
            pl.BlockSpec((tt, TOP_K), lambda i: (i, 0)),
            pl.BlockSpec((1, D_MODEL), lambda i: (0, 0)),
            pl.BlockSpec(memory_space=pl.ANY),
        ],
        out_specs=pl.BlockSpec((tt, D_MODEL), lambda i: (i, 0)),
        out_shape=jax.ShapeDtypeStruct((t, D_MODEL), F32),
        scratch_shapes=[pltpu.VMEM((TOP_K, tt * ROW_SUBLANES, LANES), F32),
                        pltpu.VMEM((TOP_K, tt * ROW_SUBLANES, LANES), F32), pltpu.SemaphoreType.DMA((2,))],
        compiler_params=_cparams(("arbitrary",), disable_bounds_checks=True),
        name="combine",
    )(*dests, *dests, x1, gates, g_final, expert_out)


def _rope_tables(seq):
    half = B_QK_ROPE // 2
    inv_freq = ROPE_THETA ** (-(jnp.arange(half, dtype=F32) / half))
    ang = jnp.arange(seq, dtype=F32)[:, None] * inv_freq[None, :]
    cos, sin = jnp.cos(ang), jnp.sin(ang)
    z = jnp.zeros((seq, B_QK_NOPE), F32)
    z2 = jnp.zeros((seq, B_QK_ROPE), F32)
    return (jnp.concatenate([z, cos, cos, z2], axis=1), jnp.concatenate([z, -sin, sin, z2], axis=1))


def _swap_halves(w):
    half = w.shape[-1] // 2
    return jnp.concatenate([w[..., half:], w[..., :half]], axis=-1)


def _layout_weights(w_in, w_q_up, w_kv_up):
    d = w_in.shape[0]
    w_kr = w_in[:, 3 * A_WIDTH + B_Q_LORA + B_KV_LORA:]
    w_in_l = jnp.concatenate(
        [w_in[:, :A_WIDTH] * (A_HEAD_DIM ** -0.5), w_in[:, A_WIDTH:3 * A_WIDTH + B_Q_LORA + B_KV_LORA],
         jnp.zeros((d, B_QK_NOPE), F32), w_kr, _swap_halves(w_kr)], axis=1).astype(BF16)
    wq = w_q_up.reshape(B_Q_LORA, B_HEADS, B_QK_NOPE + B_QK_ROPE)
    wq_l = jnp.concatenate([wq, _swap_halves(wq[..., B_QK_NOPE:])], axis=-1)
    wq_l = wq_l.reshape(B_Q_LORA, B_HEADS * LANES).astype(BF16)
    wkv = w_kv_up.reshape(B_KV_LORA, B_HEADS, B_QK_NOPE + B_V_DIM)
    wkb = jnp.concatenate([wkv[..., :B_QK_NOPE], jnp.zeros_like(wkv[..., :B_QK_NOPE])], axis=-1)
    wkb = wkb.reshape(B_KV_LORA, B_HEADS * LANES).astype(BF16)
    wvb = wkv[..., B_QK_NOPE:].reshape(B_KV_LORA, B_WIDTH).astype(BF16)
    return w_in_l, wq_l, wkb, wvb


def _block_plan(counts, n_blocks):
    padded = (counts + MOE_BLK - 1) // MOE_BLK * MOE_BLK
    ends = jnp.cumsum(padded)
    starts = ends - padded
    first_row = jnp.arange(n_blocks, dtype=I32) * MOE_BLK
    expert = jnp.minimum(jnp.sum(ends[None, :] <= first_row[:, None], axis=1), N_EXPERTS - 1).astype(I32)
    new = jnp.concatenate([jnp.ones((1,), I32), (expert[1:] != expert[:-1]).astype(I32)])
    valid = jnp.clip((starts + counts)[expert] - first_row, 0, MOE_BLK).astype(I32)
    return starts.astype(I32), expert, new, valid


def kernel(x, g_attn_norm, w_in, rel_bias, g_q_latent, w_q_up, g_kv_latent, w_kv_up, g_out_a, g_out_b, w_out,
           g_ffn_norm, w_router_group, b_router_group, w_router_expert, b_router_expert, w_gate, w_up, w_down,
           g_final):
    batch, seq, d = x.shape
    t = batch * seq
    assert g_attn_norm.shape[0] == 1 and d == D_MODEL and seq % ROW_TILE == 0
    cos_t, sin_t = _rope_tables(seq)
    tri = jnp.triu(jnp.ones((ROW_TILE, ROW_TILE), F32), 1).astype(BF16)
    n_blocks = t * TOP_K // MOE_BLK + N_EXPERTS
    x2 = x.reshape(t, d)
    row = lambda v: v.reshape(1, -1)

    w_in_l, wq_l, wkb_l, wvb_l = _layout_weights(w_in[0], w_q_up[0], w_kv_up[0])
    *qkv_a, qb, kb, vb = _proj_call(x2, row(g_attn_norm[0]), w_in_l, row(g_q_latent[0]), wq_l,
                                    row(g_kv_latent[0]), wkb_l, wvb_l, cos_t, sin_t, seq)
    oas, lses = [], []
    for pi, (window, dilation) in enumerate(DILATED_PATTERNS):
        bias = _dilated_bias(rel_bias, seq, dilation, window // (2 * dilation))
        o_p, lse_p = _dilated_call(qkv_a[pi], bias, batch, seq, dilation)
        oas.append(o_p)
        lses.append(lse_p)
    ob = _mla_call(qb, kb, vb, batch, seq)

    pad = ROUTER_ROWS - N_EXPERTS - N_GROUPS
    w_router = jnp.concatenate([w_router_expert[0], w_router_group[0], jnp.zeros((d, pad), F32)], axis=1).T
    b_router = jnp.concatenate([b_router_expert[0], b_router_group[0], jnp.zeros((pad,), F32)])
    x1, hp, idx, gates, cnt = _mix_call(oas, lses, ob, x2, row(g_out_a[0]), row(g_out_b[0]), w_out[0].astype(BF16),
                                        row(g_ffn_norm[0]), w_router, b_router.reshape(-1, 1), tri)
    pstart, block_expert, block_new, block_valid = _block_plan(cnt[:N_EXPERTS, 0], n_blocks)
    pstart_col = jnp.concatenate([pstart, jnp.zeros((ROUTER_ROWS - N_EXPERTS,), I32)]).reshape(-1, 1)
    dest = _dest_call(idx, pstart_col)
    dests = (dest[0], dest[1])
    buf = _dispatch_call(block_valid, dests, hp)
    expert_out = _expert_call(block_expert, block_new, block_valid, buf, w_gate[0], w_up[0], w_down[0])
    return _combine_call(dests, x1, gates[:TOP_K].T, row(g_final), expert_out).reshape(batch, seq, d)
```

```python
import functools
import math

import numpy as np
import jax
import jax.numpy as jnp
from jax import lax
from jax.experimental import pallas as pl
from jax.experimental.pallas import tpu as pltpu

F32 = jnp.float32
BF16 = jnp.bfloat16
I32 = jnp.int32
U32 = jnp.uint32

D_MODEL = 1024
EPS = 1e-6
NEG_INF = -1e30
LANES = 128
ROW_SUBLANES = D_MODEL // LANES

A_HEADS = 8
A_HEAD_DIM = 64
A_WIDTH = 512
A_QKV_WIDTH = 3 * A_WIDTH
DILATED_PATTERNS = ((128, 1), (512, 4), (2048, 16))
REL_BUCKETS = 32
REL_MAX_DISTANCE = 1024
A_QB = 128

B_HEADS = 8
B_Q_LORA = 256
B_KV_LORA = 128
B_QK_NOPE = 64
B_QK_ROPE = 32
B_V_DIM = 64
B_WIDTH = 512
ROPE_THETA = 10000.0
B_SCALE = (B_QK_NOPE + B_QK_ROPE) ** -0.5
B_QB = 512
B_SUB = 256
LOG2E = math.log2(math.e)

N_GROUPS = 4
EXPERTS_PER_GROUP = 8
N_EXPERTS = 32
TOP_K = 2
EXPERT_FF = 256
MOE_BLK = 512
ROUTER_ROWS = 40

ROW_TILE = 512
ISSUE_GROUP = 8
COMBINE_CHUNK = 32
PROJ_COLS = 2048

_NT = (((1,), (1,)), ((), ()))


def _cparams(semantics, vmem_mb=48, **kw):
    return pltpu.CompilerParams(dimension_semantics=semantics,
                                vmem_limit_bytes=vmem_mb * 1024 * 1024, **kw)


def _rms(x, g):
    return x * lax.rsqrt(jnp.mean(x * x, axis=-1, keepdims=True) + EPS) * g


def _lane_iota(rows=1):
    return lax.broadcasted_iota(I32, (rows, LANES), 1)


def _row_tile(r):
    return pl.ds(pl.multiple_of(r * ROW_SUBLANES, ROW_SUBLANES), ROW_SUBLANES)


def _proj_kernel(x_ref, g_ref, win_ref, gq_ref, wq_ref, gkv_ref, wkb_ref, wvb_ref, cos_ref, sin_ref, *refs):
    a_refs = refs[:len(DILATED_PATTERNS)]
    qb_ref, kb_ref, vb_ref, slab_ref = refs[len(a_refs):]
    tm = x_ref.shape[0]
    h = _rms(x_ref[...], g_ref[...]).astype(BF16)
    lo = _lane_iota() < A_HEAD_DIM
    n_slabs = A_QKV_WIDTH // LANES
    for s in range(0, n_slabs, 2):
        part = jnp.dot(h, win_ref[:, LANES * s:LANES * (s + 2)], preferred_element_type=F32)
        slab_ref[s] = part[:, :LANES]
        slab_ref[s + 1] = part[:, LANES:]
    lat = jnp.dot(h, win_ref[:, A_QKV_WIDTH:], preferred_element_type=F32)
    for qkv_out, (_, r) in zip(a_refs, DILATED_PATTERNS):
        for c in range(r):
            rows = pl.ds(c, tm // r, stride=r) if r > 1 else pl.ds(0, tm)
            for s in range(n_slabs):
                qkv_out[0, c, :, LANES * s:LANES * (s + 1)] = slab_ref[s, rows, :].astype(BF16)

    cos = cos_ref[...]
    sin = sin_ref[...]
    cq = _rms(lat[:, :B_Q_LORA], gq_ref[...]).astype(BF16)
    q = jnp.dot(cq, wq_ref[...], preferred_element_type=F32)
    q_mul = (cos + jnp.where(lo, 1.0, 0.0)) * (B_SCALE * LOG2E)
    q_rot = sin * (B_SCALE * LOG2E)
    for hd in range(B_HEADS):
        t = q[:, LANES * hd:LANES * (hd + 1)]
        qb_ref[:, LANES * hd:LANES * (hd + 1)] = (t * q_mul + pltpu.roll(t, 96, 1) * q_rot).astype(BF16)

    ckv = _rms(lat[:, B_Q_LORA:B_Q_LORA + B_KV_LORA], gkv_ref[...]).astype(BF16)
    kr = lat[:, B_Q_LORA + B_KV_LORA:]
    kr = kr * cos + pltpu.roll(kr, 96, 1) * sin
    kn = jnp.dot(ckv, wkb_ref[...], preferred_element_type=F32)
    for hd in range(B_HEADS):
        kb_ref[:, LANES * hd:LANES * (hd + 1)] = (kn[:, LANES * hd:LANES * (hd + 1)] + kr).astype(BF16)
    vb_ref[...] = jnp.dot(ckv, wvb_ref[...], preferred_element_type=F32).astype(BF16)


def _proj_call(x2, g_attn, w_in, g_q, w_q, g_kv, w_kb, w_vb, cos_t, sin_t, seq):
    t = x2.shape[0]
    tm = ROW_TILE
    nseq = seq // tm
    row = lambda i: (i, 0)
    const = lambda i: (0, 0)
    pos = lambda i: (i % nseq, 0)
    out = lambda w: jax.ShapeDtypeStruct((t, w), BF16)
    a_specs, a_shapes = [], []
    for _, r in DILATED_PATTERNS:
        a_specs.append(pl.BlockSpec((1, r, tm // r, A_QKV_WIDTH), lambda i: (i // nseq, 0, i % nseq, 0)))
        a_shapes.append(jax.ShapeDtypeStruct((t // seq, r, seq // r, A_QKV_WIDTH), BF16))
    return pl.pallas_call(
        _proj_kernel,
        grid=(t // tm,),
        in_specs=[
            pl.BlockSpec((tm, D_MODEL), row),
            pl.BlockSpec((1, D_MODEL), const),
            pl.BlockSpec((D_MODEL, PROJ_COLS), const),
            pl.BlockSpec((1, B_Q_LORA), const),
            pl.BlockSpec((B_Q_LORA, B_HEADS * LANES), const),
            pl.BlockSpec((1, B_KV_LORA), const),
            pl.BlockSpec((B_KV_LORA, B_HEADS * LANES), const),
            pl.BlockSpec((B_KV_LORA, B_WIDTH), const),
            pl.BlockSpec((tm, LANES), pos),
            pl.BlockSpec((tm, LANES), pos),
        ],
        out_specs=a_specs + [
            pl.BlockSpec((tm, B_HEADS * LANES), row),
            pl.BlockSpec((tm, B_HEADS * LANES), row),
            pl.BlockSpec((tm, B_WIDTH), row),
        ],
        out_shape=a_shapes + [out(B_HEADS * LANES), out(B_HEADS * LANES), out(B_WIDTH)],
        scratch_shapes=[pltpu.VMEM((A_QKV_WIDTH // LANES, tm, LANES), F32)],
        compiler_params=_cparams(("parallel",)),
        name="proj",
    )(x2, g_attn, w_in, g_q, w_q, g_kv, w_kb, w_vb, cos_t, sin_t)


def _pack_bf16_pair(a, b):
    a_bits = lax.bitcast_convert_type(a.astype(BF16).astype(F32), U32) >> 16
    b_bits = lax.bitcast_convert_type(b.astype(BF16).astype(F32), U32) & jnp.uint32(0xFFFF0000)
    return a_bits | b_bits


def _unpack_bf16_pair(w):
    return (lax.bitcast_convert_type(w << 16, F32), lax.bitcast_convert_type(w & jnp.uint32(0xFFFF0000), F32))


def _dilated_kernel(qkv_ref, bias_ref, o_ref, lse_ref, *, seq_len, dilation, key_width, group):
    nblk = seq_len // A_QB
    lo = _lane_iota() < A_HEAD_DIM
    first_class = pl.program_id(1) * group
    pairs = A_HEADS // 2

    def block(it, carry):
        c = it // nblk
        if nblk == 1:
            q0, ks, var = 0, 0, 0
        else:
            n = it % nblk
            q0 = pl.multiple_of(n * A_QB, A_QB)
            ks = pl.multiple_of(jnp.clip(q0 - 64, 0, seq_len - key_width), 64)
            var = jnp.where(n == 0, 0, jnp.where(n == nblk - 1, 2, 1))
        rows = pl.ds(q0, A_QB)
        keys = pl.ds(ks, key_width)
        if dilation == 1:
            out_rows = rows
        else:
            out_rows = pl.ds(first_class + c + dilation * q0, A_QB, stride=dilation)
        q_tiles = [qkv_ref[0, c, rows, LANES * p:LANES * (p + 1)] for p in range(pairs)]
        k_tiles = [qkv_ref[0, c, keys, A_WIDTH + LANES * p:A_WIDTH + LANES * (p + 1)] for p in range(pairs)]
        v_tiles = [qkv_ref[0, c, keys, 2 * A_WIDTH + LANES * p:2 * A_WIDTH + LANES * (p + 1)] for p in range(pairs)]
        zero = jnp.zeros((), BF16)
        k_heads = [jnp.where(lo, k_tiles[hd // 2], zero) if hd % 2 == 0 else jnp.where(lo, zero, k_tiles[hd // 2])
                   for hd in range(A_HEADS)]
        scores = [lax.dot_general(q_tiles[hd // 2], k_heads[hd], _NT, preferred_element_type=F32) + bias_ref[var, hd]
                  for hd in range(A_HEADS)]
        maxes = [jnp.max(s, axis=-1, keepdims=True) for s in scores]
        probs = [jnp.exp(s - m) for s, m in zip(scores, maxes)]
        dens = [jnp.sum(pr, axis=-1, keepdims=True) for pr in probs]
        pvs = [jnp.dot(pr.astype(BF16), v_tiles[hd // 2], preferred_element_type=F32) for hd, pr in enumerate(probs)]
        outs = []
        for p in range(pairs):
            h0, h1 = 2 * p, 2 * p + 1
            outs.append(jnp.where(lo, pvs[h0] * (1.0 / dens[h0]), pvs[h1] * (1.0 / dens[h1])))
            lse_ref[0, p, out_rows, :] = jnp.where(lo, maxes[h0] + jnp.log(dens[h0]), maxes[h1] + jnp.log(dens[h1]))
        for j in range(pairs // 2):
            o_ref[0, j, out_rows, :] = _pack_bf16_pair(outs[2 * j], outs[2 * j + 1])
        return carry

    lax.fori_loop(0, group * nblk, block, 0, unroll=4)


def _dilated_call(qkv, bias, batch, seq, dilation):
    r = dilation
    sl = seq // r
    kw = min(2 * A_QB, sl)
    group = max(1, min(r, (4 * A_QB) // sl))
    pairs = A_HEADS // 2
    nat = lambda n: pl.BlockSpec((1, n, seq, LANES), lambda b, c: (b, 0, 0, 0))
    return pl.pallas_call(
        functools.partial(_dilated_kernel, seq_len=sl, dilation=r, key_width=kw, group=group),
        grid=(batch, r // group),
        in_specs=[pl.BlockSpec((1, group, sl, A_QKV_WIDTH), lambda b, c: (b, c, 0, 0)),
                  pl.BlockSpec(bias.shape, lambda b, c: (0, 0, 0, 0))],
        out_specs=[nat(pairs // 2), nat(pairs)],
        out_shape=[jax.ShapeDtypeStruct((batch, pairs // 2, seq, LANES), U32),
                   jax.ShapeDtypeStruct((batch, pairs, seq, LANES), F32)],
        compiler_params=_cparams(("parallel", "arbitrary")),
        name=f"dilated_r{r}",
    )(qkv, bias)


def _t5_bucket(rel):
    half = REL_BUCKETS // 2
    max_exact = half // 2
    n = np.abs(rel)
    large = max_exact + (np.log(np.maximum(n, 1) / max_exact)
                         / math.log(REL_MAX_DISTANCE / max_exact) * (half - max_exact)).astype(np.int32)
    large = np.minimum(large, half - 1)
    return (np.where(rel > 0, half, 0) + np.where(n < max_exact, n, large)).astype(np.int32)


def _dilated_bias(rel_bias, seq, dilation, half_steps):
    sl = seq // dilation
    kw = min(2 * A_QB, sl)
    offsets = [0] if sl == kw else [0, -half_steps, A_QB - kw]
    rel = np.stack([np.arange(kw)[None, :] + off - np.arange(A_QB)[:, None] for off in offsets])
    valid = np.abs(rel) <= half_steps
    bucket = np.where(valid, _t5_bucket(rel * dilation), REL_BUCKETS).astype(np.int32)
    onehot = (jnp.asarray(bucket)[..., None] == jnp.arange(REL_BUCKETS + 1, dtype=I32)).astype(F32)
    table = jnp.concatenate([rel_bias.astype(F32), jnp.full((1, A_HEADS), NEG_INF, F32)], axis=0)
    return jnp.einsum("vqkb,bh->vhqk", onehot, table, precision=lax.Precision.HIGHEST)


def _mla_kernel(q_ref, k_ref, v_ref, o_ref):
    lo = _lane_iota() < B_V_DIM
    sub = B_QB // B_SUB
    tiles = [slice(0, LANES), slice(LANES, 2 * LANES)]

    def block(i, carry):
        r0 = pl.multiple_of(i * B_QB, B_QB)
        rows = [pl.ds(r0 + B_SUB * j, B_SUB) for j in range(sub)]
        units = [(j, half) for j in range(sub) for half in range(2)]
        scores = [lax.dot_general(q_ref[0, rows[j], tiles[half]], k_ref[0, :, tiles[half]], _NT,
                                  preferred_element_type=F32) for j, half in units]
        maxes = [jnp.max(s, axis=-1, keepdims=True) for s in scores]
        probs = [jnp.exp2(s - m) for s, m in zip(scores, maxes)]
        dens = [jnp.sum(pr, axis=-1, keepdims=True) for pr in probs]
        outs = [jnp.dot(pr.astype(BF16), v_ref[0], preferred_element_type=F32) * (1.0 / den)
                for pr, den in zip(probs, dens)]
        for j in range(sub):
            o_ref[0, rows[j], :] = jnp.where(lo, outs[2 * j], outs[2 * j + 1]).astype(BF16)
        return carry

    lax.fori_loop(0, q_ref.shape[1] // B_QB, block, 0)


def _mla_call(qb, kb, vb, batch, seq):
    qb = qb.reshape(batch, seq, B_HEADS * LANES)
    kb = kb.reshape(batch, seq, B_HEADS * LANES)
    vb = vb.reshape(batch, seq, B_WIDTH)
    pair = lambda w: pl.BlockSpec((1, seq, w), lambda b, p: (b, 0, p))
    out = pl.pallas_call(
        _mla_kernel,
        grid=(batch, B_HEADS // 2),
        in_specs=[pair(2 * LANES), pair(2 * LANES), pair(LANES)],
        out_specs=pair(LANES),
        out_shape=jax.ShapeDtypeStruct((batch, seq, B_WIDTH), BF16),
        compiler_params=_cparams(("parallel", "parallel")),
        name="mla",
    )(qb, kb, vb)
    return out.reshape(batch * seq, B_WIDTH)


def _merge_patterns(o_refs, lse_refs):
    tiles = []
    for j in range(A_HEADS // 4):
        outs = [_unpack_bf16_pair(r[0, j]) for r in o_refs]
        for half in range(2):
            lses = [r[0, 2 * j + half] for r in lse_refs]
            top = functools.reduce(jnp.maximum, lses)
            es = [jnp.exp(l - top) for l in lses]
            inv = 1.0 / functools.reduce(jnp.add, es)
            tiles.append(functools.reduce(jnp.add, [e * inv * o[half] for e, o in zip(es, outs)]))
    return jnp.concatenate(tiles, axis=1)


def _mix_kernel(o1_ref, o4_ref, o16_ref, l1_ref, l4_ref, l16_ref, ob_ref, x_ref, ga_ref, gb_ref, wo_ref,
                gf_ref, wr_ref, br_ref, tri_ref, x1_ref, hp_ref, idx_ref, gate_ref, cnt_ref, carry_ref):
    i = pl.program_id(0)

    @pl.when(i == 0)
    def _():
        carry_ref[...] = jnp.zeros_like(carry_ref)

    oa = _merge_patterns((o1_ref, o4_ref, o16_ref), (l1_ref, l4_ref, l16_ref))
    a = _rms(oa, ga_ref[...]).astype(BF16)
    b = _rms(ob_ref[...].astype(F32), gb_ref[...]).astype(BF16)
    mix = (jnp.dot(a, wo_ref[0:A_WIDTH, :], preferred_element_type=F32)
           + jnp.dot(b, wo_ref[A_WIDTH:, :], preferred_element_type=F32))
    x1 = x_ref[...] + mix
    x1_ref[...] = x1
    h2 = _rms(x1, gf_ref[...])
    for c in range(ROW_SUBLANES):
        hp_ref[pl.ds(c, h2.shape[0], stride=ROW_SUBLANES), :] = h2[:, LANES * c:LANES * (c + 1)]

    wr = wr_ref[...]
    wr_hi = wr.astype(BF16)
    wr_lo = (wr - wr_hi.astype(F32)).astype(BF16)
    h_hi = h2.astype(BF16)
    h_lo = (h2 - h_hi.astype(F32)).astype(BF16)
    lg_hi = lax.dot_general(jnp.concatenate([wr_hi, wr_lo], axis=0), h_hi, _NT, preferred_element_type=F32)
    lg = (lg_hi[:ROUTER_ROWS] + lg_hi[ROUTER_ROWS:]
          + lax.dot_general(wr_hi, h_lo, _NT, preferred_element_type=F32) + br_ref[...])
    row = lax.broadcasted_iota(I32, lg.shape, 0)
    is_g = (row >= N_EXPERTS) & (row < N_EXPERTS + N_GROUPS)
    gl = jnp.where(is_g, lg, NEG_INF)
    ge = jnp.exp(gl - jnp.max(gl, axis=0, keepdims=True))
    gp = ge / jnp.sum(ge, axis=0, keepdims=True)
    g_gate = jnp.max(gp, axis=0, keepdims=True)
    g_idx = jnp.min(jnp.where(is_g & (gp == g_gate), row - N_EXPERTS, LANES), axis=0, keepdims=True)
    sel = (row >> 3) == g_idx
    el = jnp.where(sel, lg, NEG_INF)
    ee = jnp.exp(el - jnp.max(el, axis=0, keepdims=True))
    ep = jnp.where(sel, ee / jnp.sum(ee, axis=0, keepdims=True), -1.0)
    p1 = jnp.max(ep, axis=0, keepdims=True)
    i1 = jnp.min(jnp.where(ep == p1, row, LANES), axis=0, keepdims=True)
    ep2 = jnp.where(row == i1, -1.0, ep)
    p2 = jnp.max(ep2, axis=0, keepdims=True)
    i2 = jnp.min(jnp.where(sel & (ep2 == p2) & (row != i1), row, LANES), axis=0, keepdims=True)
    den = p1 + p2
    g1 = g_gate * p1 / den
    g2 = g_gate * p2 / den

    hit1 = row == i1
    hit2 = row == i2
    onehot = jnp.where(hit1 | hit2, 1.0, 0.0)
    before = jnp.dot(onehot.astype(BF16), tri_ref[...], preferred_element_type=F32) + carry_ref[...]
    r1 = jnp.sum(jnp.where(hit1, before, 0.0), axis=0, keepdims=True).astype(I32)
    r2 = jnp.sum(jnp.where(hit2, before, 0.0), axis=0, keepdims=True).astype(I32)
    carry_ref[...] += jnp.sum(onehot, axis=1, keepdims=True)

    row8 = lax.broadcasted_iota(I32, idx_ref.shape, 0)
    idx_ref[...] = jnp.where(row8 == 0, i1, jnp.where(row8 == 1, i2,
                             jnp.where(row8 == 2, r1, jnp.where(row8 == 3, r2, 0))))
    gate_ref[...] = jnp.where(row8 == 0, g1, jnp.where(row8 == 1, g2, 0.0))

    @pl.when(i == pl.num_programs(0) - 1)
    def _():
        cnt_ref[...] = jnp.broadcast_to(carry_ref[...], cnt_ref.shape).astype(I32)


def _mix_call(oas, lses, ob, x2, g_a, g_b, w_out, g_ffn, w_router, b_router, tri):
    t = x2.shape[0]
    tm = ROW_TILE
    nseq = oas[0].shape[2] // tm
    row = lambda i: (i, 0)
    const = lambda i: (0, 0)
    slab = lambda n: pl.BlockSpec((1, n, tm, LANES), lambda i: (i // nseq, 0, i % nseq, 0))
    o_slab, lse_slab = slab(A_HEADS // 4), slab(A_HEADS // 2)
    return pl.pallas_call(
        _mix_kernel,
        grid=(t // tm,),
        in_specs=[
            o_slab, o_slab, o_slab, lse_slab, lse_slab, lse_slab,
            pl.BlockSpec((tm, B_WIDTH), row),
            pl.BlockSpec((tm, D_MODEL), row),
            pl.BlockSpec((1, A_WIDTH), const),
            pl.BlockSpec((1, B_WIDTH), const),
            pl.BlockSpec((D_MODEL, D_MODEL), const),
            pl.BlockSpec((1, D_MODEL), const),
            pl.BlockSpec((ROUTER_ROWS, D_MODEL), const),
            pl.BlockSpec((ROUTER_ROWS, 1), const),
            pl.BlockSpec((tm, tm), const),
        ],
        out_specs=[
            pl.BlockSpec((tm, D_MODEL), row),
            pl.BlockSpec((tm * ROW_SUBLANES, LANES), row),
            pl.BlockSpec((8, tm), lambda i: (0, i)),
            pl.BlockSpec((8, tm), lambda i: (0, i)),
            pl.BlockSpec((ROUTER_ROWS, LANES), const),
        ],
        out_shape=[
            jax.ShapeDtypeStruct((t, D_MODEL), F32),
            jax.ShapeDtypeStruct((t * ROW_SUBLANES, LANES), F32),
            jax.ShapeDtypeStruct((8, t), I32),
            jax.ShapeDtypeStruct((8, t), F32),
            jax.ShapeDtypeStruct((ROUTER_ROWS, LANES), I32),
        ],
        scratch_shapes=[pltpu.VMEM((ROUTER_ROWS, 1), F32)],
        compiler_params=_cparams(("arbitrary",)),
        name="mix_router",
    )(*oas, *lses, ob, x2, g_a, g_b, w_out, g_ffn, w_router, b_router, tri)


def _dest_kernel(idx_ref, pstart_ref, dest_ref):
    idx = idx_ref[...]
    row = lax.broadcasted_iota(I32, (ROUTER_ROWS, idx.shape[1]), 0)
    ps = pstart_ref[...]

    def slot(k):
        return jnp.sum(jnp.where(row == idx[k:k + 1, :], ps, 0), axis=0, keepdims=True) + idx[2 + k:3 + k, :]

    row8 = lax.broadcasted_iota(I32, idx.shape, 0)
    dest_ref[...] = jnp.where(row8 == 0, slot(0), jnp.where(row8 == 1, slot(1), 0))


def _dest_call(idx, pstart):
    t = idx.shape[1]
    tm = 4 * ROW_TILE
    return pl.pallas_call(
        _dest_kernel,
        grid=(t // tm,),
        in_specs=[pl.BlockSpec((8, tm), lambda i: (0, i)), pl.BlockSpec((ROUTER_ROWS, 1), lambda i: (0, 0))],
        out_specs=pl.BlockSpec((8, tm), lambda i: (0, i)),
        out_shape=jax.ShapeDtypeStruct((8, t), I32),
        compiler_params=_cparams(("parallel",)),
        name="dest_rows",
    )(idx, pstart)


def _dispatch_kernel(valid_ref, d0_ref, d1_ref, h_ref, buf_ref, zero_ref, sem, pad_sem):
    i = pl.program_id(0)
    tt = h_ref.shape[0] // ROW_SUBLANES
    n_blocks = valid_ref.shape[0]

    def pad_copy(j):
        n_pad = pl.multiple_of((MOE_BLK - valid_ref[j]) * ROW_SUBLANES, ROW_SUBLANES)
        first = pl.multiple_of((j * MOE_BLK + valid_ref[j]) * ROW_SUBLANES, ROW_SUBLANES)
        return pltpu.make_async_copy(zero_ref.at[pl.ds(0, n_pad)], buf_ref.at[pl.ds(first, n_pad)], pad_sem)

    def for_padded_blocks(fn):
        def body(j, c):
            @pl.when(valid_ref[j] < MOE_BLK)
            def _():
                fn(pad_copy(j))
            return c
        lax.fori_loop(0, n_blocks, body, 0)

    @pl.when(i == 0)
    def _():
        zero_ref[...] = jnp.zeros_like(zero_ref)
        for_padded_blocks(lambda cp: cp.start())

    def issue(g, c):
        base = pl.multiple_of(g * ISSUE_GROUP, ISSUE_GROUP)
        for j in range(ISSUE_GROUP):
            for prio, d_ref in enumerate((d0_ref, d1_ref)):
                pltpu.make_async_copy(h_ref.at[_row_tile(base + j)], buf_ref.at[_row_tile(d_ref[base + j])],
                                      sem).start(priority=prio)
        return c

    lax.fori_loop(0, tt // ISSUE_GROUP, issue, 0)
    for k in range(TOP_K):
        pltpu.make_async_copy(h_ref, buf_ref.at[pl.ds(0, tt * ROW_SUBLANES)], sem).wait()

    @pl.when(i == pl.num_programs(0) - 1)
    def _():
        for_padded_blocks(lambda cp: cp.wait())


def _dispatch_call(block_valid, dests, hp):
    t = hp.shape[0] // ROW_SUBLANES
    tt = 2 * ROW_TILE
    n_rows = block_valid.shape[0] * MOE_BLK
    return pl.pallas_call(
        _dispatch_kernel,
        grid_spec=pltpu.PrefetchScalarGridSpec(
            num_scalar_prefetch=1,
            grid=(t // tt,),
            in_specs=[
                pl.BlockSpec((tt,), lambda i, va: (i,), memory_space=pltpu.SMEM),
                pl.BlockSpec((tt,), lambda i, va: (i,), memory_space=pltpu.SMEM),
                pl.BlockSpec((tt * ROW_SUBLANES, LANES), lambda i, va: (i, 0)),
            ],
            out_specs=pl.BlockSpec(memory_space=pl.ANY),
            scratch_shapes=[pltpu.VMEM((MOE_BLK * ROW_SUBLANES, LANES), F32),
                            pltpu.SemaphoreType.DMA(()), pltpu.SemaphoreType.DMA(())],
        ),
        out_shape=jax.ShapeDtypeStruct((n_rows * ROW_SUBLANES, LANES), F32),
        compiler_params=_cparams(("arbitrary",), disable_bounds_checks=True, has_side_effects=True),
        name="dispatch",
    )(block_valid, *dests, hp)


def _expert_kernel(be_ref, new_ref, valid_ref, buf_ref, wg_ref, wu_ref, wd_ref, out_ref, wg_s, wu_s, wd_s):
    j = pl.program_id(0)
    del be_ref

    @pl.when(new_ref[j] == 1)
    def _():
        wg_s[...] = wg_ref[0].astype(BF16)
        wu_s[...] = wu_ref[0].astype(BF16)
        wd_s[...] = wd_ref[0].astype(BF16)

    n_valid = valid_ref[j]

    @pl.when(n_valid > 0)
    def _():
        blk = buf_ref.shape[0] // ROW_SUBLANES
        cols = [pl.ds(c, blk, stride=ROW_SUBLANES) for c in range(ROW_SUBLANES)]
        x = jnp.concatenate([buf_ref[rows, :].astype(BF16) for rows in cols], axis=1)
        g = jnp.dot(x, wg_s[...], preferred_element_type=F32)
        u = jnp.dot(x, wu_s[...], preferred_element_type=F32)
        hb = (g * jax.nn.sigmoid(g)) * u
        out = jnp.dot(hb.astype(BF16), wd_s[...], preferred_element_type=F32)
        for c, rows in enumerate(cols):
            out_ref[rows, :] = out[:, LANES * c:LANES * (c + 1)]

    @pl.when(n_valid == 0)
    def _():
        out_ref[...] = jnp.zeros_like(out_ref)


def _expert_call(block_expert, block_new, block_valid, buf, w_gate, w_up, w_down):
    nb = buf.shape[0] // (MOE_BLK * ROW_SUBLANES)
    wsel = lambda j, be, nw, va: (be[j], 0, 0)
    rows = pl.BlockSpec((MOE_BLK * ROW_SUBLANES, LANES), lambda j, be, nw, va: (j, 0))
    return pl.pallas_call(
        _expert_kernel,
        grid_spec=pltpu.PrefetchScalarGridSpec(
            num_scalar_prefetch=3,
            grid=(nb,),
            in_specs=[
                rows,
                pl.BlockSpec((1, D_MODEL, EXPERT_FF), wsel),
                pl.BlockSpec((1, D_MODEL, EXPERT_FF), wsel),
                pl.BlockSpec((1, EXPERT_FF, D_MODEL), wsel),
            ],
            out_specs=rows,
            scratch_shapes=[pltpu.VMEM((D_MODEL, EXPERT_FF), BF16),
                            pltpu.VMEM((D_MODEL, EXPERT_FF), BF16),
                            pltpu.VMEM((EXPERT_FF, D_MODEL), BF16)],
        ),
        out_shape=jax.ShapeDtypeStruct(buf.shape, F32),
        compiler_params=_cparams(("arbitrary",)),
        name="experts",
    )(block_expert, block_new, block_valid, buf, w_gate, w_up, w_down)


def _combine_kernel(d0_ref, d1_ref, d0_next_ref, d1_next_ref, d0_ahead_ref, d1_ahead_ref, x1_ref, gate_ref,
                    gf_ref, eo_ref, o_ref, rows_a, rows_b, rows_c, sems):
    i = pl.program_id(0)
    tt = x1_ref.shape[0]
    bufs = (rows_a, rows_b, rows_c)
    n_buf = len(bufs)

    def start_row(d_refs, slot, r):
        for k, d_ref in enumerate(d_refs):
            pltpu.make_async_copy(eo_ref.at[_row_tile(d_ref[r])], bufs[slot].at[k, _row_tile(r)],
                                  sems.at[slot]).start(priority=k)

    def drain(slot):
        for k in range(TOP_K):
            pltpu.make_async_copy(eo_ref.at[pl.ds(0, tt * ROW_SUBLANES)], bufs[slot].at[k], sems.at[slot]).wait()

    @pl.when(i == 0)
    def _():
        def issue(g, c):
            base = pl.multiple_of(g * ISSUE_GROUP, ISSUE_GROUP)
            for j in range(ISSUE_GROUP):
                start_row((d0_ref, d1_ref), 0, base + j)
                start_row((d0_next_ref, d1_next_ref), 1, base + j)
            return c
        lax.fori_loop(0, tt // ISSUE_GROUP, issue, 0)

    def step(slot):
        ahead = (slot + 2) % n_buf
        drain(slot)
        gf = gf_ref[...]
        for ch in range(tt // COMBINE_CHUNK):
            r0 = ch * COMBINE_CHUNK
            for j in range(COMBINE_CHUNK):
                start_row((d0_ahead_ref, d1_ahead_ref), ahead, r0 + j)
            gate = gate_ref[r0:r0 + COMBINE_CHUNK, :]

            def rows(k):
                return jnp.concatenate(
                    [bufs[slot][k, pl.ds(ROW_SUBLANES * r0 + c, COMBINE_CHUNK, stride=ROW_SUBLANES), :]
                     for c in range(ROW_SUBLANES)], axis=1)

            y = rows(0) * gate[:, 0:1] + rows(1) * gate[:, 1:2]
            o_ref[r0:r0 + COMBINE_CHUNK, :] = _rms(x1_ref[r0:r0 + COMBINE_CHUNK, :] + y, gf)

        @pl.when(i == pl.num_programs(0) - 1)
        def _():
            drain((slot + 1) % n_buf)
            drain(ahead)

    for phase in range(n_buf):
        pl.when(i % n_buf == phase)(functools.partial(step, phase))


def _combine_call(dests, x1, gates, g_final, expert_out):
    t = x1.shape[0]
    tt = ROW_TILE // 2
    last = t // tt - 1
    tile = lambda ahead: pl.BlockSpec((tt,), lambda i: (jnp.minimum(i + ahead, last),), memory_space=pltpu.SMEM)
    buf = pltpu.VMEM((TOP_K, tt * ROW_SUBLANES, LANES), F32)
    return pl.pallas_call(
        _combine_kernel,
        grid=(t // tt,),
        in_specs=[
            tile(0), tile(0), tile(1), tile(1), tile(2), tile(2),
            pl.BlockSpec((tt, D_MODEL), lambda i: (i, 0)),
            pl.BlockSpec((tt, TOP_K), lambda i: (i, 0)),
            pl.BlockSpec((1, D_MODEL), lambda i: (0, 0)),
            pl.BlockSpec(memory_space=pl.ANY),
        ],
        out_specs=pl.BlockSpec((tt, D_MODEL), lambda i: (i, 0)),
        out_shape=jax.ShapeDtypeStruct((t, D_MODEL), F32),
        scratch_shapes=[buf, buf, buf, pltpu.SemaphoreType.DMA((3,))],
        compiler_params=_cparams(("arbitrary",), disable_bounds_checks=True),
        name="combine",
    )(*dests, *dests, *dests, x1, gates, g_final, expert_out)


def _rope_tables(seq):
    half = B_QK_ROPE // 2
    inv_freq = ROPE_THETA ** (-(jnp.arange(half, dtype=F32) / half))
    ang = jnp.arange(seq, dtype=F32)[:, None] * inv_freq[None, :]
    cos, sin = jnp.cos(ang), jnp.sin(ang)
    z = jnp.zeros((seq, B_QK_NOPE), F32)
    z2 = jnp.zeros((seq, B_QK_ROPE), F32)
    return (jnp.concatenate([z, cos, cos, z2], axis=1), jnp.concatenate([z, -sin, sin, z2], axis=1))


def _swap_halves(w):
    half = w.shape[-1] // 2
    return jnp.concatenate([w[..., half:], w[..., :half]], axis=-1)


def _layout_weights(w_in, w_q_up, w_kv_up):
    d = w_in.shape[0]
    w_kr = w_in[:, 3 * A_WIDTH + B_Q_LORA + B_KV_LORA:]
    w_in_l = jnp.concatenate(
        [w_in[:, :A_WIDTH] * (A_HEAD_DIM ** -0.5), w_in[:, A_WIDTH:3 * A_WIDTH + B_Q_LORA + B_KV_LORA],
         jnp.zeros((d, B_QK_NOPE), F32), w_kr, _swap_halves(w_kr)], axis=1).astype(BF16)
    wq = w_q_up.reshape(B_Q_LORA, B_HEADS, B_QK_NOPE + B_QK_ROPE)
    wq_l = jnp.concatenate([wq, _swap_halves(wq[..., B_QK_NOPE:])], axis=-1)
    wq_l = wq_l.reshape(B_Q_LORA, B_HEADS * LANES).astype(BF16)
    wkv = w_kv_up.reshape(B_KV_LORA, B_HEADS, B_QK_NOPE + B_V_DIM)
    wkb = jnp.concatenate([wkv[..., :B_QK_NOPE], jnp.zeros_like(wkv[..., :B_QK_NOPE])], axis=-1)
    wkb = wkb.reshape(B_KV_LORA, B_HEADS * LANES).astype(BF16)
    wvb = wkv[..., B_QK_NOPE:].reshape(B_KV_LORA, B_WIDTH).astype(BF16)
    return w_in_l, wq_l, wkb, wvb


def _block_plan(counts, n_blocks):
    padded = (counts + MOE_BLK - 1) // MOE_BLK * MOE_BLK
    ends = jnp.cumsum(padded)
    starts = ends - padded
    first_row = jnp.arange(n_blocks, dtype=I32) * MOE_BLK
    expert = jnp.minimum(jnp.sum(ends[None, :] <= first_row[:, None], axis=1), N_EXPERTS - 1).astype(I32)
    new = jnp.concatenate([jnp.ones((1,), I32), (expert[1:] != expert[:-1]).astype(I32)])
    valid = jnp.clip((starts + counts)[expert] - first_row, 0, MOE_BLK).astype(I32)
    return starts.astype(I32), expert, new, valid


def kernel(x, g_attn_norm, w_in, rel_bias, g_q_latent, w_q_up, g_kv_latent, w_kv_up, g_out_a, g_out_b, w_out,
           g_ffn_norm, w_router_group, b_router_group, w_router_expert, b_router_expert, w_gate, w_up, w_down,
           g_final):
    batch, seq, d = x.shape
    t = batch * seq
    assert g_attn_norm.shape[0] == 1 and d == D_MODEL and seq % ROW_TILE == 0
    cos_t, sin_t = _rope_tables(seq)
    tri = jnp.triu(jnp.ones((ROW_TILE, ROW_TILE), F32), 1).astype(BF16)
    n_blocks = t * TOP_K // MOE_BLK + N_EXPERTS
    x2 = x.reshape(t, d)
    row = lambda v: v.reshape(1, -1)

    w_in_l, wq_l, wkb_l, wvb_l = _layout_weights(w_in[0], w_q_up[0], w_kv_up[0])
    *qkv_a, qb, kb, vb = _proj_call(x2, row(g_attn_norm[0]), w_in_l, row(g_q_latent[0]), wq_l,
                                    row(g_kv_latent[0]), wkb_l, wvb_l, cos_t, sin_t, seq)
    oas, lses = [], []
    for pi, (window, dilation) in enumerate(DILATED_PATTERNS):
        bias = _dilated_bias(rel_bias, seq, dilation, window // (2 * dilation))
        o_p, lse_p = _dilated_call(qkv_a[pi], bias, batch, seq, dilation)
        oas.append(o_p)
        lses.append(lse_p)
    ob = _mla_call(qb, kb, vb, batch, seq)

    pad = ROUTER_ROWS - N_EXPERTS - N_GROUPS
    w_router = jnp.concatenate([w_router_expert[0], w_router_group[0], jnp.zeros((d, pad), F32)], axis=1).T
    b_router = jnp.concatenate([b_router_expert[0], b_router_group[0], jnp.zeros((pad,), F32)])
    x1, hp, idx, gates, cnt = _mix_call(oas, lses, ob, x2, row(g_out_a[0]), row(g_out_b[0]), w_out[0].astype(BF16),
                                        row(g_ffn_norm[0]), w_router, b_router.reshape(-1, 1), tri)
    pstart, block_expert, block_new, block_valid = _block_plan(cnt[:N_EXPERTS, 0], n_blocks)
    pstart_col = jnp.concatenate([pstart, jnp.zeros((ROUTER_ROWS - N_EXPERTS,), I32)]).reshape(-1, 1)
    dest = _dest_call(idx, pstart_col)
    dests = (dest[0], dest[1])
    buf = _dispatch_call(block_valid, dests, hp)
    expert_out = _expert_call(block_expert, block_new, block_valid, buf, w_gate[0], w_up[0], w_down[0])
    return _combine_call(dests, x1, gates[:TOP_K].T, row(g_final), expert_out).reshape(batch, seq, d)
```

```python
import functools
import math

import numpy as np
import jax
import jax.numpy as jnp
from jax import lax
from jax.experimental import pallas as pl
from jax.experimental.pallas import tpu as pltpu

F32 = jnp.float32
BF16 = jnp.bfloat16
I32 = jnp.int32
U32 = jnp.uint32

D_MODEL = 1024
EPS = 1e-6
NEG_INF = -1e30
LANES = 128
ROW_SUBLANES = D_MODEL // LANES
PACKED_SUBLANES = ROW_SUBLANES // 2

A_HEADS = 8
A_HEAD_DIM = 64
A_WIDTH = 512
A_QKV_WIDTH = 3 * A_WIDTH
DILATED_PATTERNS = ((128, 1), (512, 4), (2048, 16))
REL_BUCKETS = 32
REL_MAX_DISTANCE = 1024
A_QB = 128

B_HEADS = 8
B_Q_LORA = 256
B_KV_LORA = 128
B_QK_NOPE = 64
B_QK_ROPE = 32
B_V_DIM = 64
B_WIDTH = 512
ROPE_THETA = 10000.0
B_SCALE = (B_QK_NOPE + B_QK_ROPE) ** -0.5
B_QB = 512
B_SUB = 256
LOG2E = math.log2(math.e)

N_GROUPS = 4
EXPERTS_PER_GROUP = 8
N_EXPERTS = 32
TOP_K = 2
EXPERT_FF = 256
MOE_BLK = 512
ROUTER_ROWS = 40

ROW_TILE = 512
ISSUE_GROUP = 8
COMBINE_CHUNK = 32
PROJ_COLS = 2048

_NT = (((1,), (1,)), ((), ()))


def _cparams(semantics, vmem_mb=48, **kw):
    return pltpu.CompilerParams(dimension_semantics=semantics,
                                vmem_limit_bytes=vmem_mb * 1024 * 1024, **kw)


def _rms(x, g):
    return x * lax.rsqrt(jnp.mean(x * x, axis=-1, keepdims=True) + EPS) * g


def _lane_iota(rows=1):
    return lax.broadcasted_iota(I32, (rows, LANES), 1)


def _row_tile(r, sublanes=ROW_SUBLANES):
    return pl.ds(pl.multiple_of(r * sublanes, sublanes), sublanes)


def _proj_kernel(x_ref, g_ref, win_ref, gq_ref, wq_ref, gkv_ref, wkb_ref, wvb_ref, cos_ref, sin_ref, *refs):
    a_refs = refs[:len(DILATED_PATTERNS)]
    qb_ref, kb_ref, vb_ref, slab_ref = refs[len(a_refs):]
    tm = x_ref.shape[0]
    h = _rms(x_ref[...], g_ref[...]).astype(BF16)
    lo = _lane_iota() < A_HEAD_DIM
    n_slabs = A_QKV_WIDTH // LANES
    for s in range(0, n_slabs, 2):
        part = jnp.dot(h, win_ref[:, LANES * s:LANES * (s + 2)], preferred_element_type=F32)
        slab_ref[s] = part[:, :LANES]
        slab_ref[s + 1] = part[:, LANES:]
    lat = jnp.dot(h, win_ref[:, A_QKV_WIDTH:], preferred_element_type=F32)
    for qkv_out, (_, r) in zip(a_refs, DILATED_PATTERNS):
        for c in range(r):
            rows = pl.ds(c, tm // r, stride=r) if r > 1 else pl.ds(0, tm)
            for s in range(n_slabs):
                qkv_out[0, c, :, LANES * s:LANES * (s + 1)] = slab_ref[s, rows, :].astype(BF16)

    cos = cos_ref[...]
    sin = sin_ref[...]
    cq = _rms(lat[:, :B_Q_LORA], gq_ref[...]).astype(BF16)
    q = jnp.dot(cq, wq_ref[...], preferred_element_type=F32)
    q_mul = (cos + jnp.where(lo, 1.0, 0.0)) * (B_SCALE * LOG2E)
    q_rot = sin * (B_SCALE * LOG2E)
    for hd in range(B_HEADS):
        t = q[:, LANES * hd:LANES * (hd + 1)]
        qb_ref[:, LANES * hd:LANES * (hd + 1)] = (t * q_mul + pltpu.roll(t, 96, 1) * q_rot).astype(BF16)

    ckv = _rms(lat[:, B_Q_LORA:B_Q_LORA + B_KV_LORA], gkv_ref[...]).astype(BF16)
    kr = lat[:, B_Q_LORA + B_KV_LORA:]
    kr = kr * cos + pltpu.roll(kr, 96, 1) * sin
    kn = jnp.dot(ckv, wkb_ref[...], preferred_element_type=F32)
    for hd in range(B_HEADS):
        kb_ref[:, LANES * hd:LANES * (hd + 1)] = (kn[:, LANES * hd:LANES * (hd + 1)] + kr).astype(BF16)
    vb_ref[...] = jnp.dot(ckv, wvb_ref[...], preferred_element_type=F32).astype(BF16)


def _proj_call(x2, g_attn, w_in, g_q, w_q, g_kv, w_kb, w_vb, cos_t, sin_t, seq):
    t = x2.shape[0]
    tm = ROW_TILE
    nseq = seq // tm
    row = lambda i: (i, 0)
    const = lambda i: (0, 0)
    pos = lambda i: (i % nseq, 0)
    out = lambda w: jax.ShapeDtypeStruct((t, w), BF16)
    a_specs, a_shapes = [], []
    for _, r in DILATED_PATTERNS:
        a_specs.append(pl.BlockSpec((1, r, tm // r, A_QKV_WIDTH), lambda i: (i // nseq, 0, i % nseq, 0)))
        a_shapes.append(jax.ShapeDtypeStruct((t // seq, r, seq // r, A_QKV_WIDTH), BF16))
    return pl.pallas_call(
        _proj_kernel,
        grid=(t // tm,),
        in_specs=[
            pl.BlockSpec((tm, D_MODEL), row),
            pl.BlockSpec((1, D_MODEL), const),
            pl.BlockSpec((D_MODEL, PROJ_COLS), const),
            pl.BlockSpec((1, B_Q_LORA), const),
            pl.BlockSpec((B_Q_LORA, B_HEADS * LANES), const),
            pl.BlockSpec((1, B_KV_LORA), const),
            pl.BlockSpec((B_KV_LORA, B_HEADS * LANES), const),
            pl.BlockSpec((B_KV_LORA, B_WIDTH), const),
            pl.BlockSpec((tm, LANES), pos),
            pl.BlockSpec((tm, LANES), pos),
        ],
        out_specs=a_specs + [
            pl.BlockSpec((tm, B_HEADS * LANES), row),
            pl.BlockSpec((tm, B_HEADS * LANES), row),
            pl.BlockSpec((tm, B_WIDTH), row),
        ],
        out_shape=a_shapes + [out(B_HEADS * LANES), out(B_HEADS * LANES), out(B_WIDTH)],
        scratch_shapes=[pltpu.VMEM((A_QKV_WIDTH // LANES, tm, LANES), F32)],
        compiler_params=_cparams(("parallel",)),
        name="proj",
    )(x2, g_attn, w_in, g_q, w_q, g_kv, w_kb, w_vb, cos_t, sin_t)


def _pack_bf16_pair(a, b):
    a_bits = lax.bitcast_convert_type(a.astype(BF16).astype(F32), U32) >> 16
    b_bits = lax.bitcast_convert_type(b.astype(BF16).astype(F32), U32) & jnp.uint32(0xFFFF0000)
    return a_bits | b_bits


def _unpack_bf16_pair(w):
    return (lax.bitcast_convert_type(w << 16, F32), lax.bitcast_convert_type(w & jnp.uint32(0xFFFF0000), F32))


def _dilated_kernel(qkv_ref, bias_ref, o_ref, lse_ref, *, seq_len, dilation, key_width, group):
    nblk = seq_len // A_QB
    lo = _lane_iota() < A_HEAD_DIM
    first_class = pl.program_id(1) * group
    pairs = A_HEADS // 2

    def block(it, carry):
        c = it // nblk
        if nblk == 1:
            q0, ks, var = 0, 0, 0
        else:
            n = it % nblk
            q0 = pl.multiple_of(n * A_QB, A_QB)
            ks = pl.multiple_of(jnp.clip(q0 - 64, 0, seq_len - key_width), 64)
            var = jnp.where(n == 0, 0, jnp.where(n == nblk - 1, 2, 1))
        rows = pl.ds(q0, A_QB)
        keys = pl.ds(ks, key_width)
        if dilation == 1:
            out_rows = rows
        else:
            out_rows = pl.ds(first_class + c + dilation * q0, A_QB, stride=dilation)
        q_tiles = [qkv_ref[0, c, rows, LANES * p:LANES * (p + 1)] for p in range(pairs)]
        k_tiles = [qkv_ref[0, c, keys, A_WIDTH + LANES * p:A_WIDTH + LANES * (p + 1)] for p in range(pairs)]
        v_tiles = [qkv_ref[0, c, keys, 2 * A_WIDTH + LANES * p:2 * A_WIDTH + LANES * (p + 1)] for p in range(pairs)]
        zero = jnp.zeros((), BF16)
        k_heads = [jnp.where(lo, k_tiles[hd // 2], zero) if hd % 2 == 0 else jnp.where(lo, zero, k_tiles[hd // 2])
                   for hd in range(A_HEADS)]
        scores = [lax.dot_general(q_tiles[hd // 2], k_heads[hd], _NT, preferred_element_type=F32) + bias_ref[var, hd]
                  for hd in range(A_HEADS)]
        maxes = [jnp.max(s, axis=-1, keepdims=True) for s in scores]
        probs = [jnp.exp(s - m) for s, m in zip(scores, maxes)]
        dens = [jnp.sum(pr, axis=-1, keepdims=True) for pr in probs]
        pvs = [jnp.dot(pr.astype(BF16), v_tiles[hd // 2], preferred_element_type=F32) for hd, pr in enumerate(probs)]
        outs = []
        for p in range(pairs):
            h0, h1 = 2 * p, 2 * p + 1
            outs.append(jnp.where(lo, pvs[h0] * (1.0 / dens[h0]), pvs[h1] * (1.0 / dens[h1])))
            lse_ref[0, p, out_rows, :] = jnp.where(lo, maxes[h0] + jnp.log(dens[h0]), maxes[h1] + jnp.log(dens[h1]))
        for j in range(pairs // 2):
            o_ref[0, j, out_rows, :] = _pack_bf16_pair(outs[2 * j], outs[2 * j + 1])
        return carry

    lax.fori_loop(0, group * nblk, block, 0, unroll=4)


def _dilated_call(qkv, bias, batch, seq, dilation):
    r = dilation
    sl = seq // r
    kw = min(2 * A_QB, sl)
    group = max(1, min(r, (4 * A_QB) // sl))
    pairs = A_HEADS // 2
    nat = lambda n: pl.BlockSpec((1, n, seq, LANES), lambda b, c: (b, 0, 0, 0))
    return pl.pallas_call(
        functools.partial(_dilated_kernel, seq_len=sl, dilation=r, key_width=kw, group=group),
        grid=(batch, r // group),
        in_specs=[pl.BlockSpec((1, group, sl, A_QKV_WIDTH), lambda b, c: (b, c, 0, 0)),
                  pl.BlockSpec(bias.shape, lambda b, c: (0, 0, 0, 0))],
        out_specs=[nat(pairs // 2), nat(pairs)],
        out_shape=[jax.ShapeDtypeStruct((batch, pairs // 2, seq, LANES), U32),
                   jax.ShapeDtypeStruct((batch, pairs, seq, LANES), F32)],
        compiler_params=_cparams(("parallel", "arbitrary")),
        name=f"dilated_r{r}",
    )(qkv, bias)


def _t5_bucket(rel):
    half = REL_BUCKETS // 2
    max_exact = half // 2
    n = np.abs(rel)
    large = max_exact + (np.log(np.maximum(n, 1) / max_exact)
                         / math.log(REL_MAX_DISTANCE / max_exact) * (half - max_exact)).astype(np.int32)
    large = np.minimum(large, half - 1)
    return (np.where(rel > 0, half, 0) + np.where(n < max_exact, n, large)).astype(np.int32)


def _dilated_bias(rel_bias, seq, dilation, half_steps):
    sl = seq // dilation
    kw = min(2 * A_QB, sl)
    offsets = [0] if sl == kw else [0, -half_steps, A_QB - kw]
    rel = np.stack([np.arange(kw)[None, :] + off - np.arange(A_QB)[:, None] for off in offsets])
    valid = np.abs(rel) <= half_steps
    bucket = np.where(valid, _t5_bucket(rel * dilation), REL_BUCKETS).astype(np.int32)
    onehot = (jnp.asarray(bucket)[..., None] == jnp.arange(REL_BUCKETS + 1, dtype=I32)).astype(F32)
    table = jnp.concatenate([rel_bias.astype(F32), jnp.full((1, A_HEADS), NEG_INF, F32)], axis=0)
    return jnp.einsum("vqkb,bh->vhqk", onehot, table, precision=lax.Precision.HIGHEST)


def _mla_kernel(q_ref, k_ref, v_ref, o_ref):
    lo = _lane_iota() < B_V_DIM
    sub = B_QB // B_SUB
    tiles = [slice(0, LANES), slice(LANES, 2 * LANES)]

    units = [(j, half) for j in range(sub) for half in range(2)]

    def rows(i):
        return [pl.ds(i * B_QB + B_SUB * j, B_SUB) for j in range(sub)]

    def scores(i):
        return [lax.dot_general(q_ref[0, rows(i)[j], tiles[half]], k_ref[0, :, tiles[half]], _NT,
                                preferred_element_type=F32) for j, half in units]

    def finish(i, sc):
        maxes = [jnp.max(s, axis=-1, keepdims=True) for s in sc]
        probs = [jnp.exp2(s - m) for s, m in zip(sc, maxes)]
        dens = [jnp.sum(pr, axis=-1, keepdims=True) for pr in probs]
        outs = [jnp.dot(pr.astype(BF16), v_ref[0], preferred_element_type=F32) * (1.0 / den)
                for pr, den in zip(probs, dens)]
        for j in range(sub):
            o_ref[0, rows(i)[j], :] = jnp.where(lo, outs[2 * j], outs[2 * j + 1]).astype(BF16)

    n_blocks = q_ref.shape[1] // B_QB
    sc = scores(0)
    for i in range(n_blocks):
        sc_next = scores(i + 1) if i + 1 < n_blocks else None
        finish(i, sc)
        sc = sc_next


def _mla_call(qb, kb, vb, batch, seq):
    qb = qb.reshape(batch, seq, B_HEADS * LANES)
    kb = kb.reshape(batch, seq, B_HEADS * LANES)
    vb = vb.reshape(batch, seq, B_WIDTH)
    pair = lambda w: pl.BlockSpec((1, seq, w), lambda b, p: (b, 0, p))
    out = pl.pallas_call(
        _mla_kernel,
        grid=(batch, B_HEADS // 2),
        in_specs=[pair(2 * LANES), pair(2 * LANES), pair(LANES)],
        out_specs=pair(LANES),
        out_shape=jax.ShapeDtypeStruct((batch, seq, B_WIDTH), BF16),
        compiler_params=_cparams(("parallel", "parallel")),
        name="mla",
    )(qb, kb, vb)
    return out.reshape(batch * seq, B_WIDTH)


def _merge_patterns(o_refs, lse_refs):
    tiles = []
    for j in range(A_HEADS // 4):
        outs = [_unpack_bf16_pair(r[0, j]) for r in o_refs]
        for half in range(2):
            lses = [r[0, 2 * j + half] for r in lse_refs]
            top = functools.reduce(jnp.maximum, lses)
            es = [jnp.exp(l - top) for l in lses]
            inv = 1.0 / functools.reduce(jnp.add, es)
            tiles.append(functools.reduce(jnp.add, [e * inv * o[half] for e, o in zip(es, outs)]))
    return jnp.concatenate(tiles, axis=1)


def _mix_kernel(o1_ref, o4_ref, o16_ref, l1_ref, l4_ref, l16_ref, ob_ref, x_ref, ga_ref, gb_ref, wo_ref,
                gf_ref, wr_ref, br_ref, tri_ref, x1_ref, hp_ref, idx_ref, gate_ref, cnt_ref, carry_ref):
    i = pl.program_id(0)

    @pl.when(i == 0)
    def _():
        carry_ref[...] = jnp.zeros_like(carry_ref)

    oa = _merge_patterns((o1_ref, o4_ref, o16_ref), (l1_ref, l4_ref, l16_ref))
    a = _rms(oa, ga_ref[...]).astype(BF16)
    b = _rms(ob_ref[...].astype(F32), gb_ref[...]).astype(BF16)
    mix = (jnp.dot(a, wo_ref[0:A_WIDTH, :], preferred_element_type=F32)
           + jnp.dot(b, wo_ref[A_WIDTH:, :], preferred_element_type=F32))
    x1 = x_ref[...] + mix
    x1_ref[...] = x1
    h2 = _rms(x1, gf_ref[...])
    half = D_MODEL // 2
    for c in range(PACKED_SUBLANES):
        hp_ref[pl.ds(c, h2.shape[0], stride=PACKED_SUBLANES), :] = _pack_bf16_pair(
            h2[:, LANES * c:LANES * (c + 1)], h2[:, half + LANES * c:half + LANES * (c + 1)])

    wr = wr_ref[...]
    wr_hi = wr.astype(BF16)
    wr_lo = (wr - wr_hi.astype(F32)).astype(BF16)
    h_hi = h2.astype(BF16)
    h_lo = (h2 - h_hi.astype(F32)).astype(BF16)
    lg_hi = lax.dot_general(jnp.concatenate([wr_hi, wr_lo], axis=0), h_hi, _NT, preferred_element_type=F32)
    lg = (lg_hi[:ROUTER_ROWS] + lg_hi[ROUTER_ROWS:]
          + lax.dot_general(wr_hi, h_lo, _NT, preferred_element_type=F32) + br_ref[...])
    row = lax.broadcasted_iota(I32, lg.shape, 0)
    is_g = (row >= N_EXPERTS) & (row < N_EXPERTS + N_GROUPS)
    gl = jnp.where(is_g, lg, NEG_INF)
    ge = jnp.exp(gl - jnp.max(gl, axis=0, keepdims=True))
    gp = ge / jnp.sum(ge, axis=0, keepdims=True)
    g_gate = jnp.max(gp, axis=0, keepdims=True)
    g_idx = jnp.min(jnp.where(is_g & (gp == g_gate), row - N_EXPERTS, LANES), axis=0, keepdims=True)
    sel = (row >> 3) == g_idx
    el = jnp.where(sel, lg, NEG_INF)
    ee = jnp.exp(el - jnp.max(el, axis=0, keepdims=True))
    ep = jnp.where(sel, ee / jnp.sum(ee, axis=0, keepdims=True), -1.0)
    p1 = jnp.max(ep, axis=0, keepdims=True)
    i1 = jnp.min(jnp.where(ep == p1, row, LANES), axis=0, keepdims=True)
    ep2 = jnp.where(row == i1, -1.0, ep)
    p2 = jnp.max(ep2, axis=0, keepdims=True)
    i2 = jnp.min(jnp.where(sel & (ep2 == p2) & (row != i1), row, LANES), axis=0, keepdims=True)
    den = p1 + p2
    g1 = g_gate * p1 / den
    g2 = g_gate * p2 / den

    hit1 = row == i1
    hit2 = row == i2
    onehot = jnp.where(hit1 | hit2, 1.0, 0.0)
    before = jnp.dot(onehot.astype(BF16), tri_ref[...], preferred_element_type=F32) + carry_ref[...]
    r1 = jnp.sum(jnp.where(hit1, before, 0.0), axis=0, keepdims=True).astype(I32)
    r2 = jnp.sum(jnp.where(hit2, before, 0.0), axis=0, keepdims=True).astype(I32)
    carry_ref[...] += jnp.sum(onehot, axis=1, keepdims=True)

    row8 = lax.broadcasted_iota(I32, idx_ref.shape, 0)
    idx_ref[...] = jnp.where(row8 == 0, i1, jnp.where(row8 == 1, i2,
                             jnp.where(row8 == 2, r1, jnp.where(row8 == 3, r2, 0))))
    gate_ref[...] = jnp.where(row8 == 0, g1, jnp.where(row8 == 1, g2, 0.0))

    @pl.when(i == pl.num_programs(0) - 1)
    def _():
        cnt_ref[...] = jnp.broadcast_to(carry_ref[...], cnt_ref.shape).astype(I32)


def _mix_call(oas, lses, ob, x2, g_a, g_b, w_out, g_ffn, w_router, b_router, tri):
    t = x2.shape[0]
    tm = ROW_TILE
    nseq = oas[0].shape[2] // tm
    row = lambda i: (i, 0)
    const = lambda i: (0, 0)
    slab = lambda n: pl.BlockSpec((1, n, tm, LANES), lambda i: (i // nseq, 0, i % nseq, 0))
    o_slab, lse_slab = slab(A_HEADS // 4), slab(A_HEADS // 2)
    return pl.pallas_call(
        _mix_kernel,
        grid=(t // tm,),
        in_specs=[
            o_slab, o_slab, o_slab, lse_slab, lse_slab, lse_slab,
            pl.BlockSpec((tm, B_WIDTH), row),
            pl.BlockSpec((tm, D_MODEL), row),
            pl.BlockSpec((1, A_WIDTH), const),
            pl.BlockSpec((1, B_WIDTH), const),
            pl.BlockSpec((D_MODEL, D_MODEL), const),
            pl.BlockSpec((1, D_MODEL), const),
            pl.BlockSpec((ROUTER_ROWS, D_MODEL), const),
            pl.BlockSpec((ROUTER_ROWS, 1), const),
            pl.BlockSpec((tm, tm), const),
        ],
        out_specs=[
            pl.BlockSpec((tm, D_MODEL), row),
            pl.BlockSpec((tm * PACKED_SUBLANES, LANES), row),
            pl.BlockSpec((8, tm), lambda i: (0, i)),
            pl.BlockSpec((8, tm), lambda i: (0, i)),
            pl.BlockSpec((ROUTER_ROWS, LANES), const),
        ],
        out_shape=[
            jax.ShapeDtypeStruct((t, D_MODEL), F32),
            jax.ShapeDtypeStruct((t * PACKED_SUBLANES, LANES), U32),
            jax.ShapeDtypeStruct((8, t), I32),
            jax.ShapeDtypeStruct((8, t), F32),
            jax.ShapeDtypeStruct((ROUTER_ROWS, LANES), I32),
        ],
        scratch_shapes=[pltpu.VMEM((ROUTER_ROWS, 1), F32)],
        compiler_params=_cparams(("arbitrary",)),
        name="mix_router",
    )(*oas, *lses, ob, x2, g_a, g_b, w_out, g_ffn, w_router, b_router, tri)


def _dest_kernel(idx_ref, pstart_ref, dest_ref):
    idx = idx_ref[...]
    row = lax.broadcasted_iota(I32, (ROUTER_ROWS, idx.shape[1]), 0)
    ps = pstart_ref[...]

    def slot(k):
        return jnp.sum(jnp.where(row == idx[k:k + 1, :], ps, 0), axis=0, keepdims=True) + idx[2 + k:3 + k, :]

    row8 = lax.broadcasted_iota(I32, idx.shape, 0)
    dest_ref[...] = jnp.where(row8 == 0, slot(0), jnp.where(row8 == 1, slot(1), 0))


def _dest_call(idx, pstart):
    t = idx.shape[1]
    tm = 4 * ROW_TILE
    return pl.pallas_call(
        _dest_kernel,
        grid=(t // tm,),
        in_specs=[pl.BlockSpec((8, tm), lambda i: (0, i)), pl.BlockSpec((ROUTER_ROWS, 1), lambda i: (0, 0))],
        out_specs=pl.BlockSpec((8, tm), lambda i: (0, i)),
        out_shape=jax.ShapeDtypeStruct((8, t), I32),
        compiler_params=_cparams(("parallel",)),
        name="dest_rows",
    )(idx, pstart)


def _dispatch_kernel(valid_ref, d0_ref, d1_ref, h_ref, buf_ref, zero_ref, sem, pad_sem):
    i = pl.program_id(0)
    tt = h_ref.shape[0] // PACKED_SUBLANES
    n_blocks = valid_ref.shape[0]

    def for_padded_blocks(fn):
        def body(j, c):
            taken = valid_ref[j]
            even = (taken + 1) // 2 * 2

            @pl.when(even < MOE_BLK)
            def _():
                n_pad = pl.multiple_of((MOE_BLK - even) * PACKED_SUBLANES, ROW_SUBLANES)
                first = pl.multiple_of((j * MOE_BLK + even) * PACKED_SUBLANES, ROW_SUBLANES)
                fn(pltpu.make_async_copy(zero_ref.at[pl.ds(0, n_pad)], buf_ref.at[pl.ds(first, n_pad)], pad_sem))

            @pl.when(even != taken)
            def _():
                fn(pltpu.make_async_copy(zero_ref.at[pl.ds(0, PACKED_SUBLANES)],
                                         buf_ref.at[_row_tile(j * MOE_BLK + taken, PACKED_SUBLANES)], pad_sem))
            return c
        lax.fori_loop(0, n_blocks, body, 0)

    @pl.when(i == 0)
    def _():
        zero_ref[...] = jnp.zeros_like(zero_ref)
        for_padded_blocks(lambda cp: cp.start())

    def issue(g, c):
        base = pl.multiple_of(g * ISSUE_GROUP, ISSUE_GROUP)
        for j in range(ISSUE_GROUP):
            for prio, d_ref in enumerate((d0_ref, d1_ref)):
                pltpu.make_async_copy(h_ref.at[_row_tile(base + j, PACKED_SUBLANES)],
                                      buf_ref.at[_row_tile(d_ref[base + j], PACKED_SUBLANES)],
                                      sem).start(priority=prio)
        return c

    lax.fori_loop(0, tt // ISSUE_GROUP, issue, 0)
    for k in range(TOP_K):
        pltpu.make_async_copy(h_ref, buf_ref.at[pl.ds(0, tt * PACKED_SUBLANES)], sem).wait()

    @pl.when(i == pl.num_programs(0) - 1)
    def _():
        for_padded_blocks(lambda cp: cp.wait())


def _dispatch_call(block_valid, dests, hp):
    t = hp.shape[0] // PACKED_SUBLANES
    tt = 2 * ROW_TILE
    n_rows = block_valid.shape[0] * MOE_BLK
    return pl.pallas_call(
        _dispatch_kernel,
        grid_spec=pltpu.PrefetchScalarGridSpec(
            num_scalar_prefetch=1,
            grid=(t // tt,),
            in_specs=[
                pl.BlockSpec((tt,), lambda i, va: (i,), memory_space=pltpu.SMEM),
                pl.BlockSpec((tt,), lambda i, va: (i,), memory_space=pltpu.SMEM),
                pl.BlockSpec((tt * PACKED_SUBLANES, LANES), lambda i, va: (i, 0)),
            ],
            out_specs=pl.BlockSpec(memory_space=pl.ANY),
            scratch_shapes=[pltpu.VMEM((MOE_BLK * PACKED_SUBLANES, LANES), U32),
                            pltpu.SemaphoreType.DMA(()), pltpu.SemaphoreType.DMA(())],
        ),
        out_shape=jax.ShapeDtypeStruct((n_rows * PACKED_SUBLANES, LANES), U32),
        compiler_params=_cparams(("arbitrary",), disable_bounds_checks=True, has_side_effects=True),
        name="dispatch",
    )(block_valid, *dests, hp)


def _expert_kernel(be_ref, new_ref, valid_ref, buf_ref, wg_ref, wu_ref, wd_ref, out_ref, wg_s, wu_s, wd_s):
    j = pl.program_id(0)
    del be_ref

    @pl.when(new_ref[j] == 1)
    def _():
        wg_s[...] = wg_ref[0].astype(BF16)
        wu_s[...] = wu_ref[0].astype(BF16)
        wd_s[...] = wd_ref[0].astype(BF16)

    n_valid = valid_ref[j]

    @pl.when(n_valid > 0)
    def _():
        blk = buf_ref.shape[0] // PACKED_SUBLANES
        words = [_unpack_bf16_pair(buf_ref[pl.ds(c, blk, stride=PACKED_SUBLANES), :]) for c in range(PACKED_SUBLANES)]
        x = jnp.concatenate([w[0] for w in words] + [w[1] for w in words], axis=1).astype(BF16)
        cols = [pl.ds(c, blk, stride=ROW_SUBLANES) for c in range(ROW_SUBLANES)]
        g = jnp.dot(x, wg_s[...], preferred_element_type=F32)
        u = jnp.dot(x, wu_s[...], preferred_element_type=F32)
        hb = (g * jax.nn.sigmoid(g)) * u
        out = jnp.dot(hb.astype(BF16), wd_s[...], preferred_element_type=F32)
        for c, rows in enumerate(cols):
            out_ref[rows, :] = out[:, LANES * c:LANES * (c + 1)]

    @pl.when(n_valid == 0)
    def _():
        out_ref[...] = jnp.zeros_like(out_ref)


def _expert_call(block_expert, block_new, block_valid, buf, w_gate, w_up, w_down):
    nb = buf.shape[0] // (MOE_BLK * PACKED_SUBLANES)
    wsel = lambda j, be, nw, va: (be[j], 0, 0)
    rows = lambda sublanes: pl.BlockSpec((MOE_BLK * sublanes, LANES), lambda j, be, nw, va: (j, 0))
    return pl.pallas_call(
        _expert_kernel,
        grid_spec=pltpu.PrefetchScalarGridSpec(
            num_scalar_prefetch=3,
            grid=(nb,),
            in_specs=[
                rows(PACKED_SUBLANES),
                pl.BlockSpec((1, D_MODEL, EXPERT_FF), wsel),
                pl.BlockSpec((1, D_MODEL, EXPERT_FF), wsel),
                pl.BlockSpec((1, EXPERT_FF, D_MODEL), wsel),
            ],
            out_specs=rows(ROW_SUBLANES),
            scratch_shapes=[pltpu.VMEM((D_MODEL, EXPERT_FF), BF16),
                            pltpu.VMEM((D_MODEL, EXPERT_FF), BF16),
                            pltpu.VMEM((EXPERT_FF, D_MODEL), BF16)],
        ),
        out_shape=jax.ShapeDtypeStruct((nb * MOE_BLK * ROW_SUBLANES, LANES), F32),
        compiler_params=_cparams(("arbitrary",)),
        name="experts",
    )(block_expert, block_new, block_valid, buf, w_gate, w_up, w_down)


def _combine_kernel(d0_ref, d1_ref, d0_next_ref, d1_next_ref, d0_ahead_ref, d1_ahead_ref, x1_ref, gate_ref,
                    gf_ref, eo_ref, o_ref, rows_a, rows_b, rows_c, sems):
    i = pl.program_id(0)
    tt = x1_ref.shape[0]
    bufs = (rows_a, rows_b, rows_c)
    n_buf = len(bufs)

    def start_row(d_refs, slot, r):
        for k, d_ref in enumerate(d_refs):
            pltpu.make_async_copy(eo_ref.at[_row_tile(d_ref[r])], bufs[slot].at[k, _row_tile(r)],
                                  sems.at[slot]).start(priority=k)

    def drain(slot):
        for k in range(TOP_K):
            pltpu.make_async_copy(eo_ref.at[pl.ds(0, tt * ROW_SUBLANES)], bufs[slot].at[k], sems.at[slot]).wait()

    @pl.when(i == 0)
    def _():
        def issue(g, c):
            base = pl.multiple_of(g * ISSUE_GROUP, ISSUE_GROUP)
            for j in range(ISSUE_GROUP):
                start_row((d0_ref, d1_ref), 0, base + j)
                start_row((d0_next_ref, d1_next_ref), 1, base + j)
            return c
        lax.fori_loop(0, tt // ISSUE_GROUP, issue, 0)

    def step(slot):
        ahead = (slot + 2) % n_buf
        drain(slot)
        gf = gf_ref[...]
        for ch in range(tt // COMBINE_CHUNK):
            r0 = ch * COMBINE_CHUNK
            for j in range(COMBINE_CHUNK):
                start_row((d0_ahead_ref, d1_ahead_ref), ahead, r0 + j)
            gate = gate_ref[r0:r0 + COMBINE_CHUNK, :]

            def rows(k):
                return jnp.concatenate(
                    [bufs[slot][k, pl.ds(ROW_SUBLANES * r0 + c, COMBINE_CHUNK, stride=ROW_SUBLANES), :]
                     for c in range(ROW_SUBLANES)], axis=1)

            y = rows(0) * gate[:, 0:1] + rows(1) * gate[:, 1:2]
            o_ref[r0:r0 + COMBINE_CHUNK, :] = _rms(x1_ref[r0:r0 + COMBINE_CHUNK, :] + y, gf)

        @pl.when(i == pl.num_programs(0) - 1)
        def _():
            drain((slot + 1) % n_buf)
            drain(ahead)

    for phase in range(n_buf):
        pl.when(i % n_buf == phase)(functools.partial(step, phase))


def _combine_call(dests, x1, gates, g_final, expert_out):
    t = x1.shape[0]
    tt = ROW_TILE // 2
    last = t // tt - 1
    tile = lambda ahead: pl.BlockSpec((tt,), lambda i: (jnp.minimum(i + ahead, last),), memory_space=pltpu.SMEM)
    buf = pltpu.VMEM((TOP_K, tt * ROW_SUBLANES, LANES), F32)
    return pl.pallas_call(
        _combine_kernel,
        grid=(t // tt,),
        in_specs=[
            tile(0), tile(0), tile(1), tile(1), tile(2), tile(2),
            pl.BlockSpec((tt, D_MODEL), lambda i: (i, 0)),
            pl.BlockSpec((tt, TOP_K), lambda i: (i, 0)),
            pl.BlockSpec((1, D_MODEL), lambda i: (0, 0)),
            pl.BlockSpec(memory_space=pl.ANY),
        ],
        out_specs=pl.BlockSpec((tt, D_MODEL), lambda i: (i, 0)),
        out_shape=jax.ShapeDtypeStruct((t, D_MODEL), F32),
        scratch_shapes=[buf, buf, buf, pltpu.SemaphoreType.DMA((3,))],
        compiler_params=_cparams(("arbitrary",), disable_bounds_checks=True),
        name="combine",
    )(*dests, *dests, *dests, x1, gates, g_final, expert_out)


def _rope_tables(seq):
    half = B_QK_ROPE // 2
    inv_freq = ROPE_THETA ** (-(jnp.arange(half, dtype=F32) / half))
    ang = jnp.arange(seq, dtype=F32)[:, None] * inv_freq[None, :]
    cos, sin = jnp.cos(ang), jnp.sin(ang)
    z = jnp.zeros((seq, B_QK_NOPE), F32)
    z2 = jnp.zeros((seq, B_QK_ROPE), F32)
    return (jnp.concatenate([z, cos, cos, z2], axis=1), jnp.concatenate([z, -sin, sin, z2], axis=1))


def _swap_halves(w):
    half = w.shape[-1] // 2
    return jnp.concatenate([w[..., half:], w[..., :half]], axis=-1)


def _layout_weights(w_in, w_q_up, w_kv_up):
    d = w_in.shape[0]
    w_kr = w_in[:, 3 * A_WIDTH + B_Q_LORA + B_KV_LORA:]
    w_in_l = jnp.concatenate(
        [w_in[:, :A_WIDTH] * (A_HEAD_DIM ** -0.5), w_in[:, A_WIDTH:3 * A_WIDTH + B_Q_LORA + B_KV_LORA],
         jnp.zeros((d, B_QK_NOPE), F32), w_kr, _swap_halves(w_kr)], axis=1).astype(BF16)
    wq = w_q_up.reshape(B_Q_LORA, B_HEADS, B_QK_NOPE + B_QK_ROPE)
    wq_l = jnp.concatenate([wq, _swap_halves(wq[..., B_QK_NOPE:])], axis=-1)
    wq_l = wq_l.reshape(B_Q_LORA, B_HEADS * LANES).astype(BF16)
    wkv = w_kv_up.reshape(B_KV_LORA, B_HEADS, B_QK_NOPE + B_V_DIM)
    wkb = jnp.concatenate([wkv[..., :B_QK_NOPE], jnp.zeros_like(wkv[..., :B_QK_NOPE])], axis=-1)
    wkb = wkb.reshape(B_KV_LORA, B_HEADS * LANES).astype(BF16)
    wvb = wkv[..., B_QK_NOPE:].reshape(B_KV_LORA, B_WIDTH).astype(BF16)
    return w_in_l, wq_l, wkb, wvb


def _block_plan(counts, n_blocks):
    padded = (counts + MOE_BLK - 1) // MOE_BLK * MOE_BLK
    ends = jnp.cumsum(padded)
    starts = ends - padded
    first_row = jnp.arange(n_blocks, dtype=I32) * MOE_BLK
    expert = jnp.minimum(jnp.sum(ends[None, :] <= first_row[:, None], axis=1), N_EXPERTS - 1).astype(I32)
    new = jnp.concatenate([jnp.ones((1,), I32), (expert[1:] != expert[:-1]).astype(I32)])
    valid = jnp.clip((starts + counts)[expert] - first_row, 0, MOE_BLK).astype(I32)
    return starts.astype(I32), expert, new, valid


def kernel(x, g_attn_norm, w_in, rel_bias, g_q_latent, w_q_up, g_kv_latent, w_kv_up, g_out_a, g_out_b, w_out,
           g_ffn_norm, w_router_group, b_router_group, w_router_expert, b_router_expert, w_gate, w_up, w_down,
           g_final):
    batch, seq, d = x.shape
    t = batch * seq
    assert g_attn_norm.shape[0] == 1 and d == D_MODEL and seq % ROW_TILE == 0
    cos_t, sin_t = _rope_tables(seq)
    tri = jnp.triu(jnp.ones((ROW_TILE, ROW_TILE), F32), 1).astype(BF16)
    n_blocks = t * TOP_K // MOE_BLK + N_EXPERTS
    x2 = x.reshape(t, d)
    row = lambda v: v.reshape(1, -1)

    w_in_l, wq_l, wkb_l, wvb_l = _layout_weights(w_in[0], w_q_up[0], w_kv_up[0])
    *qkv_a, qb, kb, vb = _proj_call(x2, row(g_attn_norm[0]), w_in_l, row(g_q_latent[0]), wq_l,
                                    row(g_kv_latent[0]), wkb_l, wvb_l, cos_t, sin_t, seq)
    oas, lses = [], []
    for pi, (window, dilation) in enumerate(DILATED_PATTERNS):
        bias = _dilated_bias(rel_bias, seq, dilation, window // (2 * dilation))
        o_p, lse_p = _dilated_call(qkv_a[pi], bias, batch, seq, dilation)
        oas.append(o_p)
        lses.append(lse_p)
    ob = _mla_call(qb, kb, vb, batch, seq)

    pad = ROUTER_ROWS - N_EXPERTS - N_GROUPS
    w_router = jnp.concatenate([w_router_expert[0], w_router_group[0], jnp.zeros((d, pad), F32)], axis=1).T
    b_router = jnp.concatenate([b_router_expert[0], b_router_group[0], jnp.zeros((pad,), F32)])
    x1, hp, idx, gates, cnt = _mix_call(oas, lses, ob, x2, row(g_out_a[0]), row(g_out_b[0]), w_out[0].astype(BF16),
                                        row(g_ffn_norm[0]), w_router, b_router.reshape(-1, 1), tri)
    pstart, block_expert, block_new, block_valid = _block_plan(cnt[:N_EXPERTS, 0], n_blocks)
    pstart_col = jnp.concatenate([pstart, jnp.zeros((ROUTER_ROWS - N_EXPERTS,), I32)]).reshape(-1, 1)
    dest = _dest_call(idx, pstart_col)
    dests = (dest[0], dest[1])
    buf = _dispatch_call(block_valid, dests, hp)
    expert_out = _expert_call(block_expert, block_new, block_valid, buf, w_gate[0], w_up[0], w_down[0])
    return _combine_call(dests, x1, gates[:TOP_K].T, row(g_final), expert_out).reshape(batch, seq, d)
```

```python
import functools
import math

import numpy as np
import jax
import jax.numpy as jnp
from jax import lax
from jax.experimental import pallas as pl
from jax.experimental.pallas import tpu as pltpu

F32 = jnp.float32
BF16 = jnp.bfloat16
I32 = jnp.int32
U32 = jnp.uint32

D_MODEL = 1024
EPS = 1e-6
NEG_INF = -1e30
LANES = 128
ROW_SUBLANES = D_MODEL // LANES
PACKED_SUBLANES = ROW_SUBLANES // 2

A_HEADS = 8
A_HEAD_DIM = 64
A_WIDTH = 512
A_QKV_WIDTH = 3 * A_WIDTH
DILATED_PATTERNS = ((128, 1), (512, 4), (2048, 16))
REL_BUCKETS = 32
REL_MAX_DISTANCE = 1024
A_QB = 128

B_HEADS = 8
B_Q_LORA = 256
B_KV_LORA = 128
B_QK_NOPE = 64
B_QK_ROPE = 32
B_V_DIM = 64
B_WIDTH = 512
ROPE_THETA = 10000.0
B_SCALE = (B_QK_NOPE + B_QK_ROPE) ** -0.5
B_QB = 512
B_SUB = 256
LOG2E = math.log2(math.e)

N_GROUPS = 4
EXPERTS_PER_GROUP = 8
N_EXPERTS = 32
TOP_K = 2
EXPERT_FF = 256
MOE_BLK = 512
ROUTER_ROWS = 40

ROW_TILE = 512
ISSUE_GROUP = 8
COMBINE_CHUNK = 32
PROJ_COLS = 2048

_NT = (((1,), (1,)), ((), ()))


def _cparams(semantics, vmem_mb=48, **kw):
    return pltpu.CompilerParams(dimension_semantics=semantics,
                                vmem_limit_bytes=vmem_mb * 1024 * 1024, **kw)


def _rms(x, g):
    return x * lax.rsqrt(jnp.mean(x * x, axis=-1, keepdims=True) + EPS) * g


def _lane_iota(rows=1):
    return lax.broadcasted_iota(I32, (rows, LANES), 1)


def _row_tile(r, sublanes=ROW_SUBLANES):
    return pl.ds(pl.multiple_of(r * sublanes, sublanes), sublanes)


def _proj_kernel(x_ref, g_ref, win_ref, gq_ref, wq_ref, gkv_ref, wkb_ref, wvb_ref, cos_ref, sin_ref, *refs):
    a_refs = refs[:len(DILATED_PATTERNS)]
    qb_ref, kb_ref, vb_ref, slab_ref = refs[len(a_refs):]
    tm = x_ref.shape[0]
    h = _rms(x_ref[...], g_ref[...]).astype(BF16)
    lo = _lane_iota() < A_HEAD_DIM
    n_slabs = A_QKV_WIDTH // LANES
    for s in range(0, n_slabs, 2):
        part = jnp.dot(h, win_ref[:, LANES * s:LANES * (s + 2)], preferred_element_type=F32)
        if LANES * s < A_WIDTH:
            part = part * LOG2E
        slab_ref[s] = part[:, :LANES]
        slab_ref[s + 1] = part[:, LANES:]
    lat = jnp.dot(h, win_ref[:, A_QKV_WIDTH:], preferred_element_type=F32)
    for qkv_out, (_, r) in zip(a_refs, DILATED_PATTERNS):
        for c in range(r):
            rows = pl.ds(c, tm // r, stride=r) if r > 1 else pl.ds(0, tm)
            for s in range(n_slabs):
                qkv_out[0, c, :, LANES * s:LANES * (s + 1)] = slab_ref[s, rows, :].astype(BF16)

    cos = cos_ref[...]
    sin = sin_ref[...]
    cq = _rms(lat[:, :B_Q_LORA], gq_ref[...]).astype(BF16)
    q = jnp.dot(cq, wq_ref[...], preferred_element_type=F32)
    q_mul = (cos + jnp.where(lo, 1.0, 0.0)) * (B_SCALE * LOG2E)
    q_rot = sin * (B_SCALE * LOG2E)
    for hd in range(B_HEADS):
        t = q[:, LANES * hd:LANES * (hd + 1)]
        qb_ref[:, LANES * hd:LANES * (hd + 1)] = (t * q_mul + pltpu.roll(t, 96, 1) * q_rot).astype(BF16)

    ckv = _rms(lat[:, B_Q_LORA:B_Q_LORA + B_KV_LORA], gkv_ref[...]).astype(BF16)
    kr = lat[:, B_Q_LORA + B_KV_LORA:]
    kr = kr * cos + pltpu.roll(kr, 96, 1) * sin
    kn = jnp.dot(ckv, wkb_ref[...], preferred_element_type=F32)
    for hd in range(B_HEADS):
        kb_ref[:, LANES * hd:LANES * (hd + 1)] = (kn[:, LANES * hd:LANES * (hd + 1)] + kr).astype(BF16)
    vb_ref[...] = jnp.dot(ckv, wvb_ref[...], preferred_element_type=F32).astype(BF16)


def _proj_call(x2, g_attn, w_in, g_q, w_q, g_kv, w_kb, w_vb, cos_t, sin_t, seq):
    t = x2.shape[0]
    tm = ROW_TILE
    nseq = seq // tm
    row = lambda i: (i, 0)
    const = lambda i: (0, 0)
    pos = lambda i: (i % nseq, 0)
    out = lambda w: jax.ShapeDtypeStruct((t, w), BF16)
    a_specs, a_shapes = [], []
    for _, r in DILATED_PATTERNS:
        a_specs.append(pl.BlockSpec((1, r, tm // r, A_QKV_WIDTH), lambda i: (i // nseq, 0, i % nseq, 0)))
        a_shapes.append(jax.ShapeDtypeStruct((t // seq, r, seq // r, A_QKV_WIDTH), BF16))
    return pl.pallas_call(
        _proj_kernel,
        grid=(t // tm,),
        in_specs=[
            pl.BlockSpec((tm, D_MODEL), row),
            pl.BlockSpec((1, D_MODEL), const),
            pl.BlockSpec((D_MODEL, PROJ_COLS), const),
            pl.BlockSpec((1, B_Q_LORA), const),
            pl.BlockSpec((B_Q_LORA, B_HEADS * LANES), const),
            pl.BlockSpec((1, B_KV_LORA), const),
            pl.BlockSpec((B_KV_LORA, B_HEADS * LANES), const),
            pl.BlockSpec((B_KV_LORA, B_WIDTH), const),
            pl.BlockSpec((tm, LANES), pos),
            pl.BlockSpec((tm, LANES), pos),
        ],
        out_specs=a_specs + [
            pl.BlockSpec((tm, B_HEADS * LANES), row),
            pl.BlockSpec((tm, B_HEADS * LANES), row),
            pl.BlockSpec((tm, B_WIDTH), row),
        ],
        out_shape=a_shapes + [out(B_HEADS * LANES), out(B_HEADS * LANES), out(B_WIDTH)],
        scratch_shapes=[pltpu.VMEM((A_QKV_WIDTH // LANES, tm, LANES), F32)],
        compiler_params=_cparams(("parallel",)),
        name="proj",
    )(x2, g_attn, w_in, g_q, w_q, g_kv, w_kb, w_vb, cos_t, sin_t)


def _pack_bf16_pair(a, b):
    a_bits = lax.bitcast_convert_type(a.astype(BF16).astype(F32), U32) >> 16
    b_bits = lax.bitcast_convert_type(b.astype(BF16).astype(F32), U32) & jnp.uint32(0xFFFF0000)
    return a_bits | b_bits


def _unpack_bf16_pair(w):
    return (lax.bitcast_convert_type(w << 16, F32), lax.bitcast_convert_type(w & jnp.uint32(0xFFFF0000), F32))


def _dilated_kernel(qkv_ref, bias_ref, o_ref, lse_ref, *, seq_len, dilation, key_width, group):
    nblk = seq_len // A_QB
    lo = _lane_iota() < A_HEAD_DIM
    first_class = pl.program_id(1) * group
    pairs = A_HEADS // 2

    def block(it, carry):
        c = it // nblk
        if nblk == 1:
            q0, ks, var = 0, 0, 0
        else:
            n = it % nblk
            q0 = pl.multiple_of(n * A_QB, A_QB)
            ks = pl.multiple_of(jnp.clip(q0 - 64, 0, seq_len - key_width), 64)
            var = jnp.where(n == 0, 0, jnp.where(n == nblk - 1, 2, 1))
        rows = pl.ds(q0, A_QB)
        keys = pl.ds(ks, key_width)
        if dilation == 1:
            out_rows = rows
        else:
            out_rows = pl.ds(first_class + c + dilation * q0, A_QB, stride=dilation)
        q_tiles = [qkv_ref[0, c, rows, LANES * p:LANES * (p + 1)] for p in range(pairs)]
        k_tiles = [qkv_ref[0, c, keys, A_WIDTH + LANES * p:A_WIDTH + LANES * (p + 1)] for p in range(pairs)]
        v_tiles = [qkv_ref[0, c, keys, 2 * A_WIDTH + LANES * p:2 * A_WIDTH + LANES * (p + 1)] for p in range(pairs)]
        zero = jnp.zeros((), BF16)
        k_heads = [jnp.where(lo, k_tiles[hd // 2], zero) if hd % 2 == 0 else jnp.where(lo, zero, k_tiles[hd // 2])
                   for hd in range(A_HEADS)]
        scores = [lax.dot_general(q_tiles[hd // 2], k_heads[hd], _NT, preferred_element_type=F32) + bias_ref[var, hd]
                  for hd in range(A_HEADS)]
        maxes = [jnp.max(s, axis=-1, keepdims=True) for s in scores]
        probs = [jnp.exp2(s - m) for s, m in zip(scores, maxes)]
        dens = [jnp.sum(pr, axis=-1, keepdims=True) for pr in probs]
        pvs = [jnp.dot(pr.astype(BF16), v_tiles[hd // 2], preferred_element_type=F32) for hd, pr in enumerate(probs)]
        outs = []
        for p in range(pairs):
            h0, h1 = 2 * p, 2 * p + 1
            outs.append(jnp.where(lo, pvs[h0] * (1.0 / dens[h0]), pvs[h1] * (1.0 / dens[h1])))
            lse_ref[0, p, out_rows, :] = jnp.where(lo, maxes[h0] + jnp.log2(dens[h0]), maxes[h1] + jnp.log2(dens[h1]))
        for j in range(pairs // 2):
            o_ref[0, j, out_rows, :] = _pack_bf16_pair(outs[2 * j], outs[2 * j + 1])
        return carry

    lax.fori_loop(0, group * nblk, block, 0, unroll=4)


def _dilated_call(qkv, bias, batch, seq, dilation):
    r = dilation
    sl = seq // r
    kw = min(2 * A_QB, sl)
    group = max(1, min(r, (4 * A_QB) // sl))
    pairs = A_HEADS // 2
    nat = lambda n: pl.BlockSpec((1, n, seq, LANES), lambda b, c: (b, 0, 0, 0))
    return pl.pallas_call(
        functools.partial(_dilated_kernel, seq_len=sl, dilation=r, key_width=kw, group=group),
        grid=(batch, r // group),
        in_specs=[pl.BlockSpec((1, group, sl, A_QKV_WIDTH), lambda b, c: (b, c, 0, 0)),
                  pl.BlockSpec(bias.shape, lambda b, c: (0, 0, 0, 0))],
        out_specs=[nat(pairs // 2), nat(pairs)],
        out_shape=[jax.ShapeDtypeStruct((batch, pairs // 2, seq, LANES), U32),
                   jax.ShapeDtypeStruct((batch, pairs, seq, LANES), F32)],
        compiler_params=_cparams(("parallel", "arbitrary")),
        name=f"dilated_r{r}",
    )(qkv, bias)


def _t5_bucket(rel):
    half = REL_BUCKETS // 2
    max_exact = half // 2
    n = np.abs(rel)
    large = max_exact + (np.log(np.maximum(n, 1) / max_exact)
                         / math.log(REL_MAX_DISTANCE / max_exact) * (half - max_exact)).astype(np.int32)
    large = np.minimum(large, half - 1)
    return (np.where(rel > 0, half, 0) + np.where(n < max_exact, n, large)).astype(np.int32)


def _dilated_bias(rel_bias, seq, dilation, half_steps):
    sl = seq // dilation
    kw = min(2 * A_QB, sl)
    offsets = [0] if sl == kw else [0, -half_steps, A_QB - kw]
    rel = np.stack([np.arange(kw)[None, :] + off - np.arange(A_QB)[:, None] for off in offsets])
    valid = np.abs(rel) <= half_steps
    bucket = np.where(valid, _t5_bucket(rel * dilation), REL_BUCKETS).astype(np.int32)
    onehot = (jnp.asarray(bucket)[..., None] == jnp.arange(REL_BUCKETS + 1, dtype=I32)).astype(F32)
    table = jnp.concatenate([rel_bias.astype(F32), jnp.full((1, A_HEADS), NEG_INF, F32)], axis=0)
    return jnp.einsum("vqkb,bh->vhqk", onehot, table * LOG2E, precision=lax.Precision.HIGHEST)


def _mla_kernel(q_ref, k_ref, v_ref, o_ref, v1_ref):
    lo = _lane_iota() < B_V_DIM
    sub = B_QB // B_SUB
    tiles = [slice(0, LANES), slice(LANES, 2 * LANES)]
    v1_ref[:, :LANES] = v_ref[0]
    v1_ref[:, LANES:] = jnp.ones((v_ref.shape[1], LANES), BF16)

    units = [(j, half) for j in range(sub) for half in range(2)]

    def rows(i):
        return [pl.ds(i * B_QB + B_SUB * j, B_SUB) for j in range(sub)]

    def scores(i):
        return [lax.dot_general(q_ref[0, rows(i)[j], tiles[half]], k_ref[0, :, tiles[half]], _NT,
                                preferred_element_type=F32) for j, half in units]

    def finish(i, sc):
        maxes = [jnp.max(s, axis=-1, keepdims=True) for s in sc]
        probs = [jnp.exp2(s - m) for s, m in zip(sc, maxes)]
        pvs = [jnp.dot(pr.astype(BF16), v1_ref[...], preferred_element_type=F32) for pr in probs]
        outs = [pv[:, :LANES] * (1.0 / pv[:, LANES:]) for pv in pvs]
        for j in range(sub):
            o_ref[0, rows(i)[j], :] = jnp.where(lo, outs[2 * j], outs[2 * j + 1]).astype(BF16)

    n_blocks = q_ref.shape[1] // B_QB
    sc = scores(0)
    for i in range(n_blocks):
        sc_next = scores(i + 1) if i + 1 < n_blocks else None
        finish(i, sc)
        sc = sc_next


def _mla_call(qb, kb, vb, batch, seq):
    qb = qb.reshape(batch, seq, B_HEADS * LANES)
    kb = kb.reshape(batch, seq, B_HEADS * LANES)
    vb = vb.reshape(batch, seq, B_WIDTH)
    pair = lambda w: pl.BlockSpec((1, seq, w), lambda b, p: (b, 0, p))
    out = pl.pallas_call(
        _mla_kernel,
        grid=(batch, B_HEADS // 2),
        in_specs=[pair(2 * LANES), pair(2 * LANES), pair(LANES)],
        out_specs=pair(LANES),
        out_shape=jax.ShapeDtypeStruct((batch, seq, B_WIDTH), BF16),
        scratch_shapes=[pltpu.VMEM((seq, 2 * LANES), BF16)],
        compiler_params=_cparams(("parallel", "parallel")),
        name="mla",
    )(qb, kb, vb)
    return out.reshape(batch * seq, B_WIDTH)


def _merge_patterns(o_refs, lse_refs):
    tiles = []
    for j in range(A_HEADS // 4):
        outs = [_unpack_bf16_pair(r[0, j]) for r in o_refs]
        for half in range(2):
            lses = [r[0, 2 * j + half] for r in lse_refs]
            top = functools.reduce(jnp.maximum, lses)
            es = [jnp.exp2(l - top) for l in lses]
            inv = 1.0 / functools.reduce(jnp.add, es)
            tiles.append(functools.reduce(jnp.add, [e * inv * o[half] for e, o in zip(es, outs)]))
    return jnp.concatenate(tiles, axis=1)


def _mix_kernel(o1_ref, o4_ref, o16_ref, l1_ref, l4_ref, l16_ref, ob_ref, x_ref, ga_ref, gb_ref, wo_ref,
                gf_ref, wr_ref, br_ref, tri_ref, x1_ref, hp_ref, idx_ref, gate_ref, cnt_ref, carry_ref):
    i = pl.program_id(0)

    @pl.when(i == 0)
    def _():
        carry_ref[...] = jnp.zeros_like(carry_ref)

    oa = _merge_patterns((o1_ref, o4_ref, o16_ref), (l1_ref, l4_ref, l16_ref))
    a = _rms(oa, ga_ref[...]).astype(BF16)
    b = _rms(ob_ref[...].astype(F32), gb_ref[...]).astype(BF16)
    mix = (jnp.dot(a, wo_ref[0:A_WIDTH, :], preferred_element_type=F32)
           + jnp.dot(b, wo_ref[A_WIDTH:, :], preferred_element_type=F32))
    x1 = x_ref[...] + mix
    x1_ref[...] = x1
    h2 = _rms(x1, gf_ref[...])
    half = D_MODEL // 2
    for c in range(PACKED_SUBLANES):
        hp_ref[pl.ds(c, h2.shape[0], stride=PACKED_SUBLANES), :] = _pack_bf16_pair(
            h2[:, LANES * c:LANES * (c + 1)], h2[:, half + LANES * c:half + LANES * (c + 1)])

    wr = wr_ref[...]
    wr_hi = wr.astype(BF16)
    wr_lo = (wr - wr_hi.astype(F32)).astype(BF16)
    h_hi = h2.astype(BF16)
    h_lo = (h2 - h_hi.astype(F32)).astype(BF16)
    lg_hi = lax.dot_general(jnp.concatenate([wr_hi, wr_lo], axis=0), h_hi, _NT, preferred_element_type=F32)
    lg = (lg_hi[:ROUTER_ROWS] + lg_hi[ROUTER_ROWS:]
          + lax.dot_general(wr_hi, h_lo, _NT, preferred_element_type=F32) + br_ref[...])
    row = lax.broadcasted_iota(I32, lg.shape, 0)
    is_g = (row >= N_EXPERTS) & (row < N_EXPERTS + N_GROUPS)
    gl = jnp.where(is_g, lg, NEG_INF)
    ge = jnp.exp(gl - jnp.max(gl, axis=0, keepdims=True))
    gp = ge / jnp.sum(ge, axis=0, keepdims=True)
    g_gate = jnp.max(gp, axis=0, keepdims=True)
    g_idx = jnp.min(jnp.where(is_g & (gp == g_gate), row - N_EXPERTS, LANES), axis=0, keepdims=True)
    sel = (row >> 3) == g_idx
    el = jnp.where(sel, lg, NEG_INF)
    ee = jnp.exp(el - jnp.max(el, axis=0, keepdims=True))
    ep = jnp.where(sel, ee / jnp.sum(ee, axis=0, keepdims=True), -1.0)
    p1 = jnp.max(ep, axis=0, keepdims=True)
    i1 = jnp.min(jnp.where(ep == p1, row, LANES), axis=0, keepdims=True)
    ep2 = jnp.where(row == i1, -1.0, ep)
    p2 = jnp.max(ep2, axis=0, keepdims=True)
    i2 = jnp.min(jnp.where(sel & (ep2 == p2) & (row != i1), row, LANES), axis=0, keepdims=True)
    den = p1 + p2
    g1 = g_gate * p1 / den
    g2 = g_gate * p2 / den

    hit1 = row == i1
    hit2 = row == i2
    onehot = jnp.where(hit1 | hit2, 1.0, 0.0)
    before = jnp.dot(onehot.astype(BF16), tri_ref[...], preferred_element_type=F32) + carry_ref[...]
    r1 = jnp.sum(jnp.where(hit1, before, 0.0), axis=0, keepdims=True).astype(I32)
    r2 = jnp.sum(jnp.where(hit2, before, 0.0), axis=0, keepdims=True).astype(I32)
    carry_ref[...] += jnp.sum(onehot, axis=1, keepdims=True)

    row8 = lax.broadcasted_iota(I32, idx_ref.shape, 0)
    idx_ref[...] = jnp.where(row8 == 0, i1, jnp.where(row8 == 1, i2,
                             jnp.where(row8 == 2, r1, jnp.where(row8 == 3, r2, 0))))
    gate_ref[...] = jnp.where(row8 == 0, g1, jnp.where(row8 == 1, g2, 0.0))

    @pl.when(i == pl.num_programs(0) - 1)
    def _():
        cnt_ref[...] = jnp.broadcast_to(carry_ref[...], cnt_ref.shape).astype(I32)


def _mix_call(oas, lses, ob, x2, g_a, g_b, w_out, g_ffn, w_router, b_router, tri):
    t = x2.shape[0]
    tm = ROW_TILE
    nseq = oas[0].shape[2] // tm
    row = lambda i: (i, 0)
    const = lambda i: (0, 0)
    slab = lambda n: pl.BlockSpec((1, n, tm, LANES), lambda i: (i // nseq, 0, i % nseq, 0))
    o_slab, lse_slab = slab(A_HEADS // 4), slab(A_HEADS // 2)
    return pl.pallas_call(
        _mix_kernel,
        grid=(t // tm,),
        in_specs=[
            o_slab, o_slab, o_slab, lse_slab, lse_slab, lse_slab,
            pl.BlockSpec((tm, B_WIDTH), row),
            pl.BlockSpec((tm, D_MODEL), row),
            pl.BlockSpec((1, A_WIDTH), const),
            pl.BlockSpec((1, B_WIDTH), const),
            pl.BlockSpec((D_MODEL, D_MODEL), const),
            pl.BlockSpec((1, D_MODEL), const),
            pl.BlockSpec((ROUTER_ROWS, D_MODEL), const),
            pl.BlockSpec((ROUTER_ROWS, 1), const),
            pl.BlockSpec((tm, tm), const),
        ],
        out_specs=[
            pl.BlockSpec((tm, D_MODEL), row),
            pl.BlockSpec((tm * PACKED_SUBLANES, LANES), row),
            pl.BlockSpec((8, tm), lambda i: (0, i)),
            pl.BlockSpec((8, tm), lambda i: (0, i)),
            pl.BlockSpec((ROUTER_ROWS, LANES), const),
        ],
        out_shape=[
            jax.ShapeDtypeStruct((t, D_MODEL), F32),
            jax.ShapeDtypeStruct((t * PACKED_SUBLANES, LANES), U32),
            jax.ShapeDtypeStruct((8, t), I32),
            jax.ShapeDtypeStruct((8, t), F32),
            jax.ShapeDtypeStruct((ROUTER_ROWS, LANES), I32),
        ],
        scratch_shapes=[pltpu.VMEM((ROUTER_ROWS, 1), F32)],
        compiler_params=_cparams(("arbitrary",)),
        name="mix_router",
    )(*oas, *lses, ob, x2, g_a, g_b, w_out, g_ffn, w_router, b_router, tri)


def _dest_kernel(idx_ref, pstart_ref, dest_ref):
    idx = idx_ref[...]
    row = lax.broadcasted_iota(I32, (ROUTER_ROWS, idx.shape[1]), 0)
    ps = pstart_ref[...]

    def slot(k):
        return jnp.sum(jnp.where(row == idx[k:k + 1, :], ps, 0), axis=0, keepdims=True) + idx[2 + k:3 + k, :]

    row8 = lax.broadcasted_iota(I32, idx.shape, 0)
    dest_ref[...] = jnp.where(row8 == 0, slot(0), jnp.where(row8 == 1, slot(1), 0))


def _dest_call(idx, pstart):
    t = idx.shape[1]
    tm = 4 * ROW_TILE
    return pl.pallas_call(
        _dest_kernel,
        grid=(t // tm,),
        in_specs=[pl.BlockSpec((8, tm), lambda i: (0, i)), pl.BlockSpec((ROUTER_ROWS, 1), lambda i: (0, 0))],
        out_specs=pl.BlockSpec((8, tm), lambda i: (0, i)),
        out_shape=jax.ShapeDtypeStruct((8, t), I32),
        compiler_params=_cparams(("parallel",)),
        name="dest_rows",
    )(idx, pstart)


def _dispatch_kernel(valid_ref, d0_ref, d1_ref, h_ref, buf_ref, zero_ref, sem, pad_sem):
    i = pl.program_id(0)
    tt = h_ref.shape[0] // PACKED_SUBLANES
    n_blocks = valid_ref.shape[0]

    def for_padded_blocks(fn):
        def body(j, c):
            taken = valid_ref[j]
            even = (taken + 1) // 2 * 2

            @pl.when(even < MOE_BLK)
            def _():
                n_pad = pl.multiple_of((MOE_BLK - even) * PACKED_SUBLANES, ROW_SUBLANES)
                first = pl.multiple_of((j * MOE_BLK + even) * PACKED_SUBLANES, ROW_SUBLANES)
                fn(pltpu.make_async_copy(zero_ref.at[pl.ds(0, n_pad)], buf_ref.at[pl.ds(first, n_pad)], pad_sem))

            @pl.when(even != taken)
            def _():
                fn(pltpu.make_async_copy(zero_ref.at[pl.ds(0, PACKED_SUBLANES)],
                                         buf_ref.at[_row_tile(j * MOE_BLK + taken, PACKED_SUBLANES)], pad_sem))
            return c
        lax.fori_loop(0, n_blocks, body, 0)

    @pl.when(i == 0)
    def _():
        zero_ref[...] = jnp.zeros_like(zero_ref)
        for_padded_blocks(lambda cp: cp.start())

    def issue(g, c):
        base = pl.multiple_of(g * ISSUE_GROUP, ISSUE_GROUP)
        for j in range(ISSUE_GROUP):
            for prio, d_ref in enumerate((d0_ref, d1_ref)):
                pltpu.make_async_copy(h_ref.at[_row_tile(base + j, PACKED_SUBLANES)],
                                      buf_ref.at[_row_tile(d_ref[base + j], PACKED_SUBLANES)],
                                      sem).start(priority=prio)
        return c

    lax.fori_loop(0, tt // ISSUE_GROUP, issue, 0)
    for k in range(TOP_K):
        pltpu.make_async_copy(h_ref, buf_ref.at[pl.ds(0, tt * PACKED_SUBLANES)], sem).wait()

    @pl.when(i == pl.num_programs(0) - 1)
    def _():
        for_padded_blocks(lambda cp: cp.wait())


def _dispatch_call(block_valid, dests, hp):
    t = hp.shape[0] // PACKED_SUBLANES
    tt = 2 * ROW_TILE
    n_rows = block_valid.shape[0] * MOE_BLK
    return pl.pallas_call(
        _dispatch_kernel,
        grid_spec=pltpu.PrefetchScalarGridSpec(
            num_scalar_prefetch=1,
            grid=(t // tt,),
            in_specs=[
                pl.BlockSpec((tt,), lambda i, va: (i,), memory_space=pltpu.SMEM),
                pl.BlockSpec((tt,), lambda i, va: (i,), memory_space=pltpu.SMEM),
                pl.BlockSpec((tt * PACKED_SUBLANES, LANES), lambda i, va: (i, 0)),
            ],
            out_specs=pl.BlockSpec(memory_space=pl.ANY),
            scratch_shapes=[pltpu.VMEM((MOE_BLK * PACKED_SUBLANES, LANES), U32),
                            pltpu.SemaphoreType.DMA(()), pltpu.SemaphoreType.DMA(())],
        ),
        out_shape=jax.ShapeDtypeStruct((n_rows * PACKED_SUBLANES, LANES), U32),
        compiler_params=_cparams(("arbitrary",), disable_bounds_checks=True, has_side_effects=True),
        name="dispatch",
    )(block_valid, *dests, hp)


def _expert_kernel(be_ref, new_ref, valid_ref, buf_ref, wg_ref, wu_ref, wd_ref, out_ref, wg_s, wu_s, wd_s):
    j = pl.program_id(0)
    del be_ref

    @pl.when(new_ref[j] == 1)
    def _():
        wg_s[...] = wg_ref[0].astype(BF16)
        wu_s[...] = wu_ref[0].astype(BF16)
        wd_s[...] = wd_ref[0].astype(BF16)

    n_valid = valid_ref[j]

    @pl.when(n_valid > 0)
    def _():
        blk = buf_ref.shape[0] // PACKED_SUBLANES
        words = [_unpack_bf16_pair(buf_ref[pl.ds(c, blk, stride=PACKED_SUBLANES), :]) for c in range(PACKED_SUBLANES)]
        x = jnp.concatenate([w[0] for w in words] + [w[1] for w in words], axis=1).astype(BF16)
        cols = [pl.ds(c, blk, stride=ROW_SUBLANES) for c in range(ROW_SUBLANES)]
        g = jnp.dot(x, wg_s[...], preferred_element_type=F32)
        u = jnp.dot(x, wu_s[...], preferred_element_type=F32)
        hb = (g * jax.nn.sigmoid(g)) * u
        out = jnp.dot(hb.astype(BF16), wd_s[...], preferred_element_type=F32)
        for c, rows in enumerate(cols):
            out_ref[rows, :] = out[:, LANES * c:LANES * (c + 1)]

    @pl.when(n_valid == 0)
    def _():
        out_ref[...] = jnp.zeros_like(out_ref)


def _expert_call(block_expert, block_new, block_valid, buf, w_gate, w_up, w_down):
    nb = buf.shape[0] // (MOE_BLK * PACKED_SUBLANES)
    wsel = lambda j, be, nw, va: (be[j], 0, 0)
    rows = lambda sublanes: pl.BlockSpec((MOE_BLK * sublanes, LANES), lambda j, be, nw, va: (j, 0))
    return pl.pallas_call(
        _expert_kernel,
        grid_spec=pltpu.PrefetchScalarGridSpec(
            num_scalar_prefetch=3,
            grid=(nb,),
            in_specs=[
                rows(PACKED_SUBLANES),
                pl.BlockSpec((1, D_MODEL, EXPERT_FF), wsel),
                pl.BlockSpec((1, D_MODEL, EXPERT_FF), wsel),
                pl.BlockSpec((1, EXPERT_FF, D_MODEL), wsel),
            ],
            out_specs=rows(ROW_SUBLANES),
            scratch_shapes=[pltpu.VMEM((D_MODEL, EXPERT_FF), BF16),
                            pltpu.VMEM((D_MODEL, EXPERT_FF), BF16),
                            pltpu.VMEM((EXPERT_FF, D_MODEL), BF16)],
        ),
        out_shape=jax.ShapeDtypeStruct((nb * MOE_BLK * ROW_SUBLANES, LANES), F32),
        compiler_params=_cparams(("arbitrary",)),
        name="experts",
    )(block_expert, block_new, block_valid, buf, w_gate, w_up, w_down)


def _combine_kernel(d0_ref, d1_ref, d0_next_ref, d1_next_ref, d0_ahead_ref, d1_ahead_ref, x1_ref, gate_ref,
                    gf_ref, eo_ref, o_ref, rows_a, rows_b, rows_c, sems):
    i = pl.program_id(0)
    tt = x1_ref.shape[0]
    bufs = (rows_a, rows_b, rows_c)
    n_buf = len(bufs)

    def start_row(d_refs, slot, r):
        for k, d_ref in enumerate(d_refs):
            pltpu.make_async_copy(eo_ref.at[_row_tile(d_ref[r])], bufs[slot].at[k, _row_tile(r)],
                                  sems.at[slot]).start(priority=k)

    def drain(slot):
        for k in range(TOP_K):
            pltpu.make_async_copy(eo_ref.at[pl.ds(0, tt * ROW_SUBLANES)], bufs[slot].at[k], sems.at[slot]).wait()

    @pl.when(i == 0)
    def _():
        def issue(g, c):
            base = pl.multiple_of(g * ISSUE_GROUP, ISSUE_GROUP)
            for j in range(ISSUE_GROUP):
                start_row((d0_ref, d1_ref), 0, base + j)
                start_row((d0_next_ref, d1_next_ref), 1, base + j)
            return c
        lax.fori_loop(0, tt // ISSUE_GROUP, issue, 0)

    def step(slot):
        ahead = (slot + 2) % n_buf
        drain(slot)
        gf = gf_ref[...]
        for ch in range(tt // COMBINE_CHUNK):
            r0 = ch * COMBINE_CHUNK
            for j in range(COMBINE_CHUNK):
                start_row((d0_ahead_ref, d1_ahead_ref), ahead, r0 + j)
            gate = gate_ref[r0:r0 + COMBINE_CHUNK, :]

            def rows(k):
                return jnp.concatenate(
                    [bufs[slot][k, pl.ds(ROW_SUBLANES * r0 + c, COMBINE_CHUNK, stride=ROW_SUBLANES), :]
                     for c in range(ROW_SUBLANES)], axis=1)

            y = rows(0) * gate[:, 0:1] + rows(1) * gate[:, 1:2]
            o_ref[r0:r0 + COMBINE_CHUNK, :] = _rms(x1_ref[r0:r0 + COMBINE_CHUNK, :] + y, gf)

        @pl.when(i == pl.num_programs(0) - 1)
        def _():
            drain((slot + 1) % n_buf)
            drain(ahead)

    for phase in range(n_buf):
        pl.when(i % n_buf == phase)(functools.partial(step, phase))


def _combine_call(dests, x1, gates, g_final, expert_out):
    t = x1.shape[0]
    tt = ROW_TILE // 2
    last = t // tt - 1
    tile = lambda ahead: pl.BlockSpec((tt,), lambda i: (jnp.minimum(i + ahead, last),), memory_space=pltpu.SMEM)
    buf = pltpu.VMEM((TOP_K, tt * ROW_SUBLANES, LANES), F32)
    return pl.pallas_call(
        _combine_kernel,
        grid=(t // tt,),
        in_specs=[
            tile(0), tile(0), tile(1), tile(1), tile(2), tile(2),
            pl.BlockSpec((tt, D_MODEL), lambda i: (i, 0)),
            pl.BlockSpec((tt, TOP_K), lambda i: (i, 0)),
            pl.BlockSpec((1, D_MODEL), lambda i: (0, 0)),
            pl.BlockSpec(memory_space=pl.ANY),
        ],
        out_specs=pl.BlockSpec((tt, D_MODEL), lambda i: (i, 0)),
        out_shape=jax.ShapeDtypeStruct((t, D_MODEL), F32),
        scratch_shapes=[buf, buf, buf, pltpu.SemaphoreType.DMA((3,))],
        compiler_params=_cparams(("arbitrary",), disable_bounds_checks=True),
        name="combine",
    )(*dests, *dests, *dests, x1, gates, g_final, expert_out)


def _rope_tables(seq):
    half = B_QK_ROPE // 2
    inv_freq = ROPE_THETA ** (-(jnp.arange(half, dtype=F32) / half))
    ang = jnp.arange(seq, dtype=F32)[:, None] * inv_freq[None, :]
    cos, sin = jnp.cos(ang), jnp.sin(ang)
    z = jnp.zeros((seq, B_QK_NOPE), F32)
    z2 = jnp.zeros((seq, B_QK_ROPE), F32)
    return (jnp.concatenate([z, cos, cos, z2], axis=1), jnp.concatenate([z, -sin, sin, z2], axis=1))


def _swap_halves(w):
    half = w.shape[-1] // 2
    return jnp.concatenate([w[..., half:], w[..., :half]], axis=-1)


def _layout_weights(w_in, w_q_up, w_kv_up):
    d = w_in.shape[0]
    w_kr = w_in[:, 3 * A_WIDTH + B_Q_LORA + B_KV_LORA:]
    w_in_l = jnp.concatenate(
        [w_in[:, :A_WIDTH] * (A_HEAD_DIM ** -0.5), w_in[:, A_WIDTH:3 * A_WIDTH + B_Q_LORA + B_KV_LORA],
         jnp.zeros((d, B_QK_NOPE), F32), w_kr, _swap_halves(w_kr)], axis=1).astype(BF16)
    wq = w_q_up.reshape(B_Q_LORA, B_HEADS, B_QK_NOPE + B_QK_ROPE)
    wq_l = jnp.concatenate([wq, _swap_halves(wq[..., B_QK_NOPE:])], axis=-1)
    wq_l = wq_l.reshape(B_Q_LORA, B_HEADS * LANES).astype(BF16)
    wkv = w_kv_up.reshape(B_KV_LORA, B_HEADS, B_QK_NOPE + B_V_DIM)
    wkb = jnp.concatenate([wkv[..., :B_QK_NOPE], jnp.zeros_like(wkv[..., :B_QK_NOPE])], axis=-1)
    wkb = wkb.reshape(B_KV_LORA, B_HEADS * LANES).astype(BF16)
    wvb = wkv[..., B_QK_NOPE:].reshape(B_KV_LORA, B_WIDTH).astype(BF16)
    return w_in_l, wq_l, wkb, wvb


def _block_plan(counts, n_blocks):
    padded = (counts + MOE_BLK - 1) // MOE_BLK * MOE_BLK
    ends = jnp.cumsum(padded)
    starts = ends - padded
    first_row = jnp.arange(n_blocks, dtype=I32) * MOE_BLK
    expert = jnp.minimum(jnp.sum(ends[None, :] <= first_row[:, None], axis=1), N_EXPERTS - 1).astype(I32)
    new = jnp.concatenate([jnp.ones((1,), I32), (expert[1:] != expert[:-1]).astype(I32)])
    valid = jnp.clip((starts + counts)[expert] - first_row, 0, MOE_BLK).astype(I32)
    return starts.astype(I32), expert, new, valid


def kernel(x, g_attn_norm, w_in, rel_bias, g_q_latent, w_q_up, g_kv_latent, w_kv_up, g_out_a, g_out_b, w_out,
           g_ffn_norm, w_router_group, b_router_group, w_router_expert, b_router_expert, w_gate, w_up, w_down,
           g_final):
    batch, seq, d = x.shape
    t = batch * seq
    assert g_attn_norm.shape[0] == 1 and d == D_MODEL and seq % ROW_TILE == 0
    cos_t, sin_t = _rope_tables(seq)
    tri = jnp.triu(jnp.ones((ROW_TILE, ROW_TILE), F32), 1).astype(BF16)
    n_blocks = t * TOP_K // MOE_BLK + N_EXPERTS
    x2 = x.reshape(t, d)
    row = lambda v: v.reshape(1, -1)

    w_in_l, wq_l, wkb_l, wvb_l = _layout_weights(w_in[0], w_q_up[0], w_kv_up[0])
    *qkv_a, qb, kb, vb = _proj_call(x2, row(g_attn_norm[0]), w_in_l, row(g_q_latent[0]), wq_l,
                                    row(g_kv_latent[0]), wkb_l, wvb_l, cos_t, sin_t, seq)
    oas, lses = [], []
    for pi, (window, dilation) in enumerate(DILATED_PATTERNS):
        bias = _dilated_bias(rel_bias, seq, dilation, window // (2 * dilation))
        o_p, lse_p = _dilated_call(qkv_a[pi], bias, batch, seq, dilation)
        oas.append(o_p)
        lses.append(lse_p)
    ob = _mla_call(qb, kb, vb, batch, seq)

    pad = ROUTER_ROWS - N_EXPERTS - N_GROUPS
    w_router = jnp.concatenate([w_router_expert[0], w_router_group[0], jnp.zeros((d, pad), F32)], axis=1).T
    b_router = jnp.concatenate([b_router_expert[0], b_router_group[0], jnp.zeros((pad,), F32)])
    x1, hp, idx, gates, cnt = _mix_call(oas, lses, ob, x2, row(g_out_a[0]), row(g_out_b[0]), w_out[0].astype(BF16),
                                        row(g_ffn_norm[0]), w_router, b_router.reshape(-1, 1), tri)
    pstart, block_expert, block_new, block_valid = _block_plan(cnt[:N_EXPERTS, 0], n_blocks)
    pstart_col = jnp.concatenate([pstart, jnp.zeros((ROUTER_ROWS - N_EXPERTS,), I32)]).reshape(-1, 1)
    dest = _dest_call(idx, pstart_col)
    dests = (dest[0], dest[1])
    buf = _dispatch_call(block_valid, dests, hp)
    expert_out = _expert_call(block_expert, block_new, block_valid, buf, w_gate[0], w_up[0], w_down[0])
    return _combine_call(dests, x1, gates[:TOP_K].T, row(g_final), expert_out).reshape(batch, seq, d)
```

```python
import functools
import math

import numpy as np
import jax
import jax.numpy as jnp
from jax import lax
from jax.experimental import pallas as pl
from jax.experimental.pallas import tpu as pltpu

F32 = jnp.float32
BF16 = jnp.bfloat16
I32 = jnp.int32
U32 = jnp.uint32

D_MODEL = 1024
EPS = 1e-6
NEG_INF = -1e30
LANES = 128
ROW_SUBLANES = D_MODEL // LANES
PACKED_SUBLANES = ROW_SUBLANES // 2

A_HEADS = 8
A_HEAD_DIM = 64
A_WIDTH = 512
A_QKV_WIDTH = 3 * A_WIDTH
DILATED_PATTERNS = ((128, 1), (512, 4), (2048, 16))
REL_BUCKETS = 32
REL_MAX_DISTANCE = 1024
A_QB = 128

B_HEADS = 8
B_Q_LORA = 256
B_KV_LORA = 128
B_QK_NOPE = 64
B_QK_ROPE = 32
B_V_DIM = 64
B_WIDTH = 512
ROPE_THETA = 10000.0
B_SCALE = (B_QK_NOPE + B_QK_ROPE) ** -0.5
B_QB = 512
B_SUB = 256
LOG2E = math.log2(math.e)

N_GROUPS = 4
EXPERTS_PER_GROUP = 8
N_EXPERTS = 32
TOP_K = 2
EXPERT_FF = 256
MOE_BLK = 512
ROUTER_ROWS = 40

ROW_TILE = 512
ISSUE_GROUP = 8
COMBINE_CHUNK = 32
PROJ_COLS = 2048

_NT = (((1,), (1,)), ((), ()))


def _cparams(semantics, vmem_mb=48, **kw):
    return pltpu.CompilerParams(dimension_semantics=semantics,
                                vmem_limit_bytes=vmem_mb * 1024 * 1024, **kw)


def _rms(x, g):
    return x * lax.rsqrt(jnp.mean(x * x, axis=-1, keepdims=True) + EPS) * g


def _lane_iota(rows=1):
    return lax.broadcasted_iota(I32, (rows, LANES), 1)


def _row_tile(r, sublanes=ROW_SUBLANES):
    return pl.ds(pl.multiple_of(r * sublanes, sublanes), sublanes)


def _proj_kernel(x_ref, g_ref, win_ref, gq_ref, wq_ref, gkv_ref, wkb_ref, wvb_ref, cos_ref, sin_ref, *refs):
    out1, out4, out16 = refs[:3]
    qb_ref, kb_ref, vb_ref, slab_ref, slab4_ref = refs[3:]
    tm = x_ref.shape[0]
    h = _rms(x_ref[...], g_ref[...]).astype(BF16)
    lo = _lane_iota() < A_HEAD_DIM
    n_slabs = A_QKV_WIDTH // LANES
    for s in range(0, n_slabs, 2):
        part = jnp.dot(h, win_ref[:, LANES * s:LANES * (s + 2)], preferred_element_type=F32)
        if LANES * s < A_WIDTH:
            part = part * LOG2E
        for half in range(2):
            col = part[:, LANES * half:LANES * (half + 1)]
            slab_ref[s + half] = col
            out1[0, 0, :, LANES * (s + half):LANES * (s + half + 1)] = col.astype(BF16)
    lat = jnp.dot(h, win_ref[:, A_QKV_WIDTH:], preferred_element_type=F32)
    n4 = tm // 4
    for c4 in range(4):
        for s in range(n_slabs):
            piece = slab_ref[s, pl.ds(c4, n4, stride=4), :]
            slab4_ref[s, c4] = piece
            out4[0, c4, :, LANES * s:LANES * (s + 1)] = piece.astype(BF16)
    for c4 in range(4):
        for j in range(4):
            for s in range(n_slabs):
                out16[0, c4 + 4 * j, :, LANES * s:LANES * (s + 1)] = (
                    slab4_ref[s, c4, pl.ds(j, n4 // 4, stride=4), :].astype(BF16))

    cos = cos_ref[...]
    sin = sin_ref[...]
    cq = _rms(lat[:, :B_Q_LORA], gq_ref[...]).astype(BF16)
    q = jnp.dot(cq, wq_ref[...], preferred_element_type=F32)
    q_mul = (cos + jnp.where(lo, 1.0, 0.0)) * (B_SCALE * LOG2E)
    q_rot = sin * (B_SCALE * LOG2E)
    for hd in range(B_HEADS):
        t = q[:, LANES * hd:LANES * (hd + 1)]
        qb_ref[:, LANES * hd:LANES * (hd + 1)] = (t * q_mul + pltpu.roll(t, 96, 1) * q_rot).astype(BF16)

    ckv = _rms(lat[:, B_Q_LORA:B_Q_LORA + B_KV_LORA], gkv_ref[...]).astype(BF16)
    kr = lat[:, B_Q_LORA + B_KV_LORA:]
    kr = kr * cos + pltpu.roll(kr, 96, 1) * sin
    kn = jnp.dot(ckv, wkb_ref[...], preferred_element_type=F32)
    for hd in range(B_HEADS):
        kb_ref[:, LANES * hd:LANES * (hd + 1)] = (kn[:, LANES * hd:LANES * (hd + 1)] + kr).astype(BF16)
    vb_ref[...] = jnp.dot(ckv, wvb_ref[...], preferred_element_type=F32).astype(BF16)


def _proj_call(x2, g_attn, w_in, g_q, w_q, g_kv, w_kb, w_vb, cos_t, sin_t, seq):
    t = x2.shape[0]
    tm = ROW_TILE
    nseq = seq // tm
    row = lambda i: (i, 0)
    const = lambda i: (0, 0)
    pos = lambda i: (i % nseq, 0)
    out = lambda w: jax.ShapeDtypeStruct((t, w), BF16)
    a_specs, a_shapes = [], []
    for _, r in DILATED_PATTERNS:
        a_specs.append(pl.BlockSpec((1, r, tm // r, A_QKV_WIDTH), lambda i: (i // nseq, 0, i % nseq, 0)))
        a_shapes.append(jax.ShapeDtypeStruct((t // seq, r, seq // r, A_QKV_WIDTH), BF16))
    return pl.pallas_call(
        _proj_kernel,
        grid=(t // tm,),
        in_specs=[
            pl.BlockSpec((tm, D_MODEL), row),
            pl.BlockSpec((1, D_MODEL), const),
            pl.BlockSpec((D_MODEL, PROJ_COLS), const),
            pl.BlockSpec((1, B_Q_LORA), const),
            pl.BlockSpec((B_Q_LORA, B_HEADS * LANES), const),
            pl.BlockSpec((1, B_KV_LORA), const),
            pl.BlockSpec((B_KV_LORA, B_HEADS * LANES), const),
            pl.BlockSpec((B_KV_LORA, B_WIDTH), const),
            pl.BlockSpec((tm, LANES), pos),
            pl.BlockSpec((tm, LANES), pos),
        ],
        out_specs=a_specs + [
            pl.BlockSpec((tm, B_HEADS * LANES), row),
            pl.BlockSpec((tm, B_HEADS * LANES), row),
            pl.BlockSpec((tm, B_WIDTH), row),
        ],
        out_shape=a_shapes + [out(B_HEADS * LANES), out(B_HEADS * LANES), out(B_WIDTH)],
        scratch_shapes=[pltpu.VMEM((A_QKV_WIDTH // LANES, tm, LANES), F32),
                        pltpu.VMEM((A_QKV_WIDTH // LANES, 4, tm // 4, LANES), F32)],
        compiler_params=_cparams(("parallel",)),
        name="proj",
    )(x2, g_attn, w_in, g_q, w_q, g_kv, w_kb, w_vb, cos_t, sin_t)


def _pack_bf16_pair(a, b):
    a_bits = lax.bitcast_convert_type(a.astype(BF16).astype(F32), U32) >> 16
    b_bits = lax.bitcast_convert_type(b.astype(BF16).astype(F32), U32) & jnp.uint32(0xFFFF0000)
    return a_bits | b_bits


def _unpack_bf16_pair(w):
    return (lax.bitcast_convert_type(w << 16, F32), lax.bitcast_convert_type(w & jnp.uint32(0xFFFF0000), F32))


def _dilated_kernel(qkv_ref, bias_ref, o_ref, lse_ref, *, seq_len, dilation, key_width, group):
    nblk = seq_len // A_QB
    lo = _lane_iota() < A_HEAD_DIM
    first_class = pl.program_id(1) * group
    pairs = A_HEADS // 2

    def block(it, carry):
        c = it // nblk
        if nblk == 1:
            q0, ks, var = 0, 0, 0
        else:
            n = it % nblk
            q0 = pl.multiple_of(n * A_QB, A_QB)
            ks = pl.multiple_of(jnp.clip(q0 - 64, 0, seq_len - key_width), 64)
            var = jnp.where(n == 0, 0, jnp.where(n == nblk - 1, 2, 1))
        rows = pl.ds(q0, A_QB)
        keys = pl.ds(ks, key_width)
        if dilation == 1:
            out_rows = rows
        else:
            out_rows = pl.ds(first_class + c + dilation * q0, A_QB, stride=dilation)
        q_tiles = [qkv_ref[0, c, rows, LANES * p:LANES * (p + 1)] for p in range(pairs)]
        k_tiles = [qkv_ref[0, c, keys, A_WIDTH + LANES * p:A_WIDTH + LANES * (p + 1)] for p in range(pairs)]
        v_tiles = [qkv_ref[0, c, keys, 2 * A_WIDTH + LANES * p:2 * A_WIDTH + LANES * (p + 1)] for p in range(pairs)]
        zero = jnp.zeros((), BF16)
        k_heads = [jnp.where(lo, k_tiles[hd // 2], zero) if hd % 2 == 0 else jnp.where(lo, zero, k_tiles[hd // 2])
                   for hd in range(A_HEADS)]
        scores = [lax.dot_general(q_tiles[hd // 2], k_heads[hd], _NT, preferred_element_type=F32) + bias_ref[var, hd]
                  for hd in range(A_HEADS)]
        maxes = [jnp.max(s, axis=-1, keepdims=True) for s in scores]
        probs = [jnp.exp2(s - m) for s, m in zip(scores, maxes)]
        dens = [jnp.sum(pr, axis=-1, keepdims=True) for pr in probs]
        pvs = [jnp.dot(pr.astype(BF16), v_tiles[hd // 2], preferred_element_type=F32) for hd, pr in enumerate(probs)]
        outs = []
        for p in range(pairs):
            h0, h1 = 2 * p, 2 * p + 1
            outs.append(jnp.where(lo, pvs[h0] * (1.0 / dens[h0]), pvs[h1] * (1.0 / dens[h1])))
            lse_ref[0, p, out_rows, :] = jnp.where(lo, maxes[h0] + jnp.log2(dens[h0]), maxes[h1] + jnp.log2(dens[h1]))
        for j in range(pairs // 2):
            o_ref[0, j, out_rows, :] = _pack_bf16_pair(outs[2 * j], outs[2 * j + 1])
        return carry

    lax.fori_loop(0, group * nblk, block, 0, unroll=4)


def _dilated_call(qkv, bias, batch, seq, dilation):
    r = dilation
    sl = seq // r
    kw = min(2 * A_QB, sl)
    group = r
    pairs = A_HEADS // 2
    nat = lambda n: pl.BlockSpec((1, n, seq, LANES), lambda b, c: (b, 0, 0, 0))
    return pl.pallas_call(
        functools.partial(_dilated_kernel, seq_len=sl, dilation=r, key_width=kw, group=group),
        grid=(batch, r // group),
        in_specs=[pl.BlockSpec((1, group, sl, A_QKV_WIDTH), lambda b, c: (b, c, 0, 0)),
                  pl.BlockSpec(bias.shape, lambda b, c: (0, 0, 0, 0))],
        out_specs=[nat(pairs // 2), nat(pairs)],
        out_shape=[jax.ShapeDtypeStruct((batch, pairs // 2, seq, LANES), U32),
                   jax.ShapeDtypeStruct((batch, pairs, seq, LANES), F32)],
        compiler_params=_cparams(("parallel", "arbitrary")),
        name=f"dilated_r{r}",
    )(qkv, bias)


def _t5_bucket(rel):
    half = REL_BUCKETS // 2
    max_exact = half // 2
    n = np.abs(rel)
    large = max_exact + (np.log(np.maximum(n, 1) / max_exact)
                         / math.log(REL_MAX_DISTANCE / max_exact) * (half - max_exact)).astype(np.int32)
    large = np.minimum(large, half - 1)
    return (np.where(rel > 0, half, 0) + np.where(n < max_exact, n, large)).astype(np.int32)


def _dilated_bias(rel_bias, seq, dilation, half_steps):
    sl = seq // dilation
    kw = min(2 * A_QB, sl)
    offsets = [0] if sl == kw else [0, -half_steps, A_QB - kw]
    rel = np.stack([np.arange(kw)[None, :] + off - np.arange(A_QB)[:, None] for off in offsets])
    valid = np.abs(rel) <= half_steps
    bucket = np.where(valid, _t5_bucket(rel * dilation), REL_BUCKETS).astype(np.int32)
    onehot = (jnp.asarray(bucket)[..., None] == jnp.arange(REL_BUCKETS + 1, dtype=I32)).astype(F32)
    table = jnp.concatenate([rel_bias.astype(F32), jnp.full((1, A_HEADS), NEG_INF, F32)], axis=0)
    return jnp.einsum("vqkb,bh->vhqk", onehot, table * LOG2E, precision=lax.Precision.HIGHEST)


def _mla_kernel(q_ref, k_ref, v_ref, o_ref, v1_ref):
    lo = _lane_iota() < B_V_DIM
    sub = B_QB // B_SUB
    tiles = [slice(0, LANES), slice(LANES, 2 * LANES)]
    v1_ref[:, :LANES] = v_ref[0]
    v1_ref[:, LANES:] = jnp.ones((v_ref.shape[1], LANES), BF16)

    units = [(j, half) for j in range(sub) for half in range(2)]

    def rows(i):
        return [pl.ds(i * B_QB + B_SUB * j, B_SUB) for j in range(sub)]

    def scores(i):
        return [lax.dot_general(q_ref[0, rows(i)[j], tiles[half]], k_ref[0, :, tiles[half]], _NT,
                                preferred_element_type=F32) for j, half in units]

    def finish(i, sc):
        maxes = [jnp.max(s, axis=-1, keepdims=True) for s in sc]
        probs = [jnp.exp2(s - m) for s, m in zip(sc, maxes)]
        pvs = [jnp.dot(pr.astype(BF16), v1_ref[...], preferred_element_type=F32) for pr in probs]
        outs = [pv[:, :LANES] * (1.0 / pv[:, LANES:]) for pv in pvs]
        for j in range(sub):
            o_ref[0, rows(i)[j], :] = jnp.where(lo, outs[2 * j], outs[2 * j + 1]).astype(BF16)

    n_blocks = q_ref.shape[1] // B_QB
    sc = scores(0)
    for i in range(n_blocks):
        sc_next = scores(i + 1) if i + 1 < n_blocks else None
        finish(i, sc)
        sc = sc_next


def _mla_call(qb, kb, vb, batch, seq):
    qb = qb.reshape(batch, seq, B_HEADS * LANES)
    kb = kb.reshape(batch, seq, B_HEADS * LANES)
    vb = vb.reshape(batch, seq, B_WIDTH)
    pair = lambda w: pl.BlockSpec((1, seq, w), lambda b, p: (b, 0, p))
    out = pl.pallas_call(
        _mla_kernel,
        grid=(batch, B_HEADS // 2),
        in_specs=[pair(2 * LANES), pair(2 * LANES), pair(LANES)],
        out_specs=pair(LANES),
        out_shape=jax.ShapeDtypeStruct((batch, seq, B_WIDTH), BF16),
        scratch_shapes=[pltpu.VMEM((seq, 2 * LANES), BF16)],
        compiler_params=_cparams(("parallel", "parallel")),
        name="mla",
    )(qb, kb, vb)
    return out.reshape(batch * seq, B_WIDTH)


def _merge_patterns(o_refs, lse_refs):
    tiles = []
    for j in range(A_HEADS // 4):
        outs = [_unpack_bf16_pair(r[0, j]) for r in o_refs]
        for half in range(2):
            lses = [r[0, 2 * j + half] for r in lse_refs]
            top = functools.reduce(jnp.maximum, lses)
            es = [jnp.exp2(l - top) for l in lses]
            inv = 1.0 / functools.reduce(jnp.add, es)
            tiles.append(functools.reduce(jnp.add, [e * inv * o[half] for e, o in zip(es, outs)]))
    return jnp.concatenate(tiles, axis=1)


def _mix_kernel(o1_ref, o4_ref, o16_ref, l1_ref, l4_ref, l16_ref, ob_ref, x_ref, ga_ref, gb_ref, wo_ref,
                gf_ref, wr_ref, br_ref, tri_ref, x1_ref, hp_ref, idx_ref, gate_ref, cnt_ref, carry_ref):
    i = pl.program_id(0)

    @pl.when(i == 0)
    def _():
        carry_ref[...] = jnp.zeros_like(carry_ref)

    oa = _merge_patterns((o1_ref, o4_ref, o16_ref), (l1_ref, l4_ref, l16_ref))
    a = _rms(oa, ga_ref[...]).astype(BF16)
    b = _rms(ob_ref[...].astype(F32), gb_ref[...]).astype(BF16)
    mix = (jnp.dot(a, wo_ref[0:A_WIDTH, :], preferred_element_type=F32)
           + jnp.dot(b, wo_ref[A_WIDTH:, :], preferred_element_type=F32))
    x1 = x_ref[...] + mix
    x1_ref[...] = x1
    h2 = _rms(x1, gf_ref[...])
    half = D_MODEL // 2
    for c in range(PACKED_SUBLANES):
        hp_ref[pl.ds(c, h2.shape[0], stride=PACKED_SUBLANES), :] = _pack_bf16_pair(
            h2[:, LANES * c:LANES * (c + 1)], h2[:, half + LANES * c:half + LANES * (c + 1)])

    wr = wr_ref[...]
    wr_hi = wr.astype(BF16)
    wr_lo = (wr - wr_hi.astype(F32)).astype(BF16)
    h_hi = h2.astype(BF16)
    h_lo = (h2 - h_hi.astype(F32)).astype(BF16)
    lg_hi = lax.dot_general(jnp.concatenate([wr_hi, wr_lo], axis=0), h_hi, _NT, preferred_element_type=F32)
    lg = (lg_hi[:ROUTER_ROWS] + lg_hi[ROUTER_ROWS:]
          + lax.dot_general(wr_hi, h_lo, _NT, preferred_element_type=F32) + br_ref[...])
    row = lax.broadcasted_iota(I32, lg.shape, 0)
    is_g = (row >= N_EXPERTS) & (row < N_EXPERTS + N_GROUPS)
    gl = jnp.where(is_g, lg, NEG_INF)
    ge = jnp.exp(gl - jnp.max(gl, axis=0, keepdims=True))
    gp = ge / jnp.sum(ge, axis=0, keepdims=True)
    g_gate = jnp.max(gp, axis=0, keepdims=True)
    g_idx = jnp.min(jnp.where(is_g & (gp == g_gate), row - N_EXPERTS, LANES), axis=0, keepdims=True)
    sel = (row >> 3) == g_idx
    el = jnp.where(sel, lg, NEG_INF)
    ee = jnp.exp(el - jnp.max(el, axis=0, keepdims=True))
    ep = jnp.where(sel, ee / jnp.sum(ee, axis=0, keepdims=True), -1.0)
    p1 = jnp.max(ep, axis=0, keepdims=True)
    i1 = jnp.min(jnp.where(ep == p1, row, LANES), axis=0, keepdims=True)
    ep2 = jnp.where(row == i1, -1.0, ep)
    p2 = jnp.max(ep2, axis=0, keepdims=True)
    i2 = jnp.min(jnp.where(sel & (ep2 == p2) & (row != i1), row, LANES), axis=0, keepdims=True)
    den = p1 + p2
    g1 = g_gate * p1 / den
    g2 = g_gate * p2 / den

    hit1 = row == i1
    hit2 = row == i2
    onehot = jnp.where(hit1 | hit2, 1.0, 0.0)
    before = jnp.dot(onehot.astype(BF16), tri_ref[...], preferred_element_type=F32) + carry_ref[...]
    r1 = jnp.sum(jnp.where(hit1, before, 0.0), axis=0, keepdims=True).astype(I32)
    r2 = jnp.sum(jnp.where(hit2, before, 0.0), axis=0, keepdims=True).astype(I32)
    carry_ref[...] += jnp.sum(onehot, axis=1, keepdims=True)

    row8 = lax.broadcasted_iota(I32, idx_ref.shape, 0)
    idx_ref[...] = jnp.where(row8 == 0, i1, jnp.where(row8 == 1, i2,
                             jnp.where(row8 == 2, r1, jnp.where(row8 == 3, r2, 0))))
    gate_ref[...] = jnp.where(row8 == 0, g1, jnp.where(row8 == 1, g2, 0.0))

    @pl.when(i == pl.num_programs(0) - 1)
    def _():
        cnt_ref[...] = jnp.broadcast_to(carry_ref[...], cnt_ref.shape).astype(I32)


def _mix_call(oas, lses, ob, x2, g_a, g_b, w_out, g_ffn, w_router, b_router, tri):
    t = x2.shape[0]
    tm = ROW_TILE
    nseq = oas[0].shape[2] // tm
    row = lambda i: (i, 0)
    const = lambda i: (0, 0)
    slab = lambda n: pl.BlockSpec((1, n, tm, LANES), lambda i: (i // nseq, 0, i % nseq, 0))
    o_slab, lse_slab = slab(A_HEADS // 4), slab(A_HEADS // 2)
    return pl.pallas_call(
        _mix_kernel,
        grid=(t // tm,),
        in_specs=[
            o_slab, o_slab, o_slab, lse_slab, lse_slab, lse_slab,
            pl.BlockSpec((tm, B_WIDTH), row),
            pl.BlockSpec((tm, D_MODEL), row),
            pl.BlockSpec((1, A_WIDTH), const),
            pl.BlockSpec((1, B_WIDTH), const),
            pl.BlockSpec((D_MODEL, D_MODEL), const),
            pl.BlockSpec((1, D_MODEL), const),
            pl.BlockSpec((ROUTER_ROWS, D_MODEL), const),
            pl.BlockSpec((ROUTER_ROWS, 1), const),
            pl.BlockSpec((tm, tm), const),
        ],
        out_specs=[
            pl.BlockSpec((tm, D_MODEL), row),
            pl.BlockSpec((tm * PACKED_SUBLANES, LANES), row),
            pl.BlockSpec((8, tm), lambda i: (0, i)),
            pl.BlockSpec((8, tm), lambda i: (0, i)),
            pl.BlockSpec((ROUTER_ROWS, LANES), const),
        ],
        out_shape=[
            jax.ShapeDtypeStruct((t, D_MODEL), F32),
            jax.ShapeDtypeStruct((t * PACKED_SUBLANES, LANES), U32),
            jax.ShapeDtypeStruct((8, t), I32),
            jax.ShapeDtypeStruct((8, t), F32),
            jax.ShapeDtypeStruct((ROUTER_ROWS, LANES), I32),
        ],
        scratch_shapes=[pltpu.VMEM((ROUTER_ROWS, 1), F32)],
        compiler_params=_cparams(("arbitrary",)),
        name="mix_router",
    )(*oas, *lses, ob, x2, g_a, g_b, w_out, g_ffn, w_router, b_router, tri)


def _dest_kernel(idx_ref, pstart_ref, dest_ref):
    idx = idx_ref[...]
    row = lax.broadcasted_iota(I32, (ROUTER_ROWS, idx.shape[1]), 0)
    ps = pstart_ref[...]

    def slot(k):
        return jnp.sum(jnp.where(row == idx[k:k + 1, :], ps, 0), axis=0, keepdims=True) + idx[2 + k:3 + k, :]

    row8 = lax.broadcasted_iota(I32, idx.shape, 0)
    dest_ref[...] = jnp.where(row8 == 0, slot(0), jnp.where(row8 == 1, slot(1), 0))


def _dest_call(idx, pstart):
    t = idx.shape[1]
    tm = 4 * ROW_TILE
    return pl.pallas_call(
        _dest_kernel,
        grid=(t // tm,),
        in_specs=[pl.BlockSpec((8, tm), lambda i: (0, i)), pl.BlockSpec((ROUTER_ROWS, 1), lambda i: (0, 0))],
        out_specs=pl.BlockSpec((8, tm), lambda i: (0, i)),
        out_shape=jax.ShapeDtypeStruct((8, t), I32),
        compiler_params=_cparams(("parallel",)),
        name="dest_rows",
    )(idx, pstart)


def _dispatch_kernel(valid_ref, d0_ref, d1_ref, h_ref, buf_ref, zero_ref, sem, pad_sem):
    i = pl.program_id(0)
    tt = h_ref.shape[0] // PACKED_SUBLANES
    n_blocks = valid_ref.shape[0]

    def for_padded_blocks(fn):
        def body(j, c):
            taken = valid_ref[j]
            even = (taken + 1) // 2 * 2

            @pl.when(even < MOE_BLK)
            def _():
                n_pad = pl.multiple_of((MOE_BLK - even) * PACKED_SUBLANES, ROW_SUBLANES)
                first = pl.multiple_of((j * MOE_BLK + even) * PACKED_SUBLANES, ROW_SUBLANES)
                fn(pltpu.make_async_copy(zero_ref.at[pl.ds(0, n_pad)], buf_ref.at[pl.ds(first, n_pad)], pad_sem))

            @pl.when(even != taken)
            def _():
                fn(pltpu.make_async_copy(zero_ref.at[pl.ds(0, PACKED_SUBLANES)],
                                         buf_ref.at[_row_tile(j * MOE_BLK + taken, PACKED_SUBLANES)], pad_sem))
            return c
        lax.fori_loop(0, n_blocks, body, 0)

    @pl.when(i == 0)
    def _():
        zero_ref[...] = jnp.zeros_like(zero_ref)
        for_padded_blocks(lambda cp: cp.start())

    def issue(g, c):
        base = pl.multiple_of(g * ISSUE_GROUP, ISSUE_GROUP)
        for j in range(ISSUE_GROUP):
            for prio, d_ref in enumerate((d0_ref, d1_ref)):
                pltpu.make_async_copy(h_ref.at[_row_tile(base + j, PACKED_SUBLANES)],
                                      buf_ref.at[_row_tile(d_ref[base + j], PACKED_SUBLANES)],
                                      sem).start(priority=prio)
        return c

    lax.fori_loop(0, tt // ISSUE_GROUP, issue, 0)
    for k in range(TOP_K):
        pltpu.make_async_copy(h_ref, buf_ref.at[pl.ds(0, tt * PACKED_SUBLANES)], sem).wait()

    @pl.when(i == pl.num_programs(0) - 1)
    def _():
        for_padded_blocks(lambda cp: cp.wait())


def _dispatch_call(block_valid, dests, hp):
    t = hp.shape[0] // PACKED_SUBLANES
    tt = 2 * ROW_TILE
    n_rows = block_valid.shape[0] * MOE_BLK
    return pl.pallas_call(
        _dispatch_kernel,
        grid_spec=pltpu.PrefetchScalarGridSpec(
            num_scalar_prefetch=1,
            grid=(t // tt,),
            in_specs=[
                pl.BlockSpec((tt,), lambda i, va: (i,), memory_space=pltpu.SMEM),
                pl.BlockSpec((tt,), lambda i, va: (i,), memory_space=pltpu.SMEM),
                pl.BlockSpec((tt * PACKED_SUBLANES, LANES), lambda i, va: (i, 0)),
            ],
            out_specs=pl.BlockSpec(memory_space=pl.ANY),
            scratch_shapes=[pltpu.VMEM((MOE_BLK * PACKED_SUBLANES, LANES), U32),
                            pltpu.SemaphoreType.DMA(()), pltpu.SemaphoreType.DMA(())],
        ),
        out_shape=jax.ShapeDtypeStruct((n_rows * PACKED_SUBLANES, LANES), U32),
        compiler_params=_cparams(("arbitrary",), disable_bounds_checks=True, has_side_effects=True),
        name="dispatch",
    )(block_valid, *dests, hp)


def _expert_kernel(be_ref, new_ref, valid_ref, buf_ref, wg_ref, wu_ref, wd_ref, out_ref, wg_s, wu_s, wd_s):
    j = pl.program_id(0)
    del be_ref

    @pl.when(new_ref[j] == 1)
    def _():
        wg_s[...] = wg_ref[0].astype(BF16)
        wu_s[...] = wu_ref[0].astype(BF16)
        wd_s[...] = wd_ref[0].astype(BF16)

    n_valid = valid_ref[j]

    @pl.when(n_valid > 0)
    def _():
        blk = buf_ref.shape[0] // PACKED_SUBLANES
        words = [_unpack_bf16_pair(buf_ref[pl.ds(c, blk, stride=PACKED_SUBLANES), :]) for c in range(PACKED_SUBLANES)]
        x = jnp.concatenate([w[0] for w in words] + [w[1] for w in words], axis=1).astype(BF16)
        cols = [pl.ds(c, blk, stride=ROW_SUBLANES) for c in range(ROW_SUBLANES)]
        g = jnp.dot(x, wg_s[...], preferred_element_type=F32)
        u = jnp.dot(x, wu_s[...], preferred_element_type=F32)
        hb = (g * jax.nn.sigmoid(g)) * u
        out = jnp.dot(hb.astype(BF16), wd_s[...], preferred_element_type=F32)
        for c, rows in enumerate(cols):
            out_ref[rows, :] = out[:, LANES * c:LANES * (c + 1)]

    @pl.when(n_valid == 0)
    def _():
        out_ref[...] = jnp.zeros_like(out_ref)


def _expert_call(block_expert, block_new, block_valid, buf, w_gate, w_up, w_down):
    nb = buf.shape[0] // (MOE_BLK * PACKED_SUBLANES)
    wsel = lambda j, be, nw, va: (be[j], 0, 0)
    rows = lambda sublanes: pl.BlockSpec((MOE_BLK * sublanes, LANES), lambda j, be, nw, va: (j, 0))
    return pl.pallas_call(
        _expert_kernel,
        grid_spec=pltpu.PrefetchScalarGridSpec(
            num_scalar_prefetch=3,
            grid=(nb,),
            in_specs=[
                rows(PACKED_SUBLANES),
                pl.BlockSpec((1, D_MODEL, EXPERT_FF), wsel),
                pl.BlockSpec((1, D_MODEL, EXPERT_FF), wsel),
                pl.BlockSpec((1, EXPERT_FF, D_MODEL), wsel),
            ],
            out_specs=rows(ROW_SUBLANES),
            scratch_shapes=[pltpu.VMEM((D_MODEL, EXPERT_FF), BF16),
                            pltpu.VMEM((D_MODEL, EXPERT_FF), BF16),
                            pltpu.VMEM((EXPERT_FF, D_MODEL), BF16)],
        ),
        out_shape=jax.ShapeDtypeStruct((nb * MOE_BLK * ROW_SUBLANES, LANES), F32),
        compiler_params=_cparams(("arbitrary",)),
        name="experts",
    )(block_expert, block_new, block_valid, buf, w_gate, w_up, w_down)


def _combine_kernel(d0_ref, d1_ref, d0_next_ref, d1_next_ref, d0_ahead_ref, d1_ahead_ref, x1_ref, gate_ref,
                    gf_ref, eo_ref, o_ref, rows_a, rows_b, rows_c, sems):
    i = pl.program_id(0)
    tt = x1_ref.shape[0]
    bufs = (rows_a, rows_b, rows_c)
    n_buf = len(bufs)

    def start_row(d_refs, slot, r):
        for k, d_ref in enumerate(d_refs):
            pltpu.make_async_copy(eo_ref.at[_row_tile(d_ref[r])], bufs[slot].at[k, _row_tile(r)],
                                  sems.at[slot]).start(priority=k)

    def drain(slot):
        for k in range(TOP_K):
            pltpu.make_async_copy(eo_ref.at[pl.ds(0, tt * ROW_SUBLANES)], bufs[slot].at[k], sems.at[slot]).wait()

    @pl.when(i == 0)
    def _():
        def issue(g, c):
            base = pl.multiple_of(g * ISSUE_GROUP, ISSUE_GROUP)
            for j in range(ISSUE_GROUP):
                start_row((d0_ref, d1_ref), 0, base + j)
                start_row((d0_next_ref, d1_next_ref), 1, base + j)
            return c
        lax.fori_loop(0, tt // ISSUE_GROUP, issue, 0)

    def step(slot):
        ahead = (slot + 2) % n_buf
        drain(slot)
        gf = gf_ref[...]
        for ch in range(tt // COMBINE_CHUNK):
            r0 = ch * COMBINE_CHUNK
            for j in range(COMBINE_CHUNK):
                start_row((d0_ahead_ref, d1_ahead_ref), ahead, r0 + j)
            gate = gate_ref[r0:r0 + COMBINE_CHUNK, :]

            def rows(k):
                return jnp.concatenate(
                    [bufs[slot][k, pl.ds(ROW_SUBLANES * r0 + c, COMBINE_CHUNK, stride=ROW_SUBLANES), :]
                     for c in range(ROW_SUBLANES)], axis=1)

            y = rows(0) * gate[:, 0:1] + rows(1) * gate[:, 1:2]
            o_ref[r0:r0 + COMBINE_CHUNK, :] = _rms(x1_ref[r0:r0 + COMBINE_CHUNK, :] + y, gf)

        @pl.when(i == pl.num_programs(0) - 1)
        def _():
            drain((slot + 1) % n_buf)
            drain(ahead)

    for phase in range(n_buf):
        pl.when(i % n_buf == phase)(functools.partial(step, phase))


def _combine_call(dests, x1, gates, g_final, expert_out):
    t = x1.shape[0]
    tt = ROW_TILE // 2
    last = t // tt - 1
    tile = lambda ahead: pl.BlockSpec((tt,), lambda i: (jnp.minimum(i + ahead, last),), memory_space=pltpu.SMEM)
    buf = pltpu.VMEM((TOP_K, tt * ROW_SUBLANES, LANES), F32)
    return pl.pallas_call(
        _combine_kernel,
        grid=(t // tt,),
        in_specs=[
            tile(0), tile(0), tile(1), tile(1), tile(2), tile(2),
            pl.BlockSpec((tt, D_MODEL), lambda i: (i, 0)),
            pl.BlockSpec((tt, TOP_K), lambda i: (i, 0)),
            pl.BlockSpec((1, D_MODEL), lambda i: (0, 0)),
            pl.BlockSpec(memory_space=pl.ANY),
        ],
        out_specs=pl.BlockSpec((tt, D_MODEL), lambda i: (i, 0)),
        out_shape=jax.ShapeDtypeStruct((t, D_MODEL), F32),
        scratch_shapes=[buf, buf, buf, pltpu.SemaphoreType.DMA((3,))],
        compiler_params=_cparams(("arbitrary",), disable_bounds_checks=True),
        name="combine",
    )(*dests, *dests, *dests, x1, gates, g_final, expert_out)


def _rope_tables(seq):
    half = B_QK_ROPE // 2
    inv_freq = ROPE_THETA ** (-(jnp.arange(half, dtype=F32) / half))
    ang = jnp.arange(seq, dtype=F32)[:, None] * inv_freq[None, :]
    cos, sin = jnp.cos(ang), jnp.sin(ang)
    z = jnp.zeros((seq, B_QK_NOPE), F32)
    z2 = jnp.zeros((seq, B_QK_ROPE), F32)
    return (jnp.concatenate([z, cos, cos, z2], axis=1), jnp.concatenate([z, -sin, sin, z2], axis=1))


def _swap_halves(w):
    half = w.shape[-1] // 2
    return jnp.concatenate([w[..., half:], w[..., :half]], axis=-1)


def _layout_weights(w_in, w_q_up, w_kv_up):
    d = w_in.shape[0]
    w_kr = w_in[:, 3 * A_WIDTH + B_Q_LORA + B_KV_LORA:]
    w_in_l = jnp.concatenate(
        [w_in[:, :A_WIDTH] * (A_HEAD_DIM ** -0.5), w_in[:, A_WIDTH:3 * A_WIDTH + B_Q_LORA + B_KV_LORA],
         jnp.zeros((d, B_QK_NOPE), F32), w_kr, _swap_halves(w_kr)], axis=1).astype(BF16)
    wq = w_q_up.reshape(B_Q_LORA, B_HEADS, B_QK_NOPE + B_QK_ROPE)
    wq_l = jnp.concatenate([wq, _swap_halves(wq[..., B_QK_NOPE:])], axis=-1)
    wq_l = wq_l.reshape(B_Q_LORA, B_HEADS * LANES).astype(BF16)
    wkv = w_kv_up.reshape(B_KV_LORA, B_HEADS, B_QK_NOPE + B_V_DIM)
    wkb = jnp.concatenate([wkv[..., :B_QK_NOPE], jnp.zeros_like(wkv[..., :B_QK_NOPE])], axis=-1)
    wkb = wkb.reshape(B_KV_LORA, B_HEADS * LANES).astype(BF16)
    wvb = wkv[..., B_QK_NOPE:].reshape(B_KV_LORA, B_WIDTH).astype(BF16)
    return w_in_l, wq_l, wkb, wvb


def _block_plan(counts, n_blocks):
    padded = (counts + MOE_BLK - 1) // MOE_BLK * MOE_BLK
    ends = jnp.cumsum(padded)
    starts = ends - padded
    first_row = jnp.arange(n_blocks, dtype=I32) * MOE_BLK
    expert = jnp.minimum(jnp.sum(ends[None, :] <= first_row[:, None], axis=1), N_EXPERTS - 1).astype(I32)
    new = jnp.concatenate([jnp.ones((1,), I32), (expert[1:] != expert[:-1]).astype(I32)])
    valid = jnp.clip((starts + counts)[expert] - first_row, 0, MOE_BLK).astype(I32)
    return starts.astype(I32), expert, new, valid


def kernel(x, g_attn_norm, w_in, rel_bias, g_q_latent, w_q_up, g_kv_latent, w_kv_up, g_out_a, g_out_b, w_out,
           g_ffn_norm, w_router_group, b_router_group, w_router_expert, b_router_expert, w_gate, w_up, w_down,
           g_final):
    batch, seq, d = x.shape
    t = batch * seq
    assert g_attn_norm.shape[0] == 1 and d == D_MODEL and seq % ROW_TILE == 0
    cos_t, sin_t = _rope_tables(seq)
    tri = jnp.triu(jnp.ones((ROW_TILE, ROW_TILE), F32), 1).astype(BF16)
    n_blocks = t * TOP_K // MOE_BLK + N_EXPERTS
    x2 = x.reshape(t, d)
    row = lambda v: v.reshape(1, -1)

    w_in_l, wq_l, wkb_l, wvb_l = _layout_weights(w_in[0], w_q_up[0], w_kv_up[0])
    *qkv_a, qb, kb, vb = _proj_call(x2, row(g_attn_norm[0]), w_in_l, row(g_q_latent[0]), wq_l,
                                    row(g_kv_latent[0]), wkb_l, wvb_l, cos_t, sin_t, seq)
    oas, lses = [], []
    for pi, (window, dilation) in enumerate(DILATED_PATTERNS):
        bias = _dilated_bias(rel_bias, seq, dilation, window // (2 * dilation))
        o_p, lse_p = _dilated_call(qkv_a[pi], bias, batch, seq, dilation)
        oas.append(o_p)
        lses.append(lse_p)
    ob = _mla_call(qb, kb, vb, batch, seq)

    pad = ROUTER_ROWS - N_EXPERTS - N_GROUPS
    w_router = jnp.concatenate([w_router_expert[0], w_router_group[0], jnp.zeros((d, pad), F32)], axis=1).T
    b_router = jnp.concatenate([b_router_expert[0], b_router_group[0], jnp.zeros((pad,), F32)])
    x1, hp, idx, gates, cnt = _mix_call(oas, lses, ob, x2, row(g_out_a[0]), row(g_out_b[0]), w_out[0].astype(BF16),
                                        row(g_ffn_norm[0]), w_router, b_router.reshape(-1, 1), tri)
    pstart, block_expert, block_new, block_valid = _block_plan(cnt[:N_EXPERTS, 0], n_blocks)
    pstart_col = jnp.concatenate([pstart, jnp.zeros((ROUTER_ROWS - N_EXPERTS,), I32)]).reshape(-1, 1)
    dest = _dest_call(idx, pstart_col)
    dests = (dest[0], dest[1])
    buf = _dispatch_call(block_valid, dests, hp)
    expert_out = _expert_call(block_expert, block_new, block_valid, buf, w_gate[0], w_up[0], w_down[0])
    return _combine_call(dests, x1, gates[:TOP_K].T, row(g_final), expert_out).reshape(batch, seq, d)
```

```python
import functools
import math

import numpy as np
import jax
import jax.numpy as jnp
from jax import lax
from jax.experimental import pallas as pl
from jax.experimental.pallas import tpu as pltpu

F32 = jnp.float32
BF16 = jnp.bfloat16
I32 = jnp.int32
U32 = jnp.uint32

D_MODEL = 1024
EPS = 1e-6
NEG_INF = -1e30
LANES = 128
ROW_SUBLANES = D_MODEL // LANES
PACKED_SUBLANES = ROW_SUBLANES // 2

A_HEADS = 8
A_HEAD_DIM = 64
A_WIDTH = 512
A_QKV_WIDTH = 3 * A_WIDTH
DILATED_PATTERNS = ((128, 1), (512, 4), (2048, 16))
REL_BUCKETS = 32
REL_MAX_DISTANCE = 1024
A_QB = 128

B_HEADS = 8
B_Q_LORA = 256
B_KV_LORA = 128
B_QK_NOPE = 64
B_QK_ROPE = 32
B_V_DIM = 64
B_WIDTH = 512
ROPE_THETA = 10000.0
B_SCALE = (B_QK_NOPE + B_QK_ROPE) ** -0.5
B_QB = 512
B_SUB = 256
LOG2E = math.log2(math.e)

N_GROUPS = 4
EXPERTS_PER_GROUP = 8
N_EXPERTS = 32
TOP_K = 2
EXPERT_FF = 256
MOE_BLK = 512
ROUTER_ROWS = 40

ROW_TILE = 512
ISSUE_GROUP = 8
COMBINE_CHUNK = 32
PROJ_COLS = 2048

_NT = (((1,), (1,)), ((), ()))


def _cparams(semantics, vmem_mb=48, **kw):
    return pltpu.CompilerParams(dimension_semantics=semantics,
                                vmem_limit_bytes=vmem_mb * 1024 * 1024, **kw)


def _rms(x, g):
    return x * lax.rsqrt(jnp.mean(x * x, axis=-1, keepdims=True) + EPS) * g


def _lane_iota(rows=1):
    return lax.broadcasted_iota(I32, (rows, LANES), 1)


def _row_tile(r, sublanes=ROW_SUBLANES):
    return pl.ds(pl.multiple_of(r * sublanes, sublanes), sublanes)


def _proj_kernel(x_ref, g_ref, win_ref, gq_ref, wq_ref, gkv_ref, wkb_ref, wvb_ref, cos_ref, sin_ref, *refs):
    out1, out4, out16 = refs[:3]
    qb_ref, kb_ref, vb_ref, slab_ref, slab4_ref = refs[3:]
    tm = x_ref.shape[0]
    h = _rms(x_ref[...], g_ref[...]).astype(BF16)
    lo = _lane_iota() < A_HEAD_DIM
    n_slabs = A_QKV_WIDTH // LANES
    for s in range(0, n_slabs, 2):
        part = jnp.dot(h, win_ref[:, LANES * s:LANES * (s + 2)], preferred_element_type=F32)
        if LANES * s < A_WIDTH:
            part = part * LOG2E
        for half in range(2):
            col = part[:, LANES * half:LANES * (half + 1)]
            slab_ref[s + half] = col
            out1[0, 0, :, LANES * (s + half):LANES * (s + half + 1)] = col.astype(BF16)
    lat = jnp.dot(h, win_ref[:, A_QKV_WIDTH:], preferred_element_type=F32)
    n4 = tm // 4
    for c4 in range(4):
        for s in range(n_slabs):
            piece = slab_ref[s, pl.ds(c4, n4, stride=4), :]
            slab4_ref[s, c4] = piece
            out4[0, c4, :, LANES * s:LANES * (s + 1)] = piece.astype(BF16)
    for c4 in range(4):
        for j in range(4):
            for s in range(n_slabs):
                out16[0, c4 + 4 * j, :, LANES * s:LANES * (s + 1)] = (
                    slab4_ref[s, c4, pl.ds(j, n4 // 4, stride=4), :].astype(BF16))

    cos = cos_ref[...]
    sin = sin_ref[...]
    cq = _rms(lat[:, :B_Q_LORA], gq_ref[...]).astype(BF16)
    q = jnp.dot(cq, wq_ref[...], preferred_element_type=F32)
    q_mul = (cos + jnp.where(lo, 1.0, 0.0)) * (B_SCALE * LOG2E)
    q_rot = sin * (B_SCALE * LOG2E)
    for hd in range(B_HEADS):
        t = q[:, LANES * hd:LANES * (hd + 1)]
        qb_ref[:, LANES * hd:LANES * (hd + 1)] = (t * q_mul + pltpu.roll(t, 96, 1) * q_rot).astype(BF16)

    ckv = _rms(lat[:, B_Q_LORA:B_Q_LORA + B_KV_LORA], gkv_ref[...]).astype(BF16)
    kr = lat[:, B_Q_LORA + B_KV_LORA:]
    kr = kr * cos + pltpu.roll(kr, 96, 1) * sin
    kn = jnp.dot(ckv, wkb_ref[...], preferred_element_type=F32)
    for hd in range(B_HEADS):
        kb_ref[:, LANES * hd:LANES * (hd + 1)] = (kn[:, LANES * hd:LANES * (hd + 1)] + kr).astype(BF16)
    vb_ref[...] = jnp.dot(ckv, wvb_ref[...], preferred_element_type=F32).astype(BF16)


def _proj_call(x2, g_attn, w_in, g_q, w_q, g_kv, w_kb, w_vb, cos_t, sin_t, seq):
    t = x2.shape[0]
    tm = ROW_TILE
    nseq = seq // tm
    row = lambda i: (i, 0)
    const = lambda i: (0, 0)
    pos = lambda i: (i % nseq, 0)
    out = lambda w: jax.ShapeDtypeStruct((t, w), BF16)
    a_specs, a_shapes = [], []
    for _, r in DILATED_PATTERNS:
        a_specs.append(pl.BlockSpec((1, r, tm // r, A_QKV_WIDTH), lambda i: (i // nseq, 0, i % nseq, 0)))
        a_shapes.append(jax.ShapeDtypeStruct((t // seq, r, seq // r, A_QKV_WIDTH), BF16))
    return pl.pallas_call(
        _proj_kernel,
        grid=(t // tm,),
        in_specs=[
            pl.BlockSpec((tm, D_MODEL), row),
            pl.BlockSpec((1, D_MODEL), const),
            pl.BlockSpec((D_MODEL, PROJ_COLS), const),
            pl.BlockSpec((1, B_Q_LORA), const),
            pl.BlockSpec((B_Q_LORA, B_HEADS * LANES), const),
            pl.BlockSpec((1, B_KV_LORA), const),
            pl.BlockSpec((B_KV_LORA, B_HEADS * LANES), const),
            pl.BlockSpec((B_KV_LORA, B_WIDTH), const),
            pl.BlockSpec((tm, LANES), pos),
            pl.BlockSpec((tm, LANES), pos),
        ],
        out_specs=a_specs + [
            pl.BlockSpec((tm, B_HEADS * LANES), row),
            pl.BlockSpec((tm, B_HEADS * LANES), row),
            pl.BlockSpec((tm, B_WIDTH), row),
        ],
        out_shape=a_shapes + [out(B_HEADS * LANES), out(B_HEADS * LANES), out(B_WIDTH)],
        scratch_shapes=[pltpu.VMEM((A_QKV_WIDTH // LANES, tm, LANES), F32),
                        pltpu.VMEM((A_QKV_WIDTH // LANES, 4, tm // 4, LANES), F32)],
        compiler_params=_cparams(("parallel",)),
        name="proj",
    )(x2, g_attn, w_in, g_q, w_q, g_kv, w_kb, w_vb, cos_t, sin_t)


def _pack_bf16_pair(a, b):
    a_bits = lax.bitcast_convert_type(a.astype(BF16).astype(F32), U32) >> 16
    b_bits = lax.bitcast_convert_type(b.astype(BF16).astype(F32), U32) & jnp.uint32(0xFFFF0000)
    return a_bits | b_bits


def _unpack_bf16_pair(w):
    return (lax.bitcast_convert_type(w << 16, F32), lax.bitcast_convert_type(w & jnp.uint32(0xFFFF0000), F32))


def _dilated_kernel(qkv_ref, bias_ref, o_ref, lse_ref, *, seq_len, dilation, key_width, group):
    nblk = seq_len // A_QB
    lane = _lane_iota()
    lo = lane < A_HEAD_DIM
    first_class = pl.program_id(1) * group
    pairs = A_HEADS // 2

    def block(it, carry):
        c = it // nblk
        if nblk == 1:
            q0, ks, var = 0, 0, 0
        else:
            n = it % nblk
            q0 = pl.multiple_of(n * A_QB, A_QB)
            ks = pl.multiple_of(jnp.clip(q0 - 64, 0, seq_len - key_width), 64)
            var = jnp.where(n == 0, 0, jnp.where(n == nblk - 1, 2, 1))
        rows = pl.ds(q0, A_QB)
        keys = pl.ds(ks, key_width)
        if dilation == 1:
            out_rows = rows
        else:
            out_rows = pl.ds(first_class + c + dilation * q0, A_QB, stride=dilation)
        q_tiles = [qkv_ref[0, c, rows, LANES * p:LANES * (p + 1)] for p in range(pairs)]
        k_tiles = [qkv_ref[0, c, keys, A_WIDTH + LANES * p:A_WIDTH + LANES * (p + 1)] for p in range(pairs)]
        v_tiles = [qkv_ref[0, c, keys, 2 * A_WIDTH + LANES * p:2 * A_WIDTH + LANES * (p + 1)] for p in range(pairs)]
        zero = jnp.zeros((), BF16)
        k_heads = [jnp.where(lo, k_tiles[hd // 2], zero) if hd % 2 == 0 else jnp.where(lo, zero, k_tiles[hd // 2])
                   for hd in range(A_HEADS)]
        scores = [lax.dot_general(q_tiles[hd // 2], k_heads[hd], _NT, preferred_element_type=F32) + bias_ref[var, hd]
                  for hd in range(A_HEADS)]
        maxes = [jnp.max(s, axis=-1, keepdims=True) for s in scores]
        probs = [jnp.exp2(s - m) for s, m in zip(scores, maxes)]
        dens = [jnp.sum(pr, axis=-1, keepdims=True) for pr in probs]
        pvs = [jnp.dot(pr.astype(BF16), v_tiles[hd // 2], preferred_element_type=F32) for hd, pr in enumerate(probs)]
        outs = []
        for p in range(pairs):
            h0, h1 = 2 * p, 2 * p + 1
            outs.append(jnp.where(lo, pvs[h0] * (1.0 / dens[h0]), pvs[h1] * (1.0 / dens[h1])))
        lse = jnp.zeros((A_QB, LANES), F32)
        for hd in range(A_HEADS):
            lse = jnp.where(lane == hd, maxes[hd] + jnp.log2(dens[hd]), lse)
        lse_ref[0, out_rows, :] = lse
        for j in range(pairs // 2):
            o_ref[0, j, out_rows, :] = _pack_bf16_pair(outs[2 * j], outs[2 * j + 1])
        return carry

    lax.fori_loop(0, group * nblk, block, 0, unroll=4)


def _dilated_call(qkv, bias, batch, seq, dilation):
    r = dilation
    sl = seq // r
    kw = min(2 * A_QB, sl)
    group = r
    pairs = A_HEADS // 2
    return pl.pallas_call(
        functools.partial(_dilated_kernel, seq_len=sl, dilation=r, key_width=kw, group=group),
        grid=(batch, r // group),
        in_specs=[pl.BlockSpec((1, group, sl, A_QKV_WIDTH), lambda b, c: (b, c, 0, 0)),
                  pl.BlockSpec(bias.shape, lambda b, c: (0, 0, 0, 0))],
        out_specs=[pl.BlockSpec((1, pairs // 2, seq, LANES), lambda b, c: (b, 0, 0, 0)),
                   pl.BlockSpec((1, seq, LANES), lambda b, c: (b, 0, 0))],
        out_shape=[jax.ShapeDtypeStruct((batch, pairs // 2, seq, LANES), U32),
                   jax.ShapeDtypeStruct((batch, seq, LANES), F32)],
        compiler_params=_cparams(("parallel", "arbitrary")),
        name=f"dilated_r{r}",
    )(qkv, bias)


def _t5_bucket(rel):
    half = REL_BUCKETS // 2
    max_exact = half // 2
    n = np.abs(rel)
    large = max_exact + (np.log(np.maximum(n, 1) / max_exact)
                         / math.log(REL_MAX_DISTANCE / max_exact) * (half - max_exact)).astype(np.int32)
    large = np.minimum(large, half - 1)
    return (np.where(rel > 0, half, 0) + np.where(n < max_exact, n, large)).astype(np.int32)


def _dilated_bias(rel_bias, seq, dilation, half_steps):
    sl = seq // dilation
    kw = min(2 * A_QB, sl)
    offsets = [0] if sl == kw else [0, -half_steps, A_QB - kw]
    rel = np.stack([np.arange(kw)[None, :] + off - np.arange(A_QB)[:, None] for off in offsets])
    valid = np.abs(rel) <= half_steps
    bucket = np.where(valid, _t5_bucket(rel * dilation), REL_BUCKETS).astype(np.int32)
    onehot = (jnp.asarray(bucket)[..., None] == jnp.arange(REL_BUCKETS + 1, dtype=I32)).astype(F32)
    table = jnp.concatenate([rel_bias.astype(F32), jnp.full((1, A_HEADS), NEG_INF, F32)], axis=0)
    return jnp.einsum("vqkb,bh->vhqk", onehot, table * LOG2E, precision=lax.Precision.HIGHEST)


def _mla_kernel(q_ref, k_ref, v_ref, o_ref, v1_ref):
    lo = _lane_iota() < B_V_DIM
    sub = B_QB // B_SUB
    tiles = [slice(0, LANES), slice(LANES, 2 * LANES)]
    v1_ref[:, :LANES] = v_ref[0]
    v1_ref[:, LANES:] = jnp.ones((v_ref.shape[1], LANES), BF16)

    units = [(j, half) for j in range(sub) for half in range(2)]

    def rows(i):
        return [pl.ds(i * B_QB + B_SUB * j, B_SUB) for j in range(sub)]

    def scores(i):
        return [lax.dot_general(q_ref[0, rows(i)[j], tiles[half]], k_ref[0, :, tiles[half]], _NT,
                                preferred_element_type=F32) for j, half in units]

    def finish(i, sc):
        maxes = [jnp.max(s, axis=-1, keepdims=True) for s in sc]
        probs = [jnp.exp2(s - m) for s, m in zip(sc, maxes)]
        pvs = [jnp.dot(pr.astype(BF16), v1_ref[...], preferred_element_type=F32) for pr in probs]
        outs = [pv[:, :LANES] * (1.0 / pv[:, LANES:]) for pv in pvs]
        for j in range(sub):
            o_ref[0, rows(i)[j], :] = jnp.where(lo, outs[2 * j], outs[2 * j + 1]).astype(BF16)

    n_blocks = q_ref.shape[1] // B_QB
    sc = scores(0)
    for i in range(n_blocks):
        sc_next = scores(i + 1) if i + 1 < n_blocks else None
        finish(i, sc)
        sc = sc_next


def _mla_call(qb, kb, vb, batch, seq):
    qb = qb.reshape(batch, seq, B_HEADS * LANES)
    kb = kb.reshape(batch, seq, B_HEADS * LANES)
    vb = vb.reshape(batch, seq, B_WIDTH)
    pair = lambda w: pl.BlockSpec((1, seq, w), lambda b, p: (b, 0, p))
    out = pl.pallas_call(
        _mla_kernel,
        grid=(batch, B_HEADS // 2),
        in_specs=[pair(2 * LANES), pair(2 * LANES), pair(LANES)],
        out_specs=pair(LANES),
        out_shape=jax.ShapeDtypeStruct((batch, seq, B_WIDTH), BF16),
        scratch_shapes=[pltpu.VMEM((seq, 2 * LANES), BF16)],
        compiler_params=_cparams(("parallel", "parallel")),
        name="mla",
    )(qb, kb, vb)
    return out.reshape(batch * seq, B_WIDTH)


def _merge_patterns(o_refs, lse_refs, spread_ref):
    lses = [r[0] for r in lse_refs]
    top = functools.reduce(jnp.maximum, lses)
    es = [jnp.exp2(l - top) for l in lses]
    inv = 1.0 / functools.reduce(jnp.add, es)
    spread = spread_ref[...]
    weights = []
    for e in es:
        w = e * inv
        w_hi = w.astype(BF16)
        w_lo = (w - w_hi.astype(F32)).astype(BF16)
        weights.append(jnp.dot(jnp.concatenate([w_hi, w_lo], axis=1), spread, preferred_element_type=F32))
    tiles = []
    for j in range(A_HEADS // 4):
        outs = [_unpack_bf16_pair(r[0, j]) for r in o_refs]
        for half in range(2):
            tile = slice(LANES * (2 * j + half), LANES * (2 * j + half + 1))
            tiles.append(functools.reduce(jnp.add, [w[:, tile] * o[half] for w, o in zip(weights, outs)]))
    return jnp.concatenate(tiles, axis=1)


def _mix_kernel(o1_ref, o4_ref, o16_ref, l1_ref, l4_ref, l16_ref, ob_ref, x_ref, ga_ref, gb_ref, wo_ref,
                gf_ref, wr_ref, br_ref, tri_ref, spread_ref, x1_ref, hp_ref, idx_ref, gate_ref, cnt_ref, carry_ref):
    i = pl.program_id(0)

    @pl.when(i == 0)
    def _():
        carry_ref[...] = jnp.zeros_like(carry_ref)

    oa = _merge_patterns((o1_ref, o4_ref, o16_ref), (l1_ref, l4_ref, l16_ref), spread_ref)
    a = _rms(oa, ga_ref[...]).astype(BF16)
    b = _rms(ob_ref[...].astype(F32), gb_ref[...]).astype(BF16)
    mix = (jnp.dot(a, wo_ref[0:A_WIDTH, :], preferred_element_type=F32)
           + jnp.dot(b, wo_ref[A_WIDTH:, :], preferred_element_type=F32))
    x1 = x_ref[...] + mix
    x1_ref[...] = x1
    h2 = _rms(x1, gf_ref[...])
    half = D_MODEL // 2
    for c in range(PACKED_SUBLANES):
        hp_ref[pl.ds(c, h2.shape[0], stride=PACKED_SUBLANES), :] = _pack_bf16_pair(
            h2[:, LANES * c:LANES * (c + 1)], h2[:, half + LANES * c:half + LANES * (c + 1)])

    wr = wr_ref[...]
    wr_hi = wr.astype(BF16)
    wr_lo = (wr - wr_hi.astype(F32)).astype(BF16)
    h_hi = h2.astype(BF16)
    h_lo = (h2 - h_hi.astype(F32)).astype(BF16)
    lg_hi = lax.dot_general(jnp.concatenate([wr_hi, wr_lo], axis=0), h_hi, _NT, preferred_element_type=F32)
    lg = (lg_hi[:ROUTER_ROWS] + lg_hi[ROUTER_ROWS:]
          + lax.dot_general(wr_hi, h_lo, _NT, preferred_element_type=F32) + br_ref[...])
    row = lax.broadcasted_iota(I32, lg.shape, 0)
    is_g = (row >= N_EXPERTS) & (row < N_EXPERTS + N_GROUPS)
    gl = jnp.where(is_g, lg, NEG_INF)
    ge = jnp.exp(gl - jnp.max(gl, axis=0, keepdims=True))
    gp = ge / jnp.sum(ge, axis=0, keepdims=True)
    g_gate = jnp.max(gp, axis=0, keepdims=True)
    g_idx = jnp.min(jnp.where(is_g & (gp == g_gate), row - N_EXPERTS, LANES), axis=0, keepdims=True)
    sel = (row >> 3) == g_idx
    el = jnp.where(sel, lg, NEG_INF)
    ee = jnp.exp(el - jnp.max(el, axis=0, keepdims=True))
    ep = jnp.where(sel, ee / jnp.sum(ee, axis=0, keepdims=True), -1.0)
    p1 = jnp.max(ep, axis=0, keepdims=True)
    i1 = jnp.min(jnp.where(ep == p1, row, LANES), axis=0, keepdims=True)
    ep2 = jnp.where(row == i1, -1.0, ep)
    p2 = jnp.max(ep2, axis=0, keepdims=True)
    i2 = jnp.min(jnp.where(sel & (ep2 == p2) & (row != i1), row, LANES), axis=0, keepdims=True)
    den = p1 + p2
    g1 = g_gate * p1 / den
    g2 = g_gate * p2 / den

    hit1 = row == i1
    hit2 = row == i2
    onehot = jnp.where(hit1 | hit2, 1.0, 0.0)
    before = jnp.dot(onehot.astype(BF16), tri_ref[...], preferred_element_type=F32) + carry_ref[...]
    r1 = jnp.sum(jnp.where(hit1, before, 0.0), axis=0, keepdims=True).astype(I32)
    r2 = jnp.sum(jnp.where(hit2, before, 0.0), axis=0, keepdims=True).astype(I32)
    carry_ref[...] += jnp.sum(onehot, axis=1, keepdims=True)

    row8 = lax.broadcasted_iota(I32, idx_ref.shape, 0)
    idx_ref[...] = jnp.where(row8 == 0, i1, jnp.where(row8 == 1, i2,
                             jnp.where(row8 == 2, r1, jnp.where(row8 == 3, r2, 0))))
    gate_ref[...] = jnp.where(row8 == 0, g1, jnp.where(row8 == 1, g2, 0.0))

    @pl.when(i == pl.num_programs(0) - 1)
    def _():
        cnt_ref[...] = jnp.broadcast_to(carry_ref[...], cnt_ref.shape).astype(I32)


def _mix_call(oas, lses, ob, x2, g_a, g_b, w_out, g_ffn, w_router, b_router, tri, spread):
    t = x2.shape[0]
    tm = ROW_TILE
    nseq = oas[0].shape[2] // tm
    row = lambda i: (i, 0)
    const = lambda i: (0, 0)
    slab = lambda n: pl.BlockSpec((1, n, tm, LANES), lambda i: (i // nseq, 0, i % nseq, 0))
    o_slab = slab(A_HEADS // 4)
    lse_slab = pl.BlockSpec((1, tm, LANES), lambda i: (i // nseq, i % nseq, 0))
    return pl.pallas_call(
        _mix_kernel,
        grid=(t // tm,),
        in_specs=[
            o_slab, o_slab, o_slab, lse_slab, lse_slab, lse_slab,
            pl.BlockSpec((tm, B_WIDTH), row),
            pl.BlockSpec((tm, D_MODEL), row),
            pl.BlockSpec((1, A_WIDTH), const),
            pl.BlockSpec((1, B_WIDTH), const),
            pl.BlockSpec((D_MODEL, D_MODEL), const),
            pl.BlockSpec((1, D_MODEL), const),
            pl.BlockSpec((ROUTER_ROWS, D_MODEL), const),
            pl.BlockSpec((ROUTER_ROWS, 1), const),
            pl.BlockSpec((tm, tm), const),
            pl.BlockSpec((2 * LANES, A_WIDTH), const),
        ],
        out_specs=[
            pl.BlockSpec((tm, D_MODEL), row),
            pl.BlockSpec((tm * PACKED_SUBLANES, LANES), row),
            pl.BlockSpec((8, tm), lambda i: (0, i)),
            pl.BlockSpec((8, tm), lambda i: (0, i)),
            pl.BlockSpec((ROUTER_ROWS, LANES), const),
        ],
        out_shape=[
            jax.ShapeDtypeStruct((t, D_MODEL), F32),
            jax.ShapeDtypeStruct((t * PACKED_SUBLANES, LANES), U32),
            jax.ShapeDtypeStruct((8, t), I32),
            jax.ShapeDtypeStruct((8, t), F32),
            jax.ShapeDtypeStruct((ROUTER_ROWS, LANES), I32),
        ],
        scratch_shapes=[pltpu.VMEM((ROUTER_ROWS, 1), F32)],
        compiler_params=_cparams(("arbitrary",)),
        name="mix_router",
    )(*oas, *lses, ob, x2, g_a, g_b, w_out, g_ffn, w_router, b_router, tri, spread)


def _dest_kernel(idx_ref, pstart_ref, dest_ref):
    idx = idx_ref[...]
    row = lax.broadcasted_iota(I32, (ROUTER_ROWS, idx.shape[1]), 0)
    ps = pstart_ref[...]

    def slot(k):
        return jnp.sum(jnp.where(row == idx[k:k + 1, :], ps, 0), axis=0, keepdims=True) + idx[2 + k:3 + k, :]

    row8 = lax.broadcasted_iota(I32, idx.shape, 0)
    dest_ref[...] = jnp.where(row8 == 0, slot(0), jnp.where(row8 == 1, slot(1), 0))


def _dest_call(idx, pstart):
    t = idx.shape[1]
    tm = 4 * ROW_TILE
    return pl.pallas_call(
        _dest_kernel,
        grid=(t // tm,),
        in_specs=[pl.BlockSpec((8, tm), lambda i: (0, i)), pl.BlockSpec((ROUTER_ROWS, 1), lambda i: (0, 0))],
        out_specs=pl.BlockSpec((8, tm), lambda i: (0, i)),
        out_shape=jax.ShapeDtypeStruct((8, t), I32),
        compiler_params=_cparams(("parallel",)),
        name="dest_rows",
    )(idx, pstart)


def _dispatch_kernel(valid_ref, d0_ref, d1_ref, h_ref, buf_ref, zero_ref, sem, pad_sem):
    i = pl.program_id(0)
    tt = h_ref.shape[0] // PACKED_SUBLANES
    n_blocks = valid_ref.shape[0]

    def for_padded_blocks(fn):
        def body(j, c):
            taken = valid_ref[j]
            even = (taken + 1) // 2 * 2

            @pl.when(even < MOE_BLK)
            def _():
                n_pad = pl.multiple_of((MOE_BLK - even) * PACKED_SUBLANES, ROW_SUBLANES)
                first = pl.multiple_of((j * MOE_BLK + even) * PACKED_SUBLANES, ROW_SUBLANES)
                fn(pltpu.make_async_copy(zero_ref.at[pl.ds(0, n_pad)], buf_ref.at[pl.ds(first, n_pad)], pad_sem))

            @pl.when(even != taken)
            def _():
                fn(pltpu.make_async_copy(zero_ref.at[pl.ds(0, PACKED_SUBLANES)],
                                         buf_ref.at[_row_tile(j * MOE_BLK + taken, PACKED_SUBLANES)], pad_sem))
            return c
        lax.fori_loop(0, n_blocks, body, 0)

    @pl.when(i == 0)
    def _():
        zero_ref[...] = jnp.zeros_like(zero_ref)
        for_padded_blocks(lambda cp: cp.start())

    def issue(g, c):
        base = pl.multiple_of(g * ISSUE_GROUP, ISSUE_GROUP)
        for j in range(ISSUE_GROUP):
            for prio, d_ref in enumerate((d0_ref, d1_ref)):
                pltpu.make_async_copy(h_ref.at[_row_tile(base + j, PACKED_SUBLANES)],
                                      buf_ref.at[_row_tile(d_ref[base + j], PACKED_SUBLANES)],
                                      sem).start(priority=prio)
        return c

    lax.fori_loop(0, tt // ISSUE_GROUP, issue, 0)
    for k in range(TOP_K):
        pltpu.make_async_copy(h_ref, buf_ref.at[pl.ds(0, tt * PACKED_SUBLANES)], sem).wait()

    @pl.when(i == pl.num_programs(0) - 1)
    def _():
        for_padded_blocks(lambda cp: cp.wait())


def _dispatch_call(block_valid, dests, hp):
    t = hp.shape[0] // PACKED_SUBLANES
    tt = 2 * ROW_TILE
    n_rows = block_valid.shape[0] * MOE_BLK
    return pl.pallas_call(
        _dispatch_kernel,
        grid_spec=pltpu.PrefetchScalarGridSpec(
            num_scalar_prefetch=1,
            grid=(t // tt,),
            in_specs=[
                pl.BlockSpec((tt,), lambda i, va: (i,), memory_space=pltpu.SMEM),
                pl.BlockSpec((tt,), lambda i, va: (i,), memory_space=pltpu.SMEM),
                pl.BlockSpec((tt * PACKED_SUBLANES, LANES), lambda i, va: (i, 0)),
            ],
            out_specs=pl.BlockSpec(memory_space=pl.ANY),
            scratch_shapes=[pltpu.VMEM((MOE_BLK * PACKED_SUBLANES, LANES), U32),
                            pltpu.SemaphoreType.DMA(()), pltpu.SemaphoreType.DMA(())],
        ),
        out_shape=jax.ShapeDtypeStruct((n_rows * PACKED_SUBLANES, LANES), U32),
        compiler_params=_cparams(("arbitrary",), disable_bounds_checks=True, has_side_effects=True),
        name="dispatch",
    )(block_valid, *dests, hp)


def _expert_kernel(be_ref, new_ref, valid_ref, buf_ref, wg_ref, wu_ref, wd_ref, out_ref, wg_s, wu_s, wd_s):
    j = pl.program_id(0)
    del be_ref

    @pl.when(new_ref[j] == 1)
    def _():
        wg_s[...] = wg_ref[0].astype(BF16)
        wu_s[...] = wu_ref[0].astype(BF16)
        wd_s[...] = wd_ref[0].astype(BF16)

    n_valid = valid_ref[j]

    @pl.when(n_valid > 0)
    def _():
        blk = buf_ref.shape[0] // PACKED_SUBLANES
        words = [_unpack_bf16_pair(buf_ref[pl.ds(c, blk, stride=PACKED_SUBLANES), :]) for c in range(PACKED_SUBLANES)]
        x = jnp.concatenate([w[0] for w in words] + [w[1] for w in words], axis=1).astype(BF16)
        cols = [pl.ds(c, blk, stride=ROW_SUBLANES) for c in range(ROW_SUBLANES)]
        g = jnp.dot(x, wg_s[...], preferred_element_type=F32)
        u = jnp.dot(x, wu_s[...], preferred_element_type=F32)
        hb = (g * jax.nn.sigmoid(g)) * u
        out = jnp.dot(hb.astype(BF16), wd_s[...], preferred_element_type=F32)
        for c, rows in enumerate(cols):
            out_ref[rows, :] = out[:, LANES * c:LANES * (c + 1)]

    @pl.when(n_valid == 0)
    def _():
        out_ref[...] = jnp.zeros_like(out_ref)


def _expert_call(block_expert, block_new, block_valid, buf, w_gate, w_up, w_down):
    nb = buf.shape[0] // (MOE_BLK * PACKED_SUBLANES)
    wsel = lambda j, be, nw, va: (be[j], 0, 0)
    rows = lambda sublanes: pl.BlockSpec((MOE_BLK * sublanes, LANES), lambda j, be, nw, va: (j, 0))
    return pl.pallas_call(
        _expert_kernel,
        grid_spec=pltpu.PrefetchScalarGridSpec(
            num_scalar_prefetch=3,
            grid=(nb,),
            in_specs=[
                rows(PACKED_SUBLANES),
                pl.BlockSpec((1, D_MODEL, EXPERT_FF), wsel),
                pl.BlockSpec((1, D_MODEL, EXPERT_FF), wsel),
                pl.BlockSpec((1, EXPERT_FF, D_MODEL), wsel),
            ],
            out_specs=rows(ROW_SUBLANES),
            scratch_shapes=[pltpu.VMEM((D_MODEL, EXPERT_FF), BF16),
                            pltpu.VMEM((D_MODEL, EXPERT_FF), BF16),
                            pltpu.VMEM((EXPERT_FF, D_MODEL), BF16)],
        ),
        out_shape=jax.ShapeDtypeStruct((nb * MOE_BLK * ROW_SUBLANES, LANES), F32),
        compiler_params=_cparams(("arbitrary",)),
        name="experts",
    )(block_expert, block_new, block_valid, buf, w_gate, w_up, w_down)


def _combine_kernel(d0_ref, d1_ref, d0_next_ref, d1_next_ref, d0_ahead_ref, d1_ahead_ref, x1_ref, gate_ref,
                    gf_ref, eo_ref, o_ref, rows_a, rows_b, rows_c, sems):
    i = pl.program_id(0)
    tt = x1_ref.shape[0]
    bufs = (rows_a, rows_b, rows_c)
    n_buf = len(bufs)

    def start_row(d_refs, slot, r):
        for k, d_ref in enumerate(d_refs):
            pltpu.make_async_copy(eo_ref.at[_row_tile(d_ref[r])], bufs[slot].at[k, _row_tile(r)],
                                  sems.at[slot]).start(priority=k)

    def drain(slot):
        for k in range(TOP_K):
            pltpu.make_async_copy(eo_ref.at[pl.ds(0, tt * ROW_SUBLANES)], bufs[slot].at[k], sems.at[slot]).wait()

    @pl.when(i == 0)
    def _():
        def issue(g, c):
            base = pl.multiple_of(g * ISSUE_GROUP, ISSUE_GROUP)
            for j in range(ISSUE_GROUP):
                start_row((d0_ref, d1_ref), 0, base + j)
                start_row((d0_next_ref, d1_next_ref), 1, base + j)
            return c
        lax.fori_loop(0, tt // ISSUE_GROUP, issue, 0)

    def step(slot):
        ahead = (slot + 2) % n_buf
        drain(slot)
        gf = gf_ref[...]
        for ch in range(tt // COMBINE_CHUNK):
            r0 = ch * COMBINE_CHUNK
            for j in range(COMBINE_CHUNK):
                start_row((d0_ahead_ref, d1_ahead_ref), ahead, r0 + j)
            gate = gate_ref[r0:r0 + COMBINE_CHUNK, :]

            def rows(k):
                return jnp.concatenate(
                    [bufs[slot][k, pl.ds(ROW_SUBLANES * r0 + c, COMBINE_CHUNK, stride=ROW_SUBLANES), :]
                     for c in range(ROW_SUBLANES)], axis=1)

            y = rows(0) * gate[:, 0:1] + rows(1) * gate[:, 1:2]
            o_ref[r0:r0 + COMBINE_CHUNK, :] = _rms(x1_ref[r0:r0 + COMBINE_CHUNK, :] + y, gf)

        @pl.when(i == pl.num_programs(0) - 1)
        def _():
            drain((slot + 1) % n_buf)
            drain(ahead)

    for phase in range(n_buf):
        pl.when(i % n_buf == phase)(functools.partial(step, phase))


def _combine_call(dests, x1, gates, g_final, expert_out):
    t = x1.shape[0]
    tt = ROW_TILE // 2
    last = t // tt - 1
    tile = lambda ahead: pl.BlockSpec((tt,), lambda i: (jnp.minimum(i + ahead, last),), memory_space=pltpu.SMEM)
    buf = pltpu.VMEM((TOP_K, tt * ROW_SUBLANES, LANES), F32)
    return pl.pallas_call(
        _combine_kernel,
        grid=(t // tt,),
        in_specs=[
            tile(0), tile(0), tile(1), tile(1), tile(2), tile(2),
            pl.BlockSpec((tt, D_MODEL), lambda i: (i, 0)),
            pl.BlockSpec((tt, TOP_K), lambda i: (i, 0)),
            pl.BlockSpec((1, D_MODEL), lambda i: (0, 0)),
            pl.BlockSpec(memory_space=pl.ANY),
        ],
        out_specs=pl.BlockSpec((tt, D_MODEL), lambda i: (i, 0)),
        out_shape=jax.ShapeDtypeStruct((t, D_MODEL), F32),
        scratch_shapes=[buf, buf, buf, pltpu.SemaphoreType.DMA((3,))],
        compiler_params=_cparams(("arbitrary",), disable_bounds_checks=True),
        name="combine",
    )(*dests, *dests, *dests, x1, gates, g_final, expert_out)


def _rope_tables(seq):
    half = B_QK_ROPE // 2
    inv_freq = ROPE_THETA ** (-(jnp.arange(half, dtype=F32) / half))
    ang = jnp.arange(seq, dtype=F32)[:, None] * inv_freq[None, :]
    cos, sin = jnp.cos(ang), jnp.sin(ang)
    z = jnp.zeros((seq, B_QK_NOPE), F32)
    z2 = jnp.zeros((seq, B_QK_ROPE), F32)
    return (jnp.concatenate([z, cos, cos, z2], axis=1), jnp.concatenate([z, -sin, sin, z2], axis=1))


def _swap_halves(w):
    half = w.shape[-1] // 2
    return jnp.concatenate([w[..., half:], w[..., :half]], axis=-1)


def _layout_weights(w_in, w_q_up, w_kv_up):
    d = w_in.shape[0]
    w_kr = w_in[:, 3 * A_WIDTH + B_Q_LORA + B_KV_LORA:]
    w_in_l = jnp.concatenate(
        [w_in[:, :A_WIDTH] * (A_HEAD_DIM ** -0.5), w_in[:, A_WIDTH:3 * A_WIDTH + B_Q_LORA + B_KV_LORA],
         jnp.zeros((d, B_QK_NOPE), F32), w_kr, _swap_halves(w_kr)], axis=1).astype(BF16)
    wq = w_q_up.reshape(B_Q_LORA, B_HEADS, B_QK_NOPE + B_QK_ROPE)
    wq_l = jnp.concatenate([wq, _swap_halves(wq[..., B_QK_NOPE:])], axis=-1)
    wq_l = wq_l.reshape(B_Q_LORA, B_HEADS * LANES).astype(BF16)
    wkv = w_kv_up.reshape(B_KV_LORA, B_HEADS, B_QK_NOPE + B_V_DIM)
    wkb = jnp.concatenate([wkv[..., :B_QK_NOPE], jnp.zeros_like(wkv[..., :B_QK_NOPE])], axis=-1)
    wkb = wkb.reshape(B_KV_LORA, B_HEADS * LANES).astype(BF16)
    wvb = wkv[..., B_QK_NOPE:].reshape(B_KV_LORA, B_WIDTH).astype(BF16)
    return w_in_l, wq_l, wkb, wvb


def _block_plan(counts, n_blocks):
    padded = (counts + MOE_BLK - 1) // MOE_BLK * MOE_BLK
    ends = jnp.cumsum(padded)
    starts = ends - padded
    first_row = jnp.arange(n_blocks, dtype=I32) * MOE_BLK
    expert = jnp.minimum(jnp.sum(ends[None, :] <= first_row[:, None], axis=1), N_EXPERTS - 1).astype(I32)
    new = jnp.concatenate([jnp.ones((1,), I32), (expert[1:] != expert[:-1]).astype(I32)])
    valid = jnp.clip((starts + counts)[expert] - first_row, 0, MOE_BLK).astype(I32)
    return starts.astype(I32), expert, new, valid


def kernel(x, g_attn_norm, w_in, rel_bias, g_q_latent, w_q_up, g_kv_latent, w_kv_up, g_out_a, g_out_b, w_out,
           g_ffn_norm, w_router_group, b_router_group, w_router_expert, b_router_expert, w_gate, w_up, w_down,
           g_final):
    batch, seq, d = x.shape
    t = batch * seq
    assert g_attn_norm.shape[0] == 1 and d == D_MODEL and seq % ROW_TILE == 0
    cos_t, sin_t = _rope_tables(seq)
    tri = jnp.triu(jnp.ones((ROW_TILE, ROW_TILE), F32), 1).astype(BF16)
    spread = (jnp.arange(2 * LANES)[:, None] % LANES == jnp.arange(A_WIDTH)[None, :] // A_HEAD_DIM).astype(BF16)
    n_blocks = t * TOP_K // MOE_BLK + N_EXPERTS
    x2 = x.reshape(t, d)
    row = lambda v: v.reshape(1, -1)

    w_in_l, wq_l, wkb_l, wvb_l = _layout_weights(w_in[0], w_q_up[0], w_kv_up[0])
    *qkv_a, qb, kb, vb = _proj_call(x2, row(g_attn_norm[0]), w_in_l, row(g_q_latent[0]), wq_l,
                                    row(g_kv_latent[0]), wkb_l, wvb_l, cos_t, sin_t, seq)
    oas, lses = [], []
    for pi, (window, dilation) in enumerate(DILATED_PATTERNS):
        bias = _dilated_bias(rel_bias, seq, dilation, window // (2 * dilation))
        o_p, lse_p = _dilated_call(qkv_a[pi], bias, batch, seq, dilation)
        oas.append(o_p)
        lses.append(lse_p)
    ob = _mla_call(qb, kb, vb, batch, seq)

    pad = ROUTER_ROWS - N_EXPERTS - N_GROUPS
    w_router = jnp.concatenate([w_router_expert[0], w_router_group[0], jnp.zeros((d, pad), F32)], axis=1).T
    b_router = jnp.concatenate([b_router_expert[0], b_router_group[0], jnp.zeros((pad,), F32)])
    x1, hp, idx, gates, cnt = _mix_call(oas, lses, ob, x2, row(g_out_a[0]), row(g_out_b[0]), w_out[0].astype(BF16),
                                        row(g_ffn_norm[0]), w_router, b_router.reshape(-1, 1), tri, spread)
    pstart, block_expert, block_new, block_valid = _block_plan(cnt[:N_EXPERTS, 0], n_blocks)
    pstart_col = jnp.concatenate([pstart, jnp.zeros((ROUTER_ROWS - N_EXPERTS,), I32)]).reshape(-1, 1)
    dest = _dest_call(idx, pstart_col)
    dests = (dest[0], dest[1])
    buf = _dispatch_call(block_valid, dests, hp)
    expert_out = _expert_call(block_expert, block_new, block_valid, buf, w_gate[0], w_up[0], w_down[0])
    return _combine_call(dests, x1, gates[:TOP_K].T, row(g_final), expert_out).reshape(batch, seq, d)
```

```python
import functools
import math

import numpy as np
import jax
import jax.numpy as jnp
from jax import lax
from jax.experimental import pallas as pl
from jax.experimental.pallas import tpu as pltpu

F32 = jnp.float32
BF16 = jnp.bfloat16
I32 = jnp.int32
U32 = jnp.uint32

D_MODEL = 1024
EPS = 1e-6
NEG_INF = -1e30
LANES = 128
ROW_SUBLANES = D_MODEL // LANES
PACKED_SUBLANES = ROW_SUBLANES // 2

A_HEADS = 8
A_HEAD_DIM = 64
A_WIDTH = 512
A_QKV_WIDTH = 3 * A_WIDTH
DILATED_PATTERNS = ((128, 1), (512, 4), (2048, 16))
REL_BUCKETS = 32
REL_MAX_DISTANCE = 1024
A_QB = 128

B_HEADS = 8
B_Q_LORA = 256
B_KV_LORA = 128
B_QK_NOPE = 64
B_QK_ROPE = 32
B_V_DIM = 64
B_WIDTH = 512
ROPE_THETA = 10000.0
B_SCALE = (B_QK_NOPE + B_QK_ROPE) ** -0.5
B_QB = 512
B_SUB = 256
LOG2E = math.log2(math.e)

N_GROUPS = 4
EXPERTS_PER_GROUP = 8
N_EXPERTS = 32
TOP_K = 2
EXPERT_FF = 256
MOE_BLK = 512
ROUTER_ROWS = 40

ROW_TILE = 512
ISSUE_GROUP = 8
COMBINE_CHUNK = 32
PROJ_COLS = 2048

_NT = (((1,), (1,)), ((), ()))


def _cparams(semantics, vmem_mb=48, **kw):
    return pltpu.CompilerParams(dimension_semantics=semantics,
                                vmem_limit_bytes=vmem_mb * 1024 * 1024, **kw)


def _rms(x, g):
    return x * lax.rsqrt(jnp.mean(x * x, axis=-1, keepdims=True) + EPS) * g


def _lane_iota(rows=1):
    return lax.broadcasted_iota(I32, (rows, LANES), 1)


def _row_tile(r, sublanes=ROW_SUBLANES):
    return pl.ds(pl.multiple_of(r * sublanes, sublanes), sublanes)


def _proj_kernel(x_ref, g_ref, win_ref, gq_ref, wq_ref, gkv_ref, wkb_ref, wvb_ref, cos_ref, sin_ref, *refs):
    out1, out4, out16 = refs[:3]
    qb_ref, kb_ref, vb_ref, slab_ref, slab4_ref = refs[3:]
    tm = x_ref.shape[0]
    h = _rms(x_ref[...], g_ref[...]).astype(BF16)
    lo = _lane_iota() < A_HEAD_DIM
    n_slabs = A_QKV_WIDTH // LANES
    parts = []
    for s in range(0, n_slabs, 2):
        part = jnp.dot(h, win_ref[:, LANES * s:LANES * (s + 2)], preferred_element_type=F32)
        if LANES * s < A_WIDTH:
            part = part * LOG2E
        slab_ref[s] = part[:, :LANES]
        slab_ref[s + 1] = part[:, LANES:]
        parts.append(part)
    out1[0, 0] = jnp.concatenate(parts, axis=1).astype(BF16)
    lat = jnp.dot(h, win_ref[:, A_QKV_WIDTH:], preferred_element_type=F32)
    cos = cos_ref[...]
    sin = sin_ref[...]
    n4 = tm // 4

    cq = _rms(lat[:, :B_Q_LORA], gq_ref[...]).astype(BF16)
    q = jnp.dot(cq, wq_ref[...], preferred_element_type=F32)
    q_mul = (cos + jnp.where(lo, 1.0, 0.0)) * (B_SCALE * LOG2E)
    q_rot = sin * (B_SCALE * LOG2E)
    qb_ref[...] = jnp.concatenate(
        [q[:, LANES * hd:LANES * (hd + 1)] * q_mul + pltpu.roll(q[:, LANES * hd:LANES * (hd + 1)], 96, 1) * q_rot
         for hd in range(B_HEADS)], axis=1).astype(BF16)

    ckv = _rms(lat[:, B_Q_LORA:B_Q_LORA + B_KV_LORA], gkv_ref[...]).astype(BF16)
    kr = lat[:, B_Q_LORA + B_KV_LORA:]
    kr = kr * cos + pltpu.roll(kr, 96, 1) * sin
    kn = jnp.dot(ckv, wkb_ref[...], preferred_element_type=F32)
    kb_ref[...] = jnp.concatenate([kn[:, LANES * hd:LANES * (hd + 1)] + kr for hd in range(B_HEADS)],
                                  axis=1).astype(BF16)
    vb_ref[...] = jnp.dot(ckv, wvb_ref[...], preferred_element_type=F32).astype(BF16)

    for c4 in range(4):
        pieces = [slab_ref[s, pl.ds(c4, n4, stride=4), :] for s in range(n_slabs)]
        for s, piece in enumerate(pieces):
            slab4_ref[s, c4] = piece
        out4[0, c4] = jnp.concatenate(pieces, axis=1).astype(BF16)
    for c4 in range(4):
        for j in range(4):
            out16[0, c4 + 4 * j] = jnp.concatenate(
                [slab4_ref[s, c4, pl.ds(j, n4 // 4, stride=4), :] for s in range(n_slabs)], axis=1).astype(BF16)


def _proj_call(x2, g_attn, w_in, g_q, w_q, g_kv, w_kb, w_vb, cos_t, sin_t, seq):
    t = x2.shape[0]
    tm = ROW_TILE
    nseq = seq // tm
    row = lambda i: (i, 0)
    const = lambda i: (0, 0)
    pos = lambda i: (i % nseq, 0)
    out = lambda w: jax.ShapeDtypeStruct((t, w), BF16)
    a_specs, a_shapes = [], []
    for _, r in DILATED_PATTERNS:
        a_specs.append(pl.BlockSpec((1, r, tm // r, A_QKV_WIDTH), lambda i: (i // nseq, 0, i % nseq, 0)))
        a_shapes.append(jax.ShapeDtypeStruct((t // seq, r, seq // r, A_QKV_WIDTH), BF16))
    return pl.pallas_call(
        _proj_kernel,
        grid=(t // tm,),
        in_specs=[
            pl.BlockSpec((tm, D_MODEL), row),
            pl.BlockSpec((1, D_MODEL), const),
            pl.BlockSpec((D_MODEL, PROJ_COLS), const),
            pl.BlockSpec((1, B_Q_LORA), const),
            pl.BlockSpec((B_Q_LORA, B_HEADS * LANES), const),
            pl.BlockSpec((1, B_KV_LORA), const),
            pl.BlockSpec((B_KV_LORA, B_HEADS * LANES), const),
            pl.BlockSpec((B_KV_LORA, B_WIDTH), const),
            pl.BlockSpec((tm, LANES), pos),
            pl.BlockSpec((tm, LANES), pos),
        ],
        out_specs=a_specs + [
            pl.BlockSpec((tm, B_HEADS * LANES), row),
            pl.BlockSpec((tm, B_HEADS * LANES), row),
            pl.BlockSpec((tm, B_WIDTH), row),
        ],
        out_shape=a_shapes + [out(B_HEADS * LANES), out(B_HEADS * LANES), out(B_WIDTH)],
        scratch_shapes=[pltpu.VMEM((A_QKV_WIDTH // LANES, tm, LANES), F32),
                        pltpu.VMEM((A_QKV_WIDTH // LANES, 4, tm // 4, LANES), F32)],
        compiler_params=_cparams(("parallel",)),
        name="proj",
    )(x2, g_attn, w_in, g_q, w_q, g_kv, w_kb, w_vb, cos_t, sin_t)


def _pack_bf16_pair(a, b):
    a_bits = lax.bitcast_convert_type(a.astype(BF16).astype(F32), U32) >> 16
    b_bits = lax.bitcast_convert_type(b.astype(BF16).astype(F32), U32) & jnp.uint32(0xFFFF0000)
    return a_bits | b_bits


def _unpack_bf16_pair(w):
    return (lax.bitcast_convert_type(w << 16, F32), lax.bitcast_convert_type(w & jnp.uint32(0xFFFF0000), F32))


def _dilated_kernel(qkv_ref, bias_ref, o_ref, lse_ref, *, seq_len, dilation, key_width, group):
    nblk = seq_len // A_QB
    lane = _lane_iota()
    lo = lane < A_HEAD_DIM
    first_class = pl.program_id(1) * group
    pairs = A_HEADS // 2

    def block(it, carry):
        c = it // nblk
        if nblk == 1:
            q0, ks, var = 0, 0, 0
        else:
            n = it % nblk
            q0 = pl.multiple_of(n * A_QB, A_QB)
            ks = pl.multiple_of(jnp.clip(q0 - 64, 0, seq_len - key_width), 64)
            var = jnp.where(n == 0, 0, jnp.where(n == nblk - 1, 2, 1))
        rows = pl.ds(q0, A_QB)
        keys = pl.ds(ks, key_width)
        if dilation == 1:
            out_rows = rows
        else:
            out_rows = pl.ds(first_class + c + dilation * q0, A_QB, stride=dilation)
        q_tiles = [qkv_ref[0, c, rows, LANES * p:LANES * (p + 1)] for p in range(pairs)]
        k_tiles = [qkv_ref[0, c, keys, A_WIDTH + LANES * p:A_WIDTH + LANES * (p + 1)] for p in range(pairs)]
        v_tiles = [qkv_ref[0, c, keys, 2 * A_WIDTH + LANES * p:2 * A_WIDTH + LANES * (p + 1)] for p in range(pairs)]
        zero = jnp.zeros((), BF16)
        k_heads = [jnp.where(lo, k_tiles[hd // 2], zero) if hd % 2 == 0 else jnp.where(lo, zero, k_tiles[hd // 2])
                   for hd in range(A_HEADS)]
        scores = [lax.dot_general(q_tiles[hd // 2], k_heads[hd], _NT, preferred_element_type=F32) + bias_ref[var, hd]
                  for hd in range(A_HEADS)]
        maxes = [jnp.max(s, axis=-1, keepdims=True) for s in scores]
        probs = [jnp.exp2(s - m) for s, m in zip(scores, maxes)]
        dens = [jnp.sum(pr, axis=-1, keepdims=True) for pr in probs]
        pvs = [jnp.dot(pr.astype(BF16), v_tiles[hd // 2], preferred_element_type=F32) for hd, pr in enumerate(probs)]
        outs = []
        for p in range(pairs):
            h0, h1 = 2 * p, 2 * p + 1
            outs.append(jnp.where(lo, pvs[h0] * (1.0 / dens[h0]), pvs[h1] * (1.0 / dens[h1])))
        lse = jnp.zeros((A_QB, LANES), F32)
        for hd in range(A_HEADS):
            lse = jnp.where(lane == hd, maxes[hd] + jnp.log2(dens[hd]), lse)
        lse_ref[0, out_rows, :] = lse
        for j in range(pairs // 2):
            o_ref[0, j, out_rows, :] = _pack_bf16_pair(outs[2 * j], outs[2 * j + 1])
        return carry

    lax.fori_loop(0, group * nblk, block, 0, unroll=4)


def _dilated_call(qkv, bias, batch, seq, dilation):
    r = dilation
    sl = seq // r
    kw = min(2 * A_QB, sl)
    group = r
    pairs = A_HEADS // 2
    return pl.pallas_call(
        functools.partial(_dilated_kernel, seq_len=sl, dilation=r, key_width=kw, group=group),
        grid=(batch, r // group),
        in_specs=[pl.BlockSpec((1, group, sl, A_QKV_WIDTH), lambda b, c: (b, c, 0, 0)),
                  pl.BlockSpec(bias.shape, lambda b, c: (0, 0, 0, 0))],
        out_specs=[pl.BlockSpec((1, pairs // 2, seq, LANES), lambda b, c: (b, 0, 0, 0)),
                   pl.BlockSpec((1, seq, LANES), lambda b, c: (b, 0, 0))],
        out_shape=[jax.ShapeDtypeStruct((batch, pairs // 2, seq, LANES), U32),
                   jax.ShapeDtypeStruct((batch, seq, LANES), F32)],
        compiler_params=_cparams(("parallel", "arbitrary")),
        name=f"dilated_r{r}",
    )(qkv, bias)


def _t5_bucket(rel):
    half = REL_BUCKETS // 2
    max_exact = half // 2
    n = np.abs(rel)
    large = max_exact + (np.log(np.maximum(n, 1) / max_exact)
                         / math.log(REL_MAX_DISTANCE / max_exact) * (half - max_exact)).astype(np.int32)
    large = np.minimum(large, half - 1)
    return (np.where(rel > 0, half, 0) + np.where(n < max_exact, n, large)).astype(np.int32)


def _dilated_bias(rel_bias, seq, dilation, half_steps):
    sl = seq // dilation
    kw = min(2 * A_QB, sl)
    offsets = [0] if sl == kw else [0, -half_steps, A_QB - kw]
    rel = np.stack([np.arange(kw)[None, :] + off - np.arange(A_QB)[:, None] for off in offsets])
    valid = np.abs(rel) <= half_steps
    bucket = np.where(valid, _t5_bucket(rel * dilation), REL_BUCKETS).astype(np.int32)
    onehot = (jnp.asarray(bucket)[..., None] == jnp.arange(REL_BUCKETS + 1, dtype=I32)).astype(F32)
    table = jnp.concatenate([rel_bias.astype(F32), jnp.full((1, A_HEADS), NEG_INF, F32)], axis=0)
    return jnp.einsum("vqkb,bh->vhqk", onehot, table * LOG2E, precision=lax.Precision.HIGHEST)


def _mla_kernel(q_ref, k_ref, v_ref, o_ref, v1_ref):
    lo = _lane_iota() < B_V_DIM
    sub = B_QB // B_SUB
    tiles = [slice(0, LANES), slice(LANES, 2 * LANES)]
    v1_ref[:, :LANES] = v_ref[0]
    v1_ref[:, LANES:] = jnp.ones((v_ref.shape[1], LANES), BF16)

    units = [(j, half) for j in range(sub) for half in range(2)]

    def rows(i):
        return [pl.ds(i * B_QB + B_SUB * j, B_SUB) for j in range(sub)]

    def scores(i):
        return [lax.dot_general(q_ref[0, rows(i)[j], tiles[half]], k_ref[0, :, tiles[half]], _NT,
                                preferred_element_type=F32) for j, half in units]

    def finish(i, sc):
        maxes = [jnp.max(s, axis=-1, keepdims=True) for s in sc]
        probs = [jnp.exp2(s - m) for s, m in zip(sc, maxes)]
        pvs = [jnp.dot(pr.astype(BF16), v1_ref[...], preferred_element_type=F32) for pr in probs]
        outs = [pv[:, :LANES] * (1.0 / pv[:, LANES:]) for pv in pvs]
        for j in range(sub):
            o_ref[0, rows(i)[j], :] = jnp.where(lo, outs[2 * j], outs[2 * j + 1]).astype(BF16)

    n_blocks = q_ref.shape[1] // B_QB
    sc = scores(0)
    for i in range(n_blocks):
        sc_next = scores(i + 1) if i + 1 < n_blocks else None
        finish(i, sc)
        sc = sc_next


def _mla_call(qb, kb, vb, batch, seq):
    qb = qb.reshape(batch, seq, B_HEADS * LANES)
    kb = kb.reshape(batch, seq, B_HEADS * LANES)
    vb = vb.reshape(batch, seq, B_WIDTH)
    pair = lambda w: pl.BlockSpec((1, seq, w), lambda b, p: (b, 0, p))
    out = pl.pallas_call(
        _mla_kernel,
        grid=(batch, B_HEADS // 2),
        in_specs=[pair(2 * LANES), pair(2 * LANES), pair(LANES)],
        out_specs=pair(LANES),
        out_shape=jax.ShapeDtypeStruct((batch, seq, B_WIDTH), BF16),
        scratch_shapes=[pltpu.VMEM((seq, 2 * LANES), BF16)],
        compiler_params=_cparams(("parallel", "parallel")),
        name="mla",
    )(qb, kb, vb)
    return out.reshape(batch * seq, B_WIDTH)


def _merge_patterns(o_refs, lse_refs, spread_ref):
    lses = [r[0] for r in lse_refs]
    top = functools.reduce(jnp.maximum, lses)
    es = [jnp.exp2(l - top) for l in lses]
    inv = 1.0 / functools.reduce(jnp.add, es)
    spread = spread_ref[...]
    weights = []
    for e in es:
        w = e * inv
        w_hi = w.astype(BF16)
        w_lo = (w - w_hi.astype(F32)).astype(BF16)
        weights.append(jnp.dot(jnp.concatenate([w_hi, w_lo], axis=1), spread, preferred_element_type=F32))
    tiles = []
    for j in range(A_HEADS // 4):
        outs = [_unpack_bf16_pair(r[0, j]) for r in o_refs]
        for half in range(2):
            tile = slice(LANES * (2 * j + half), LANES * (2 * j + half + 1))
            tiles.append(functools.reduce(jnp.add, [w[:, tile] * o[half] for w, o in zip(weights, outs)]))
    return jnp.concatenate(tiles, axis=1)


def _mix_kernel(o1_ref, o4_ref, o16_ref, l1_ref, l4_ref, l16_ref, ob_ref, x_ref, ga_ref, gb_ref, wo_ref,
                gf_ref, wr_ref, br_ref, tri_ref, spread_ref, x1_ref, hp_ref, idx_ref, gate_ref, cnt_ref, carry_ref):
    i = pl.program_id(0)

    @pl.when(i == 0)
    def _():
        carry_ref[...] = jnp.zeros_like(carry_ref)

    oa = _merge_patterns((o1_ref, o4_ref, o16_ref), (l1_ref, l4_ref, l16_ref), spread_ref)
    a = _rms(oa, ga_ref[...]).astype(BF16)
    b = _rms(ob_ref[...].astype(F32), gb_ref[...]).astype(BF16)
    mix = (jnp.dot(a, wo_ref[0:A_WIDTH, :], preferred_element_type=F32)
           + jnp.dot(b, wo_ref[A_WIDTH:, :], preferred_element_type=F32))
    x1 = x_ref[...] + mix
    x1_ref[...] = x1
    h2 = _rms(x1, gf_ref[...])
    half = D_MODEL // 2
    for c in range(PACKED_SUBLANES):
        hp_ref[pl.ds(c, h2.shape[0], stride=PACKED_SUBLANES), :] = _pack_bf16_pair(
            h2[:, LANES * c:LANES * (c + 1)], h2[:, half + LANES * c:half + LANES * (c + 1)])

    wr = wr_ref[...]
    wr_hi = wr.astype(BF16)
    wr_lo = (wr - wr_hi.astype(F32)).astype(BF16)
    h_hi = h2.astype(BF16)
    h_lo = (h2 - h_hi.astype(F32)).astype(BF16)
    lg_hi = lax.dot_general(jnp.concatenate([wr_hi, wr_lo], axis=0), h_hi, _NT, preferred_element_type=F32)
    lg = (lg_hi[:ROUTER_ROWS] + lg_hi[ROUTER_ROWS:]
          + lax.dot_general(wr_hi, h_lo, _NT, preferred_element_type=F32) + br_ref[...])
    row = lax.broadcasted_iota(I32, lg.shape, 0)
    is_g = (row >= N_EXPERTS) & (row < N_EXPERTS + N_GROUPS)
    gl = jnp.where(is_g, lg, NEG_INF)
    ge = jnp.exp(gl - jnp.max(gl, axis=0, keepdims=True))
    gp = ge / jnp.sum(ge, axis=0, keepdims=True)
    g_gate = jnp.max(gp, axis=0, keepdims=True)
    g_idx = jnp.min(jnp.where(is_g & (gp == g_gate), row - N_EXPERTS, LANES), axis=0, keepdims=True)
    sel = (row >> 3) == g_idx
    el = jnp.where(sel, lg, NEG_INF)
    ee = jnp.exp(el - jnp.max(el, axis=0, keepdims=True))
    ep = jnp.where(sel, ee / jnp.sum(ee, axis=0, keepdims=True), -1.0)
    p1 = jnp.max(ep, axis=0, keepdims=True)
    i1 = jnp.min(jnp.where(ep == p1, row, LANES), axis=0, keepdims=True)
    ep2 = jnp.where(row == i1, -1.0, ep)
    p2 = jnp.max(ep2, axis=0, keepdims=True)
    i2 = jnp.min(jnp.where(sel & (ep2 == p2) & (row != i1), row, LANES), axis=0, keepdims=True)
    den = p1 + p2
    g1 = g_gate * p1 / den
    g2 = g_gate * p2 / den

    hit1 = row == i1
    hit2 = row == i2
    onehot = jnp.where(hit1 | hit2, 1.0, 0.0)
    before = jnp.dot(onehot.astype(BF16), tri_ref[...], preferred_element_type=F32) + carry_ref[...]
    r1 = jnp.sum(jnp.where(hit1, before, 0.0), axis=0, keepdims=True).astype(I32)
    r2 = jnp.sum(jnp.where(hit2, before, 0.0), axis=0, keepdims=True).astype(I32)
    carry_ref[...] += jnp.sum(onehot, axis=1, keepdims=True)

    row8 = lax.broadcasted_iota(I32, idx_ref.shape, 0)
    idx_ref[...] = jnp.where(row8 == 0, i1, jnp.where(row8 == 1, i2,
                             jnp.where(row8 == 2, r1, jnp.where(row8 == 3, r2, 0))))
    gate_ref[...] = jnp.where(row8 == 0, g1, jnp.where(row8 == 1, g2, 0.0))

    @pl.when(i == pl.num_programs(0) - 1)
    def _():
        cnt_ref[...] = jnp.broadcast_to(carry_ref[...], cnt_ref.shape).astype(I32)


def _mix_call(oas, lses, ob, x2, g_a, g_b, w_out, g_ffn, w_router, b_router, tri, spread):
    t = x2.shape[0]
    tm = ROW_TILE
    nseq = oas[0].shape[2] // tm
    row = lambda i: (i, 0)
    const = lambda i: (0, 0)
    slab = lambda n: pl.BlockSpec((1, n, tm, LANES), lambda i: (i // nseq, 0, i % nseq, 0))
    o_slab = slab(A_HEADS // 4)
    lse_slab = pl.BlockSpec((1, tm, LANES), lambda i: (i // nseq, i % nseq, 0))
    return pl.pallas_call(
        _mix_kernel,
        grid=(t // tm,),
        in_specs=[
            o_slab, o_slab, o_slab, lse_slab, lse_slab, lse_slab,
            pl.BlockSpec((tm, B_WIDTH), row),
            pl.BlockSpec((tm, D_MODEL), row),
            pl.BlockSpec((1, A_WIDTH), const),
            pl.BlockSpec((1, B_WIDTH), const),
            pl.BlockSpec((D_MODEL, D_MODEL), const),
            pl.BlockSpec((1, D_MODEL), const),
            pl.BlockSpec((ROUTER_ROWS, D_MODEL), const),
            pl.BlockSpec((ROUTER_ROWS, 1), const),
            pl.BlockSpec((tm, tm), const),
            pl.BlockSpec((2 * LANES, A_WIDTH), const),
        ],
        out_specs=[
            pl.BlockSpec((tm, D_MODEL), row),
            pl.BlockSpec((tm * PACKED_SUBLANES, LANES), row),
            pl.BlockSpec((8, tm), lambda i: (0, i)),
            pl.BlockSpec((8, tm), lambda i: (0, i)),
            pl.BlockSpec((ROUTER_ROWS, LANES), const),
        ],
        out_shape=[
            jax.ShapeDtypeStruct((t, D_MODEL), F32),
            jax.ShapeDtypeStruct((t * PACKED_SUBLANES, LANES), U32),
            jax.ShapeDtypeStruct((8, t), I32),
            jax.ShapeDtypeStruct((8, t), F32),
            jax.ShapeDtypeStruct((ROUTER_ROWS, LANES), I32),
        ],
        scratch_shapes=[pltpu.VMEM((ROUTER_ROWS, 1), F32)],
        compiler_params=_cparams(("arbitrary",)),
        name="mix_router",
    )(*oas, *lses, ob, x2, g_a, g_b, w_out, g_ffn, w_router, b_router, tri, spread)


def _dest_kernel(idx_ref, pstart_ref, dest_ref):
    idx = idx_ref[...]
    row = lax.broadcasted_iota(I32, (ROUTER_ROWS, idx.shape[1]), 0)
    ps = pstart_ref[...]

    def slot(k):
        return jnp.sum(jnp.where(row == idx[k:k + 1, :], ps, 0), axis=0, keepdims=True) + idx[2 + k:3 + k, :]

    row8 = lax.broadcasted_iota(I32, idx.shape, 0)
    dest_ref[...] = jnp.where(row8 == 0, slot(0), jnp.where(row8 == 1, slot(1), 0))


def _dest_call(idx, pstart):
    t = idx.shape[1]
    tm = 4 * ROW_TILE
    return pl.pallas_call(
        _dest_kernel,
        grid=(t // tm,),
        in_specs=[pl.BlockSpec((8, tm), lambda i: (0, i)), pl.BlockSpec((ROUTER_ROWS, 1), lambda i: (0, 0))],
        out_specs=pl.BlockSpec((8, tm), lambda i: (0, i)),
        out_shape=jax.ShapeDtypeStruct((8, t), I32),
        compiler_params=_cparams(("parallel",)),
        name="dest_rows",
    )(idx, pstart)


def _dispatch_kernel(valid_ref, d0_ref, d1_ref, h_ref, buf_ref, zero_ref, sem, pad_sem):
    i = pl.program_id(0)
    tt = h_ref.shape[0] // PACKED_SUBLANES
    n_blocks = valid_ref.shape[0]

    def for_padded_blocks(fn):
        def body(j, c):
            taken = valid_ref[j]
            even = (taken + 1) // 2 * 2

            @pl.when(even < MOE_BLK)
            def _():
                n_pad = pl.multiple_of((MOE_BLK - even) * PACKED_SUBLANES, ROW_SUBLANES)
                first = pl.multiple_of((j * MOE_BLK + even) * PACKED_SUBLANES, ROW_SUBLANES)
                fn(pltpu.make_async_copy(zero_ref.at[pl.ds(0, n_pad)], buf_ref.at[pl.ds(first, n_pad)], pad_sem))

            @pl.when(even != taken)
            def _():
                fn(pltpu.make_async_copy(zero_ref.at[pl.ds(0, PACKED_SUBLANES)],
                                         buf_ref.at[_row_tile(j * MOE_BLK + taken, PACKED_SUBLANES)], pad_sem))
            return c
        lax.fori_loop(0, n_blocks, body, 0)

    @pl.when(i == 0)
    def _():
        zero_ref[...] = jnp.zeros_like(zero_ref)
        for_padded_blocks(lambda cp: cp.start())

    def issue(g, c):
        base = pl.multiple_of(g * ISSUE_GROUP, ISSUE_GROUP)
        for j in range(ISSUE_GROUP):
            for prio, d_ref in enumerate((d0_ref, d1_ref)):
                pltpu.make_async_copy(h_ref.at[_row_tile(base + j, PACKED_SUBLANES)],
                                      buf_ref.at[_row_tile(d_ref[base + j], PACKED_SUBLANES)],
                                      sem).start(priority=prio)
        return c

    lax.fori_loop(0, tt // ISSUE_GROUP, issue, 0)
    for k in range(TOP_K):
        pltpu.make_async_copy(h_ref, buf_ref.at[pl.ds(0, tt * PACKED_SUBLANES)], sem).wait()

    @pl.when(i == pl.num_programs(0) - 1)
    def _():
        for_padded_blocks(lambda cp: cp.wait())


def _dispatch_call(block_valid, dests, hp):
    t = hp.shape[0] // PACKED_SUBLANES
    tt = 2 * ROW_TILE
    n_rows = block_valid.shape[0] * MOE_BLK
    return pl.pallas_call(
        _dispatch_kernel,
        grid_spec=pltpu.PrefetchScalarGridSpec(
            num_scalar_prefetch=1,
            grid=(t // tt,),
            in_specs=[
                pl.BlockSpec((tt,), lambda i, va: (i,), memory_space=pltpu.SMEM),
                pl.BlockSpec((tt,), lambda i, va: (i,), memory_space=pltpu.SMEM),
                pl.BlockSpec((tt * PACKED_SUBLANES, LANES), lambda i, va: (i, 0)),
            ],
            out_specs=pl.BlockSpec(memory_space=pl.ANY),
            scratch_shapes=[pltpu.VMEM((MOE_BLK * PACKED_SUBLANES, LANES), U32),
                            pltpu.SemaphoreType.DMA(()), pltpu.SemaphoreType.DMA(())],
        ),
        out_shape=jax.ShapeDtypeStruct((n_rows * PACKED_SUBLANES, LANES), U32),
        compiler_params=_cparams(("arbitrary",), disable_bounds_checks=True, has_side_effects=True),
        name="dispatch",
    )(block_valid, *dests, hp)


def _expert_kernel(be_ref, new_ref, valid_ref, buf_ref, wg_ref, wu_ref, wd_ref, out_ref, wg_s, wu_s, wd_s):
    j = pl.program_id(0)
    del be_ref

    @pl.when(new_ref[j] == 1)
    def _():
        wg_s[...] = wg_ref[0].astype(BF16)
        wu_s[...] = wu_ref[0].astype(BF16)
        wd_s[...] = wd_ref[0].astype(BF16)

    n_valid = valid_ref[j]

    @pl.when(n_valid > 0)
    def _():
        blk = buf_ref.shape[0] // PACKED_SUBLANES
        words = [_unpack_bf16_pair(buf_ref[pl.ds(c, blk, stride=PACKED_SUBLANES), :]) for c in range(PACKED_SUBLANES)]
        x = jnp.concatenate([w[0] for w in words] + [w[1] for w in words], axis=1).astype(BF16)
        cols = [pl.ds(c, blk, stride=ROW_SUBLANES) for c in range(ROW_SUBLANES)]
        g = jnp.dot(x, wg_s[...], preferred_element_type=F32)
        u = jnp.dot(x, wu_s[...], preferred_element_type=F32)
        hb = (g * jax.nn.sigmoid(g)) * u
        out = jnp.dot(hb.astype(BF16), wd_s[...], preferred_element_type=F32)
        for c, rows in enumerate(cols):
            out_ref[rows, :] = out[:, LANES * c:LANES * (c + 1)]

    @pl.when(n_valid == 0)
    def _():
        out_ref[...] = jnp.zeros_like(out_ref)


def _expert_call(block_expert, block_new, block_valid, buf, w_gate, w_up, w_down):
    nb = buf.shape[0] // (MOE_BLK * PACKED_SUBLANES)
    wsel = lambda j, be, nw, va: (be[j], 0, 0)
    rows = lambda sublanes: pl.BlockSpec((MOE_BLK * sublanes, LANES), lambda j, be, nw, va: (j, 0))
    return pl.pallas_call(
        _expert_kernel,
        grid_spec=pltpu.PrefetchScalarGridSpec(
            num_scalar_prefetch=3,
            grid=(nb,),
            in_specs=[
                rows(PACKED_SUBLANES),
                pl.BlockSpec((1, D_MODEL, EXPERT_FF), wsel),
                pl.BlockSpec((1, D_MODEL, EXPERT_FF), wsel),
                pl.BlockSpec((1, EXPERT_FF, D_MODEL), wsel),
            ],
            out_specs=rows(ROW_SUBLANES),
            scratch_shapes=[pltpu.VMEM((D_MODEL, EXPERT_FF), BF16),
                            pltpu.VMEM((D_MODEL, EXPERT_FF), BF16),
                            pltpu.VMEM((EXPERT_FF, D_MODEL), BF16)],
        ),
        out_shape=jax.ShapeDtypeStruct((nb * MOE_BLK * ROW_SUBLANES, LANES), F32),
        compiler_params=_cparams(("arbitrary",)),
        name="experts",
    )(block_expert, block_new, block_valid, buf, w_gate, w_up, w_down)


def _combine_kernel(d0_ref, d1_ref, d0_next_ref, d1_next_ref, d0_ahead_ref, d1_ahead_ref, x1_ref, gate_ref,
                    gf_ref, eo_ref, o_ref, rows_a, rows_b, rows_c, sems):
    i = pl.program_id(0)
    tt = x1_ref.shape[0]
    bufs = (rows_a, rows_b, rows_c)
    n_buf = len(bufs)

    def start_row(d_refs, slot, r):
        for k, d_ref in enumerate(d_refs):
            pltpu.make_async_copy(eo_ref.at[_row_tile(d_ref[r])], bufs[slot].at[k, _row_tile(r)],
                                  sems.at[slot]).start(priority=k)

    def drain(slot):
        for k in range(TOP_K):
            pltpu.make_async_copy(eo_ref.at[pl.ds(0, tt * ROW_SUBLANES)], bufs[slot].at[k], sems.at[slot]).wait()

    @pl.when(i == 0)
    def _():
        def issue(g, c):
            base = pl.multiple_of(g * ISSUE_GROUP, ISSUE_GROUP)
            for j in range(ISSUE_GROUP):
                start_row((d0_ref, d1_ref), 0, base + j)
                start_row((d0_next_ref, d1_next_ref), 1, base + j)
            return c
        lax.fori_loop(0, tt // ISSUE_GROUP, issue, 0)

    def step(slot):
        ahead = (slot + 2) % n_buf
        drain(slot)
        gf = gf_ref[...]
        for ch in range(tt // COMBINE_CHUNK):
            r0 = ch * COMBINE_CHUNK
            for j in range(COMBINE_CHUNK):
                start_row((d0_ahead_ref, d1_ahead_ref), ahead, r0 + j)
            gate = gate_ref[r0:r0 + COMBINE_CHUNK, :]

            def rows(k):
                return jnp.concatenate(
                    [bufs[slot][k, pl.ds(ROW_SUBLANES * r0 + c, COMBINE_CHUNK, stride=ROW_SUBLANES), :]
                     for c in range(ROW_SUBLANES)], axis=1)

            y = rows(0) * gate[:, 0:1] + rows(1) * gate[:, 1:2]
            o_ref[r0:r0 + COMBINE_CHUNK, :] = _rms(x1_ref[r0:r0 + COMBINE_CHUNK, :] + y, gf)

        @pl.when(i == pl.num_programs(0) - 1)
        def _():
            drain((slot + 1) % n_buf)
            drain(ahead)

    for phase in range(n_buf):
        pl.when(i % n_buf == phase)(functools.partial(step, phase))


def _combine_call(dests, x1, gates, g_final, expert_out):
    t = x1.shape[0]
    tt = ROW_TILE // 2
    last = t // tt - 1
    tile = lambda ahead: pl.BlockSpec((tt,), lambda i: (jnp.minimum(i + ahead, last),), memory_space=pltpu.SMEM)
    buf = pltpu.VMEM((TOP_K, tt * ROW_SUBLANES, LANES), F32)
    return pl.pallas_call(
        _combine_kernel,
        grid=(t // tt,),
        in_specs=[
            tile(0), tile(0), tile(1), tile(1), tile(2), tile(2),
            pl.BlockSpec((tt, D_MODEL), lambda i: (i, 0)),
            pl.BlockSpec((tt, TOP_K), lambda i: (i, 0)),
            pl.BlockSpec((1, D_MODEL), lambda i: (0, 0)),
            pl.BlockSpec(memory_space=pl.ANY),
        ],
        out_specs=pl.BlockSpec((tt, D_MODEL), lambda i: (i, 0)),
        out_shape=jax.ShapeDtypeStruct((t, D_MODEL), F32),
        scratch_shapes=[buf, buf, buf, pltpu.SemaphoreType.DMA((3,))],
        compiler_params=_cparams(("arbitrary",), disable_bounds_checks=True),
        name="combine",
    )(*dests, *dests, *dests, x1, gates, g_final, expert_out)


def _rope_tables(seq):
    half = B_QK_ROPE // 2
    inv_freq = ROPE_THETA ** (-(jnp.arange(half, dtype=F32) / half))
    ang = jnp.arange(seq, dtype=F32)[:, None] * inv_freq[None, :]
    cos, sin = jnp.cos(ang), jnp.sin(ang)
    z = jnp.zeros((seq, B_QK_NOPE), F32)
    z2 = jnp.zeros((seq, B_QK_ROPE), F32)
    return (jnp.concatenate([z, cos, cos, z2], axis=1), jnp.concatenate([z, -sin, sin, z2], axis=1))


def _swap_halves(w):
    half = w.shape[-1] // 2
    return jnp.concatenate([w[..., half:], w[..., :half]], axis=-1)


def _layout_weights(w_in, w_q_up, w_kv_up):
    d = w_in.shape[0]
    w_kr = w_in[:, 3 * A_WIDTH + B_Q_LORA + B_KV_LORA:]
    w_in_l = jnp.concatenate(
        [w_in[:, :A_WIDTH] * (A_HEAD_DIM ** -0.5), w_in[:, A_WIDTH:3 * A_WIDTH + B_Q_LORA + B_KV_LORA],
         jnp.zeros((d, B_QK_NOPE), F32), w_kr, _swap_halves(w_kr)], axis=1).astype(BF16)
    wq = w_q_up.reshape(B_Q_LORA, B_HEADS, B_QK_NOPE + B_QK_ROPE)
    wq_l = jnp.concatenate([wq, _swap_halves(wq[..., B_QK_NOPE:])], axis=-1)
    wq_l = wq_l.reshape(B_Q_LORA, B_HEADS * LANES).astype(BF16)
    wkv = w_kv_up.reshape(B_KV_LORA, B_HEADS, B_QK_NOPE + B_V_DIM)
    wkb = jnp.concatenate([wkv[..., :B_QK_NOPE], jnp.zeros_like(wkv[..., :B_QK_NOPE])], axis=-1)
    wkb = wkb.reshape(B_KV_LORA, B_HEADS * LANES).astype(BF16)
    wvb = wkv[..., B_QK_NOPE:].reshape(B_KV_LORA, B_WIDTH).astype(BF16)
    return w_in_l, wq_l, wkb, wvb


def _block_plan(counts, n_blocks):
    padded = (counts + MOE_BLK - 1) // MOE_BLK * MOE_BLK
    ends = jnp.cumsum(padded)
    starts = ends - padded
    first_row = jnp.arange(n_blocks, dtype=I32) * MOE_BLK
    expert = jnp.minimum(jnp.sum(ends[None, :] <= first_row[:, None], axis=1), N_EXPERTS - 1).astype(I32)
    new = jnp.concatenate([jnp.ones((1,), I32), (expert[1:] != expert[:-1]).astype(I32)])
    valid = jnp.clip((starts + counts)[expert] - first_row, 0, MOE_BLK).astype(I32)
    return starts.astype(I32), expert, new, valid


def kernel(x, g_attn_norm, w_in, rel_bias, g_q_latent, w_q_up, g_kv_latent, w_kv_up, g_out_a, g_out_b, w_out,
           g_ffn_norm, w_router_group, b_router_group, w_router_expert, b_router_expert, w_gate, w_up, w_down,
           g_final):
    batch, seq, d = x.shape
    t = batch * seq
    assert g_attn_norm.shape[0] == 1 and d == D_MODEL and seq % ROW_TILE == 0
    cos_t, sin_t = _rope_tables(seq)
    tri = jnp.triu(jnp.ones((ROW_TILE, ROW_TILE), F32), 1).astype(BF16)
    spread = (jnp.arange(2 * LANES)[:, None] % LANES == jnp.arange(A_WIDTH)[None, :] // A_HEAD_DIM).astype(BF16)
    n_blocks = t * TOP_K // MOE_BLK + N_EXPERTS
    x2 = x.reshape(t, d)
    row = lambda v: v.reshape(1, -1)

    w_in_l, wq_l, wkb_l, wvb_l = _layout_weights(w_in[0], w_q_up[0], w_kv_up[0])
    *qkv_a, qb, kb, vb = _proj_call(x2, row(g_attn_norm[0]), w_in_l, row(g_q_latent[0]), wq_l,
                                    row(g_kv_latent[0]), wkb_l, wvb_l, cos_t, sin_t, seq)
    oas, lses = [], []
    for pi, (window, dilation) in enumerate(DILATED_PATTERNS):
        bias = _dilated_bias(rel_bias, seq, dilation, window // (2 * dilation))
        o_p, lse_p = _dilated_call(qkv_a[pi], bias, batch, seq, dilation)
        oas.append(o_p)
        lses.append(lse_p)
    ob = _mla_call(qb, kb, vb, batch, seq)

    pad = ROUTER_ROWS - N_EXPERTS - N_GROUPS
    w_router = jnp.concatenate([w_router_expert[0], w_router_group[0], jnp.zeros((d, pad), F32)], axis=1).T
    b_router = jnp.concatenate([b_router_expert[0], b_router_group[0], jnp.zeros((pad,), F32)])
    x1, hp, idx, gates, cnt = _mix_call(oas, lses, ob, x2, row(g_out_a[0]), row(g_out_b[0]), w_out[0].astype(BF16),
                                        row(g_ffn_norm[0]), w_router, b_router.reshape(-1, 1), tri, spread)
    pstart, block_expert, block_new, block_valid = _block_plan(cnt[:N_EXPERTS, 0], n_blocks)
    pstart_col = jnp.concatenate([pstart, jnp.zeros((ROUTER_ROWS - N_EXPERTS,), I32)]).reshape(-1, 1)
    dest = _dest_call(idx, pstart_col)
    dests = (dest[0], dest[1])
    buf = _dispatch_call(block_valid, dests, hp)
    expert_out = _expert_call(block_expert, block_new, block_valid, buf, w_gate[0], w_up[0], w_down[0])
    return _combine_call(dests, x1, gates[:TOP_K].T, row(g_final), expert_out).reshape(batch, seq, d)
```

```python
import functools
import math

import numpy as np
import jax
import jax.numpy as jnp
from jax import lax
from jax.experimental import pallas as pl
from jax.experimental.pallas import tpu as pltpu

F32 = jnp.float32
BF16 = jnp.bfloat16
I32 = jnp.int32
U32 = jnp.uint32

D_MODEL = 1024
EPS = 1e-6
NEG_INF = -1e30
LANES = 128
ROW_SUBLANES = D_MODEL // LANES
PACKED_SUBLANES = ROW_SUBLANES // 2

A_HEADS = 8
A_HEAD_DIM = 64
A_WIDTH = 512
A_QKV_WIDTH = 3 * A_WIDTH
DILATED_PATTERNS = ((128, 1), (512, 4), (2048, 16))
REL_BUCKETS = 32
REL_MAX_DISTANCE = 1024
A_QB = 128

B_HEADS = 8
B_Q_LORA = 256
B_KV_LORA = 128
B_QK_NOPE = 64
B_QK_ROPE = 32
B_V_DIM = 64
B_WIDTH = 512
ROPE_THETA = 10000.0
B_SCALE = (B_QK_NOPE + B_QK_ROPE) ** -0.5
B_QB = 512
B_SUB = 256
LOG2E = math.log2(math.e)

N_GROUPS = 4
EXPERTS_PER_GROUP = 8
N_EXPERTS = 32
TOP_K = 2
EXPERT_FF = 256
MOE_BLK = 512
ROUTER_ROWS = 40
MIX_UNITS = 2

ROW_TILE = 512
ISSUE_GROUP = 8
COMBINE_CHUNK = 32
PROJ_COLS = 2048

_NT = (((1,), (1,)), ((), ()))


def _cparams(semantics, vmem_mb=48, **kw):
    return pltpu.CompilerParams(dimension_semantics=semantics,
                                vmem_limit_bytes=vmem_mb * 1024 * 1024, **kw)


def _rms(x, g):
    return x * lax.rsqrt(jnp.mean(x * x, axis=-1, keepdims=True) + EPS) * g


def _lane_iota(rows=1):
    return lax.broadcasted_iota(I32, (rows, LANES), 1)


def _row_tile(r, sublanes=ROW_SUBLANES):
    return pl.ds(pl.multiple_of(r * sublanes, sublanes), sublanes)


def _proj_kernel(x_ref, g_ref, win_ref, gq_ref, wq_ref, gkv_ref, wkb_ref, wvb_ref, cos_ref, sin_ref, *refs):
    out1, out4, out16 = refs[:3]
    qb_ref, kb_ref, vb_ref, slab_ref, slab4_ref = refs[3:]
    tm = x_ref.shape[0]
    h = _rms(x_ref[...], g_ref[...]).astype(BF16)
    lo = _lane_iota() < A_HEAD_DIM
    n_slabs = A_QKV_WIDTH // LANES
    parts = []
    for s in range(0, n_slabs, 2):
        part = jnp.dot(h, win_ref[:, LANES * s:LANES * (s + 2)], preferred_element_type=F32)
        if LANES * s < A_WIDTH:
            part = part * LOG2E
        slab_ref[s] = part[:, :LANES]
        slab_ref[s + 1] = part[:, LANES:]
        parts.append(part)
    out1[0, 0] = jnp.concatenate(parts, axis=1).astype(BF16)
    lat = jnp.dot(h, win_ref[:, A_QKV_WIDTH:], preferred_element_type=F32)
    cos = cos_ref[...]
    sin = sin_ref[...]
    n4 = tm // 4

    cq = _rms(lat[:, :B_Q_LORA], gq_ref[...]).astype(BF16)
    q = jnp.dot(cq, wq_ref[...], preferred_element_type=F32)
    q_mul = (cos + jnp.where(lo, 1.0, 0.0)) * (B_SCALE * LOG2E)
    q_rot = sin * (B_SCALE * LOG2E)
    qb_ref[...] = jnp.concatenate(
        [q[:, LANES * hd:LANES * (hd + 1)] * q_mul + pltpu.roll(q[:, LANES * hd:LANES * (hd + 1)], 96, 1) * q_rot
         for hd in range(B_HEADS)], axis=1).astype(BF16)

    ckv = _rms(lat[:, B_Q_LORA:B_Q_LORA + B_KV_LORA], gkv_ref[...]).astype(BF16)
    kr = lat[:, B_Q_LORA + B_KV_LORA:]
    kr = kr * cos + pltpu.roll(kr, 96, 1) * sin
    kn = jnp.dot(ckv, wkb_ref[...], preferred_element_type=F32)
    kb_ref[...] = jnp.concatenate([kn[:, LANES * hd:LANES * (hd + 1)] + kr for hd in range(B_HEADS)],
                                  axis=1).astype(BF16)
    vb_ref[...] = jnp.dot(ckv, wvb_ref[...], preferred_element_type=F32).astype(BF16)

    for c4 in range(4):
        pieces = [slab_ref[s, pl.ds(c4, n4, stride=4), :] for s in range(n_slabs)]
        for s, piece in enumerate(pieces):
            slab4_ref[s, c4] = piece
        out4[0, c4] = jnp.concatenate(pieces, axis=1).astype(BF16)
    for c4 in range(4):
        for j in range(4):
            out16[0, c4 + 4 * j] = jnp.concatenate(
                [slab4_ref[s, c4, pl.ds(j, n4 // 4, stride=4), :] for s in range(n_slabs)], axis=1).astype(BF16)


def _proj_call(x2, g_attn, w_in, g_q, w_q, g_kv, w_kb, w_vb, cos_t, sin_t, seq):
    t = x2.shape[0]
    tm = ROW_TILE
    nseq = seq // tm
    row = lambda i: (i, 0)
    const = lambda i: (0, 0)
    pos = lambda i: (i % nseq, 0)
    out = lambda w: jax.ShapeDtypeStruct((t, w), BF16)
    a_specs, a_shapes = [], []
    for _, r in DILATED_PATTERNS:
        a_specs.append(pl.BlockSpec((1, r, tm // r, A_QKV_WIDTH), lambda i: (i // nseq, 0, i % nseq, 0)))
        a_shapes.append(jax.ShapeDtypeStruct((t // seq, r, seq // r, A_QKV_WIDTH), BF16))
    return pl.pallas_call(
        _proj_kernel,
        grid=(t // tm,),
        in_specs=[
            pl.BlockSpec((tm, D_MODEL), row),
            pl.BlockSpec((1, D_MODEL), const),
            pl.BlockSpec((D_MODEL, PROJ_COLS), const),
            pl.BlockSpec((1, B_Q_LORA), const),
            pl.BlockSpec((B_Q_LORA, B_HEADS * LANES), const),
            pl.BlockSpec((1, B_KV_LORA), const),
            pl.BlockSpec((B_KV_LORA, B_HEADS * LANES), const),
            pl.BlockSpec((B_KV_LORA, B_WIDTH), const),
            pl.BlockSpec((tm, LANES), pos),
            pl.BlockSpec((tm, LANES), pos),
        ],
        out_specs=a_specs + [
            pl.BlockSpec((tm, B_HEADS * LANES), row),
            pl.BlockSpec((tm, B_HEADS * LANES), row),
            pl.BlockSpec((tm, B_WIDTH), row),
        ],
        out_shape=a_shapes + [out(B_HEADS * LANES), out(B_HEADS * LANES), out(B_WIDTH)],
        scratch_shapes=[pltpu.VMEM((A_QKV_WIDTH // LANES, tm, LANES), F32),
                        pltpu.VMEM((A_QKV_WIDTH // LANES, 4, tm // 4, LANES), F32)],
        compiler_params=_cparams(("parallel",)),
        name="proj",
    )(x2, g_attn, w_in, g_q, w_q, g_kv, w_kb, w_vb, cos_t, sin_t)


def _pack_bf16_pair(a, b):
    a_bits = lax.bitcast_convert_type(a.astype(BF16).astype(F32), U32) >> 16
    b_bits = lax.bitcast_convert_type(b.astype(BF16).astype(F32), U32) & jnp.uint32(0xFFFF0000)
    return a_bits | b_bits


def _unpack_bf16_pair(w):
    return (lax.bitcast_convert_type(w << 16, F32), lax.bitcast_convert_type(w & jnp.uint32(0xFFFF0000), F32))


def _dilated_kernel(qkv_ref, bias_ref, o_ref, lse_ref, *, seq_len, dilation, key_width, group):
    nblk = seq_len // A_QB
    lane = _lane_iota()
    lo = lane < A_HEAD_DIM
    first_class = pl.program_id(1) * group
    pairs = A_HEADS // 2

    def block(it, carry):
        c = it // nblk
        if nblk == 1:
            q0, ks, var = 0, 0, 0
        else:
            n = it % nblk
            q0 = pl.multiple_of(n * A_QB, A_QB)
            ks = pl.multiple_of(jnp.clip(q0 - 64, 0, seq_len - key_width), 64)
            var = jnp.where(n == 0, 0, jnp.where(n == nblk - 1, 2, 1))
        rows = pl.ds(q0, A_QB)
        keys = pl.ds(ks, key_width)
        if dilation == 1:
            out_rows = rows
        else:
            out_rows = pl.ds(first_class + c + dilation * q0, A_QB, stride=dilation)
        q_tiles = [qkv_ref[0, c, rows, LANES * p:LANES * (p + 1)] for p in range(pairs)]
        k_tiles = [qkv_ref[0, c, keys, A_WIDTH + LANES * p:A_WIDTH + LANES * (p + 1)] for p in range(pairs)]
        v_tiles = [qkv_ref[0, c, keys, 2 * A_WIDTH + LANES * p:2 * A_WIDTH + LANES * (p + 1)] for p in range(pairs)]
        zero = jnp.zeros((), BF16)
        k_heads = [jnp.where(lo, k_tiles[hd // 2], zero) if hd % 2 == 0 else jnp.where(lo, zero, k_tiles[hd // 2])
                   for hd in range(A_HEADS)]
        scores = [lax.dot_general(q_tiles[hd // 2], k_heads[hd], _NT, preferred_element_type=F32) + bias_ref[var, hd]
                  for hd in range(A_HEADS)]
        maxes = [jnp.max(s, axis=-1, keepdims=True) for s in scores]
        probs = [jnp.exp2(s - m) for s, m in zip(scores, maxes)]
        dens = [jnp.sum(pr, axis=-1, keepdims=True) for pr in probs]
        pvs = [jnp.dot(pr.astype(BF16), v_tiles[hd // 2], preferred_element_type=F32) for hd, pr in enumerate(probs)]
        outs = []
        for p in range(pairs):
            h0, h1 = 2 * p, 2 * p + 1
            outs.append(jnp.where(lo, pvs[h0] * (1.0 / dens[h0]), pvs[h1] * (1.0 / dens[h1])))
        lse = jnp.zeros((A_QB, LANES), F32)
        for hd in range(A_HEADS):
            lse = jnp.where(lane == hd, maxes[hd] + jnp.log2(dens[hd]), lse)
        lse_ref[0, out_rows, :] = lse
        for j in range(pairs // 2):
            o_ref[0, j, out_rows, :] = _pack_bf16_pair(outs[2 * j], outs[2 * j + 1])
        return carry

    lax.fori_loop(0, group * nblk, block, 0, unroll=4)


def _dilated_call(qkv, bias, batch, seq, dilation):
    r = dilation
    sl = seq // r
    kw = min(2 * A_QB, sl)
    group = r
    pairs = A_HEADS // 2
    return pl.pallas_call(
        functools.partial(_dilated_kernel, seq_len=sl, dilation=r, key_width=kw, group=group),
        grid=(batch, r // group),
        in_specs=[pl.BlockSpec((1, group, sl, A_QKV_WIDTH), lambda b, c: (b, c, 0, 0)),
                  pl.BlockSpec(bias.shape, lambda b, c: (0, 0, 0, 0))],
        out_specs=[pl.BlockSpec((1, pairs // 2, seq, LANES), lambda b, c: (b, 0, 0, 0)),
                   pl.BlockSpec((1, seq, LANES), lambda b, c: (b, 0, 0))],
        out_shape=[jax.ShapeDtypeStruct((batch, pairs // 2, seq, LANES), U32),
                   jax.ShapeDtypeStruct((batch, seq, LANES), F32)],
        compiler_params=_cparams(("parallel", "arbitrary")),
        name=f"dilated_r{r}",
    )(qkv, bias)


def _t5_bucket(rel):
    half = REL_BUCKETS // 2
    max_exact = half // 2
    n = np.abs(rel)
    large = max_exact + (np.log(np.maximum(n, 1) / max_exact)
                         / math.log(REL_MAX_DISTANCE / max_exact) * (half - max_exact)).astype(np.int32)
    large = np.minimum(large, half - 1)
    return (np.where(rel > 0, half, 0) + np.where(n < max_exact, n, large)).astype(np.int32)


def _dilated_bias(rel_bias, seq, dilation, half_steps):
    sl = seq // dilation
    kw = min(2 * A_QB, sl)
    offsets = [0] if sl == kw else [0, -half_steps, A_QB - kw]
    rel = np.stack([np.arange(kw)[None, :] + off - np.arange(A_QB)[:, None] for off in offsets])
    valid = np.abs(rel) <= half_steps
    bucket = np.where(valid, _t5_bucket(rel * dilation), REL_BUCKETS).astype(np.int32)
    onehot = (jnp.asarray(bucket)[..., None] == jnp.arange(REL_BUCKETS + 1, dtype=I32)).astype(F32)
    table = jnp.concatenate([rel_bias.astype(F32), jnp.full((1, A_HEADS), NEG_INF, F32)], axis=0)
    return jnp.einsum("vqkb,bh->vhqk", onehot, table * LOG2E, precision=lax.Precision.HIGHEST)


def _mla_kernel(q_ref, k_ref, v_ref, o_ref, v1_ref):
    lo = _lane_iota() < B_V_DIM
    sub = B_QB // B_SUB
    tiles = [slice(0, LANES), slice(LANES, 2 * LANES)]
    v1_ref[:, :LANES] = v_ref[0]
    v1_ref[:, LANES:] = jnp.ones((v_ref.shape[1], LANES), BF16)

    units = [(j, half) for j in range(sub) for half in range(2)]

    def rows(i):
        return [pl.ds(i * B_QB + B_SUB * j, B_SUB) for j in range(sub)]

    def scores(i):
        return [lax.dot_general(q_ref[0, rows(i)[j], tiles[half]], k_ref[0, :, tiles[half]], _NT,
                                preferred_element_type=F32) for j, half in units]

    def finish(i, sc):
        maxes = [jnp.max(s, axis=-1, keepdims=True) for s in sc]
        probs = [jnp.exp2(s - m) for s, m in zip(sc, maxes)]
        pvs = [jnp.dot(pr.astype(BF16), v1_ref[...], preferred_element_type=F32) for pr in probs]
        outs = [pv[:, :LANES] * (1.0 / pv[:, LANES:]) for pv in pvs]
        for j in range(sub):
            o_ref[0, rows(i)[j], :] = jnp.where(lo, outs[2 * j], outs[2 * j + 1]).astype(BF16)

    n_blocks = q_ref.shape[1] // B_QB
    sc = scores(0)
    for i in range(n_blocks):
        sc_next = scores(i + 1) if i + 1 < n_blocks else None
        finish(i, sc)
        sc = sc_next


def _mla_call(qb, kb, vb, batch, seq):
    qb = qb.reshape(batch, seq, B_HEADS * LANES)
    kb = kb.reshape(batch, seq, B_HEADS * LANES)
    vb = vb.reshape(batch, seq, B_WIDTH)
    pair = lambda w: pl.BlockSpec((1, seq, w), lambda b, p: (b, 0, p))
    out = pl.pallas_call(
        _mla_kernel,
        grid=(batch, B_HEADS // 2),
        in_specs=[pair(2 * LANES), pair(2 * LANES), pair(LANES)],
        out_specs=pair(LANES),
        out_shape=jax.ShapeDtypeStruct((batch, seq, B_WIDTH), BF16),
        scratch_shapes=[pltpu.VMEM((seq, 2 * LANES), BF16)],
        compiler_params=_cparams(("parallel", "parallel")),
        name="mla",
    )(qb, kb, vb)
    return out.reshape(batch * seq, B_WIDTH)


def _merge_patterns(o_refs, lse_refs, spread_ref, rows):
    lses = [r[0, rows, :] for r in lse_refs]
    top = functools.reduce(jnp.maximum, lses)
    es = [jnp.exp2(l - top) for l in lses]
    inv = 1.0 / functools.reduce(jnp.add, es)
    spread = spread_ref[...]
    weights = []
    for e in es:
        w = e * inv
        w_hi = w.astype(BF16)
        w_lo = (w - w_hi.astype(F32)).astype(BF16)
        weights.append(jnp.dot(jnp.concatenate([w_hi, w_lo], axis=1), spread, preferred_element_type=F32))
    tiles = []
    for j in range(A_HEADS // 4):
        outs = [_unpack_bf16_pair(r[0, j, rows, :]) for r in o_refs]
        for half in range(2):
            tile = slice(LANES * (2 * j + half), LANES * (2 * j + half + 1))
            tiles.append(functools.reduce(jnp.add, [w[:, tile] * o[half] for w, o in zip(weights, outs)]))
    return jnp.concatenate(tiles, axis=1)


def _mix_kernel(o1_ref, o4_ref, o16_ref, l1_ref, l4_ref, l16_ref, ob_ref, x_ref, ga_ref, gb_ref, wo_ref,
                gf_ref, wr_ref, br_ref, tri_ref, spread_ref, x1_ref, hp_ref, idx_ref, gate_ref, cnt_ref, carry_ref):
    i = pl.program_id(0)
    unit = x_ref.shape[0] // MIX_UNITS
    units = [slice(unit * u, unit * (u + 1)) for u in range(MIX_UNITS)]

    @pl.when(i == 0)
    def _():
        carry_ref[...] = jnp.zeros_like(carry_ref)

    def residual_stream(u):
        rows = units[u]
        oa = _merge_patterns((o1_ref, o4_ref, o16_ref), (l1_ref, l4_ref, l16_ref), spread_ref, rows)
        a = _rms(oa, ga_ref[...]).astype(BF16)
        b = _rms(ob_ref[rows, :].astype(F32), gb_ref[...]).astype(BF16)
        mix = (jnp.dot(a, wo_ref[0:A_WIDTH, :], preferred_element_type=F32)
               + jnp.dot(b, wo_ref[A_WIDTH:, :], preferred_element_type=F32))
        x1 = x_ref[rows, :] + mix
        x1_ref[rows, :] = x1
        h2 = _rms(x1, gf_ref[...])
        half = D_MODEL // 2
        for c in range(PACKED_SUBLANES):
            hp_ref[pl.ds(PACKED_SUBLANES * unit * u + c, unit, stride=PACKED_SUBLANES), :] = _pack_bf16_pair(
                h2[:, LANES * c:LANES * (c + 1)], h2[:, half + LANES * c:half + LANES * (c + 1)])
        return h2

    wr = wr_ref[...]
    wr_hi = wr.astype(BF16)
    wr_lo = (wr - wr_hi.astype(F32)).astype(BF16)
    wr_both = jnp.concatenate([wr_hi, wr_lo], axis=0)

    def logits(h2):
        h_hi = h2.astype(BF16)
        h_lo = (h2 - h_hi.astype(F32)).astype(BF16)
        lg_hi = lax.dot_general(wr_both, h_hi, _NT, preferred_element_type=F32)
        return (lg_hi[:ROUTER_ROWS] + lg_hi[ROUTER_ROWS:]
                + lax.dot_general(wr_hi, h_lo, _NT, preferred_element_type=F32) + br_ref[...])

    def route(lg):
        row = lax.broadcasted_iota(I32, lg.shape, 0)
        is_g = (row >= N_EXPERTS) & (row < N_EXPERTS + N_GROUPS)
        gl = jnp.where(is_g, lg, NEG_INF)
        ge = jnp.exp(gl - jnp.max(gl, axis=0, keepdims=True))
        gp = ge / jnp.sum(ge, axis=0, keepdims=True)
        g_gate = jnp.max(gp, axis=0, keepdims=True)
        g_idx = jnp.min(jnp.where(is_g & (gp == g_gate), row - N_EXPERTS, LANES), axis=0, keepdims=True)
        sel = (row // EXPERTS_PER_GROUP) == g_idx
        el = jnp.where(sel, lg, NEG_INF)
        ee = jnp.exp(el - jnp.max(el, axis=0, keepdims=True))
        ep = jnp.where(sel, ee / jnp.sum(ee, axis=0, keepdims=True), -1.0)
        p1 = jnp.max(ep, axis=0, keepdims=True)
        i1 = jnp.min(jnp.where(ep == p1, row, LANES), axis=0, keepdims=True)
        ep2 = jnp.where(row == i1, -1.0, ep)
        p2 = jnp.max(ep2, axis=0, keepdims=True)
        i2 = jnp.min(jnp.where(sel & (ep2 == p2) & (row != i1), row, LANES), axis=0, keepdims=True)
        den = p1 + p2
        return i1, i2, g_gate * p1 / den, g_gate * p2 / den, row == i1, row == i2

    h2s = [residual_stream(u) for u in range(MIX_UNITS)]
    lgs = [logits(h2) for h2 in h2s]
    routes = [route(lg) for lg in lgs]

    carry = carry_ref[...]
    row8 = lax.broadcasted_iota(I32, (idx_ref.shape[0], unit), 0)
    for u, (i1, i2, g1, g2, hit1, hit2) in enumerate(routes):
        onehot = jnp.where(hit1 | hit2, 1.0, 0.0)
        before = jnp.dot(onehot.astype(BF16), tri_ref[...], preferred_element_type=F32) + carry
        r1 = jnp.sum(jnp.where(hit1, before, 0.0), axis=0, keepdims=True).astype(I32)
        r2 = jnp.sum(jnp.where(hit2, before, 0.0), axis=0, keepdims=True).astype(I32)
        carry = carry + jnp.sum(onehot, axis=1, keepdims=True)
        idx_ref[:, units[u]] = jnp.where(row8 == 0, i1, jnp.where(row8 == 1, i2,
                                         jnp.where(row8 == 2, r1, jnp.where(row8 == 3, r2, 0))))
        gate_ref[:, units[u]] = jnp.where(row8 == 0, g1, jnp.where(row8 == 1, g2, 0.0))
    carry_ref[...] = carry

    @pl.when(i == pl.num_programs(0) - 1)
    def _():
        cnt_ref[...] = jnp.broadcast_to(carry_ref[...], cnt_ref.shape).astype(I32)


def _mix_call(oas, lses, ob, x2, g_a, g_b, w_out, g_ffn, w_router, b_router, tri, spread):
    t = x2.shape[0]
    tm = ROW_TILE
    nseq = oas[0].shape[2] // tm
    row = lambda i: (i, 0)
    const = lambda i: (0, 0)
    slab = lambda n: pl.BlockSpec((1, n, tm, LANES), lambda i: (i // nseq, 0, i % nseq, 0))
    o_slab = slab(A_HEADS // 4)
    lse_slab = pl.BlockSpec((1, tm, LANES), lambda i: (i // nseq, i % nseq, 0))
    return pl.pallas_call(
        _mix_kernel,
        grid=(t // tm,),
        in_specs=[
            o_slab, o_slab, o_slab, lse_slab, lse_slab, lse_slab,
            pl.BlockSpec((tm, B_WIDTH), row),
            pl.BlockSpec((tm, D_MODEL), row),
            pl.BlockSpec((1, A_WIDTH), const),
            pl.BlockSpec((1, B_WIDTH), const),
            pl.BlockSpec((D_MODEL, D_MODEL), const),
            pl.BlockSpec((1, D_MODEL), const),
            pl.BlockSpec((ROUTER_ROWS, D_MODEL), const),
            pl.BlockSpec((ROUTER_ROWS, 1), const),
            pl.BlockSpec((tm // MIX_UNITS, tm // MIX_UNITS), const),
            pl.BlockSpec((2 * LANES, A_WIDTH), const),
        ],
        out_specs=[
            pl.BlockSpec((tm, D_MODEL), row),
            pl.BlockSpec((tm * PACKED_SUBLANES, LANES), row),
            pl.BlockSpec((8, tm), lambda i: (0, i)),
            pl.BlockSpec((8, tm), lambda i: (0, i)),
            pl.BlockSpec((ROUTER_ROWS, LANES), const),
        ],
        out_shape=[
            jax.ShapeDtypeStruct((t, D_MODEL), F32),
            jax.ShapeDtypeStruct((t * PACKED_SUBLANES, LANES), U32),
            jax.ShapeDtypeStruct((8, t), I32),
            jax.ShapeDtypeStruct((8, t), F32),
            jax.ShapeDtypeStruct((ROUTER_ROWS, LANES), I32),
        ],
        scratch_shapes=[pltpu.VMEM((ROUTER_ROWS, 1), F32)],
        compiler_params=_cparams(("arbitrary",)),
        name="mix_router",
    )(*oas, *lses, ob, x2, g_a, g_b, w_out, g_ffn, w_router, b_router, tri, spread)


def _dest_kernel(idx_ref, pstart_ref, dest_ref):
    idx = idx_ref[...]
    row = lax.broadcasted_iota(I32, (ROUTER_ROWS, idx.shape[1]), 0)
    ps = pstart_ref[...]

    def slot(k):
        return jnp.sum(jnp.where(row == idx[k:k + 1, :], ps, 0), axis=0, keepdims=True) + idx[2 + k:3 + k, :]

    row8 = lax.broadcasted_iota(I32, idx.shape, 0)
    dest_ref[...] = jnp.where(row8 == 0, slot(0), jnp.where(row8 == 1, slot(1), 0))


def _dest_call(idx, pstart):
    t = idx.shape[1]
    tm = 4 * ROW_TILE
    return pl.pallas_call(
        _dest_kernel,
        grid=(t // tm,),
        in_specs=[pl.BlockSpec((8, tm), lambda i: (0, i)), pl.BlockSpec((ROUTER_ROWS, 1), lambda i: (0, 0))],
        out_specs=pl.BlockSpec((8, tm), lambda i: (0, i)),
        out_shape=jax.ShapeDtypeStruct((8, t), I32),
        compiler_params=_cparams(("parallel",)),
        name="dest_rows",
    )(idx, pstart)


def _dispatch_kernel(valid_ref, d0_ref, d1_ref, h_ref, buf_ref, zero_ref, sem, pad_sem):
    i = pl.program_id(0)
    tt = h_ref.shape[0] // PACKED_SUBLANES
    n_blocks = valid_ref.shape[0]

    def for_padded_blocks(fn):
        def body(j, c):
            taken = valid_ref[j]
            even = (taken + 1) // 2 * 2

            @pl.when(even < MOE_BLK)
            def _():
                n_pad = pl.multiple_of((MOE_BLK - even) * PACKED_SUBLANES, ROW_SUBLANES)
                first = pl.multiple_of((j * MOE_BLK + even) * PACKED_SUBLANES, ROW_SUBLANES)
                fn(pltpu.make_async_copy(zero_ref.at[pl.ds(0, n_pad)], buf_ref.at[pl.ds(first, n_pad)], pad_sem))

            @pl.when(even != taken)
            def _():
                fn(pltpu.make_async_copy(zero_ref.at[pl.ds(0, PACKED_SUBLANES)],
                                         buf_ref.at[_row_tile(j * MOE_BLK + taken, PACKED_SUBLANES)], pad_sem))
            return c
        lax.fori_loop(0, n_blocks, body, 0)

    @pl.when(i == 0)
    def _():
        zero_ref[...] = jnp.zeros_like(zero_ref)
        for_padded_blocks(lambda cp: cp.start())

    def issue(g, c):
        base = pl.multiple_of(g * ISSUE_GROUP, ISSUE_GROUP)
        for j in range(ISSUE_GROUP):
            for prio, d_ref in enumerate((d0_ref, d1_ref)):
                pltpu.make_async_copy(h_ref.at[_row_tile(base + j, PACKED_SUBLANES)],
                                      buf_ref.at[_row_tile(d_ref[base + j], PACKED_SUBLANES)],
                                      sem).start(priority=prio)
        return c

    lax.fori_loop(0, tt // ISSUE_GROUP, issue, 0)
    for k in range(TOP_K):
        pltpu.make_async_copy(h_ref, buf_ref.at[pl.ds(0, tt * PACKED_SUBLANES)], sem).wait()

    @pl.when(i == pl.num_programs(0) - 1)
    def _():
        for_padded_blocks(lambda cp: cp.wait())


def _dispatch_call(block_valid, dests, hp):
    t = hp.shape[0] // PACKED_SUBLANES
    tt = 2 * ROW_TILE
    n_rows = block_valid.shape[0] * MOE_BLK
    return pl.pallas_call(
        _dispatch_kernel,
        grid_spec=pltpu.PrefetchScalarGridSpec(
            num_scalar_prefetch=1,
            grid=(t // tt,),
            in_specs=[
                pl.BlockSpec((tt,), lambda i, va: (i,), memory_space=pltpu.SMEM),
                pl.BlockSpec((tt,), lambda i, va: (i,), memory_space=pltpu.SMEM),
                pl.BlockSpec((tt * PACKED_SUBLANES, LANES), lambda i, va: (i, 0)),
            ],
            out_specs=pl.BlockSpec(memory_space=pl.ANY),
            scratch_shapes=[pltpu.VMEM((MOE_BLK * PACKED_SUBLANES, LANES), U32),
                            pltpu.SemaphoreType.DMA(()), pltpu.SemaphoreType.DMA(())],
        ),
        out_shape=jax.ShapeDtypeStruct((n_rows * PACKED_SUBLANES, LANES), U32),
        compiler_params=_cparams(("arbitrary",), disable_bounds_checks=True, has_side_effects=True),
        name="dispatch",
    )(block_valid, *dests, hp)


def _expert_kernel(be_ref, new_ref, valid_ref, buf_ref, wg_ref, wu_ref, wd_ref, out_ref, wg_s, wu_s, wd_s):
    j = pl.program_id(0)
    del be_ref

    @pl.when(new_ref[j] == 1)
    def _():
        wg_s[...] = wg_ref[0].astype(BF16)
        wu_s[...] = wu_ref[0].astype(BF16)
        wd_s[...] = wd_ref[0].astype(BF16)

    n_valid = valid_ref[j]

    @pl.when(n_valid > 0)
    def _():
        blk = buf_ref.shape[0] // PACKED_SUBLANES
        words = [_unpack_bf16_pair(buf_ref[pl.ds(c, blk, stride=PACKED_SUBLANES), :]) for c in range(PACKED_SUBLANES)]
        x = jnp.concatenate([w[0] for w in words] + [w[1] for w in words], axis=1).astype(BF16)
        cols = [pl.ds(c, blk, stride=ROW_SUBLANES) for c in range(ROW_SUBLANES)]
        g = jnp.dot(x, wg_s[...], preferred_element_type=F32)
        u = jnp.dot(x, wu_s[...], preferred_element_type=F32)
        hb = (g * jax.nn.sigmoid(g)) * u
        out = jnp.dot(hb.astype(BF16), wd_s[...], preferred_element_type=F32)
        for c, rows in enumerate(cols):
            out_ref[rows, :] = out[:, LANES * c:LANES * (c + 1)]

    @pl.when(n_valid == 0)
    def _():
        out_ref[...] = jnp.zeros_like(out_ref)


def _expert_call(block_expert, block_new, block_valid, buf, w_gate, w_up, w_down):
    nb = buf.shape[0] // (MOE_BLK * PACKED_SUBLANES)
    wsel = lambda j, be, nw, va: (be[j], 0, 0)
    rows = lambda sublanes: pl.BlockSpec((MOE_BLK * sublanes, LANES), lambda j, be, nw, va: (j, 0))
    return pl.pallas_call(
        _expert_kernel,
        grid_spec=pltpu.PrefetchScalarGridSpec(
            num_scalar_prefetch=3,
            grid=(nb,),
            in_specs=[
                rows(PACKED_SUBLANES),
                pl.BlockSpec((1, D_MODEL, EXPERT_FF), wsel),
                pl.BlockSpec((1, D_MODEL, EXPERT_FF), wsel),
                pl.BlockSpec((1, EXPERT_FF, D_MODEL), wsel),
            ],
            out_specs=rows(ROW_SUBLANES),
            scratch_shapes=[pltpu.VMEM((D_MODEL, EXPERT_FF), BF16),
                            pltpu.VMEM((D_MODEL, EXPERT_FF), BF16),
                            pltpu.VMEM((EXPERT_FF, D_MODEL), BF16)],
        ),
        out_shape=jax.ShapeDtypeStruct((nb * MOE_BLK * ROW_SUBLANES, LANES), F32),
        compiler_params=_cparams(("arbitrary",)),
        name="experts",
    )(block_expert, block_new, block_valid, buf, w_gate, w_up, w_down)


def _combine_kernel(d0_ref, d1_ref, d0_next_ref, d1_next_ref, d0_ahead_ref, d1_ahead_ref, x1_ref, gate_ref,
                    gf_ref, eo_ref, o_ref, rows_a, rows_b, rows_c, sems):
    i = pl.program_id(0)
    tt = x1_ref.shape[0]
    bufs = (rows_a, rows_b, rows_c)
    n_buf = len(bufs)

    def start_row(d_refs, slot, r):
        for k, d_ref in enumerate(d_refs):
            pltpu.make_async_copy(eo_ref.at[_row_tile(d_ref[r])], bufs[slot].at[k, _row_tile(r)],
                                  sems.at[slot]).start(priority=k)

    def drain(slot):
        for k in range(TOP_K):
            pltpu.make_async_copy(eo_ref.at[pl.ds(0, tt * ROW_SUBLANES)], bufs[slot].at[k], sems.at[slot]).wait()

    @pl.when(i == 0)
    def _():
        def issue(g, c):
            base = pl.multiple_of(g * ISSUE_GROUP, ISSUE_GROUP)
            for j in range(ISSUE_GROUP):
                start_row((d0_ref, d1_ref), 0, base + j)
                start_row((d0_next_ref, d1_next_ref), 1, base + j)
            return c
        lax.fori_loop(0, tt // ISSUE_GROUP, issue, 0)

    def step(slot):
        ahead = (slot + 2) % n_buf
        drain(slot)
        gf = gf_ref[...]
        for ch in range(tt // COMBINE_CHUNK):
            r0 = ch * COMBINE_CHUNK
            for j in range(COMBINE_CHUNK):
                start_row((d0_ahead_ref, d1_ahead_ref), ahead, r0 + j)
            gate = gate_ref[r0:r0 + COMBINE_CHUNK, :]

            def rows(k):
                return jnp.concatenate(
                    [bufs[slot][k, pl.ds(ROW_SUBLANES * r0 + c, COMBINE_CHUNK, stride=ROW_SUBLANES), :]
                     for c in range(ROW_SUBLANES)], axis=1)

            y = rows(0) * gate[:, 0:1] + rows(1) * gate[:, 1:2]
            o_ref[r0:r0 + COMBINE_CHUNK, :] = _rms(x1_ref[r0:r0 + COMBINE_CHUNK, :] + y, gf)

        @pl.when(i == pl.num_programs(0) - 1)
        def _():
            drain((slot + 1) % n_buf)
            drain(ahead)

    for phase in range(n_buf):
        pl.when(i % n_buf == phase)(functools.partial(step, phase))


def _combine_call(dests, x1, gates, g_final, expert_out):
    t = x1.shape[0]
    tt = ROW_TILE // 2
    last = t // tt - 1
    tile = lambda ahead: pl.BlockSpec((tt,), lambda i: (jnp.minimum(i + ahead, last),), memory_space=pltpu.SMEM)
    buf = pltpu.VMEM((TOP_K, tt * ROW_SUBLANES, LANES), F32)
    return pl.pallas_call(
        _combine_kernel,
        grid=(t // tt,),
        in_specs=[
            tile(0), tile(0), tile(1), tile(1), tile(2), tile(2),
            pl.BlockSpec((tt, D_MODEL), lambda i: (i, 0)),
            pl.BlockSpec((tt, TOP_K), lambda i: (i, 0)),
            pl.BlockSpec((1, D_MODEL), lambda i: (0, 0)),
            pl.BlockSpec(memory_space=pl.ANY),
        ],
        out_specs=pl.BlockSpec((tt, D_MODEL), lambda i: (i, 0)),
        out_shape=jax.ShapeDtypeStruct((t, D_MODEL), F32),
        scratch_shapes=[buf, buf, buf, pltpu.SemaphoreType.DMA((3,))],
        compiler_params=_cparams(("arbitrary",), disable_bounds_checks=True),
        name="combine",
    )(*dests, *dests, *dests, x1, gates, g_final, expert_out)


def _rope_tables(seq):
    half = B_QK_ROPE // 2
    inv_freq = ROPE_THETA ** (-(jnp.arange(half, dtype=F32) / half))
    ang = jnp.arange(seq, dtype=F32)[:, None] * inv_freq[None, :]
    cos, sin = jnp.cos(ang), jnp.sin(ang)
    z = jnp.zeros((seq, B_QK_NOPE), F32)
    z2 = jnp.zeros((seq, B_QK_ROPE), F32)
    return (jnp.concatenate([z, cos, cos, z2], axis=1), jnp.concatenate([z, -sin, sin, z2], axis=1))


def _swap_halves(w):
    half = w.shape[-1] // 2
    return jnp.concatenate([w[..., half:], w[..., :half]], axis=-1)


def _layout_weights(w_in, w_q_up, w_kv_up):
    d = w_in.shape[0]
    w_kr = w_in[:, 3 * A_WIDTH + B_Q_LORA + B_KV_LORA:]
    w_in_l = jnp.concatenate(
        [w_in[:, :A_WIDTH] * (A_HEAD_DIM ** -0.5), w_in[:, A_WIDTH:3 * A_WIDTH + B_Q_LORA + B_KV_LORA],
         jnp.zeros((d, B_QK_NOPE), F32), w_kr, _swap_halves(w_kr)], axis=1).astype(BF16)
    wq = w_q_up.reshape(B_Q_LORA, B_HEADS, B_QK_NOPE + B_QK_ROPE)
    wq_l = jnp.concatenate([wq, _swap_halves(wq[..., B_QK_NOPE:])], axis=-1)
    wq_l = wq_l.reshape(B_Q_LORA, B_HEADS * LANES).astype(BF16)
    wkv = w_kv_up.reshape(B_KV_LORA, B_HEADS, B_QK_NOPE + B_V_DIM)
    wkb = jnp.concatenate([wkv[..., :B_QK_NOPE], jnp.zeros_like(wkv[..., :B_QK_NOPE])], axis=-1)
    wkb = wkb.reshape(B_KV_LORA, B_HEADS * LANES).astype(BF16)
    wvb = wkv[..., B_QK_NOPE:].reshape(B_KV_LORA, B_WIDTH).astype(BF16)
    return w_in_l, wq_l, wkb, wvb


def _block_plan(counts, n_blocks):
    padded = (counts + MOE_BLK - 1) // MOE_BLK * MOE_BLK
    ends = jnp.cumsum(padded)
    starts = ends - padded
    first_row = jnp.arange(n_blocks, dtype=I32) * MOE_BLK
    expert = jnp.minimum(jnp.sum(ends[None, :] <= first_row[:, None], axis=1), N_EXPERTS - 1).astype(I32)
    new = jnp.concatenate([jnp.ones((1,), I32), (expert[1:] != expert[:-1]).astype(I32)])
    valid = jnp.clip((starts + counts)[expert] - first_row, 0, MOE_BLK).astype(I32)
    return starts.astype(I32), expert, new, valid


def kernel(x, g_attn_norm, w_in, rel_bias, g_q_latent, w_q_up, g_kv_latent, w_kv_up, g_out_a, g_out_b, w_out,
           g_ffn_norm, w_router_group, b_router_group, w_router_expert, b_router_expert, w_gate, w_up, w_down,
           g_final):
    batch, seq, d = x.shape
    t = batch * seq
    assert g_attn_norm.shape[0] == 1 and d == D_MODEL and seq % ROW_TILE == 0
    cos_t, sin_t = _rope_tables(seq)
    unit = ROW_TILE // MIX_UNITS
    tri = jnp.triu(jnp.ones((unit, unit), F32), 1).astype(BF16)
    spread = (jnp.arange(2 * LANES)[:, None] % LANES == jnp.arange(A_WIDTH)[None, :] // A_HEAD_DIM).astype(BF16)
    n_blocks = t * TOP_K // MOE_BLK + N_EXPERTS
    x2 = x.reshape(t, d)
    row = lambda v: v.reshape(1, -1)

    w_in_l, wq_l, wkb_l, wvb_l = _layout_weights(w_in[0], w_q_up[0], w_kv_up[0])
    *qkv_a, qb, kb, vb = _proj_call(x2, row(g_attn_norm[0]), w_in_l, row(g_q_latent[0]), wq_l,
                                    row(g_kv_latent[0]), wkb_l, wvb_l, cos_t, sin_t, seq)
    oas, lses = [], []
    for pi, (window, dilation) in enumerate(DILATED_PATTERNS):
        bias = _dilated_bias(rel_bias, seq, dilation, window // (2 * dilation))
        o_p, lse_p = _dilated_call(qkv_a[pi], bias, batch, seq, dilation)
        oas.append(o_p)
        lses.append(lse_p)
    ob = _mla_call(qb, kb, vb, batch, seq)

    pad = ROUTER_ROWS - N_EXPERTS - N_GROUPS
    w_router = jnp.concatenate([w_router_expert[0], w_router_group[0], jnp.zeros((d, pad), F32)], axis=1).T
    b_router = jnp.concatenate([b_router_expert[0], b_router_group[0], jnp.zeros((pad,), F32)])
    x1, hp, idx, gates, cnt = _mix_call(oas, lses, ob, x2, row(g_out_a[0]), row(g_out_b[0]), w_out[0].astype(BF16),
                                        row(g_ffn_norm[0]), w_router, b_router.reshape(-1, 1), tri, spread)
    pstart, block_expert, block_new, block_valid = _block_plan(cnt[:N_EXPERTS, 0], n_blocks)
    pstart_col = jnp.concatenate([pstart, jnp.zeros((ROUTER_ROWS - N_EXPERTS,), I32)]).reshape(-1, 1)
    dest = _dest_call(idx, pstart_col)
    dests = (dest[0], dest[1])
    buf = _dispatch_call(block_valid, dests, hp)
    expert_out = _expert_call(block_expert, block_new, block_valid, buf, w_gate[0], w_up[0], w_down[0])
    return _combine_call(dests, x1, gates[:TOP_K].T, row(g_final), expert_out).reshape(batch, seq, d)
```

```python
import functools
import math

import numpy as np
import jax
import jax.numpy as jnp
from jax import lax
from jax.experimental import pallas as pl
from jax.experimental.pallas import tpu as pltpu

F32 = jnp.float32
BF16 = jnp.bfloat16
I32 = jnp.int32
U32 = jnp.uint32

D_MODEL = 1024
EPS = 1e-6
NEG_INF = -1e30
LANES = 128
ROW_SUBLANES = D_MODEL // LANES
PACKED_SUBLANES = ROW_SUBLANES // 2

A_HEADS = 8
A_HEAD_DIM = 64
A_WIDTH = 512
A_QKV_WIDTH = 3 * A_WIDTH
DILATED_PATTERNS = ((128, 1), (512, 4), (2048, 16))
REL_BUCKETS = 32
REL_MAX_DISTANCE = 1024
A_QB = 128

B_HEADS = 8
B_Q_LORA = 256
B_KV_LORA = 128
B_QK_NOPE = 64
B_QK_ROPE = 32
B_V_DIM = 64
B_WIDTH = 512
ROPE_THETA = 10000.0
B_SCALE = (B_QK_NOPE + B_QK_ROPE) ** -0.5
B_QB = 512
B_SUB = 256
LOG2E = math.log2(math.e)

N_GROUPS = 4
EXPERTS_PER_GROUP = 8
N_EXPERTS = 32
TOP_K = 2
EXPERT_FF = 256
MOE_BLK = 512
ROUTER_ROWS = 40
MIX_UNITS = 2

ROW_TILE = 512
ISSUE_GROUP = 8
COMBINE_CHUNK = 32
PROJ_COLS = 2048

_NT = (((1,), (1,)), ((), ()))


def _cparams(semantics, vmem_mb=48, **kw):
    return pltpu.CompilerParams(dimension_semantics=semantics,
                                vmem_limit_bytes=vmem_mb * 1024 * 1024, **kw)


def _rms(x, g):
    return x * lax.rsqrt(jnp.mean(x * x, axis=-1, keepdims=True) + EPS) * g


def _lane_iota(rows=1):
    return lax.broadcasted_iota(I32, (rows, LANES), 1)


def _row_tile(r, sublanes=ROW_SUBLANES):
    return pl.ds(pl.multiple_of(r * sublanes, sublanes), sublanes)


def _proj_kernel(x_ref, g_ref, win_ref, gq_ref, wq_ref, gkv_ref, wkb_ref, wvb_ref, cos_ref, sin_ref, *refs):
    out1, out4, out16 = refs[:3]
    qb_ref, kb_ref, vb_ref, slab_ref, slab4_ref = refs[3:]
    tm = x_ref.shape[0]
    h = _rms(x_ref[...], g_ref[...]).astype(BF16)
    lo = _lane_iota() < A_HEAD_DIM
    n_slabs = A_QKV_WIDTH // LANES
    parts = []
    for s in range(0, n_slabs, 2):
        part = jnp.dot(h, win_ref[:, LANES * s:LANES * (s + 2)], preferred_element_type=F32)
        if LANES * s < A_WIDTH:
            part = part * LOG2E
        slab_ref[s] = part[:, :LANES]
        slab_ref[s + 1] = part[:, LANES:]
        parts.append(part)
    out1[0, 0] = jnp.concatenate(parts, axis=1).astype(BF16)
    lat = jnp.dot(h, win_ref[:, A_QKV_WIDTH:], preferred_element_type=F32)
    cos = cos_ref[...]
    sin = sin_ref[...]
    n4 = tm // 4

    cq = _rms(lat[:, :B_Q_LORA], gq_ref[...]).astype(BF16)
    q = jnp.dot(cq, wq_ref[...], preferred_element_type=F32)
    q_mul = (cos + jnp.where(lo, 1.0, 0.0)) * (B_SCALE * LOG2E)
    q_rot = sin * (B_SCALE * LOG2E)
    qb_ref[...] = jnp.concatenate(
        [q[:, LANES * hd:LANES * (hd + 1)] * q_mul + pltpu.roll(q[:, LANES * hd:LANES * (hd + 1)], 96, 1) * q_rot
         for hd in range(B_HEADS)], axis=1).astype(BF16)

    ckv = _rms(lat[:, B_Q_LORA:B_Q_LORA + B_KV_LORA], gkv_ref[...]).astype(BF16)
    kr = lat[:, B_Q_LORA + B_KV_LORA:]
    kr = kr * cos + pltpu.roll(kr, 96, 1) * sin
    kn = jnp.dot(ckv, wkb_ref[...], preferred_element_type=F32)
    kb_ref[...] = jnp.concatenate([kn[:, LANES * hd:LANES * (hd + 1)] + kr for hd in range(B_HEADS)],
                                  axis=1).astype(BF16)
    vb_ref[...] = jnp.dot(ckv, wvb_ref[...], preferred_element_type=F32).astype(BF16)

    for c4 in range(4):
        pieces = [slab_ref[s, pl.ds(c4, n4, stride=4), :] for s in range(n_slabs)]
        for s, piece in enumerate(pieces):
            slab4_ref[s, c4] = piece
        out4[0, c4] = jnp.concatenate(pieces, axis=1).astype(BF16)
    for c4 in range(4):
        for j in range(4):
            out16[0, c4 + 4 * j] = jnp.concatenate(
                [slab4_ref[s, c4, pl.ds(j, n4 // 4, stride=4), :] for s in range(n_slabs)], axis=1).astype(BF16)


def _proj_call(x2, g_attn, w_in, g_q, w_q, g_kv, w_kb, w_vb, cos_t, sin_t, seq):
    t = x2.shape[0]
    tm = ROW_TILE
    nseq = seq // tm
    row = lambda i: (i, 0)
    const = lambda i: (0, 0)
    pos = lambda i: (i % nseq, 0)
    out = lambda w: jax.ShapeDtypeStruct((t, w), BF16)
    a_specs, a_shapes = [], []
    for _, r in DILATED_PATTERNS:
        a_specs.append(pl.BlockSpec((1, r, tm // r, A_QKV_WIDTH), lambda i: (i // nseq, 0, i % nseq, 0)))
        a_shapes.append(jax.ShapeDtypeStruct((t // seq, r, seq // r, A_QKV_WIDTH), BF16))
    return pl.pallas_call(
        _proj_kernel,
        grid=(t // tm,),
        in_specs=[
            pl.BlockSpec((tm, D_MODEL), row),
            pl.BlockSpec((1, D_MODEL), const),
            pl.BlockSpec((D_MODEL, PROJ_COLS), const),
            pl.BlockSpec((1, B_Q_LORA), const),
            pl.BlockSpec((B_Q_LORA, B_HEADS * LANES), const),
            pl.BlockSpec((1, B_KV_LORA), const),
            pl.BlockSpec((B_KV_LORA, B_HEADS * LANES), const),
            pl.BlockSpec((B_KV_LORA, B_WIDTH), const),
            pl.BlockSpec((tm, LANES), pos),
            pl.BlockSpec((tm, LANES), pos),
        ],
        out_specs=a_specs + [
            pl.BlockSpec((tm, B_HEADS * LANES), row),
            pl.BlockSpec((tm, B_HEADS * LANES), row),
            pl.BlockSpec((tm, B_WIDTH), row),
        ],
        out_shape=a_shapes + [out(B_HEADS * LANES), out(B_HEADS * LANES), out(B_WIDTH)],
        scratch_shapes=[pltpu.VMEM((A_QKV_WIDTH // LANES, tm, LANES), F32),
                        pltpu.VMEM((A_QKV_WIDTH // LANES, 4, tm // 4, LANES), F32)],
        compiler_params=_cparams(("parallel",)),
        name="proj",
    )(x2, g_attn, w_in, g_q, w_q, g_kv, w_kb, w_vb, cos_t, sin_t)


def _pack_bf16_pair(a, b):
    a_bits = lax.bitcast_convert_type(a.astype(BF16).astype(F32), U32) >> 16
    b_bits = lax.bitcast_convert_type(b.astype(BF16).astype(F32), U32) & jnp.uint32(0xFFFF0000)
    return a_bits | b_bits


def _unpack_bf16_pair(w):
    return (lax.bitcast_convert_type(w << 16, F32), lax.bitcast_convert_type(w & jnp.uint32(0xFFFF0000), F32))


def _dilated_kernel(qkv_ref, bias_ref, o_ref, lse_ref, *, seq_len, dilation, key_width, group):
    nblk = seq_len // A_QB
    lane = _lane_iota()
    lo = lane < A_HEAD_DIM
    first_class = pl.program_id(1) * group
    pairs = A_HEADS // 2

    def block(it, carry):
        c = it // nblk
        if nblk == 1:
            q0, ks, var = 0, 0, 0
        else:
            n = it % nblk
            q0 = pl.multiple_of(n * A_QB, A_QB)
            ks = pl.multiple_of(jnp.clip(q0 - 64, 0, seq_len - key_width), 64)
            var = jnp.where(n == 0, 0, jnp.where(n == nblk - 1, 2, 1))
        rows = pl.ds(q0, A_QB)
        keys = pl.ds(ks, key_width)
        if dilation == 1:
            out_rows = rows
        else:
            out_rows = pl.ds(first_class + c + dilation * q0, A_QB, stride=dilation)
        q_tiles = [qkv_ref[0, c, rows, LANES * p:LANES * (p + 1)] for p in range(pairs)]
        k_tiles = [qkv_ref[0, c, keys, A_WIDTH + LANES * p:A_WIDTH + LANES * (p + 1)] for p in range(pairs)]
        v_tiles = [qkv_ref[0, c, keys, 2 * A_WIDTH + LANES * p:2 * A_WIDTH + LANES * (p + 1)] for p in range(pairs)]
        zero = jnp.zeros((), BF16)
        k_heads = [jnp.where(lo, k_tiles[hd // 2], zero) if hd % 2 == 0 else jnp.where(lo, zero, k_tiles[hd // 2])
                   for hd in range(A_HEADS)]
        scores = [lax.dot_general(q_tiles[hd // 2], k_heads[hd], _NT, preferred_element_type=F32) + bias_ref[var, hd]
                  for hd in range(A_HEADS)]
        maxes = [jnp.max(s, axis=-1, keepdims=True) for s in scores]
        probs = [jnp.exp2(s - m) for s, m in zip(scores, maxes)]
        dens = [jnp.sum(pr, axis=-1, keepdims=True) for pr in probs]
        pvs = [jnp.dot(pr.astype(BF16), v_tiles[hd // 2], preferred_element_type=F32) for hd, pr in enumerate(probs)]
        outs = []
        for p in range(pairs):
            h0, h1 = 2 * p, 2 * p + 1
            outs.append(jnp.where(lo, pvs[h0] * (1.0 / dens[h0]), pvs[h1] * (1.0 / dens[h1])))
        lse = jnp.zeros((A_QB, LANES), F32)
        for hd in range(A_HEADS):
            lse = jnp.where(lane == hd, maxes[hd] + jnp.log2(dens[hd]), lse)
        lse_ref[0, out_rows, :] = lse
        for j in range(pairs // 2):
            o_ref[0, j, out_rows, :] = _pack_bf16_pair(outs[2 * j], outs[2 * j + 1])
        return carry

    lax.fori_loop(0, group * nblk, block, 0, unroll=4)


def _dilated_call(qkv, bias, batch, seq, dilation):
    r = dilation
    sl = seq // r
    kw = min(2 * A_QB, sl)
    group = r
    pairs = A_HEADS // 2
    return pl.pallas_call(
        functools.partial(_dilated_kernel, seq_len=sl, dilation=r, key_width=kw, group=group),
        grid=(batch, r // group),
        in_specs=[pl.BlockSpec((1, group, sl, A_QKV_WIDTH), lambda b, c: (b, c, 0, 0)),
                  pl.BlockSpec(bias.shape, lambda b, c: (0, 0, 0, 0))],
        out_specs=[pl.BlockSpec((1, pairs // 2, seq, LANES), lambda b, c: (b, 0, 0, 0)),
                   pl.BlockSpec((1, seq, LANES), lambda b, c: (b, 0, 0))],
        out_shape=[jax.ShapeDtypeStruct((batch, pairs // 2, seq, LANES), U32),
                   jax.ShapeDtypeStruct((batch, seq, LANES), F32)],
        compiler_params=_cparams(("parallel", "arbitrary")),
        name=f"dilated_r{r}",
    )(qkv, bias)


def _t5_bucket(rel):
    half = REL_BUCKETS // 2
    max_exact = half // 2
    n = np.abs(rel)
    large = max_exact + (np.log(np.maximum(n, 1) / max_exact)
                         / math.log(REL_MAX_DISTANCE / max_exact) * (half - max_exact)).astype(np.int32)
    large = np.minimum(large, half - 1)
    return (np.where(rel > 0, half, 0) + np.where(n < max_exact, n, large)).astype(np.int32)


def _dilated_bias(rel_bias, seq, dilation, half_steps):
    sl = seq // dilation
    kw = min(2 * A_QB, sl)
    offsets = [0] if sl == kw else [0, -half_steps, A_QB - kw]
    rel = np.stack([np.arange(kw)[None, :] + off - np.arange(A_QB)[:, None] for off in offsets])
    valid = np.abs(rel) <= half_steps
    bucket = np.where(valid, _t5_bucket(rel * dilation), REL_BUCKETS).astype(np.int32)
    onehot = (jnp.asarray(bucket)[..., None] == jnp.arange(REL_BUCKETS + 1, dtype=I32)).astype(F32)
    table = jnp.concatenate([rel_bias.astype(F32), jnp.full((1, A_HEADS), NEG_INF, F32)], axis=0)
    return jnp.einsum("vqkb,bh->vhqk", onehot, table * LOG2E, precision=lax.Precision.HIGHEST)


def _mla_kernel(q_ref, k_ref, v_ref, o_ref, v1_ref):
    lo = _lane_iota() < B_V_DIM
    sub = B_QB // B_SUB
    tiles = [slice(0, LANES), slice(LANES, 2 * LANES)]
    v1_ref[:, :LANES] = v_ref[0]
    v1_ref[:, LANES:] = jnp.ones((v_ref.shape[1], LANES), BF16)

    units = [(j, half) for j in range(sub) for half in range(2)]

    def rows(i):
        return [pl.ds(i * B_QB + B_SUB * j, B_SUB) for j in range(sub)]

    def scores(i):
        return [lax.dot_general(q_ref[0, rows(i)[j], tiles[half]], k_ref[0, :, tiles[half]], _NT,
                                preferred_element_type=F32) for j, half in units]

    def finish(i, sc):
        maxes = [jnp.max(s, axis=-1, keepdims=True) for s in sc]
        probs = [jnp.exp2(s - m) for s, m in zip(sc, maxes)]
        pvs = [jnp.dot(pr.astype(BF16), v1_ref[...], preferred_element_type=F32) for pr in probs]
        outs = [pv[:, :LANES] * (1.0 / pv[:, LANES:]) for pv in pvs]
        for j in range(sub):
            o_ref[0, rows(i)[j], :] = jnp.where(lo, outs[2 * j], outs[2 * j + 1]).astype(BF16)

    n_blocks = q_ref.shape[1] // B_QB
    sc = scores(0)
    for i in range(n_blocks):
        sc_next = scores(i + 1) if i + 1 < n_blocks else None
        finish(i, sc)
        sc = sc_next


def _mla_call(qb, kb, vb, batch, seq):
    qb = qb.reshape(batch, seq, B_HEADS * LANES)
    kb = kb.reshape(batch, seq, B_HEADS * LANES)
    vb = vb.reshape(batch, seq, B_WIDTH)
    pair = lambda w: pl.BlockSpec((1, seq, w), lambda b, p: (b, 0, p))
    out = pl.pallas_call(
        _mla_kernel,
        grid=(batch, B_HEADS // 2),
        in_specs=[pair(2 * LANES), pair(2 * LANES), pair(LANES)],
        out_specs=pair(LANES),
        out_shape=jax.ShapeDtypeStruct((batch, seq, B_WIDTH), BF16),
        scratch_shapes=[pltpu.VMEM((seq, 2 * LANES), BF16)],
        compiler_params=_cparams(("parallel", "parallel")),
        name="mla",
    )(qb, kb, vb)
    return out.reshape(batch * seq, B_WIDTH)


def _merge_patterns(o_refs, lse_refs, spread_ref, rows):
    lses = [r[0, rows, :] for r in lse_refs]
    top = functools.reduce(jnp.maximum, lses)
    es = [jnp.exp2(l - top) for l in lses]
    inv = 1.0 / functools.reduce(jnp.add, es)
    spread = spread_ref[...]
    weights = []
    for e in es:
        w = e * inv
        w_hi = w.astype(BF16)
        w_lo = (w - w_hi.astype(F32)).astype(BF16)
        weights.append(jnp.dot(jnp.concatenate([w_hi, w_lo], axis=1), spread, preferred_element_type=F32))
    tiles = []
    for j in range(A_HEADS // 4):
        outs = [_unpack_bf16_pair(r[0, j, rows, :]) for r in o_refs]
        for half in range(2):
            tile = slice(LANES * (2 * j + half), LANES * (2 * j + half + 1))
            tiles.append(functools.reduce(jnp.add, [w[:, tile] * o[half] for w, o in zip(weights, outs)]))
    return jnp.concatenate(tiles, axis=1)


def _mix_kernel(o1_ref, o4_ref, o16_ref, l1_ref, l4_ref, l16_ref, ob_ref, x_ref, ga_ref, gb_ref, wo_ref,
                gf_ref, wr_ref, br_ref, tri_ref, spread_ref, x1_ref, hp_ref, idx_ref, gate_ref, cnt_ref, carry_ref):
    i = pl.program_id(0)
    unit = x_ref.shape[0] // MIX_UNITS
    units = [slice(unit * u, unit * (u + 1)) for u in range(MIX_UNITS)]

    @pl.when(i == 0)
    def _():
        carry_ref[...] = jnp.zeros_like(carry_ref)

    def residual_stream(u):
        rows = units[u]
        oa = _merge_patterns((o1_ref, o4_ref, o16_ref), (l1_ref, l4_ref, l16_ref), spread_ref, rows)
        a = _rms(oa, ga_ref[...]).astype(BF16)
        b = _rms(ob_ref[rows, :].astype(F32), gb_ref[...]).astype(BF16)
        mix = (jnp.dot(a, wo_ref[0:A_WIDTH, :], preferred_element_type=F32)
               + jnp.dot(b, wo_ref[A_WIDTH:, :], preferred_element_type=F32))
        x1 = x_ref[rows, :] + mix
        x1_ref[rows, :] = x1
        h2 = _rms(x1, gf_ref[...])
        half = D_MODEL // 2
        for c in range(PACKED_SUBLANES):
            hp_ref[pl.ds(PACKED_SUBLANES * unit * u + c, unit, stride=PACKED_SUBLANES), :] = _pack_bf16_pair(
                h2[:, LANES * c:LANES * (c + 1)], h2[:, half + LANES * c:half + LANES * (c + 1)])
        return h2

    wr = wr_ref[...]
    wr_hi = wr.astype(BF16)
    wr_lo = (wr - wr_hi.astype(F32)).astype(BF16)
    wr_both = jnp.concatenate([wr_hi, wr_lo], axis=0)

    def logits(h2):
        h_hi = h2.astype(BF16)
        h_lo = (h2 - h_hi.astype(F32)).astype(BF16)
        lg_hi = lax.dot_general(wr_both, h_hi, _NT, preferred_element_type=F32)
        return (lg_hi[:ROUTER_ROWS] + lg_hi[ROUTER_ROWS:]
                + lax.dot_general(wr_hi, h_lo, _NT, preferred_element_type=F32) + br_ref[...])

    def route(lg):
        row = lax.broadcasted_iota(I32, lg.shape, 0)
        is_g = (row >= N_EXPERTS) & (row < N_EXPERTS + N_GROUPS)
        gl = jnp.where(is_g, lg, NEG_INF)
        ge = jnp.exp(gl - jnp.max(gl, axis=0, keepdims=True))
        gp = ge / jnp.sum(ge, axis=0, keepdims=True)
        g_gate = jnp.max(gp, axis=0, keepdims=True)
        g_idx = jnp.min(jnp.where(is_g & (gp == g_gate), row - N_EXPERTS, LANES), axis=0, keepdims=True)
        sel = (row // EXPERTS_PER_GROUP) == g_idx
        el = jnp.where(sel, lg, NEG_INF)
        ee = jnp.exp(el - jnp.max(el, axis=0, keepdims=True))
        ep = jnp.where(sel, ee / jnp.sum(ee, axis=0, keepdims=True), -1.0)
        p1 = jnp.max(ep, axis=0, keepdims=True)
        i1 = jnp.min(jnp.where(ep == p1, row, LANES), axis=0, keepdims=True)
        ep2 = jnp.where(row == i1, -1.0, ep)
        p2 = jnp.max(ep2, axis=0, keepdims=True)
        i2 = jnp.min(jnp.where(sel & (ep2 == p2) & (row != i1), row, LANES), axis=0, keepdims=True)
        den = p1 + p2
        return i1, i2, g_gate * p1 / den, g_gate * p2 / den, row == i1, row == i2

    h2s = [residual_stream(u) for u in range(MIX_UNITS)]
    lgs = [logits(h2) for h2 in h2s]
    routes = [route(lg) for lg in lgs]

    carry = carry_ref[...]
    row8 = lax.broadcasted_iota(I32, (idx_ref.shape[0], unit), 0)
    for u, (i1, i2, g1, g2, hit1, hit2) in enumerate(routes):
        onehot = jnp.where(hit1 | hit2, 1.0, 0.0)
        before = jnp.dot(onehot.astype(BF16), tri_ref[...], preferred_element_type=F32) + carry
        r1 = jnp.sum(jnp.where(hit1, before, 0.0), axis=0, keepdims=True).astype(I32)
        r2 = jnp.sum(jnp.where(hit2, before, 0.0), axis=0, keepdims=True).astype(I32)
        carry = carry + jnp.sum(onehot, axis=1, keepdims=True)
        idx_ref[:, units[u]] = jnp.where(row8 == 0, i1, jnp.where(row8 == 1, i2,
                                         jnp.where(row8 == 2, r1, jnp.where(row8 == 3, r2, 0))))
        gate_ref[:, units[u]] = jnp.where(row8 == 0, g1, jnp.where(row8 == 1, g2, 0.0))
    carry_ref[...] = carry

    @pl.when(i == pl.num_programs(0) - 1)
    def _():
        cnt_ref[...] = jnp.broadcast_to(carry_ref[...], cnt_ref.shape).astype(I32)


def _mix_call(oas, lses, ob, x2, g_a, g_b, w_out, g_ffn, w_router, b_router, tri, spread):
    t = x2.shape[0]
    tm = ROW_TILE
    nseq = oas[0].shape[2] // tm
    row = lambda i: (i, 0)
    const = lambda i: (0, 0)
    slab = lambda n: pl.BlockSpec((1, n, tm, LANES), lambda i: (i // nseq, 0, i % nseq, 0))
    o_slab = slab(A_HEADS // 4)
    lse_slab = pl.BlockSpec((1, tm, LANES), lambda i: (i // nseq, i % nseq, 0))
    return pl.pallas_call(
        _mix_kernel,
        grid=(t // tm,),
        in_specs=[
            o_slab, o_slab, o_slab, lse_slab, lse_slab, lse_slab,
            pl.BlockSpec((tm, B_WIDTH), row),
            pl.BlockSpec((tm, D_MODEL), row),
            pl.BlockSpec((1, A_WIDTH), const),
            pl.BlockSpec((1, B_WIDTH), const),
            pl.BlockSpec((D_MODEL, D_MODEL), const),
            pl.BlockSpec((1, D_MODEL), const),
            pl.BlockSpec((ROUTER_ROWS, D_MODEL), const),
            pl.BlockSpec((ROUTER_ROWS, 1), const),
            pl.BlockSpec((tm // MIX_UNITS, tm // MIX_UNITS), const),
            pl.BlockSpec((2 * LANES, A_WIDTH), const),
        ],
        out_specs=[
            pl.BlockSpec((tm, D_MODEL), row),
            pl.BlockSpec((tm * PACKED_SUBLANES, LANES), row),
            pl.BlockSpec((8, tm), lambda i: (0, i)),
            pl.BlockSpec((8, tm), lambda i: (0, i)),
            pl.BlockSpec((ROUTER_ROWS, LANES), const),
        ],
        out_shape=[
            jax.ShapeDtypeStruct((t, D_MODEL), F32),
            jax.ShapeDtypeStruct((t * PACKED_SUBLANES, LANES), U32),
            jax.ShapeDtypeStruct((8, t), I32),
            jax.ShapeDtypeStruct((8, t), F32),
            jax.ShapeDtypeStruct((ROUTER_ROWS, LANES), I32),
        ],
        scratch_shapes=[pltpu.VMEM((ROUTER_ROWS, 1), F32)],
        compiler_params=_cparams(("arbitrary",)),
        name="mix_router",
    )(*oas, *lses, ob, x2, g_a, g_b, w_out, g_ffn, w_router, b_router, tri, spread)


def _dest_kernel(idx_ref, pstart_ref, dest_ref):
    idx = idx_ref[...]
    row = lax.broadcasted_iota(I32, (ROUTER_ROWS, idx.shape[1]), 0)
    ps = pstart_ref[...]

    def slot(k):
        return jnp.sum(jnp.where(row == idx[k:k + 1, :], ps, 0), axis=0, keepdims=True) + idx[2 + k:3 + k, :]

    row8 = lax.broadcasted_iota(I32, idx.shape, 0)
    dest_ref[...] = jnp.where(row8 == 0, slot(0), jnp.where(row8 == 1, slot(1), 0))


def _dest_call(idx, pstart):
    t = idx.shape[1]
    tm = 4 * ROW_TILE
    return pl.pallas_call(
        _dest_kernel,
        grid=(t // tm,),
        in_specs=[pl.BlockSpec((8, tm), lambda i: (0, i)), pl.BlockSpec((ROUTER_ROWS, 1), lambda i: (0, 0))],
        out_specs=pl.BlockSpec((8, tm), lambda i: (0, i)),
        out_shape=jax.ShapeDtypeStruct((8, t), I32),
        compiler_params=_cparams(("parallel",)),
        name="dest_rows",
    )(idx, pstart)


def _dispatch_kernel(valid_ref, d0_ref, d1_ref, h_ref, buf_ref, zero_ref, sem, pad_sem):
    i = pl.program_id(0)
    tt = h_ref.shape[0] // PACKED_SUBLANES
    n_blocks = valid_ref.shape[0]

    def for_padded_blocks(fn):
        def body(j, c):
            taken = valid_ref[j]
            even = (taken + 1) // 2 * 2

            @pl.when(even < MOE_BLK)
            def _():
                n_pad = pl.multiple_of((MOE_BLK - even) * PACKED_SUBLANES, ROW_SUBLANES)
                first = pl.multiple_of((j * MOE_BLK + even) * PACKED_SUBLANES, ROW_SUBLANES)
                fn(pltpu.make_async_copy(zero_ref.at[pl.ds(0, n_pad)], buf_ref.at[pl.ds(first, n_pad)], pad_sem))

            @pl.when(even != taken)
            def _():
                fn(pltpu.make_async_copy(zero_ref.at[pl.ds(0, PACKED_SUBLANES)],
                                         buf_ref.at[_row_tile(j * MOE_BLK + taken, PACKED_SUBLANES)], pad_sem))
            return c
        lax.fori_loop(0, n_blocks, body, 0)

    @pl.when(i == 0)
    def _():
        zero_ref[...] = jnp.zeros_like(zero_ref)
        for_padded_blocks(lambda cp: cp.start())

    def issue(g, c):
        base = pl.multiple_of(g * ISSUE_GROUP, ISSUE_GROUP)
        for j in range(ISSUE_GROUP):
            for prio, d_ref in enumerate((d0_ref, d1_ref)):
                pltpu.make_async_copy(h_ref.at[_row_tile(base + j, PACKED_SUBLANES)],
                                      buf_ref.at[_row_tile(d_ref[base + j], PACKED_SUBLANES)],
                                      sem).start(priority=prio)
        return c

    lax.fori_loop(0, tt // ISSUE_GROUP, issue, 0)
    for k in range(TOP_K):
        pltpu.make_async_copy(h_ref, buf_ref.at[pl.ds(0, tt * PACKED_SUBLANES)], sem).wait()

    @pl.when(i == pl.num_programs(0) - 1)
    def _():
        for_padded_blocks(lambda cp: cp.wait())


def _dispatch_call(block_valid, dests, hp):
    t = hp.shape[0] // PACKED_SUBLANES
    tt = 2 * ROW_TILE
    n_rows = block_valid.shape[0] * MOE_BLK
    return pl.pallas_call(
        _dispatch_kernel,
        grid_spec=pltpu.PrefetchScalarGridSpec(
            num_scalar_prefetch=1,
            grid=(t // tt,),
            in_specs=[
                pl.BlockSpec((tt,), lambda i, va: (i,), memory_space=pltpu.SMEM),
                pl.BlockSpec((tt,), lambda i, va: (i,), memory_space=pltpu.SMEM),
                pl.BlockSpec((tt * PACKED_SUBLANES, LANES), lambda i, va: (i, 0)),
            ],
            out_specs=pl.BlockSpec(memory_space=pl.ANY),
            scratch_shapes=[pltpu.VMEM((MOE_BLK * PACKED_SUBLANES, LANES), U32),
                            pltpu.SemaphoreType.DMA(()), pltpu.SemaphoreType.DMA(())],
        ),
        out_shape=jax.ShapeDtypeStruct((n_rows * PACKED_SUBLANES, LANES), U32),
        compiler_params=_cparams(("arbitrary",), disable_bounds_checks=True, has_side_effects=True),
        name="dispatch",
    )(block_valid, *dests, hp)


def _expert_kernel(be_ref, new_ref, valid_ref, buf_ref, wg_ref, wu_ref, wd_ref, out_ref, wg_s, wu_s, wd_s):
    j = pl.program_id(0)
    del be_ref

    @pl.when(new_ref[j] == 1)
    def _():
        wg_s[...] = wg_ref[0, 0].astype(BF16)
        wu_s[...] = wu_ref[0, 0].astype(BF16)
        wd_s[...] = wd_ref[0, 0].astype(BF16)

    n_valid = valid_ref[j]

    @pl.when(n_valid > 0)
    def _():
        blk = buf_ref.shape[0] // PACKED_SUBLANES
        words = [_unpack_bf16_pair(buf_ref[pl.ds(c, blk, stride=PACKED_SUBLANES), :]) for c in range(PACKED_SUBLANES)]
        x = jnp.concatenate([w[0] for w in words] + [w[1] for w in words], axis=1).astype(BF16)
        cols = [pl.ds(c, blk, stride=ROW_SUBLANES) for c in range(ROW_SUBLANES)]
        g = jnp.dot(x, wg_s[...], preferred_element_type=F32)
        u = jnp.dot(x, wu_s[...], preferred_element_type=F32)
        hb = (g * jax.nn.sigmoid(g)) * u
        out = jnp.dot(hb.astype(BF16), wd_s[...], preferred_element_type=F32)
        for c, rows in enumerate(cols):
            out_ref[rows, :] = out[:, LANES * c:LANES * (c + 1)]

    @pl.when(n_valid == 0)
    def _():
        out_ref[...] = jnp.zeros_like(out_ref)


def _expert_call(block_expert, block_new, block_valid, buf, w_gate, w_up, w_down):
    nb = buf.shape[0] // (MOE_BLK * PACKED_SUBLANES)
    wsel = lambda j, be, nw, va: (0, be[j], 0, 0)
    rows = lambda sublanes: pl.BlockSpec((MOE_BLK * sublanes, LANES), lambda j, be, nw, va: (j, 0))
    return pl.pallas_call(
        _expert_kernel,
        grid_spec=pltpu.PrefetchScalarGridSpec(
            num_scalar_prefetch=3,
            grid=(nb,),
            in_specs=[
                rows(PACKED_SUBLANES),
                pl.BlockSpec((1, 1, D_MODEL, EXPERT_FF), wsel),
                pl.BlockSpec((1, 1, D_MODEL, EXPERT_FF), wsel),
                pl.BlockSpec((1, 1, EXPERT_FF, D_MODEL), wsel),
            ],
            out_specs=rows(ROW_SUBLANES),
            scratch_shapes=[pltpu.VMEM((D_MODEL, EXPERT_FF), BF16),
                            pltpu.VMEM((D_MODEL, EXPERT_FF), BF16),
                            pltpu.VMEM((EXPERT_FF, D_MODEL), BF16)],
        ),
        out_shape=jax.ShapeDtypeStruct((nb * MOE_BLK * ROW_SUBLANES, LANES), F32),
        compiler_params=_cparams(("arbitrary",)),
        name="experts",
    )(block_expert, block_new, block_valid, buf, w_gate, w_up, w_down)


def _combine_kernel(d0_ref, d1_ref, d0_next_ref, d1_next_ref, d0_ahead_ref, d1_ahead_ref, x1_ref, gate_ref,
                    gf_ref, eo_ref, o_ref, rows_a, rows_b, rows_c, sems):
    i = pl.program_id(0)
    tt = x1_ref.shape[0]
    bufs = (rows_a, rows_b, rows_c)
    n_buf = len(bufs)

    def start_row(d_refs, slot, r):
        for k, d_ref in enumerate(d_refs):
            pltpu.make_async_copy(eo_ref.at[_row_tile(d_ref[r])], bufs[slot].at[k, _row_tile(r)],
                                  sems.at[slot]).start(priority=k)

    def drain(slot):
        for k in range(TOP_K):
            pltpu.make_async_copy(eo_ref.at[pl.ds(0, tt * ROW_SUBLANES)], bufs[slot].at[k], sems.at[slot]).wait()

    @pl.when(i == 0)
    def _():
        def issue(g, c):
            base = pl.multiple_of(g * ISSUE_GROUP, ISSUE_GROUP)
            for j in range(ISSUE_GROUP):
                start_row((d0_ref, d1_ref), 0, base + j)
                start_row((d0_next_ref, d1_next_ref), 1, base + j)
            return c
        lax.fori_loop(0, tt // ISSUE_GROUP, issue, 0)

    def step(slot):
        ahead = (slot + 2) % n_buf
        drain(slot)
        gf = gf_ref[...]
        for ch in range(tt // COMBINE_CHUNK):
            r0 = ch * COMBINE_CHUNK
            for j in range(COMBINE_CHUNK):
                start_row((d0_ahead_ref, d1_ahead_ref), ahead, r0 + j)
            gate = gate_ref[r0:r0 + COMBINE_CHUNK, :]

            def rows(k):
                return jnp.concatenate(
                    [bufs[slot][k, pl.ds(ROW_SUBLANES * r0 + c, COMBINE_CHUNK, stride=ROW_SUBLANES), :]
                     for c in range(ROW_SUBLANES)], axis=1)

            y = rows(0) * gate[:, 0:1] + rows(1) * gate[:, 1:2]
            o_ref[r0:r0 + COMBINE_CHUNK, :] = _rms(x1_ref[r0:r0 + COMBINE_CHUNK, :] + y, gf)

        @pl.when(i == pl.num_programs(0) - 1)
        def _():
            drain((slot + 1) % n_buf)
            drain(ahead)

    for phase in range(n_buf):
        pl.when(i % n_buf == phase)(functools.partial(step, phase))


def _combine_call(dests, x1, gates, g_final, expert_out):
    t = x1.shape[0]
    tt = ROW_TILE // 2
    last = t // tt - 1
    tile = lambda ahead: pl.BlockSpec((tt,), lambda i: (jnp.minimum(i + ahead, last),), memory_space=pltpu.SMEM)
    buf = pltpu.VMEM((TOP_K, tt * ROW_SUBLANES, LANES), F32)
    return pl.pallas_call(
        _combine_kernel,
        grid=(t // tt,),
        in_specs=[
            tile(0), tile(0), tile(1), tile(1), tile(2), tile(2),
            pl.BlockSpec((tt, D_MODEL), lambda i: (i, 0)),
            pl.BlockSpec((tt, TOP_K), lambda i: (i, 0)),
            pl.BlockSpec((1, D_MODEL), lambda i: (0, 0)),
            pl.BlockSpec(memory_space=pl.ANY),
        ],
        out_specs=pl.BlockSpec((tt, D_MODEL), lambda i: (i, 0)),
        out_shape=jax.ShapeDtypeStruct((t, D_MODEL), F32),
        scratch_shapes=[buf, buf, buf, pltpu.SemaphoreType.DMA((3,))],
        compiler_params=_cparams(("arbitrary",), disable_bounds_checks=True),
        name="combine",
    )(*dests, *dests, *dests, x1, gates, g_final, expert_out)


def _rope_tables(seq):
    half = B_QK_ROPE // 2
    inv_freq = ROPE_THETA ** (-(jnp.arange(half, dtype=F32) / half))
    ang = jnp.arange(seq, dtype=F32)[:, None] * inv_freq[None, :]
    cos, sin = jnp.cos(ang), jnp.sin(ang)
    z = jnp.zeros((seq, B_QK_NOPE), F32)
    z2 = jnp.zeros((seq, B_QK_ROPE), F32)
    return (jnp.concatenate([z, cos, cos, z2], axis=1), jnp.concatenate([z, -sin, sin, z2], axis=1))


def _swap_halves(w):
    half = w.shape[-1] // 2
    return jnp.concatenate([w[..., half:], w[..., :half]], axis=-1)


def _layout_weights(w_in, w_q_up, w_kv_up):
    d = w_in.shape[0]
    w_kr = w_in[:, 3 * A_WIDTH + B_Q_LORA + B_KV_LORA:]
    w_in_l = jnp.concatenate(
        [w_in[:, :A_WIDTH] * (A_HEAD_DIM ** -0.5), w_in[:, A_WIDTH:3 * A_WIDTH + B_Q_LORA + B_KV_LORA],
         jnp.zeros((d, B_QK_NOPE), F32), w_kr, _swap_halves(w_kr)], axis=1).astype(BF16)
    wq = w_q_up.reshape(B_Q_LORA, B_HEADS, B_QK_NOPE + B_QK_ROPE)
    wq_l = jnp.concatenate([wq, _swap_halves(wq[..., B_QK_NOPE:])], axis=-1)
    wq_l = wq_l.reshape(B_Q_LORA, B_HEADS * LANES).astype(BF16)
    wkv = w_kv_up.reshape(B_KV_LORA, B_HEADS, B_QK_NOPE + B_V_DIM)
    wkb = jnp.concatenate([wkv[..., :B_QK_NOPE], jnp.zeros_like(wkv[..., :B_QK_NOPE])], axis=-1)
    wkb = wkb.reshape(B_KV_LORA, B_HEADS * LANES).astype(BF16)
    wvb = wkv[..., B_QK_NOPE:].reshape(B_KV_LORA, B_WIDTH).astype(BF16)
    return w_in_l, wq_l, wkb, wvb


def _block_plan(counts, n_blocks):
    padded = (counts + MOE_BLK - 1) // MOE_BLK * MOE_BLK
    ends = jnp.cumsum(padded)
    starts = ends - padded
    first_row = jnp.arange(n_blocks, dtype=I32) * MOE_BLK
    expert = jnp.minimum(jnp.sum(ends[None, :] <= first_row[:, None], axis=1), N_EXPERTS - 1).astype(I32)
    new = jnp.concatenate([jnp.ones((1,), I32), (expert[1:] != expert[:-1]).astype(I32)])
    valid = jnp.clip((starts + counts)[expert] - first_row, 0, MOE_BLK).astype(I32)
    return starts.astype(I32), expert, new, valid


def kernel(x, g_attn_norm, w_in, rel_bias, g_q_latent, w_q_up, g_kv_latent, w_kv_up, g_out_a, g_out_b, w_out,
           g_ffn_norm, w_router_group, b_router_group, w_router_expert, b_router_expert, w_gate, w_up, w_down,
           g_final):
    batch, seq, d = x.shape
    t = batch * seq
    assert g_attn_norm.shape[0] == 1 and d == D_MODEL and seq % ROW_TILE == 0
    cos_t, sin_t = _rope_tables(seq)
    unit = ROW_TILE // MIX_UNITS
    tri = jnp.triu(jnp.ones((unit, unit), F32), 1).astype(BF16)
    spread = (jnp.arange(2 * LANES)[:, None] % LANES == jnp.arange(A_WIDTH)[None, :] // A_HEAD_DIM).astype(BF16)
    n_blocks = t * TOP_K // MOE_BLK + N_EXPERTS
    x2 = x.reshape(t, d)
    row = lambda v: v.reshape(1, -1)

    w_in_l, wq_l, wkb_l, wvb_l = _layout_weights(w_in[0], w_q_up[0], w_kv_up[0])
    *qkv_a, qb, kb, vb = _proj_call(x2, row(g_attn_norm[0]), w_in_l, row(g_q_latent[0]), wq_l,
                                    row(g_kv_latent[0]), wkb_l, wvb_l, cos_t, sin_t, seq)
    oas, lses = [], []
    for pi, (window, dilation) in enumerate(DILATED_PATTERNS):
        bias = _dilated_bias(rel_bias, seq, dilation, window // (2 * dilation))
        o_p, lse_p = _dilated_call(qkv_a[pi], bias, batch, seq, dilation)
        oas.append(o_p)
        lses.append(lse_p)
    ob = _mla_call(qb, kb, vb, batch, seq)

    pad = ROUTER_ROWS - N_EXPERTS - N_GROUPS
    w_router = jnp.concatenate([w_router_expert[0], w_router_group[0], jnp.zeros((d, pad), F32)], axis=1).T
    b_router = jnp.concatenate([b_router_expert[0], b_router_group[0], jnp.zeros((pad,), F32)])
    x1, hp, idx, gates, cnt = _mix_call(oas, lses, ob, x2, row(g_out_a[0]), row(g_out_b[0]), w_out[0].astype(BF16),
                                        row(g_ffn_norm[0]), w_router, b_router.reshape(-1, 1), tri, spread)
    pstart, block_expert, block_new, block_valid = _block_plan(cnt[:N_EXPERTS, 0], n_blocks)
    pstart_col = jnp.concatenate([pstart, jnp.zeros((ROUTER_ROWS - N_EXPERTS,), I32)]).reshape(-1, 1)
    dest = _dest_call(idx, pstart_col)
    dests = (dest[0], dest[1])
    buf = _dispatch_call(block_valid, dests, hp)
    expert_out = _expert_call(block_expert, block_new, block_valid, buf, w_gate, w_up, w_down)
    return _combine_call(dests, x1, gates[:TOP_K].T, row(g_final), expert_out).reshape(batch, seq, d)
```

```python
import functools
import math

import numpy as np
import jax
import jax.numpy as jnp
from jax import lax
from jax.experimental import pallas as pl
from jax.experimental.pallas import tpu as pltpu

F32 = jnp.float32
BF16 = jnp.bfloat16
I32 = jnp.int32
U32 = jnp.uint32

D_MODEL = 1024
EPS = 1e-6
NEG_INF = -1e30
LANES = 128
ROW_SUBLANES = D_MODEL // LANES
PACKED_SUBLANES = ROW_SUBLANES // 2

A_HEADS = 8
A_HEAD_DIM = 64
A_WIDTH = 512
A_QKV_WIDTH = 3 * A_WIDTH
DILATED_PATTERNS = ((128, 1), (512, 4), (2048, 16))
REL_BUCKETS = 32
REL_MAX_DISTANCE = 1024
A_QB = 128

B_HEADS = 8
B_Q_LORA = 256
B_KV_LORA = 128
B_QK_NOPE = 64
B_QK_ROPE = 32
B_V_DIM = 64
B_WIDTH = 512
ROPE_THETA = 10000.0
B_SCALE = (B_QK_NOPE + B_QK_ROPE) ** -0.5
B_QB = 512
B_SUB = 256
LOG2E = math.log2(math.e)

N_GROUPS = 4
EXPERTS_PER_GROUP = 8
N_EXPERTS = 32
TOP_K = 2
EXPERT_FF = 256
MOE_BLK = 512
ROUTER_ROWS = 40
MIX_UNITS = 2

ROW_TILE = 512
ISSUE_GROUP = 8
COMBINE_CHUNK = 32
PROJ_COLS = 2048

_NT = (((1,), (1,)), ((), ()))


def _cparams(semantics, vmem_mb=48, **kw):
    return pltpu.CompilerParams(dimension_semantics=semantics,
                                vmem_limit_bytes=vmem_mb * 1024 * 1024, **kw)


def _rms(x, g):
    return x * lax.rsqrt(jnp.mean(x * x, axis=-1, keepdims=True) + EPS) * g


def _lane_iota(rows=1):
    return lax.broadcasted_iota(I32, (rows, LANES), 1)


def _row_tile(r, sublanes=ROW_SUBLANES):
    return pl.ds(pl.multiple_of(r * sublanes, sublanes), sublanes)


def _proj_kernel(x_ref, g_ref, win_ref, gq_ref, wq_ref, gkv_ref, wkb_ref, wvb_ref, cos_ref, sin_ref, *refs):
    out1, out4, out16 = refs[:3]
    qb_ref, kb_ref, vb_ref, slab_ref, slab4_ref = refs[3:]
    tm = x_ref.shape[0]
    h = _rms(x_ref[...], g_ref[...]).astype(BF16)
    lo = _lane_iota() < A_HEAD_DIM
    n_slabs = A_QKV_WIDTH // LANES
    parts = []
    for s in range(0, n_slabs, 2):
        part = jnp.dot(h, win_ref[:, LANES * s:LANES * (s + 2)], preferred_element_type=F32)
        if LANES * s < A_WIDTH:
            part = part * LOG2E
        slab_ref[s] = part[:, :LANES]
        slab_ref[s + 1] = part[:, LANES:]
        parts.append(part)
    out1[0, 0] = jnp.concatenate(parts, axis=1).astype(BF16)
    lat = jnp.dot(h, win_ref[:, A_QKV_WIDTH:], preferred_element_type=F32)
    cos = cos_ref[...]
    sin = sin_ref[...]
    n4 = tm // 4

    cq = _rms(lat[:, :B_Q_LORA], gq_ref[...]).astype(BF16)
    q = jnp.dot(cq, wq_ref[...], preferred_element_type=F32)
    q_mul = (cos + jnp.where(lo, 1.0, 0.0)) * (B_SCALE * LOG2E)
    q_rot = sin * (B_SCALE * LOG2E)
    qb_ref[...] = jnp.concatenate(
        [q[:, LANES * hd:LANES * (hd + 1)] * q_mul + pltpu.roll(q[:, LANES * hd:LANES * (hd + 1)], 96, 1) * q_rot
         for hd in range(B_HEADS)], axis=1).astype(BF16)

    ckv = _rms(lat[:, B_Q_LORA:B_Q_LORA + B_KV_LORA], gkv_ref[...]).astype(BF16)
    kr = lat[:, B_Q_LORA + B_KV_LORA:]
    kr = kr * cos + pltpu.roll(kr, 96, 1) * sin
    kn = jnp.dot(ckv, wkb_ref[...], preferred_element_type=F32)
    kb_ref[...] = jnp.concatenate([kn[:, LANES * hd:LANES * (hd + 1)] + kr for hd in range(B_HEADS)],
                                  axis=1).astype(BF16)
    vb_ref[...] = jnp.dot(ckv, wvb_ref[...], preferred_element_type=F32).astype(BF16)

    for c4 in range(4):
        pieces = [slab_ref[s, pl.ds(c4, n4, stride=4), :] for s in range(n_slabs)]
        for s, piece in enumerate(pieces):
            slab4_ref[s, c4] = piece
        out4[0, c4] = jnp.concatenate(pieces, axis=1).astype(BF16)
    for c4 in range(4):
        for j in range(4):
            out16[0, c4 + 4 * j] = jnp.concatenate(
                [slab4_ref[s, c4, pl.ds(j, n4 // 4, stride=4), :] for s in range(n_slabs)], axis=1).astype(BF16)


def _proj_call(x2, g_attn, w_in, g_q, w_q, g_kv, w_kb, w_vb, cos_t, sin_t, seq):
    t = x2.shape[0]
    tm = ROW_TILE
    nseq = seq // tm
    row = lambda i: (i, 0)
    const = lambda i: (0, 0)
    pos = lambda i: (i % nseq, 0)
    out = lambda w: jax.ShapeDtypeStruct((t, w), BF16)
    a_specs, a_shapes = [], []
    for _, r in DILATED_PATTERNS:
        a_specs.append(pl.BlockSpec((1, r, tm // r, A_QKV_WIDTH), lambda i: (i // nseq, 0, i % nseq, 0)))
        a_shapes.append(jax.ShapeDtypeStruct((t // seq, r, seq // r, A_QKV_WIDTH), BF16))
    return pl.pallas_call(
        _proj_kernel,
        grid=(t // tm,),
        in_specs=[
            pl.BlockSpec((tm, D_MODEL), row),
            pl.BlockSpec((1, D_MODEL), const),
            pl.BlockSpec((D_MODEL, PROJ_COLS), const),
            pl.BlockSpec((1, B_Q_LORA), const),
            pl.BlockSpec((B_Q_LORA, B_HEADS * LANES), const),
            pl.BlockSpec((1, B_KV_LORA), const),
            pl.BlockSpec((B_KV_LORA, B_HEADS * LANES), const),
            pl.BlockSpec((B_KV_LORA, B_WIDTH), const),
            pl.BlockSpec((tm, LANES), pos),
            pl.BlockSpec((tm, LANES), pos),
        ],
        out_specs=a_specs + [
            pl.BlockSpec((tm, B_HEADS * LANES), row),
            pl.BlockSpec((tm, B_HEADS * LANES), row),
            pl.BlockSpec((tm, B_WIDTH), row),
        ],
        out_shape=a_shapes + [out(B_HEADS * LANES), out(B_HEADS * LANES), out(B_WIDTH)],
        scratch_shapes=[pltpu.VMEM((A_QKV_WIDTH // LANES, tm, LANES), F32),
                        pltpu.VMEM((A_QKV_WIDTH // LANES, 4, tm // 4, LANES), F32)],
        compiler_params=_cparams(("parallel",)),
        name="proj",
    )(x2, g_attn, w_in, g_q, w_q, g_kv, w_kb, w_vb, cos_t, sin_t)


def _pack_bf16_pair(a, b):
    a_bits = lax.bitcast_convert_type(a.astype(BF16).astype(F32), U32) >> 16
    b_bits = lax.bitcast_convert_type(b.astype(BF16).astype(F32), U32) & jnp.uint32(0xFFFF0000)
    return a_bits | b_bits


def _unpack_bf16_pair(w):
    return (lax.bitcast_convert_type(w << 16, F32), lax.bitcast_convert_type(w & jnp.uint32(0xFFFF0000), F32))


def _dilated_kernel(qkv_ref, bias_ref, o_ref, lse_ref, *, seq_len, dilation, key_width, group):
    nblk = seq_len // A_QB
    lane = _lane_iota()
    lo = lane < A_HEAD_DIM
    first_class = pl.program_id(1) * group
    pairs = A_HEADS // 2

    def block(it, carry):
        c = it // nblk
        if nblk == 1:
            q0, ks, var = 0, 0, 0
        else:
            n = it % nblk
            q0 = pl.multiple_of(n * A_QB, A_QB)
            ks = pl.multiple_of(jnp.clip(q0 - 64, 0, seq_len - key_width), 64)
            var = jnp.where(n == 0, 0, jnp.where(n == nblk - 1, 2, 1))
        rows = pl.ds(q0, A_QB)
        keys = pl.ds(ks, key_width)
        if dilation == 1:
            out_rows = rows
        else:
            out_rows = pl.ds(first_class + c + dilation * q0, A_QB, stride=dilation)
        q_tiles = [qkv_ref[0, c, rows, LANES * p:LANES * (p + 1)] for p in range(pairs)]
        k_tiles = [qkv_ref[0, c, keys, A_WIDTH + LANES * p:A_WIDTH + LANES * (p + 1)] for p in range(pairs)]
        v_tiles = [qkv_ref[0, c, keys, 2 * A_WIDTH + LANES * p:2 * A_WIDTH + LANES * (p + 1)] for p in range(pairs)]
        zero = jnp.zeros((), BF16)
        k_heads = [jnp.where(lo, k_tiles[hd // 2], zero) if hd % 2 == 0 else jnp.where(lo, zero, k_tiles[hd // 2])
                   for hd in range(A_HEADS)]
        scores = [lax.dot_general(q_tiles[hd // 2], k_heads[hd], _NT, preferred_element_type=F32) + bias_ref[var, hd]
                  for hd in range(A_HEADS)]
        maxes = [jnp.max(s, axis=-1, keepdims=True) for s in scores]
        probs = [jnp.exp2(s - m) for s, m in zip(scores, maxes)]
        dens = [jnp.sum(pr, axis=-1, keepdims=True) for pr in probs]
        pvs = [jnp.dot(pr.astype(BF16), v_tiles[hd // 2], preferred_element_type=F32) for hd, pr in enumerate(probs)]
        outs = []
        for p in range(pairs):
            h0, h1 = 2 * p, 2 * p + 1
            outs.append(jnp.where(lo, pvs[h0] * (1.0 / dens[h0]), pvs[h1] * (1.0 / dens[h1])))
        lse = jnp.zeros((A_QB, LANES), F32)
        for hd in range(A_HEADS):
            lse = jnp.where(lane == hd, maxes[hd] + jnp.log2(dens[hd]), lse)
        lse_ref[0, out_rows, :] = lse
        for j in range(pairs // 2):
            o_ref[0, j, out_rows, :] = _pack_bf16_pair(outs[2 * j], outs[2 * j + 1])
        return carry

    lax.fori_loop(0, group * nblk, block, 0, unroll=4)


def _dilated_call(qkv, bias, batch, seq, dilation):
    r = dilation
    sl = seq // r
    kw = min(2 * A_QB, sl)
    group = r
    pairs = A_HEADS // 2
    return pl.pallas_call(
        functools.partial(_dilated_kernel, seq_len=sl, dilation=r, key_width=kw, group=group),
        grid=(batch, r // group),
        in_specs=[pl.BlockSpec((1, group, sl, A_QKV_WIDTH), lambda b, c: (b, c, 0, 0)),
                  pl.BlockSpec(bias.shape, lambda b, c: (0, 0, 0, 0))],
        out_specs=[pl.BlockSpec((1, pairs // 2, seq, LANES), lambda b, c: (b, 0, 0, 0)),
                   pl.BlockSpec((1, seq, LANES), lambda b, c: (b, 0, 0))],
        out_shape=[jax.ShapeDtypeStruct((batch, pairs // 2, seq, LANES), U32),
                   jax.ShapeDtypeStruct((batch, seq, LANES), F32)],
        compiler_params=_cparams(("parallel", "arbitrary")),
        name=f"dilated_r{r}",
    )(qkv, bias)


def _t5_bucket(rel):
    half = REL_BUCKETS // 2
    max_exact = half // 2
    n = np.abs(rel)
    large = max_exact + (np.log(np.maximum(n, 1) / max_exact)
                         / math.log(REL_MAX_DISTANCE / max_exact) * (half - max_exact)).astype(np.int32)
    large = np.minimum(large, half - 1)
    return (np.where(rel > 0, half, 0) + np.where(n < max_exact, n, large)).astype(np.int32)


def _dilated_bias(rel_bias, seq, dilation, half_steps):
    sl = seq // dilation
    kw = min(2 * A_QB, sl)
    offsets = [0] if sl == kw else [0, -half_steps, A_QB - kw]
    span = A_QB + kw - 1
    rel = np.stack([np.arange(span) - (A_QB - 1) + off for off in offsets])
    bucket = np.where(np.abs(rel) <= half_steps, _t5_bucket(rel * dilation), REL_BUCKETS).astype(np.int32)
    onehot = (jnp.asarray(bucket)[..., None] == jnp.arange(REL_BUCKETS + 1, dtype=I32)).astype(F32)
    table = jnp.concatenate([rel_bias.astype(F32), jnp.full((1, A_HEADS), NEG_INF, F32)], axis=0)
    u = jnp.einsum("vmb,bh->vhm", onehot, table * LOG2E, precision=lax.Precision.HIGHEST)
    period = span + 1
    w = jnp.concatenate([u, jnp.zeros(u.shape[:2] + (1,), F32)], axis=-1)
    hankel = jnp.tile(w, (1, 1, A_QB + 1))[..., :A_QB * (period + 1)].reshape(u.shape[:2] + (A_QB, period + 1))
    return hankel[:, :, ::-1, :kw]


def _mla_kernel(q_ref, k_ref, v_ref, o_ref, v1_ref):
    lo = _lane_iota() < B_V_DIM
    sub = B_QB // B_SUB
    tiles = [slice(0, LANES), slice(LANES, 2 * LANES)]
    v1_ref[:, :LANES] = v_ref[0]
    v1_ref[:, LANES:] = jnp.ones((v_ref.shape[1], LANES), BF16)

    units = [(j, half) for j in range(sub) for half in range(2)]

    def rows(i):
        return [pl.ds(i * B_QB + B_SUB * j, B_SUB) for j in range(sub)]

    def scores(i):
        return [lax.dot_general(q_ref[0, rows(i)[j], tiles[half]], k_ref[0, :, tiles[half]], _NT,
                                preferred_element_type=F32) for j, half in units]

    def finish(i, sc):
        maxes = [jnp.max(s, axis=-1, keepdims=True) for s in sc]
        probs = [jnp.exp2(s - m) for s, m in zip(sc, maxes)]
        pvs = [jnp.dot(pr.astype(BF16), v1_ref[...], preferred_element_type=F32) for pr in probs]
        outs = [pv[:, :LANES] * (1.0 / pv[:, LANES:]) for pv in pvs]
        for j in range(sub):
            o_ref[0, rows(i)[j], :] = jnp.where(lo, outs[2 * j], outs[2 * j + 1]).astype(BF16)

    n_blocks = q_ref.shape[1] // B_QB
    sc = scores(0)
    for i in range(n_blocks):
        sc_next = scores(i + 1) if i + 1 < n_blocks else None
        finish(i, sc)
        sc = sc_next


def _mla_call(qb, kb, vb, batch, seq):
    qb = qb.reshape(batch, seq, B_HEADS * LANES)
    kb = kb.reshape(batch, seq, B_HEADS * LANES)
    vb = vb.reshape(batch, seq, B_WIDTH)
    pair = lambda w: pl.BlockSpec((1, seq, w), lambda b, p: (b, 0, p))
    out = pl.pallas_call(
        _mla_kernel,
        grid=(batch, B_HEADS // 2),
        in_specs=[pair(2 * LANES), pair(2 * LANES), pair(LANES)],
        out_specs=pair(LANES),
        out_shape=jax.ShapeDtypeStruct((batch, seq, B_WIDTH), BF16),
        scratch_shapes=[pltpu.VMEM((seq, 2 * LANES), BF16)],
        compiler_params=_cparams(("parallel", "parallel")),
        name="mla",
    )(qb, kb, vb)
    return out.reshape(batch * seq, B_WIDTH)


def _merge_patterns(o_refs, lse_refs, spread_ref, rows):
    lses = [r[0, rows, :] for r in lse_refs]
    top = functools.reduce(jnp.maximum, lses)
    es = [jnp.exp2(l - top) for l in lses]
    inv = 1.0 / functools.reduce(jnp.add, es)
    spread = spread_ref[...]
    weights = []
    for e in es:
        w = e * inv
        w_hi = w.astype(BF16)
        w_lo = (w - w_hi.astype(F32)).astype(BF16)
        weights.append(jnp.dot(jnp.concatenate([w_hi, w_lo], axis=1), spread, preferred_element_type=F32))
    tiles = []
    for j in range(A_HEADS // 4):
        outs = [_unpack_bf16_pair(r[0, j, rows, :]) for r in o_refs]
        for half in range(2):
            tile = slice(LANES * (2 * j + half), LANES * (2 * j + half + 1))
            tiles.append(functools.reduce(jnp.add, [w[:, tile] * o[half] for w, o in zip(weights, outs)]))
    return jnp.concatenate(tiles, axis=1)


def _mix_kernel(o1_ref, o4_ref, o16_ref, l1_ref, l4_ref, l16_ref, ob_ref, x_ref, ga_ref, gb_ref, wo_ref,
                gf_ref, wr_ref, br_ref, tri_ref, spread_ref, x1_ref, hp_ref, idx_ref, gate_ref, cnt_ref, carry_ref):
    i = pl.program_id(0)
    unit = x_ref.shape[0] // MIX_UNITS
    units = [slice(unit * u, unit * (u + 1)) for u in range(MIX_UNITS)]

    @pl.when(i == 0)
    def _():
        carry_ref[...] = jnp.zeros_like(carry_ref)

    def residual_stream(u):
        rows = units[u]
        oa = _merge_patterns((o1_ref, o4_ref, o16_ref), (l1_ref, l4_ref, l16_ref), spread_ref, rows)
        a = _rms(oa, ga_ref[...]).astype(BF16)
        b = _rms(ob_ref[rows, :].astype(F32), gb_ref[...]).astype(BF16)
        mix = (jnp.dot(a, wo_ref[0:A_WIDTH, :], preferred_element_type=F32)
               + jnp.dot(b, wo_ref[A_WIDTH:, :], preferred_element_type=F32))
        x1 = x_ref[rows, :] + mix
        x1_ref[rows, :] = x1
        h2 = _rms(x1, gf_ref[...])
        half = D_MODEL // 2
        for c in range(PACKED_SUBLANES):
            hp_ref[pl.ds(PACKED_SUBLANES * unit * u + c, unit, stride=PACKED_SUBLANES), :] = _pack_bf16_pair(
                h2[:, LANES * c:LANES * (c + 1)], h2[:, half + LANES * c:half + LANES * (c + 1)])
        return h2

    wr = wr_ref[...]
    wr_hi = wr.astype(BF16)
    wr_lo = (wr - wr_hi.astype(F32)).astype(BF16)
    wr_both = jnp.concatenate([wr_hi, wr_lo], axis=0)

    def logits(h2):
        h_hi = h2.astype(BF16)
        h_lo = (h2 - h_hi.astype(F32)).astype(BF16)
        lg_hi = lax.dot_general(wr_both, h_hi, _NT, preferred_element_type=F32)
        return (lg_hi[:ROUTER_ROWS] + lg_hi[ROUTER_ROWS:]
                + lax.dot_general(wr_hi, h_lo, _NT, preferred_element_type=F32) + br_ref[...])

    def route(lg):
        row = lax.broadcasted_iota(I32, lg.shape, 0)
        is_g = (row >= N_EXPERTS) & (row < N_EXPERTS + N_GROUPS)
        gl = jnp.where(is_g, lg, NEG_INF)
        ge = jnp.exp(gl - jnp.max(gl, axis=0, keepdims=True))
        gp = ge / jnp.sum(ge, axis=0, keepdims=True)
        g_gate = jnp.max(gp, axis=0, keepdims=True)
        g_idx = jnp.min(jnp.where(is_g & (gp == g_gate), row - N_EXPERTS, LANES), axis=0, keepdims=True)
        sel = (row // EXPERTS_PER_GROUP) == g_idx
        el = jnp.where(sel, lg, NEG_INF)
        ee = jnp.exp(el - jnp.max(el, axis=0, keepdims=True))
        ep = jnp.where(sel, ee / jnp.sum(ee, axis=0, keepdims=True), -1.0)
        p1 = jnp.max(ep, axis=0, keepdims=True)
        i1 = jnp.min(jnp.where(ep == p1, row, LANES), axis=0, keepdims=True)
        ep2 = jnp.where(row == i1, -1.0, ep)
        p2 = jnp.max(ep2, axis=0, keepdims=True)
        i2 = jnp.min(jnp.where(sel & (ep2 == p2) & (row != i1), row, LANES), axis=0, keepdims=True)
        den = p1 + p2
        return i1, i2, g_gate * p1 / den, g_gate * p2 / den, row == i1, row == i2

    h2s = [residual_stream(u) for u in range(MIX_UNITS)]
    lgs = [logits(h2) for h2 in h2s]
    routes = [route(lg) for lg in lgs]

    carry = carry_ref[...]
    row8 = lax.broadcasted_iota(I32, (idx_ref.shape[0], unit), 0)
    for u, (i1, i2, g1, g2, hit1, hit2) in enumerate(routes):
        onehot = jnp.where(hit1 | hit2, 1.0, 0.0)
        before = jnp.dot(onehot.astype(BF16), tri_ref[...], preferred_element_type=F32) + carry
        r1 = jnp.sum(jnp.where(hit1, before, 0.0), axis=0, keepdims=True).astype(I32)
        r2 = jnp.sum(jnp.where(hit2, before, 0.0), axis=0, keepdims=True).astype(I32)
        carry = carry + jnp.sum(onehot, axis=1, keepdims=True)
        idx_ref[:, units[u]] = jnp.where(row8 == 0, i1, jnp.where(row8 == 1, i2,
                                         jnp.where(row8 == 2, r1, jnp.where(row8 == 3, r2, 0))))
        gate_ref[:, units[u]] = jnp.where(row8 == 0, g1, jnp.where(row8 == 1, g2, 0.0))
    carry_ref[...] = carry

    @pl.when(i == pl.num_programs(0) - 1)
    def _():
        cnt_ref[...] = jnp.broadcast_to(carry_ref[...], cnt_ref.shape).astype(I32)


def _mix_call(oas, lses, ob, x2, g_a, g_b, w_out, g_ffn, w_router, b_router, tri, spread):
    t = x2.shape[0]
    tm = ROW_TILE
    nseq = oas[0].shape[2] // tm
    row = lambda i: (i, 0)
    const = lambda i: (0, 0)
    slab = lambda n: pl.BlockSpec((1, n, tm, LANES), lambda i: (i // nseq, 0, i % nseq, 0))
    o_slab = slab(A_HEADS // 4)
    lse_slab = pl.BlockSpec((1, tm, LANES), lambda i: (i // nseq, i % nseq, 0))
    return pl.pallas_call(
        _mix_kernel,
        grid=(t // tm,),
        in_specs=[
            o_slab, o_slab, o_slab, lse_slab, lse_slab, lse_slab,
            pl.BlockSpec((tm, B_WIDTH), row),
            pl.BlockSpec((tm, D_MODEL), row),
            pl.BlockSpec((1, A_WIDTH), const),
            pl.BlockSpec((1, B_WIDTH), const),
            pl.BlockSpec((D_MODEL, D_MODEL), const),
            pl.BlockSpec((1, D_MODEL), const),
            pl.BlockSpec((ROUTER_ROWS, D_MODEL), const),
            pl.BlockSpec((ROUTER_ROWS, 1), const),
            pl.BlockSpec((tm // MIX_UNITS, tm // MIX_UNITS), const),
            pl.BlockSpec((2 * LANES, A_WIDTH), const),
        ],
        out_specs=[
            pl.BlockSpec((tm, D_MODEL), row),
            pl.BlockSpec((tm * PACKED_SUBLANES, LANES), row),
            pl.BlockSpec((8, tm), lambda i: (0, i)),
            pl.BlockSpec((8, tm), lambda i: (0, i)),
            pl.BlockSpec((ROUTER_ROWS, LANES), const),
        ],
        out_shape=[
            jax.ShapeDtypeStruct((t, D_MODEL), F32),
            jax.ShapeDtypeStruct((t * PACKED_SUBLANES, LANES), U32),
            jax.ShapeDtypeStruct((8, t), I32),
            jax.ShapeDtypeStruct((8, t), F32),
            jax.ShapeDtypeStruct((ROUTER_ROWS, LANES), I32),
        ],
        scratch_shapes=[pltpu.VMEM((ROUTER_ROWS, 1), F32)],
        compiler_params=_cparams(("arbitrary",)),
        name="mix_router",
    )(*oas, *lses, ob, x2, g_a, g_b, w_out, g_ffn, w_router, b_router, tri, spread)


def _dest_kernel(idx_ref, pstart_ref, dest_ref):
    idx = idx_ref[...]
    row = lax.broadcasted_iota(I32, (ROUTER_ROWS, idx.shape[1]), 0)
    ps = pstart_ref[...]

    def slot(k):
        return jnp.sum(jnp.where(row == idx[k:k + 1, :], ps, 0), axis=0, keepdims=True) + idx[2 + k:3 + k, :]

    row8 = lax.broadcasted_iota(I32, idx.shape, 0)
    dest_ref[...] = jnp.where(row8 == 0, slot(0), jnp.where(row8 == 1, slot(1), 0))


def _dest_call(idx, pstart):
    t = idx.shape[1]
    tm = 4 * ROW_TILE
    return pl.pallas_call(
        _dest_kernel,
        grid=(t // tm,),
        in_specs=[pl.BlockSpec((8, tm), lambda i: (0, i)), pl.BlockSpec((ROUTER_ROWS, 1), lambda i: (0, 0))],
        out_specs=pl.BlockSpec((8, tm), lambda i: (0, i)),
        out_shape=jax.ShapeDtypeStruct((8, t), I32),
        compiler_params=_cparams(("parallel",)),
        name="dest_rows",
    )(idx, pstart)


def _dispatch_kernel(valid_ref, d0_ref, d1_ref, h_ref, buf_ref, zero_ref, sem, pad_sem):
    i = pl.program_id(0)
    tt = h_ref.shape[0] // PACKED_SUBLANES
    n_blocks = valid_ref.shape[0]

    def for_padded_blocks(fn):
        def body(j, c):
            taken = valid_ref[j]
            even = (taken + 1) // 2 * 2

            @pl.when(even < MOE_BLK)
            def _():
                n_pad = pl.multiple_of((MOE_BLK - even) * PACKED_SUBLANES, ROW_SUBLANES)
                first = pl.multiple_of((j * MOE_BLK + even) * PACKED_SUBLANES, ROW_SUBLANES)
                fn(pltpu.make_async_copy(zero_ref.at[pl.ds(0, n_pad)], buf_ref.at[pl.ds(first, n_pad)], pad_sem))

            @pl.when(even != taken)
            def _():
                fn(pltpu.make_async_copy(zero_ref.at[pl.ds(0, PACKED_SUBLANES)],
                                         buf_ref.at[_row_tile(j * MOE_BLK + taken, PACKED_SUBLANES)], pad_sem))
            return c
        lax.fori_loop(0, n_blocks, body, 0)

    @pl.when(i == 0)
    def _():
        zero_ref[...] = jnp.zeros_like(zero_ref)
        for_padded_blocks(lambda cp: cp.start())

    def issue(g, c):
        base = pl.multiple_of(g * ISSUE_GROUP, ISSUE_GROUP)
        for j in range(ISSUE_GROUP):
            for prio, d_ref in enumerate((d0_ref, d1_ref)):
                pltpu.make_async_copy(h_ref.at[_row_tile(base + j, PACKED_SUBLANES)],
                                      buf_ref.at[_row_tile(d_ref[base + j], PACKED_SUBLANES)],
                                      sem).start(priority=prio)
        return c

    lax.fori_loop(0, tt // ISSUE_GROUP, issue, 0)
    for k in range(TOP_K):
        pltpu.make_async_copy(h_ref, buf_ref.at[pl.ds(0, tt * PACKED_SUBLANES)], sem).wait()

    @pl.when(i == pl.num_programs(0) - 1)
    def _():
        for_padded_blocks(lambda cp: cp.wait())


def _dispatch_call(block_valid, dests, hp):
    t = hp.shape[0] // PACKED_SUBLANES
    tt = 2 * ROW_TILE
    n_rows = block_valid.shape[0] * MOE_BLK
    return pl.pallas_call(
        _dispatch_kernel,
        grid_spec=pltpu.PrefetchScalarGridSpec(
            num_scalar_prefetch=1,
            grid=(t // tt,),
            in_specs=[
                pl.BlockSpec((tt,), lambda i, va: (i,), memory_space=pltpu.SMEM),
                pl.BlockSpec((tt,), lambda i, va: (i,), memory_space=pltpu.SMEM),
                pl.BlockSpec((tt * PACKED_SUBLANES, LANES), lambda i, va: (i, 0)),
            ],
            out_specs=pl.BlockSpec(memory_space=pl.ANY),
            scratch_shapes=[pltpu.VMEM((MOE_BLK * PACKED_SUBLANES, LANES), U32),
                            pltpu.SemaphoreType.DMA(()), pltpu.SemaphoreType.DMA(())],
        ),
        out_shape=jax.ShapeDtypeStruct((n_rows * PACKED_SUBLANES, LANES), U32),
        compiler_params=_cparams(("arbitrary",), disable_bounds_checks=True, has_side_effects=True),
        name="dispatch",
    )(block_valid, *dests, hp)


def _expert_kernel(be_ref, new_ref, valid_ref, buf_ref, wg_ref, wu_ref, wd_ref, out_ref, wg_s, wu_s, wd_s):
    j = pl.program_id(0)
    del be_ref

    @pl.when(new_ref[j] == 1)
    def _():
        wg_s[...] = wg_ref[0, 0].astype(BF16)
        wu_s[...] = wu_ref[0, 0].astype(BF16)
        wd_s[...] = wd_ref[0, 0].astype(BF16)

    n_valid = valid_ref[j]

    @pl.when(n_valid > 0)
    def _():
        blk = buf_ref.shape[0] // PACKED_SUBLANES
        words = [_unpack_bf16_pair(buf_ref[pl.ds(c, blk, stride=PACKED_SUBLANES), :]) for c in range(PACKED_SUBLANES)]
        x = jnp.concatenate([w[0] for w in words] + [w[1] for w in words], axis=1).astype(BF16)
        cols = [pl.ds(c, blk, stride=ROW_SUBLANES) for c in range(ROW_SUBLANES)]
        g = jnp.dot(x, wg_s[...], preferred_element_type=F32)
        u = jnp.dot(x, wu_s[...], preferred_element_type=F32)
        hb = (g * jax.nn.sigmoid(g)) * u
        out = jnp.dot(hb.astype(BF16), wd_s[...], preferred_element_type=F32)
        for c, rows in enumerate(cols):
            out_ref[rows, :] = out[:, LANES * c:LANES * (c + 1)]

    @pl.when(n_valid == 0)
    def _():
        out_ref[...] = jnp.zeros_like(out_ref)


def _expert_call(block_expert, block_new, block_valid, buf, w_gate, w_up, w_down):
    nb = buf.shape[0] // (MOE_BLK * PACKED_SUBLANES)
    wsel = lambda j, be, nw, va: (0, be[j], 0, 0)
    rows = lambda sublanes: pl.BlockSpec((MOE_BLK * sublanes, LANES), lambda j, be, nw, va: (j, 0))
    return pl.pallas_call(
        _expert_kernel,
        grid_spec=pltpu.PrefetchScalarGridSpec(
            num_scalar_prefetch=3,
            grid=(nb,),
            in_specs=[
                rows(PACKED_SUBLANES),
                pl.BlockSpec((1, 1, D_MODEL, EXPERT_FF), wsel),
                pl.BlockSpec((1, 1, D_MODEL, EXPERT_FF), wsel),
                pl.BlockSpec((1, 1, EXPERT_FF, D_MODEL), wsel),
            ],
            out_specs=rows(ROW_SUBLANES),
            scratch_shapes=[pltpu.VMEM((D_MODEL, EXPERT_FF), BF16),
                            pltpu.VMEM((D_MODEL, EXPERT_FF), BF16),
                            pltpu.VMEM((EXPERT_FF, D_MODEL), BF16)],
        ),
        out_shape=jax.ShapeDtypeStruct((nb * MOE_BLK * ROW_SUBLANES, LANES), F32),
        compiler_params=_cparams(("arbitrary",)),
        name="experts",
    )(block_expert, block_new, block_valid, buf, w_gate, w_up, w_down)


def _combine_kernel(d0_ref, d1_ref, d0_next_ref, d1_next_ref, d0_ahead_ref, d1_ahead_ref, x1_ref, gate_ref,
                    gf_ref, eo_ref, o_ref, rows_a, rows_b, rows_c, sems):
    i = pl.program_id(0)
    tt = x1_ref.shape[0]
    bufs = (rows_a, rows_b, rows_c)
    n_buf = len(bufs)

    def start_row(d_refs, slot, r):
        for k, d_ref in enumerate(d_refs):
            pltpu.make_async_copy(eo_ref.at[_row_tile(d_ref[r])], bufs[slot].at[k, _row_tile(r)],
                                  sems.at[slot]).start(priority=k)

    def drain(slot):
        for k in range(TOP_K):
            pltpu.make_async_copy(eo_ref.at[pl.ds(0, tt * ROW_SUBLANES)], bufs[slot].at[k], sems.at[slot]).wait()

    @pl.when(i == 0)
    def _():
        def issue(g, c):
            base = pl.multiple_of(g * ISSUE_GROUP, ISSUE_GROUP)
            for j in range(ISSUE_GROUP):
                start_row((d0_ref, d1_ref), 0, base + j)
                start_row((d0_next_ref, d1_next_ref), 1, base + j)
            return c
        lax.fori_loop(0, tt // ISSUE_GROUP, issue, 0)

    def step(slot):
        ahead = (slot + 2) % n_buf
        drain(slot)
        gf = gf_ref[...]
        for ch in range(tt // COMBINE_CHUNK):
            r0 = ch * COMBINE_CHUNK
            for j in range(COMBINE_CHUNK):
                start_row((d0_ahead_ref, d1_ahead_ref), ahead, r0 + j)
            gate = gate_ref[r0:r0 + COMBINE_CHUNK, :]

            def rows(k):
                return jnp.concatenate(
                    [bufs[slot][k, pl.ds(ROW_SUBLANES * r0 + c, COMBINE_CHUNK, stride=ROW_SUBLANES), :]
                     for c in range(ROW_SUBLANES)], axis=1)

            y = rows(0) * gate[:, 0:1] + rows(1) * gate[:, 1:2]
            o_ref[r0:r0 + COMBINE_CHUNK, :] = _rms(x1_ref[r0:r0 + COMBINE_CHUNK, :] + y, gf)

        @pl.when(i == pl.num_programs(0) - 1)
        def _():
            drain((slot + 1) % n_buf)
            drain(ahead)

    for phase in range(n_buf):
        pl.when(i % n_buf == phase)(functools.partial(step, phase))


def _combine_call(dests, x1, gates, g_final, expert_out):
    t = x1.shape[0]
    tt = ROW_TILE // 2
    last = t // tt - 1
    tile = lambda ahead: pl.BlockSpec((tt,), lambda i: (jnp.minimum(i + ahead, last),), memory_space=pltpu.SMEM)
    buf = pltpu.VMEM((TOP_K, tt * ROW_SUBLANES, LANES), F32)
    return pl.pallas_call(
        _combine_kernel,
        grid=(t // tt,),
        in_specs=[
            tile(0), tile(0), tile(1), tile(1), tile(2), tile(2),
            pl.BlockSpec((tt, D_MODEL), lambda i: (i, 0)),
            pl.BlockSpec((tt, TOP_K), lambda i: (i, 0)),
            pl.BlockSpec((1, D_MODEL), lambda i: (0, 0)),
            pl.BlockSpec(memory_space=pl.ANY),
        ],
        out_specs=pl.BlockSpec((tt, D_MODEL), lambda i: (i, 0)),
        out_shape=jax.ShapeDtypeStruct((t, D_MODEL), F32),
        scratch_shapes=[buf, buf, buf, pltpu.SemaphoreType.DMA((3,))],
        compiler_params=_cparams(("arbitrary",), disable_bounds_checks=True),
        name="combine",
    )(*dests, *dests, *dests, x1, gates, g_final, expert_out)


def _rope_tables(seq):
    half = B_QK_ROPE // 2
    inv_freq = ROPE_THETA ** (-(jnp.arange(half, dtype=F32) / half))
    ang = jnp.arange(seq, dtype=F32)[:, None] * inv_freq[None, :]
    cos, sin = jnp.cos(ang), jnp.sin(ang)
    z = jnp.zeros((seq, B_QK_NOPE), F32)
    z2 = jnp.zeros((seq, B_QK_ROPE), F32)
    return (jnp.concatenate([z, cos, cos, z2], axis=1), jnp.concatenate([z, -sin, sin, z2], axis=1))


def _swap_halves(w):
    half = w.shape[-1] // 2
    return jnp.concatenate([w[..., half:], w[..., :half]], axis=-1)


def _layout_weights(w_in, w_q_up, w_kv_up):
    d = w_in.shape[0]
    w_kr = w_in[:, 3 * A_WIDTH + B_Q_LORA + B_KV_LORA:]
    w_in_l = jnp.concatenate(
        [w_in[:, :A_WIDTH] * (A_HEAD_DIM ** -0.5), w_in[:, A_WIDTH:3 * A_WIDTH + B_Q_LORA + B_KV_LORA],
         jnp.zeros((d, B_QK_NOPE), F32), w_kr, _swap_halves(w_kr)], axis=1).astype(BF16)
    wq = w_q_up.reshape(B_Q_LORA, B_HEADS, B_QK_NOPE + B_QK_ROPE)
    wq_l = jnp.concatenate([wq, _swap_halves(wq[..., B_QK_NOPE:])], axis=-1)
    wq_l = wq_l.reshape(B_Q_LORA, B_HEADS * LANES).astype(BF16)
    wkv = w_kv_up.reshape(B_KV_LORA, B_HEADS, B_QK_NOPE + B_V_DIM)
    wkb = jnp.concatenate([wkv[..., :B_QK_NOPE], jnp.zeros_like(wkv[..., :B_QK_NOPE])], axis=-1)
    wkb = wkb.reshape(B_KV_LORA, B_HEADS * LANES).astype(BF16)
    wvb = wkv[..., B_QK_NOPE:].reshape(B_KV_LORA, B_WIDTH).astype(BF16)
    return w_in_l, wq_l, wkb, wvb


def _block_plan(counts, n_blocks):
    padded = (counts + MOE_BLK - 1) // MOE_BLK * MOE_BLK
    ends = jnp.cumsum(padded)
    starts = ends - padded
    first_row = jnp.arange(n_blocks, dtype=I32) * MOE_BLK
    expert = jnp.minimum(jnp.sum(ends[None, :] <= first_row[:, None], axis=1), N_EXPERTS - 1).astype(I32)
    new = jnp.concatenate([jnp.ones((1,), I32), (expert[1:] != expert[:-1]).astype(I32)])
    valid = jnp.clip((starts + counts)[expert] - first_row, 0, MOE_BLK).astype(I32)
    return starts.astype(I32), expert, new, valid


def kernel(x, g_attn_norm, w_in, rel_bias, g_q_latent, w_q_up, g_kv_latent, w_kv_up, g_out_a, g_out_b, w_out,
           g_ffn_norm, w_router_group, b_router_group, w_router_expert, b_router_expert, w_gate, w_up, w_down,
           g_final):
    batch, seq, d = x.shape
    t = batch * seq
    assert g_attn_norm.shape[0] == 1 and d == D_MODEL and seq % ROW_TILE == 0
    cos_t, sin_t = _rope_tables(seq)
    unit = ROW_TILE // MIX_UNITS
    tri = jnp.triu(jnp.ones((unit, unit), F32), 1).astype(BF16)
    spread = (jnp.arange(2 * LANES)[:, None] % LANES == jnp.arange(A_WIDTH)[None, :] // A_HEAD_DIM).astype(BF16)
    n_blocks = t * TOP_K // MOE_BLK + N_EXPERTS
    x2 = x.reshape(t, d)
    row = lambda v: v.reshape(1, -1)

    w_in_l, wq_l, wkb_l, wvb_l = _layout_weights(w_in[0], w_q_up[0], w_kv_up[0])
    *qkv_a, qb, kb, vb = _proj_call(x2, row(g_attn_norm[0]), w_in_l, row(g_q_latent[0]), wq_l,
                                    row(g_kv_latent[0]), wkb_l, wvb_l, cos_t, sin_t, seq)
    oas, lses = [], []
    for pi, (window, dilation) in enumerate(DILATED_PATTERNS):
        bias = _dilated_bias(rel_bias, seq, dilation, window // (2 * dilation))
        o_p, lse_p = _dilated_call(qkv_a[pi], bias, batch, seq, dilation)
        oas.append(o_p)
        lses.append(lse_p)
    ob = _mla_call(qb, kb, vb, batch, seq)

    pad = ROUTER_ROWS - N_EXPERTS - N_GROUPS
    w_router = jnp.concatenate([w_router_expert[0], w_router_group[0], jnp.zeros((d, pad), F32)], axis=1).T
    b_router = jnp.concatenate([b_router_expert[0], b_router_group[0], jnp.zeros((pad,), F32)])
    x1, hp, idx, gates, cnt = _mix_call(oas, lses, ob, x2, row(g_out_a[0]), row(g_out_b[0]), w_out[0].astype(BF16),
                                        row(g_ffn_norm[0]), w_router, b_router.reshape(-1, 1), tri, spread)
    pstart, block_expert, block_new, block_valid = _block_plan(cnt[:N_EXPERTS, 0], n_blocks)
    pstart_col = jnp.concatenate([pstart, jnp.zeros((ROUTER_ROWS - N_EXPERTS,), I32)]).reshape(-1, 1)
    dest = _dest_call(idx, pstart_col)
    dests = (dest[0], dest[1])
    buf = _dispatch_call(block_valid, dests, hp)
    expert_out = _expert_call(block_expert, block_new, block_valid, buf, w_gate, w_up, w_down)
    return _combine_call(dests, x1, gates[:TOP_K].T, row(g_final), expert_out).reshape(batch, seq, d)
```

```python
import functools
import math

import numpy as np
import jax
import jax.numpy as jnp
from jax import lax
from jax.experimental import pallas as pl
from jax.experimental.pallas import tpu as pltpu

F32 = jnp.float32
BF16 = jnp.bfloat16
I32 = jnp.int32
U32 = jnp.uint32

D_MODEL = 1024
EPS = 1e-6
NEG_INF = -1e30
LANES = 128
ROW_SUBLANES = D_MODEL // LANES
PACKED_SUBLANES = ROW_SUBLANES // 2

A_HEADS = 8
A_HEAD_DIM = 64
A_WIDTH = 512
A_QKV_WIDTH = 3 * A_WIDTH
DILATED_PATTERNS = ((128, 1), (512, 4), (2048, 16))
REL_BUCKETS = 32
REL_MAX_DISTANCE = 1024
A_QB = 128

B_HEADS = 8
B_Q_LORA = 256
B_KV_LORA = 128
B_QK_NOPE = 64
B_QK_ROPE = 32
B_V_DIM = 64
B_WIDTH = 512
ROPE_THETA = 10000.0
B_SCALE = (B_QK_NOPE + B_QK_ROPE) ** -0.5
B_QB = 512
B_SUB = 256
LOG2E = math.log2(math.e)

N_GROUPS = 4
EXPERTS_PER_GROUP = 8
N_EXPERTS = 32
TOP_K = 2
EXPERT_FF = 256
MOE_BLK = 512
ROUTER_ROWS = 40
MIX_UNITS = 2

ROW_TILE = 512
ISSUE_GROUP = 8
COMBINE_CHUNK = 32
PROJ_COLS = 2048

_NT = (((1,), (1,)), ((), ()))


def _cparams(semantics, vmem_mb=48, **kw):
    return pltpu.CompilerParams(dimension_semantics=semantics,
                                vmem_limit_bytes=vmem_mb * 1024 * 1024, **kw)


def _rms(x, g):
    return x * lax.rsqrt(jnp.mean(x * x, axis=-1, keepdims=True) + EPS) * g


def _lane_iota(rows=1):
    return lax.broadcasted_iota(I32, (rows, LANES), 1)


def _row_tile(r, sublanes=ROW_SUBLANES):
    return pl.ds(pl.multiple_of(r * sublanes, sublanes), sublanes)


def _proj_kernel(x_ref, g_ref, win_ref, gq_ref, wq_ref, gkv_ref, wkb_ref, wvb_ref, cos_ref, sin_ref, *refs):
    out1, out4, out16 = refs[:3]
    qb_ref, kb_ref, vb_ref, slab_ref, slab4_ref = refs[3:]
    tm = x_ref.shape[0]
    h = _rms(x_ref[...], g_ref[...]).astype(BF16)
    lo = _lane_iota() < A_HEAD_DIM
    n_slabs = A_QKV_WIDTH // LANES
    parts = []
    for s in range(0, n_slabs, 2):
        part = jnp.dot(h, win_ref[:, LANES * s:LANES * (s + 2)], preferred_element_type=F32)
        if LANES * s < A_WIDTH:
            part = part * LOG2E
        slab_ref[s] = part[:, :LANES]
        slab_ref[s + 1] = part[:, LANES:]
        parts.append(part)
    out1[0, 0] = jnp.concatenate(parts, axis=1).astype(BF16)
    lat = jnp.dot(h, win_ref[:, A_QKV_WIDTH:], preferred_element_type=F32)
    cos = cos_ref[...]
    sin = sin_ref[...]
    n4 = tm // 4

    cq = _rms(lat[:, :B_Q_LORA], gq_ref[...]).astype(BF16)
    q = jnp.dot(cq, wq_ref[...], preferred_element_type=F32)
    q_mul = (cos + jnp.where(lo, 1.0, 0.0)) * (B_SCALE * LOG2E)
    q_rot = sin * (B_SCALE * LOG2E)
    qb_ref[...] = jnp.concatenate(
        [q[:, LANES * hd:LANES * (hd + 1)] * q_mul + pltpu.roll(q[:, LANES * hd:LANES * (hd + 1)], 96, 1) * q_rot
         for hd in range(B_HEADS)], axis=1).astype(BF16)

    ckv = _rms(lat[:, B_Q_LORA:B_Q_LORA + B_KV_LORA], gkv_ref[...]).astype(BF16)
    kr = lat[:, B_Q_LORA + B_KV_LORA:]
    kr = kr * cos + pltpu.roll(kr, 96, 1) * sin
    kn = jnp.dot(ckv, wkb_ref[...], preferred_element_type=F32)
    kb_ref[...] = jnp.concatenate([kn[:, LANES * hd:LANES * (hd + 1)] + kr for hd in range(B_HEADS)],
                                  axis=1).astype(BF16)
    vb_ref[...] = jnp.dot(ckv, wvb_ref[...], preferred_element_type=F32).astype(BF16)

    for c4 in range(4):
        pieces = [slab_ref[s, pl.ds(c4, n4, stride=4), :] for s in range(n_slabs)]
        for s, piece in enumerate(pieces):
            slab4_ref[s, c4] = piece
        out4[0, c4] = jnp.concatenate(pieces, axis=1).astype(BF16)
    for c4 in range(4):
        for j in range(4):
            out16[0, c4 + 4 * j] = jnp.concatenate(
                [slab4_ref[s, c4, pl.ds(j, n4 // 4, stride=4), :] for s in range(n_slabs)], axis=1).astype(BF16)


def _proj_call(x2, g_attn, w_in, g_q, w_q, g_kv, w_kb, w_vb, cos_t, sin_t, seq):
    t = x2.shape[0]
    tm = ROW_TILE
    nseq = seq // tm
    row = lambda i: (i, 0)
    const = lambda i: (0, 0)
    pos = lambda i: (i % nseq, 0)
    out = lambda w: jax.ShapeDtypeStruct((t, w), BF16)
    a_specs, a_shapes = [], []
    for _, r in DILATED_PATTERNS:
        a_specs.append(pl.BlockSpec((1, r, tm // r, A_QKV_WIDTH), lambda i: (i // nseq, 0, i % nseq, 0)))
        a_shapes.append(jax.ShapeDtypeStruct((t // seq, r, seq // r, A_QKV_WIDTH), BF16))
    return pl.pallas_call(
        _proj_kernel,
        grid=(t // tm,),
        in_specs=[
            pl.BlockSpec((tm, D_MODEL), row),
            pl.BlockSpec((1, D_MODEL), const),
            pl.BlockSpec((D_MODEL, PROJ_COLS), const),
            pl.BlockSpec((1, B_Q_LORA), const),
            pl.BlockSpec((B_Q_LORA, B_HEADS * LANES), const),
            pl.BlockSpec((1, B_KV_LORA), const),
            pl.BlockSpec((B_KV_LORA, B_HEADS * LANES), const),
            pl.BlockSpec((B_KV_LORA, B_WIDTH), const),
            pl.BlockSpec((tm, LANES), pos),
            pl.BlockSpec((tm, LANES), pos),
        ],
        out_specs=a_specs + [
            pl.BlockSpec((tm, B_HEADS * LANES), row),
            pl.BlockSpec((tm, B_HEADS * LANES), row),
            pl.BlockSpec((tm, B_WIDTH), row),
        ],
        out_shape=a_shapes + [out(B_HEADS * LANES), out(B_HEADS * LANES), out(B_WIDTH)],
        scratch_shapes=[pltpu.VMEM((A_QKV_WIDTH // LANES, tm, LANES), F32),
                        pltpu.VMEM((A_QKV_WIDTH // LANES, 4, tm // 4, LANES), F32)],
        compiler_params=_cparams(("parallel",)),
        name="proj",
    )(x2, g_attn, w_in, g_q, w_q, g_kv, w_kb, w_vb, cos_t, sin_t)


def _pack_bf16_pair(a, b):
    a_bits = lax.bitcast_convert_type(a.astype(BF16).astype(F32), U32) >> 16
    b_bits = lax.bitcast_convert_type(b.astype(BF16).astype(F32), U32) & jnp.uint32(0xFFFF0000)
    return a_bits | b_bits


def _unpack_bf16_pair(w):
    return (lax.bitcast_convert_type(w << 16, F32), lax.bitcast_convert_type(w & jnp.uint32(0xFFFF0000), F32))


def _dilated_kernel(qkv_ref, bias_ref, o_ref, lse_ref, *, seq_len, dilation, key_width, group):
    nblk = seq_len // A_QB
    lane = _lane_iota()
    lo = lane < A_HEAD_DIM
    first_class = pl.program_id(1) * group
    pairs = A_HEADS // 2

    def block(it, carry):
        c = it // nblk
        if nblk == 1:
            q0, ks, var = 0, 0, 0
        else:
            n = it % nblk
            q0 = pl.multiple_of(n * A_QB, A_QB)
            ks = pl.multiple_of(jnp.clip(q0 - 64, 0, seq_len - key_width), 64)
            var = jnp.where(n == 0, 0, jnp.where(n == nblk - 1, 2, 1))
        rows = pl.ds(q0, A_QB)
        keys = pl.ds(ks, key_width)
        if dilation == 1:
            out_rows = rows
        else:
            out_rows = pl.ds(first_class + c + dilation * q0, A_QB, stride=dilation)
        q_tiles = [qkv_ref[0, c, rows, LANES * p:LANES * (p + 1)] for p in range(pairs)]
        k_tiles = [qkv_ref[0, c, keys, A_WIDTH + LANES * p:A_WIDTH + LANES * (p + 1)] for p in range(pairs)]
        v_tiles = [qkv_ref[0, c, keys, 2 * A_WIDTH + LANES * p:2 * A_WIDTH + LANES * (p + 1)] for p in range(pairs)]
        zero = jnp.zeros((), BF16)
        k_heads = [jnp.where(lo, k_tiles[hd // 2], zero) if hd % 2 == 0 else jnp.where(lo, zero, k_tiles[hd // 2])
                   for hd in range(A_HEADS)]
        scores = [lax.dot_general(q_tiles[hd // 2], k_heads[hd], _NT, preferred_element_type=F32) + bias_ref[var, hd]
                  for hd in range(A_HEADS)]
        maxes = [jnp.max(s, axis=-1, keepdims=True) for s in scores]
        probs = [jnp.exp2(s - m) for s, m in zip(scores, maxes)]
        dens = [jnp.sum(pr, axis=-1, keepdims=True) for pr in probs]
        pvs = [jnp.dot(pr.astype(BF16), v_tiles[hd // 2], preferred_element_type=F32) for hd, pr in enumerate(probs)]
        outs = []
        for p in range(pairs):
            h0, h1 = 2 * p, 2 * p + 1
            outs.append(jnp.where(lo, pvs[h0] * (1.0 / dens[h0]), pvs[h1] * (1.0 / dens[h1])))
        lse = jnp.zeros((A_QB, LANES), F32)
        for hd in range(A_HEADS):
            lse = jnp.where(lane == hd, maxes[hd] + jnp.log2(dens[hd]), lse)
        lse_ref[0, out_rows, :] = lse
        for j in range(pairs // 2):
            o_ref[0, j, out_rows, :] = _pack_bf16_pair(outs[2 * j], outs[2 * j + 1])
        return carry

    lax.fori_loop(0, group * nblk, block, 0, unroll=4)


def _dilated_call(qkv, bias, batch, seq, dilation):
    r = dilation
    sl = seq // r
    kw = min(2 * A_QB, sl)
    group = r
    pairs = A_HEADS // 2
    return pl.pallas_call(
        functools.partial(_dilated_kernel, seq_len=sl, dilation=r, key_width=kw, group=group),
        grid=(batch, r // group),
        in_specs=[pl.BlockSpec((1, group, sl, A_QKV_WIDTH), lambda b, c: (b, c, 0, 0)),
                  pl.BlockSpec(bias.shape, lambda b, c: (0, 0, 0, 0))],
        out_specs=[pl.BlockSpec((1, pairs // 2, seq, LANES), lambda b, c: (b, 0, 0, 0)),
                   pl.BlockSpec((1, seq, LANES), lambda b, c: (b, 0, 0))],
        out_shape=[jax.ShapeDtypeStruct((batch, pairs // 2, seq, LANES), U32),
                   jax.ShapeDtypeStruct((batch, seq, LANES), F32)],
        compiler_params=_cparams(("parallel", "arbitrary")),
        name=f"dilated_r{r}",
    )(qkv, bias)


def _t5_bucket(rel):
    half = REL_BUCKETS // 2
    max_exact = half // 2
    n = np.abs(rel)
    large = max_exact + (np.log(np.maximum(n, 1) / max_exact)
                         / math.log(REL_MAX_DISTANCE / max_exact) * (half - max_exact)).astype(np.int32)
    large = np.minimum(large, half - 1)
    return (np.where(rel > 0, half, 0) + np.where(n < max_exact, n, large)).astype(np.int32)


def _dilated_bias(rel_bias, seq, dilation, half_steps):
    sl = seq // dilation
    kw = min(2 * A_QB, sl)
    offsets = [0] if sl == kw else [0, -half_steps, A_QB - kw]
    rel = np.stack([np.arange(kw)[None, :] + off - np.arange(A_QB)[:, None] for off in offsets])
    valid = np.abs(rel) <= half_steps
    bucket = np.where(valid, _t5_bucket(rel * dilation), REL_BUCKETS).astype(np.int32)
    onehot = (jnp.asarray(bucket)[..., None] == jnp.arange(REL_BUCKETS + 1, dtype=I32)).astype(F32)
    table = jnp.concatenate([rel_bias.astype(F32), jnp.full((1, A_HEADS), NEG_INF, F32)], axis=0)
    return jnp.einsum("vqkb,bh->vhqk", onehot, table * LOG2E, precision=lax.Precision.HIGHEST)


def _mla_kernel(q_ref, k_ref, v_ref, o_ref, v1_ref):
    lo = _lane_iota() < B_V_DIM
    sub = B_QB // B_SUB
    tiles = [slice(0, LANES), slice(LANES, 2 * LANES)]
    v1_ref[:, :LANES] = v_ref[0]
    v1_ref[:, LANES:] = jnp.ones((v_ref.shape[1], LANES), BF16)

    units = [(j, half) for j in range(sub) for half in range(2)]

    def rows(i):
        return [pl.ds(i * B_QB + B_SUB * j, B_SUB) for j in range(sub)]

    def scores(i):
        return [lax.dot_general(q_ref[0, rows(i)[j], tiles[half]], k_ref[0, :, tiles[half]], _NT,
                                preferred_element_type=F32) for j, half in units]

    def finish(i, sc):
        maxes = [jnp.max(s, axis=-1, keepdims=True) for s in sc]
        probs = [jnp.exp2(s - m) for s, m in zip(sc, maxes)]
        pvs = [jnp.dot(pr.astype(BF16), v1_ref[...], preferred_element_type=F32) for pr in probs]
        outs = [pv[:, :LANES] * (1.0 / pv[:, LANES:]) for pv in pvs]
        for j in range(sub):
            o_ref[0, rows(i)[j], :] = jnp.where(lo, outs[2 * j], outs[2 * j + 1]).astype(BF16)

    n_blocks = q_ref.shape[1] // B_QB
    sc = scores(0)
    for i in range(n_blocks):
        sc_next = scores(i + 1) if i + 1 < n_blocks else None
        finish(i, sc)
        sc = sc_next


def _mla_call(qb, kb, vb, batch, seq):
    qb = qb.reshape(batch, seq, B_HEADS * LANES)
    kb = kb.reshape(batch, seq, B_HEADS * LANES)
    vb = vb.reshape(batch, seq, B_WIDTH)
    pair = lambda w: pl.BlockSpec((1, seq, w), lambda b, p: (b, 0, p))
    out = pl.pallas_call(
        _mla_kernel,
        grid=(batch, B_HEADS // 2),
        in_specs=[pair(2 * LANES), pair(2 * LANES), pair(LANES)],
        out_specs=pair(LANES),
        out_shape=jax.ShapeDtypeStruct((batch, seq, B_WIDTH), BF16),
        scratch_shapes=[pltpu.VMEM((seq, 2 * LANES), BF16)],
        compiler_params=_cparams(("parallel", "parallel")),
        name="mla",
    )(qb, kb, vb)
    return out.reshape(batch * seq, B_WIDTH)


def _merge_patterns(o_refs, lse_refs, spread_ref, rows):
    lses = [r[0, rows, :] for r in lse_refs]
    top = functools.reduce(jnp.maximum, lses)
    es = [jnp.exp2(l - top) for l in lses]
    inv = 1.0 / functools.reduce(jnp.add, es)
    spread = spread_ref[...]
    weights = []
    for e in es:
        w = e * inv
        w_hi = w.astype(BF16)
        w_lo = (w - w_hi.astype(F32)).astype(BF16)
        weights.append(jnp.dot(jnp.concatenate([w_hi, w_lo], axis=1), spread, preferred_element_type=F32))
    tiles = []
    for j in range(A_HEADS // 4):
        outs = [_unpack_bf16_pair(r[0, j, rows, :]) for r in o_refs]
        for half in range(2):
            tile = slice(LANES * (2 * j + half), LANES * (2 * j + half + 1))
            tiles.append(functools.reduce(jnp.add, [w[:, tile] * o[half] for w, o in zip(weights, outs)]))
    return jnp.concatenate(tiles, axis=1)


def _mix_kernel(o1_ref, o4_ref, o16_ref, l1_ref, l4_ref, l16_ref, ob_ref, x_ref, ga_ref, gb_ref, wo_ref,
                gf_ref, wr_ref, br_ref, tri_ref, spread_ref, x1_ref, hp_ref, idx_ref, gate_ref, cnt_ref, carry_ref):
    i = pl.program_id(0)
    unit = x_ref.shape[0] // MIX_UNITS
    units = [slice(unit * u, unit * (u + 1)) for u in range(MIX_UNITS)]

    @pl.when(i == 0)
    def _():
        carry_ref[...] = jnp.zeros_like(carry_ref)

    def residual_stream(u):
        rows = units[u]
        oa = _merge_patterns((o1_ref, o4_ref, o16_ref), (l1_ref, l4_ref, l16_ref), spread_ref, rows)
        a = _rms(oa, ga_ref[...]).astype(BF16)
        b = _rms(ob_ref[rows, :].astype(F32), gb_ref[...]).astype(BF16)
        mix = (jnp.dot(a, wo_ref[0:A_WIDTH, :], preferred_element_type=F32)
               + jnp.dot(b, wo_ref[A_WIDTH:, :], preferred_element_type=F32))
        x1 = x_ref[rows, :] + mix
        x1_ref[rows, :] = x1
        h2 = _rms(x1, gf_ref[...])
        half = D_MODEL // 2
        for c in range(PACKED_SUBLANES):
            hp_ref[pl.ds(PACKED_SUBLANES * unit * u + c, unit, stride=PACKED_SUBLANES), :] = _pack_bf16_pair(
                h2[:, LANES * c:LANES * (c + 1)], h2[:, half + LANES * c:half + LANES * (c + 1)])
        return h2

    wr = wr_ref[...]
    wr_hi = wr.astype(BF16)
    wr_lo = (wr - wr_hi.astype(F32)).astype(BF16)
    wr_both = jnp.concatenate([wr_hi, wr_lo], axis=0)

    def logits(h2):
        h_hi = h2.astype(BF16)
        h_lo = (h2 - h_hi.astype(F32)).astype(BF16)
        lg_hi = lax.dot_general(wr_both, h_hi, _NT, preferred_element_type=F32)
        return (lg_hi[:ROUTER_ROWS] + lg_hi[ROUTER_ROWS:]
                + lax.dot_general(wr_hi, h_lo, _NT, preferred_element_type=F32) + br_ref[...])

    def route(lg):
        row = lax.broadcasted_iota(I32, lg.shape, 0)
        is_g = (row >= N_EXPERTS) & (row < N_EXPERTS + N_GROUPS)
        gl = jnp.where(is_g, lg, NEG_INF)
        ge = jnp.exp(gl - jnp.max(gl, axis=0, keepdims=True))
        gp = ge / jnp.sum(ge, axis=0, keepdims=True)
        g_gate = jnp.max(gp, axis=0, keepdims=True)
        g_idx = jnp.min(jnp.where(is_g & (gp == g_gate), row - N_EXPERTS, LANES), axis=0, keepdims=True)
        sel = (row // EXPERTS_PER_GROUP) == g_idx
        el = jnp.where(sel, lg, NEG_INF)
        ee = jnp.exp(el - jnp.max(el, axis=0, keepdims=True))
        ep = jnp.where(sel, ee / jnp.sum(ee, axis=0, keepdims=True), -1.0)
        p1 = jnp.max(ep, axis=0, keepdims=True)
        i1 = jnp.min(jnp.where(ep == p1, row, LANES), axis=0, keepdims=True)
        ep2 = jnp.where(row == i1, -1.0, ep)
        p2 = jnp.max(ep2, axis=0, keepdims=True)
        i2 = jnp.min(jnp.where(sel & (ep2 == p2) & (row != i1), row, LANES), axis=0, keepdims=True)
        den = p1 + p2
        return i1, i2, g_gate * p1 / den, g_gate * p2 / den, row == i1, row == i2

    h2s = [residual_stream(u) for u in range(MIX_UNITS)]
    lgs = [logits(h2) for h2 in h2s]
    routes = [route(lg) for lg in lgs]

    carry = carry_ref[...]
    row8 = lax.broadcasted_iota(I32, (idx_ref.shape[0], unit), 0)
    for u, (i1, i2, g1, g2, hit1, hit2) in enumerate(routes):
        onehot = jnp.where(hit1 | hit2, 1.0, 0.0)
        before = jnp.dot(onehot.astype(BF16), tri_ref[...], preferred_element_type=F32) + carry
        r1 = jnp.sum(jnp.where(hit1, before, 0.0), axis=0, keepdims=True).astype(I32)
        r2 = jnp.sum(jnp.where(hit2, before, 0.0), axis=0, keepdims=True).astype(I32)
        carry = carry + jnp.sum(onehot, axis=1, keepdims=True)
        idx_ref[:, units[u]] = jnp.where(row8 == 0, i1, jnp.where(row8 == 1, i2,
                                         jnp.where(row8 == 2, r1, jnp.where(row8 == 3, r2, 0))))
        gate_ref[:, units[u]] = jnp.where(row8 == 0, g1, jnp.where(row8 == 1, g2, 0.0))
    carry_ref[...] = carry

    @pl.when(i == pl.num_programs(0) - 1)
    def _():
        cnt_ref[...] = jnp.broadcast_to(carry_ref[...], cnt_ref.shape).astype(I32)


def _mix_call(oas, lses, ob, x2, g_a, g_b, w_out, g_ffn, w_router, b_router, tri, spread):
    t = x2.shape[0]
    tm = ROW_TILE
    nseq = oas[0].shape[2] // tm
    row = lambda i: (i, 0)
    const = lambda i: (0, 0)
    slab = lambda n: pl.BlockSpec((1, n, tm, LANES), lambda i: (i // nseq, 0, i % nseq, 0))
    o_slab = slab(A_HEADS // 4)
    lse_slab = pl.BlockSpec((1, tm, LANES), lambda i: (i // nseq, i % nseq, 0))
    return pl.pallas_call(
        _mix_kernel,
        grid=(t // tm,),
        in_specs=[
            o_slab, o_slab, o_slab, lse_slab, lse_slab, lse_slab,
            pl.BlockSpec((tm, B_WIDTH), row),
            pl.BlockSpec((tm, D_MODEL), row),
            pl.BlockSpec((1, A_WIDTH), const),
            pl.BlockSpec((1, B_WIDTH), const),
            pl.BlockSpec((D_MODEL, D_MODEL), const),
            pl.BlockSpec((1, D_MODEL), const),
            pl.BlockSpec((ROUTER_ROWS, D_MODEL), const),
            pl.BlockSpec((ROUTER_ROWS, 1), const),
            pl.BlockSpec((tm // MIX_UNITS, tm // MIX_UNITS), const),
            pl.BlockSpec((2 * LANES, A_WIDTH), const),
        ],
        out_specs=[
            pl.BlockSpec((tm, D_MODEL), row),
            pl.BlockSpec((tm * PACKED_SUBLANES, LANES), row),
            pl.BlockSpec((8, tm), lambda i: (0, i)),
            pl.BlockSpec((8, tm), lambda i: (0, i)),
            pl.BlockSpec((ROUTER_ROWS, LANES), const),
        ],
        out_shape=[
            jax.ShapeDtypeStruct((t, D_MODEL), F32),
            jax.ShapeDtypeStruct((t * PACKED_SUBLANES, LANES), U32),
            jax.ShapeDtypeStruct((8, t), I32),
            jax.ShapeDtypeStruct((8, t), F32),
            jax.ShapeDtypeStruct((ROUTER_ROWS, LANES), I32),
        ],
        scratch_shapes=[pltpu.VMEM((ROUTER_ROWS, 1), F32)],
        compiler_params=_cparams(("arbitrary",)),
        name="mix_router",
    )(*oas, *lses, ob, x2, g_a, g_b, w_out, g_ffn, w_router, b_router, tri, spread)


def _dest_kernel(idx_ref, pstart_ref, dest_ref):
    idx = idx_ref[...]
    row = lax.broadcasted_iota(I32, (ROUTER_ROWS, idx.shape[1]), 0)
    ps = pstart_ref[...]

    def slot(k):
        return jnp.sum(jnp.where(row == idx[k:k + 1, :], ps, 0), axis=0, keepdims=True) + idx[2 + k:3 + k, :]

    row8 = lax.broadcasted_iota(I32, idx.shape, 0)
    dest_ref[...] = jnp.where(row8 == 0, slot(0), jnp.where(row8 == 1, slot(1), 0))


def _dest_call(idx, pstart):
    t = idx.shape[1]
    tm = 4 * ROW_TILE
    return pl.pallas_call(
        _dest_kernel,
        grid=(t // tm,),
        in_specs=[pl.BlockSpec((8, tm), lambda i: (0, i)), pl.BlockSpec((ROUTER_ROWS, 1), lambda i: (0, 0))],
        out_specs=pl.BlockSpec((8, tm), lambda i: (0, i)),
        out_shape=jax.ShapeDtypeStruct((8, t), I32),
        compiler_params=_cparams(("parallel",)),
        name="dest_rows",
    )(idx, pstart)


def _dispatch_kernel(valid_ref, d0_ref, d1_ref, h_ref, buf_ref, zero_ref, sem, pad_sem):
    i = pl.program_id(0)
    tt = h_ref.shape[0] // PACKED_SUBLANES
    n_blocks = valid_ref.shape[0]

    def for_padded_blocks(fn):
        def body(j, c):
            taken = valid_ref[j]
            even = (taken + 1) // 2 * 2

            @pl.when(even < MOE_BLK)
            def _():
                n_pad = pl.multiple_of((MOE_BLK - even) * PACKED_SUBLANES, ROW_SUBLANES)
                first = pl.multiple_of((j * MOE_BLK + even) * PACKED_SUBLANES, ROW_SUBLANES)
                fn(pltpu.make_async_copy(zero_ref.at[pl.ds(0, n_pad)], buf_ref.at[pl.ds(first, n_pad)], pad_sem))

            @pl.when(even != taken)
            def _():
                fn(pltpu.make_async_copy(zero_ref.at[pl.ds(0, PACKED_SUBLANES)],
                                         buf_ref.at[_row_tile(j * MOE_BLK + taken, PACKED_SUBLANES)], pad_sem))
            return c
        lax.fori_loop(0, n_blocks, body, 0)

    @pl.when(i == 0)
    def _():
        zero_ref[...] = jnp.zeros_like(zero_ref)
        for_padded_blocks(lambda cp: cp.start())

    def issue(g, c):
        base = pl.multiple_of(g * ISSUE_GROUP, ISSUE_GROUP)
        for j in range(ISSUE_GROUP):
            for prio, d_ref in enumerate((d0_ref, d1_ref)):
                pltpu.make_async_copy(h_ref.at[_row_tile(base + j, PACKED_SUBLANES)],
                                      buf_ref.at[_row_tile(d_ref[base + j], PACKED_SUBLANES)],
                                      sem).start(priority=prio)
        return c

    lax.fori_loop(0, tt // ISSUE_GROUP, issue, 0)
    for k in range(TOP_K):
        pltpu.make_async_copy(h_ref, buf_ref.at[pl.ds(0, tt * PACKED_SUBLANES)], sem).wait()

    @pl.when(i == pl.num_programs(0) - 1)
    def _():
        for_padded_blocks(lambda cp: cp.wait())


def _dispatch_call(block_valid, dests, hp):
    t = hp.shape[0] // PACKED_SUBLANES
    tt = 2 * ROW_TILE
    n_rows = block_valid.shape[0] * MOE_BLK
    return pl.pallas_call(
        _dispatch_kernel,
        grid_spec=pltpu.PrefetchScalarGridSpec(
            num_scalar_prefetch=1,
            grid=(t // tt,),
            in_specs=[
                pl.BlockSpec((tt,), lambda i, va: (i,), memory_space=pltpu.SMEM),
                pl.BlockSpec((tt,), lambda i, va: (i,), memory_space=pltpu.SMEM),
                pl.BlockSpec((tt * PACKED_SUBLANES, LANES), lambda i, va: (i, 0)),
            ],
            out_specs=pl.BlockSpec(memory_space=pl.ANY),
            scratch_shapes=[pltpu.VMEM((MOE_BLK * PACKED_SUBLANES, LANES), U32),
                            pltpu.SemaphoreType.DMA(()), pltpu.SemaphoreType.DMA(())],
        ),
        out_shape=jax.ShapeDtypeStruct((n_rows * PACKED_SUBLANES, LANES), U32),
        compiler_params=_cparams(("arbitrary",), disable_bounds_checks=True, has_side_effects=True),
        name="dispatch",
    )(block_valid, *dests, hp)


def _expert_kernel(be_ref, new_ref, valid_ref, buf_ref, wg_ref, wu_ref, wd_ref, out_ref, wg_s, wu_s, wd_s):
    j = pl.program_id(0)
    del be_ref

    @pl.when(new_ref[j] == 1)
    def _():
        wg_s[...] = wg_ref[0, 0].astype(BF16)
        wu_s[...] = wu_ref[0, 0].astype(BF16)
        wd_s[...] = wd_ref[0, 0].astype(BF16)

    n_valid = valid_ref[j]

    @pl.when(n_valid > 0)
    def _():
        blk = buf_ref.shape[0] // PACKED_SUBLANES
        words = [_unpack_bf16_pair(buf_ref[pl.ds(c, blk, stride=PACKED_SUBLANES), :]) for c in range(PACKED_SUBLANES)]
        x = jnp.concatenate([w[0] for w in words] + [w[1] for w in words], axis=1).astype(BF16)
        cols = [pl.ds(c, blk, stride=ROW_SUBLANES) for c in range(ROW_SUBLANES)]
        g = jnp.dot(x, wg_s[...], preferred_element_type=F32)
        u = jnp.dot(x, wu_s[...], preferred_element_type=F32)
        hb = (g * jax.nn.sigmoid(g)) * u
        out = jnp.dot(hb.astype(BF16), wd_s[...], preferred_element_type=F32)
        for c, rows in enumerate(cols):
            out_ref[rows, :] = out[:, LANES * c:LANES * (c + 1)]

    @pl.when(n_valid == 0)
    def _():
        out_ref[...] = jnp.zeros_like(out_ref)


def _expert_call(block_expert, block_new, block_valid, buf, w_gate, w_up, w_down):
    nb = buf.shape[0] // (MOE_BLK * PACKED_SUBLANES)
    wsel = lambda j, be, nw, va: (0, be[j], 0, 0)
    rows = lambda sublanes: pl.BlockSpec((MOE_BLK * sublanes, LANES), lambda j, be, nw, va: (j, 0))
    return pl.pallas_call(
        _expert_kernel,
        grid_spec=pltpu.PrefetchScalarGridSpec(
            num_scalar_prefetch=3,
            grid=(nb,),
            in_specs=[
                rows(PACKED_SUBLANES),
                pl.BlockSpec((1, 1, D_MODEL, EXPERT_FF), wsel),
                pl.BlockSpec((1, 1, D_MODEL, EXPERT_FF), wsel),
                pl.BlockSpec((1, 1, EXPERT_FF, D_MODEL), wsel),
            ],
            out_specs=rows(ROW_SUBLANES),
            scratch_shapes=[pltpu.VMEM((D_MODEL, EXPERT_FF), BF16),
                            pltpu.VMEM((D_MODEL, EXPERT_FF), BF16),
                            pltpu.VMEM((EXPERT_FF, D_MODEL), BF16)],
        ),
        out_shape=jax.ShapeDtypeStruct((nb * MOE_BLK * ROW_SUBLANES, LANES), F32),
        compiler_params=_cparams(("arbitrary",)),
        name="experts",
    )(block_expert, block_new, block_valid, buf, w_gate, w_up, w_down)


def _combine_kernel(d0_ref, d1_ref, d0_next_ref, d1_next_ref, d0_ahead_ref, d1_ahead_ref, x1_ref, gate_ref,
                    gf_ref, eo_ref, o_ref, rows_a, rows_b, rows_c, sems):
    i = pl.program_id(0)
    tt = x1_ref.shape[0]
    bufs = (rows_a, rows_b, rows_c)
    n_buf = len(bufs)

    def start_row(d_refs, slot, r):
        for k, d_ref in enumerate(d_refs):
            pltpu.make_async_copy(eo_ref.at[_row_tile(d_ref[r])], bufs[slot].at[k, _row_tile(r)],
                                  sems.at[slot]).start(priority=k)

    def drain(slot):
        for k in range(TOP_K):
            pltpu.make_async_copy(eo_ref.at[pl.ds(0, tt * ROW_SUBLANES)], bufs[slot].at[k], sems.at[slot]).wait()

    @pl.when(i == 0)
    def _():
        def issue(g, c):
            base = pl.multiple_of(g * ISSUE_GROUP, ISSUE_GROUP)
            for j in range(ISSUE_GROUP):
                start_row((d0_ref, d1_ref), 0, base + j)
                start_row((d0_next_ref, d1_next_ref), 1, base + j)
            return c
        lax.fori_loop(0, tt // ISSUE_GROUP, issue, 0)

    def step(slot):
        ahead = (slot + 2) % n_buf
        drain(slot)
        gates = gate_ref[...].T
        gf = gf_ref[...]
        for ch in range(tt // COMBINE_CHUNK):
            r0 = ch * COMBINE_CHUNK
            for j in range(COMBINE_CHUNK):
                start_row((d0_ahead_ref, d1_ahead_ref), ahead, r0 + j)
            gate = gates[r0:r0 + COMBINE_CHUNK, :]

            def rows(k):
                return jnp.concatenate(
                    [bufs[slot][k, pl.ds(ROW_SUBLANES * r0 + c, COMBINE_CHUNK, stride=ROW_SUBLANES), :]
                     for c in range(ROW_SUBLANES)], axis=1)

            y = rows(0) * gate[:, 0:1] + rows(1) * gate[:, 1:2]
            o_ref[r0:r0 + COMBINE_CHUNK, :] = _rms(x1_ref[r0:r0 + COMBINE_CHUNK, :] + y, gf)

        @pl.when(i == pl.num_programs(0) - 1)
        def _():
            drain((slot + 1) % n_buf)
            drain(ahead)

    for phase in range(n_buf):
        pl.when(i % n_buf == phase)(functools.partial(step, phase))


def _combine_call(dests, x1, gates, g_final, expert_out):
    t = x1.shape[0]
    tt = ROW_TILE // 2
    last = t // tt - 1
    tile = lambda ahead: pl.BlockSpec((tt,), lambda i: (jnp.minimum(i + ahead, last),), memory_space=pltpu.SMEM)
    buf = pltpu.VMEM((TOP_K, tt * ROW_SUBLANES, LANES), F32)
    return pl.pallas_call(
        _combine_kernel,
        grid=(t // tt,),
        in_specs=[
            tile(0), tile(0), tile(1), tile(1), tile(2), tile(2),
            pl.BlockSpec((tt, D_MODEL), lambda i: (i, 0)),
            pl.BlockSpec((8, tt), lambda i: (0, i)),
            pl.BlockSpec((1, D_MODEL), lambda i: (0, 0)),
            pl.BlockSpec(memory_space=pl.ANY),
        ],
        out_specs=pl.BlockSpec((tt, D_MODEL), lambda i: (i, 0)),
        out_shape=jax.ShapeDtypeStruct((t, D_MODEL), F32),
        scratch_shapes=[buf, buf, buf, pltpu.SemaphoreType.DMA((3,))],
        compiler_params=_cparams(("arbitrary",), disable_bounds_checks=True),
        name="combine",
    )(*dests, *dests, *dests, x1, gates, g_final, expert_out)


def _rope_tables(seq):
    half = B_QK_ROPE // 2
    inv_freq = ROPE_THETA ** (-(jnp.arange(half, dtype=F32) / half))
    ang = jnp.arange(seq, dtype=F32)[:, None] * inv_freq[None, :]
    cos, sin = jnp.cos(ang), jnp.sin(ang)
    z = jnp.zeros((seq, B_QK_NOPE), F32)
    z2 = jnp.zeros((seq, B_QK_ROPE), F32)
    return (jnp.concatenate([z, cos, cos, z2], axis=1), jnp.concatenate([z, -sin, sin, z2], axis=1))


def _swap_halves(w):
    half = w.shape[-1] // 2
    return jnp.concatenate([w[..., half:], w[..., :half]], axis=-1)


def _layout_weights(w_in, w_q_up, w_kv_up):
    d = w_in.shape[0]
    w_kr = w_in[:, 3 * A_WIDTH + B_Q_LORA + B_KV_LORA:]
    w_in_l = jnp.concatenate(
        [w_in[:, :A_WIDTH] * (A_HEAD_DIM ** -0.5), w_in[:, A_WIDTH:3 * A_WIDTH + B_Q_LORA + B_KV_LORA],
         jnp.zeros((d, B_QK_NOPE), F32), w_kr, _swap_halves(w_kr)], axis=1).astype(BF16)
    wq = w_q_up.reshape(B_Q_LORA, B_HEADS, B_QK_NOPE + B_QK_ROPE)
    wq_l = jnp.concatenate([wq, _swap_halves(wq[..., B_QK_NOPE:])], axis=-1)
    wq_l = wq_l.reshape(B_Q_LORA, B_HEADS * LANES).astype(BF16)
    wkv = w_kv_up.reshape(B_KV_LORA, B_HEADS, B_QK_NOPE + B_V_DIM)
    wkb = jnp.concatenate([wkv[..., :B_QK_NOPE], jnp.zeros_like(wkv[..., :B_QK_NOPE])], axis=-1)
    wkb = wkb.reshape(B_KV_LORA, B_HEADS * LANES).astype(BF16)
    wvb = wkv[..., B_QK_NOPE:].reshape(B_KV_LORA, B_WIDTH).astype(BF16)
    return w_in_l, wq_l, wkb, wvb


def _block_plan(counts, n_blocks):
    padded = (counts + MOE_BLK - 1) // MOE_BLK * MOE_BLK
    ends = jnp.cumsum(padded)
    starts = ends - padded
    first_row = jnp.arange(n_blocks, dtype=I32) * MOE_BLK
    expert = jnp.minimum(jnp.sum(ends[None, :] <= first_row[:, None], axis=1), N_EXPERTS - 1).astype(I32)
    new = jnp.concatenate([jnp.ones((1,), I32), (expert[1:] != expert[:-1]).astype(I32)])
    taken_end = jnp.sum(jnp.where(expert[:, None] == jnp.arange(N_EXPERTS), (starts + counts)[None, :], 0), axis=1)
    valid = jnp.clip(taken_end - first_row, 0, MOE_BLK).astype(I32)
    return starts.astype(I32), expert, new, valid


def kernel(x, g_attn_norm, w_in, rel_bias, g_q_latent, w_q_up, g_kv_latent, w_kv_up, g_out_a, g_out_b, w_out,
           g_ffn_norm, w_router_group, b_router_group, w_router_expert, b_router_expert, w_gate, w_up, w_down,
           g_final):
    batch, seq, d = x.shape
    t = batch * seq
    assert g_attn_norm.shape[0] == 1 and d == D_MODEL and seq % ROW_TILE == 0
    cos_t, sin_t = _rope_tables(seq)
    unit = ROW_TILE // MIX_UNITS
    tri = jnp.triu(jnp.ones((unit, unit), F32), 1).astype(BF16)
    spread = (jnp.arange(2 * LANES)[:, None] % LANES == jnp.arange(A_WIDTH)[None, :] // A_HEAD_DIM).astype(BF16)
    n_blocks = t * TOP_K // MOE_BLK + N_EXPERTS
    x2 = x.reshape(t, d)
    row = lambda v: v.reshape(1, -1)

    w_in_l, wq_l, wkb_l, wvb_l = _layout_weights(w_in[0], w_q_up[0], w_kv_up[0])
    *qkv_a, qb, kb, vb = _proj_call(x2, row(g_attn_norm[0]), w_in_l, row(g_q_latent[0]), wq_l,
                                    row(g_kv_latent[0]), wkb_l, wvb_l, cos_t, sin_t, seq)
    oas, lses = [], []
    for pi, (window, dilation) in enumerate(DILATED_PATTERNS):
        bias = _dilated_bias(rel_bias, seq, dilation, window // (2 * dilation))
        o_p, lse_p = _dilated_call(qkv_a[pi], bias, batch, seq, dilation)
        oas.append(o_p)
        lses.append(lse_p)
    ob = _mla_call(qb, kb, vb, batch, seq)

    pad = ROUTER_ROWS - N_EXPERTS - N_GROUPS
    w_router = jnp.concatenate([w_router_expert[0], w_router_group[0], jnp.zeros((d, pad), F32)], axis=1).T
    b_router = jnp.concatenate([b_router_expert[0], b_router_group[0], jnp.zeros((pad,), F32)])
    x1, hp, idx, gates, cnt = _mix_call(oas, lses, ob, x2, row(g_out_a[0]), row(g_out_b[0]), w_out[0].astype(BF16),
                                        row(g_ffn_norm[0]), w_router, b_router.reshape(-1, 1), tri, spread)
    pstart, block_expert, block_new, block_valid = _block_plan(cnt[:N_EXPERTS, 0], n_blocks)
    pstart_col = jnp.concatenate([pstart, jnp.zeros((ROUTER_ROWS - N_EXPERTS,), I32)]).reshape(-1, 1)
    dest = _dest_call(idx, pstart_col)
    dests = (dest[0], dest[1])
    buf = _dispatch_call(block_valid, dests, hp)
    expert_out = _expert_call(block_expert, block_new, block_valid, buf, w_gate, w_up, w_down)
    return _combine_call(dests, x1, gates, row(g_final), expert_out).reshape(batch, seq, d)
```

```python
import functools
import math

import numpy as np
import jax
import jax.numpy as jnp
from jax import lax
from jax.experimental import pallas as pl
from jax.experimental.pallas import tpu as pltpu

F32 = jnp.float32
BF16 = jnp.bfloat16
I32 = jnp.int32
U32 = jnp.uint32

D_MODEL = 1024
EPS = 1e-6
NEG_INF = -1e30
LANES = 128
ROW_SUBLANES = D_MODEL // LANES
PACKED_SUBLANES = ROW_SUBLANES // 2

A_HEADS = 8
A_HEAD_DIM = 64
A_WIDTH = 512
A_QKV_WIDTH = 3 * A_WIDTH
DILATED_PATTERNS = ((128, 1), (512, 4), (2048, 16))
REL_BUCKETS = 32
REL_MAX_DISTANCE = 1024
A_QB = 128

B_HEADS = 8
B_Q_LORA = 256
B_KV_LORA = 128
B_QK_NOPE = 64
B_QK_ROPE = 32
B_V_DIM = 64
B_WIDTH = 512
ROPE_THETA = 10000.0
B_SCALE = (B_QK_NOPE + B_QK_ROPE) ** -0.5
B_QB = 512
B_SUB = 256
LOG2E = math.log2(math.e)

N_GROUPS = 4
EXPERTS_PER_GROUP = 8
N_EXPERTS = 32
TOP_K = 2
EXPERT_FF = 256
MOE_BLK = 512
ROUTER_ROWS = 40
MIX_UNITS = 2

ROW_TILE = 512
ISSUE_GROUP = 8
COMBINE_CHUNK = 32
PROJ_COLS = 2048

_NT = (((1,), (1,)), ((), ()))


def _cparams(semantics, vmem_mb=48, **kw):
    return pltpu.CompilerParams(dimension_semantics=semantics,
                                vmem_limit_bytes=vmem_mb * 1024 * 1024, **kw)


def _rms(x, g):
    return x * lax.rsqrt(jnp.mean(x * x, axis=-1, keepdims=True) + EPS) * g


def _lane_iota(rows=1):
    return lax.broadcasted_iota(I32, (rows, LANES), 1)


def _row_tile(r, sublanes=ROW_SUBLANES):
    return pl.ds(pl.multiple_of(r * sublanes, sublanes), sublanes)


def _proj_kernel(x_ref, g_ref, win_ref, gq_ref, wq_ref, gkv_ref, wkb_ref, wvb_ref, cos_ref, sin_ref, *refs):
    out1, out4, out16 = refs[:3]
    qb_ref, kb_ref, vb_ref, slab_ref, slab4_ref = refs[3:]
    tm = x_ref.shape[0]
    h = _rms(x_ref[...], g_ref[...]).astype(BF16)
    lo = _lane_iota() < A_HEAD_DIM
    n_slabs = A_QKV_WIDTH // LANES
    parts = []
    for s in range(0, n_slabs, 2):
        part = jnp.dot(h, win_ref[:, LANES * s:LANES * (s + 2)], preferred_element_type=F32)
        if LANES * s < A_WIDTH:
            part = part * LOG2E
        slab_ref[s] = part[:, :LANES]
        slab_ref[s + 1] = part[:, LANES:]
        parts.append(part)
    out1[0, 0] = jnp.concatenate(parts, axis=1).astype(BF16)
    lat = jnp.dot(h, win_ref[:, A_QKV_WIDTH:], preferred_element_type=F32)
    cos = cos_ref[...]
    sin = sin_ref[...]
    n4 = tm // 4

    cq = _rms(lat[:, :B_Q_LORA], gq_ref[...]).astype(BF16)
    q = jnp.dot(cq, wq_ref[...], preferred_element_type=F32)
    q_mul = (cos + jnp.where(lo, 1.0, 0.0)) * (B_SCALE * LOG2E)
    q_rot = sin * (B_SCALE * LOG2E)
    qb_ref[...] = jnp.concatenate(
        [q[:, LANES * hd:LANES * (hd + 1)] * q_mul + pltpu.roll(q[:, LANES * hd:LANES * (hd + 1)], 96, 1) * q_rot
         for hd in range(B_HEADS)], axis=1).astype(BF16)

    ckv = _rms(lat[:, B_Q_LORA:B_Q_LORA + B_KV_LORA], gkv_ref[...]).astype(BF16)
    kr = lat[:, B_Q_LORA + B_KV_LORA:]
    kr = kr * cos + pltpu.roll(kr, 96, 1) * sin
    kn = jnp.dot(ckv, wkb_ref[...], preferred_element_type=F32)
    kb_ref[...] = jnp.concatenate([kn[:, LANES * hd:LANES * (hd + 1)] + kr for hd in range(B_HEADS)],
                                  axis=1).astype(BF16)
    vb_ref[...] = jnp.dot(ckv, wvb_ref[...], preferred_element_type=F32).astype(BF16)

    for c4 in range(4):
        pieces = [slab_ref[s, pl.ds(c4, n4, stride=4), :] for s in range(n_slabs)]
        for s, piece in enumerate(pieces):
            slab4_ref[s, c4] = piece
        out4[0, c4] = jnp.concatenate(pieces, axis=1).astype(BF16)
    for c4 in range(4):
        for j in range(4):
            out16[0, c4 + 4 * j] = jnp.concatenate(
                [slab4_ref[s, c4, pl.ds(j, n4 // 4, stride=4), :] for s in range(n_slabs)], axis=1).astype(BF16)


def _proj_call(x2, g_attn, w_in, g_q, w_q, g_kv, w_kb, w_vb, cos_t, sin_t, seq):
    t = x2.shape[0]
    tm = ROW_TILE
    nseq = seq // tm
    row = lambda i: (i, 0)
    const = lambda i: (0, 0)
    pos = lambda i: (i % nseq, 0)
    out = lambda w: jax.ShapeDtypeStruct((t, w), BF16)
    a_specs, a_shapes = [], []
    for _, r in DILATED_PATTERNS:
        a_specs.append(pl.BlockSpec((1, r, tm // r, A_QKV_WIDTH), lambda i: (i // nseq, 0, i % nseq, 0)))
        a_shapes.append(jax.ShapeDtypeStruct((t // seq, r, seq // r, A_QKV_WIDTH), BF16))
    return pl.pallas_call(
        _proj_kernel,
        grid=(t // tm,),
        in_specs=[
            pl.BlockSpec((tm, D_MODEL), row),
            pl.BlockSpec((1, D_MODEL), const),
            pl.BlockSpec((D_MODEL, PROJ_COLS), const),
            pl.BlockSpec((1, B_Q_LORA), const),
            pl.BlockSpec((B_Q_LORA, B_HEADS * LANES), const),
            pl.BlockSpec((1, B_KV_LORA), const),
            pl.BlockSpec((B_KV_LORA, B_HEADS * LANES), const),
            pl.BlockSpec((B_KV_LORA, B_WIDTH), const),
            pl.BlockSpec((tm, LANES), pos),
            pl.BlockSpec((tm, LANES), pos),
        ],
        out_specs=a_specs + [
            pl.BlockSpec((tm, B_HEADS * LANES), row),
            pl.BlockSpec((tm, B_HEADS * LANES), row),
            pl.BlockSpec((tm, B_WIDTH), row),
        ],
        out_shape=a_shapes + [out(B_HEADS * LANES), out(B_HEADS * LANES), out(B_WIDTH)],
        scratch_shapes=[pltpu.VMEM((A_QKV_WIDTH // LANES, tm, LANES), F32),
                        pltpu.VMEM((A_QKV_WIDTH // LANES, 4, tm // 4, LANES), F32)],
        compiler_params=_cparams(("parallel",)),
        name="proj",
    )(x2, g_attn, w_in, g_q, w_q, g_kv, w_kb, w_vb, cos_t, sin_t)


def _pack_bf16_pair(a, b):
    a_bits = lax.bitcast_convert_type(a.astype(BF16).astype(F32), U32) >> 16
    b_bits = lax.bitcast_convert_type(b.astype(BF16).astype(F32), U32) & jnp.uint32(0xFFFF0000)
    return a_bits | b_bits


def _unpack_bf16_pair(w):
    return (lax.bitcast_convert_type(w << 16, F32), lax.bitcast_convert_type(w & jnp.uint32(0xFFFF0000), F32))


def _dilated_kernel(qkv_ref, bias_ref, o_ref, lse_ref, *, seq_len, dilation, key_width, group):
    nblk = seq_len // A_QB
    lane = _lane_iota()
    lo = lane < A_HEAD_DIM
    first_class = pl.program_id(1) * group
    pairs = A_HEADS // 2

    def block(it, carry):
        c = it // nblk
        if nblk == 1:
            q0, ks, var = 0, 0, 0
        else:
            n = it % nblk
            q0 = pl.multiple_of(n * A_QB, A_QB)
            ks = pl.multiple_of(jnp.clip(q0 - 64, 0, seq_len - key_width), 64)
            var = jnp.where(n == 0, 0, jnp.where(n == nblk - 1, 2, 1))
        rows = pl.ds(q0, A_QB)
        keys = pl.ds(ks, key_width)
        if dilation == 1:
            out_rows = rows
        else:
            out_rows = pl.ds(first_class + c + dilation * q0, A_QB, stride=dilation)
        q_tiles = [qkv_ref[0, c, rows, LANES * p:LANES * (p + 1)] for p in range(pairs)]
        k_tiles = [qkv_ref[0, c, keys, A_WIDTH + LANES * p:A_WIDTH + LANES * (p + 1)] for p in range(pairs)]
        v_tiles = [qkv_ref[0, c, keys, 2 * A_WIDTH + LANES * p:2 * A_WIDTH + LANES * (p + 1)] for p in range(pairs)]
        zero = jnp.zeros((), BF16)
        k_heads = [jnp.where(lo, k_tiles[hd // 2], zero) if hd % 2 == 0 else jnp.where(lo, zero, k_tiles[hd // 2])
                   for hd in range(A_HEADS)]
        scores = [lax.dot_general(q_tiles[hd // 2], k_heads[hd], _NT, preferred_element_type=F32) + bias_ref[var, hd]
                  for hd in range(A_HEADS)]
        maxes = [jnp.max(s, axis=-1, keepdims=True) for s in scores]
        probs = [jnp.exp2(s - m) for s, m in zip(scores, maxes)]
        dens = [jnp.sum(pr, axis=-1, keepdims=True) for pr in probs]
        pvs = [jnp.dot(pr.astype(BF16), v_tiles[hd // 2], preferred_element_type=F32) for hd, pr in enumerate(probs)]
        outs = []
        for p in range(pairs):
            h0, h1 = 2 * p, 2 * p + 1
            outs.append(jnp.where(lo, pvs[h0] * (1.0 / dens[h0]), pvs[h1] * (1.0 / dens[h1])))
        lse = jnp.zeros((A_QB, LANES), F32)
        for hd in range(A_HEADS):
            lse = jnp.where(lane == hd, maxes[hd] + jnp.log2(dens[hd]), lse)
        lse_ref[0, out_rows, :] = lse
        for j in range(pairs // 2):
            o_ref[0, j, out_rows, :] = _pack_bf16_pair(outs[2 * j], outs[2 * j + 1])
        return carry

    lax.fori_loop(0, group * nblk, block, 0, unroll=4)


def _dilated_call(qkv, bias, batch, seq, dilation):
    r = dilation
    sl = seq // r
    kw = min(2 * A_QB, sl)
    group = r
    pairs = A_HEADS // 2
    return pl.pallas_call(
        functools.partial(_dilated_kernel, seq_len=sl, dilation=r, key_width=kw, group=group),
        grid=(batch, r // group),
        in_specs=[pl.BlockSpec((1, group, sl, A_QKV_WIDTH), lambda b, c: (b, c, 0, 0)),
                  pl.BlockSpec(bias.shape, lambda b, c: (0, 0, 0, 0))],
        out_specs=[pl.BlockSpec((1, pairs // 2, seq, LANES), lambda b, c: (b, 0, 0, 0)),
                   pl.BlockSpec((1, seq, LANES), lambda b, c: (b, 0, 0))],
        out_shape=[jax.ShapeDtypeStruct((batch, pairs // 2, seq, LANES), U32),
                   jax.ShapeDtypeStruct((batch, seq, LANES), F32)],
        compiler_params=_cparams(("parallel", "arbitrary")),
        name=f"dilated_r{r}",
    )(qkv, bias)


def _t5_bucket(rel):
    half = REL_BUCKETS // 2
    max_exact = half // 2
    n = np.abs(rel)
    large = max_exact + (np.log(np.maximum(n, 1) / max_exact)
                         / math.log(REL_MAX_DISTANCE / max_exact) * (half - max_exact)).astype(np.int32)
    large = np.minimum(large, half - 1)
    return (np.where(rel > 0, half, 0) + np.where(n < max_exact, n, large)).astype(np.int32)


def _dilated_bias(rel_bias, seq, dilation, half_steps):
    sl = seq // dilation
    kw = min(2 * A_QB, sl)
    offsets = [0] if sl == kw else [0, -half_steps, A_QB - kw]
    rel = np.stack([np.arange(kw)[None, :] + off - np.arange(A_QB)[:, None] for off in offsets])
    valid = np.abs(rel) <= half_steps
    bucket = np.where(valid, _t5_bucket(rel * dilation), REL_BUCKETS).astype(np.int32)
    onehot = (jnp.asarray(bucket)[..., None] == jnp.arange(REL_BUCKETS + 1, dtype=I32)).astype(F32)
    table = jnp.concatenate([rel_bias.astype(F32), jnp.full((1, A_HEADS), NEG_INF, F32)], axis=0)
    return jnp.einsum("vqkb,bh->vhqk", onehot, table * LOG2E, precision=lax.Precision.HIGHEST)


def _mla_kernel(q_ref, k_ref, v_ref, o_ref, v1_ref):
    lo = _lane_iota() < B_V_DIM
    sub = B_QB // B_SUB
    tiles = [slice(0, LANES), slice(LANES, 2 * LANES)]
    v1_ref[:, :LANES] = v_ref[0]
    v1_ref[:, LANES:] = jnp.ones((v_ref.shape[1], LANES), BF16)

    units = [(j, half) for j in range(sub) for half in range(2)]

    def rows(i):
        return [pl.ds(i * B_QB + B_SUB * j, B_SUB) for j in range(sub)]

    def scores(i):
        return [lax.dot_general(q_ref[0, rows(i)[j], tiles[half]], k_ref[0, :, tiles[half]], _NT,
                                preferred_element_type=F32) for j, half in units]

    def finish(i, sc):
        maxes = [jnp.max(s, axis=-1, keepdims=True) for s in sc]
        probs = [jnp.exp2(s - m) for s, m in zip(sc, maxes)]
        pvs = [jnp.dot(pr.astype(BF16), v1_ref[...], preferred_element_type=F32) for pr in probs]
        outs = [pv[:, :LANES] * (1.0 / pv[:, LANES:]) for pv in pvs]
        for j in range(sub):
            o_ref[0, rows(i)[j], :] = jnp.where(lo, outs[2 * j], outs[2 * j + 1]).astype(BF16)

    n_blocks = q_ref.shape[1] // B_QB
    sc = scores(0)
    for i in range(n_blocks):
        sc_next = scores(i + 1) if i + 1 < n_blocks else None
        finish(i, sc)
        sc = sc_next


def _mla_call(qb, kb, vb, batch, seq):
    qb = qb.reshape(batch, seq, B_HEADS * LANES)
    kb = kb.reshape(batch, seq, B_HEADS * LANES)
    vb = vb.reshape(batch, seq, B_WIDTH)
    pair = lambda w: pl.BlockSpec((1, seq, w), lambda b, p: (b, 0, p))
    out = pl.pallas_call(
        _mla_kernel,
        grid=(batch, B_HEADS // 2),
        in_specs=[pair(2 * LANES), pair(2 * LANES), pair(LANES)],
        out_specs=pair(LANES),
        out_shape=jax.ShapeDtypeStruct((batch, seq, B_WIDTH), BF16),
        scratch_shapes=[pltpu.VMEM((seq, 2 * LANES), BF16)],
        compiler_params=_cparams(("parallel", "parallel")),
        name="mla",
    )(qb, kb, vb)
    return out.reshape(batch * seq, B_WIDTH)


def _merge_patterns(o_refs, lse_refs, spread_ref, rows):
    lses = [r[0, rows, :] for r in lse_refs]
    top = functools.reduce(jnp.maximum, lses)
    es = [jnp.exp2(l - top) for l in lses]
    inv = 1.0 / functools.reduce(jnp.add, es)
    spread = spread_ref[...]
    weights = []
    for e in es:
        w = e * inv
        w_hi = w.astype(BF16)
        w_lo = (w - w_hi.astype(F32)).astype(BF16)
        weights.append(jnp.dot(jnp.concatenate([w_hi, w_lo], axis=1), spread, preferred_element_type=F32))
    tiles = []
    for j in range(A_HEADS // 4):
        outs = [_unpack_bf16_pair(r[0, j, rows, :]) for r in o_refs]
        for half in range(2):
            tile = slice(LANES * (2 * j + half), LANES * (2 * j + half + 1))
            tiles.append(functools.reduce(jnp.add, [w[:, tile] * o[half] for w, o in zip(weights, outs)]))
    return jnp.concatenate(tiles, axis=1)


def _mix_kernel(o1_ref, o4_ref, o16_ref, l1_ref, l4_ref, l16_ref, ob_ref, x_ref, ga_ref, gb_ref, wo_ref,
                gf_ref, wr_ref, br_ref, tri_ref, spread_ref, x1_ref, hp_ref, idx_ref, gate_ref, cnt_ref, carry_ref):
    i = pl.program_id(0)
    unit = x_ref.shape[0] // MIX_UNITS
    units = [slice(unit * u, unit * (u + 1)) for u in range(MIX_UNITS)]

    @pl.when(i == 0)
    def _():
        carry_ref[...] = jnp.zeros_like(carry_ref)

    def residual_stream(u):
        rows = units[u]
        oa = _merge_patterns((o1_ref, o4_ref, o16_ref), (l1_ref, l4_ref, l16_ref), spread_ref, rows)
        a = _rms(oa, ga_ref[...]).astype(BF16)
        b = _rms(ob_ref[rows, :].astype(F32), gb_ref[...]).astype(BF16)
        mix = (jnp.dot(a, wo_ref[0:A_WIDTH, :], preferred_element_type=F32)
               + jnp.dot(b, wo_ref[A_WIDTH:, :], preferred_element_type=F32))
        x1 = x_ref[rows, :] + mix
        x1_ref[rows, :] = x1
        h2 = _rms(x1, gf_ref[...])
        half = D_MODEL // 2
        for c in range(PACKED_SUBLANES):
            hp_ref[pl.ds(PACKED_SUBLANES * unit * u + c, unit, stride=PACKED_SUBLANES), :] = _pack_bf16_pair(
                h2[:, LANES * c:LANES * (c + 1)], h2[:, half + LANES * c:half + LANES * (c + 1)])
        return h2

    wr = wr_ref[...]
    wr_hi = wr.astype(BF16)
    wr_lo = (wr - wr_hi.astype(F32)).astype(BF16)
    wr_both = jnp.concatenate([wr_hi, wr_lo], axis=0)

    def logits(h2):
        h_hi = h2.astype(BF16)
        h_lo = (h2 - h_hi.astype(F32)).astype(BF16)
        lg_hi = lax.dot_general(wr_both, h_hi, _NT, preferred_element_type=F32)
        return (lg_hi[:ROUTER_ROWS] + lg_hi[ROUTER_ROWS:]
                + lax.dot_general(wr_hi, h_lo, _NT, preferred_element_type=F32) + br_ref[...])

    def route(lg):
        row = lax.broadcasted_iota(I32, lg.shape, 0)
        is_g = (row >= N_EXPERTS) & (row < N_EXPERTS + N_GROUPS)
        gl = jnp.where(is_g, lg, NEG_INF)
        ge = jnp.exp(gl - jnp.max(gl, axis=0, keepdims=True))
        gp = ge / jnp.sum(ge, axis=0, keepdims=True)
        g_gate = jnp.max(gp, axis=0, keepdims=True)
        g_idx = jnp.min(jnp.where(is_g & (gp == g_gate), row - N_EXPERTS, LANES), axis=0, keepdims=True)
        sel = (row // EXPERTS_PER_GROUP) == g_idx
        el = jnp.where(sel, lg, NEG_INF)
        ee = jnp.exp(el - jnp.max(el, axis=0, keepdims=True))
        ep = jnp.where(sel, ee / jnp.sum(ee, axis=0, keepdims=True), -1.0)
        p1 = jnp.max(ep, axis=0, keepdims=True)
        i1 = jnp.min(jnp.where(ep == p1, row, LANES), axis=0, keepdims=True)
        ep2 = jnp.where(row == i1, -1.0, ep)
        p2 = jnp.max(ep2, axis=0, keepdims=True)
        i2 = jnp.min(jnp.where(sel & (ep2 == p2) & (row != i1), row, LANES), axis=0, keepdims=True)
        den = p1 + p2
        return i1, i2, g_gate * p1 / den, g_gate * p2 / den, row == i1, row == i2

    h2s = [residual_stream(u) for u in range(MIX_UNITS)]
    lgs = [logits(h2) for h2 in h2s]
    routes = [route(lg) for lg in lgs]

    carry = carry_ref[...]
    row8 = lax.broadcasted_iota(I32, (idx_ref.shape[0], unit), 0)
    for u, (i1, i2, g1, g2, hit1, hit2) in enumerate(routes):
        onehot = jnp.where(hit1 | hit2, 1.0, 0.0)
        before = jnp.dot(onehot.astype(BF16), tri_ref[...], preferred_element_type=F32) + carry
        r1 = jnp.sum(jnp.where(hit1, before, 0.0), axis=0, keepdims=True).astype(I32)
        r2 = jnp.sum(jnp.where(hit2, before, 0.0), axis=0, keepdims=True).astype(I32)
        carry = carry + jnp.sum(onehot, axis=1, keepdims=True)
        idx_ref[:, units[u]] = jnp.where(row8 == 0, i1, jnp.where(row8 == 1, i2,
                                         jnp.where(row8 == 2, r1, jnp.where(row8 == 3, r2, 0))))
        gate_ref[:, units[u]] = jnp.where(row8 == 0, g1, jnp.where(row8 == 1, g2, 0.0))
    carry_ref[...] = carry

    @pl.when(i == pl.num_programs(0) - 1)
    def _():
        cnt_ref[...] = jnp.broadcast_to(carry_ref[...], cnt_ref.shape).astype(I32)


def _mix_call(oas, lses, ob, x2, g_a, g_b, w_out, g_ffn, w_router, b_router, tri, spread):
    t = x2.shape[0]
    tm = ROW_TILE
    nseq = oas[0].shape[2] // tm
    row = lambda i: (i, 0)
    const = lambda i: (0, 0)
    slab = lambda n: pl.BlockSpec((1, n, tm, LANES), lambda i: (i // nseq, 0, i % nseq, 0))
    o_slab = slab(A_HEADS // 4)
    lse_slab = pl.BlockSpec((1, tm, LANES), lambda i: (i // nseq, i % nseq, 0))
    return pl.pallas_call(
        _mix_kernel,
        grid=(t // tm,),
        in_specs=[
            o_slab, o_slab, o_slab, lse_slab, lse_slab, lse_slab,
            pl.BlockSpec((tm, B_WIDTH), row),
            pl.BlockSpec((tm, D_MODEL), row),
            pl.BlockSpec((1, A_WIDTH), const),
            pl.BlockSpec((1, B_WIDTH), const),
            pl.BlockSpec((D_MODEL, D_MODEL), const),
            pl.BlockSpec((1, D_MODEL), const),
            pl.BlockSpec((ROUTER_ROWS, D_MODEL), const),
            pl.BlockSpec((ROUTER_ROWS, 1), const),
            pl.BlockSpec((tm // MIX_UNITS, tm // MIX_UNITS), const),
            pl.BlockSpec((2 * LANES, A_WIDTH), const),
        ],
        out_specs=[
            pl.BlockSpec((tm, D_MODEL), row),
            pl.BlockSpec((tm * PACKED_SUBLANES, LANES), row),
            pl.BlockSpec((8, tm), lambda i: (0, i)),
            pl.BlockSpec((8, tm), lambda i: (0, i)),
            pl.BlockSpec((ROUTER_ROWS, LANES), const),
        ],
        out_shape=[
            jax.ShapeDtypeStruct((t, D_MODEL), F32),
            jax.ShapeDtypeStruct((t * PACKED_SUBLANES, LANES), U32),
            jax.ShapeDtypeStruct((8, t), I32),
            jax.ShapeDtypeStruct((8, t), F32),
            jax.ShapeDtypeStruct((ROUTER_ROWS, LANES), I32),
        ],
        scratch_shapes=[pltpu.VMEM((ROUTER_ROWS, 1), F32)],
        compiler_params=_cparams(("arbitrary",)),
        name="mix_router",
    )(*oas, *lses, ob, x2, g_a, g_b, w_out, g_ffn, w_router, b_router, tri, spread)


def _dest_kernel(idx_ref, pstart_ref, dest_ref):
    idx = idx_ref[...]
    row = lax.broadcasted_iota(I32, (ROUTER_ROWS, idx.shape[1]), 0)
    ps = pstart_ref[...]

    def slot(k):
        return jnp.sum(jnp.where(row == idx[k:k + 1, :], ps, 0), axis=0, keepdims=True) + idx[2 + k:3 + k, :]

    row8 = lax.broadcasted_iota(I32, idx.shape, 0)
    dest_ref[...] = jnp.where(row8 == 0, slot(0), jnp.where(row8 == 1, slot(1), 0))


def _dest_call(idx, pstart):
    t = idx.shape[1]
    tm = 4 * ROW_TILE
    return pl.pallas_call(
        _dest_kernel,
        grid=(t // tm,),
        in_specs=[pl.BlockSpec((8, tm), lambda i: (0, i)), pl.BlockSpec((ROUTER_ROWS, 1), lambda i: (0, 0))],
        out_specs=pl.BlockSpec((8, tm), lambda i: (0, i)),
        out_shape=jax.ShapeDtypeStruct((8, t), I32),
        compiler_params=_cparams(("parallel",)),
        name="dest_rows",
    )(idx, pstart)


def _dispatch_kernel(valid_ref, d0_ref, d1_ref, h_ref, buf_ref, zero_ref, sem, pad_sem):
    i = pl.program_id(0)
    tt = h_ref.shape[0] // PACKED_SUBLANES
    n_blocks = valid_ref.shape[0]

    def for_padded_blocks(fn):
        def body(j, c):
            taken = valid_ref[j]
            even = (taken + 1) // 2 * 2

            @pl.when(even < MOE_BLK)
            def _():
                n_pad = pl.multiple_of((MOE_BLK - even) * PACKED_SUBLANES, ROW_SUBLANES)
                first = pl.multiple_of((j * MOE_BLK + even) * PACKED_SUBLANES, ROW_SUBLANES)
                fn(pltpu.make_async_copy(zero_ref.at[pl.ds(0, n_pad)], buf_ref.at[pl.ds(first, n_pad)], pad_sem))

            @pl.when(even != taken)
            def _():
                fn(pltpu.make_async_copy(zero_ref.at[pl.ds(0, PACKED_SUBLANES)],
                                         buf_ref.at[_row_tile(j * MOE_BLK + taken, PACKED_SUBLANES)], pad_sem))
            return c
        lax.fori_loop(0, n_blocks, body, 0)

    @pl.when(i == 0)
    def _():
        zero_ref[...] = jnp.zeros_like(zero_ref)
        for_padded_blocks(lambda cp: cp.start())

    def issue(g, c):
        base = pl.multiple_of(g * ISSUE_GROUP, ISSUE_GROUP)
        for j in range(ISSUE_GROUP):
            for prio, d_ref in enumerate((d0_ref, d1_ref)):
                pltpu.make_async_copy(h_ref.at[_row_tile(base + j, PACKED_SUBLANES)],
                                      buf_ref.at[_row_tile(d_ref[base + j], PACKED_SUBLANES)],
                                      sem).start(priority=prio)
        return c

    lax.fori_loop(0, tt // ISSUE_GROUP, issue, 0)
    for k in range(TOP_K):
        pltpu.make_async_copy(h_ref, buf_ref.at[pl.ds(0, tt * PACKED_SUBLANES)], sem).wait()

    @pl.when(i == pl.num_programs(0) - 1)
    def _():
        for_padded_blocks(lambda cp: cp.wait())


def _dispatch_call(block_valid, dests, hp):
    t = hp.shape[0] // PACKED_SUBLANES
    tt = 4 * ROW_TILE
    n_rows = block_valid.shape[0] * MOE_BLK
    return pl.pallas_call(
        _dispatch_kernel,
        grid_spec=pltpu.PrefetchScalarGridSpec(
            num_scalar_prefetch=1,
            grid=(t // tt,),
            in_specs=[
                pl.BlockSpec((tt,), lambda i, va: (i,), memory_space=pltpu.SMEM),
                pl.BlockSpec((tt,), lambda i, va: (i,), memory_space=pltpu.SMEM),
                pl.BlockSpec((tt * PACKED_SUBLANES, LANES), lambda i, va: (i, 0)),
            ],
            out_specs=pl.BlockSpec(memory_space=pl.ANY),
            scratch_shapes=[pltpu.VMEM((MOE_BLK * PACKED_SUBLANES, LANES), U32),
                            pltpu.SemaphoreType.DMA(()), pltpu.SemaphoreType.DMA(())],
        ),
        out_shape=jax.ShapeDtypeStruct((n_rows * PACKED_SUBLANES, LANES), U32),
        compiler_params=_cparams(("arbitrary",), disable_bounds_checks=True, has_side_effects=True),
        name="dispatch",
    )(block_valid, *dests, hp)


def _expert_kernel(be_ref, new_ref, valid_ref, buf_ref, wg_ref, wu_ref, wd_ref, out_ref, wg_s, wu_s, wd_s):
    j = pl.program_id(0)
    del be_ref

    @pl.when(new_ref[j] == 1)
    def _():
        wg_s[...] = wg_ref[0, 0].astype(BF16)
        wu_s[...] = wu_ref[0, 0].astype(BF16)
        wd_s[...] = wd_ref[0, 0].astype(BF16)

    n_valid = valid_ref[j]

    @pl.when(n_valid > 0)
    def _():
        blk = buf_ref.shape[0] // PACKED_SUBLANES
        words = [_unpack_bf16_pair(buf_ref[pl.ds(c, blk, stride=PACKED_SUBLANES), :]) for c in range(PACKED_SUBLANES)]
        x = jnp.concatenate([w[0] for w in words] + [w[1] for w in words], axis=1).astype(BF16)
        cols = [pl.ds(c, blk, stride=ROW_SUBLANES) for c in range(ROW_SUBLANES)]
        g = jnp.dot(x, wg_s[...], preferred_element_type=F32)
        u = jnp.dot(x, wu_s[...], preferred_element_type=F32)
        hb = (g * jax.nn.sigmoid(g)) * u
        out = jnp.dot(hb.astype(BF16), wd_s[...], preferred_element_type=F32)
        for c, rows in enumerate(cols):
            out_ref[rows, :] = out[:, LANES * c:LANES * (c + 1)]

    @pl.when(n_valid == 0)
    def _():
        out_ref[...] = jnp.zeros_like(out_ref)


def _expert_call(block_expert, block_new, block_valid, buf, w_gate, w_up, w_down):
    nb = buf.shape[0] // (MOE_BLK * PACKED_SUBLANES)
    wsel = lambda j, be, nw, va: (0, be[j], 0, 0)
    rows = lambda sublanes: pl.BlockSpec((MOE_BLK * sublanes, LANES), lambda j, be, nw, va: (j, 0))
    return pl.pallas_call(
        _expert_kernel,
        grid_spec=pltpu.PrefetchScalarGridSpec(
            num_scalar_prefetch=3,
            grid=(nb,),
            in_specs=[
                rows(PACKED_SUBLANES),
                pl.BlockSpec((1, 1, D_MODEL, EXPERT_FF), wsel),
                pl.BlockSpec((1, 1, D_MODEL, EXPERT_FF), wsel),
                pl.BlockSpec((1, 1, EXPERT_FF, D_MODEL), wsel),
            ],
            out_specs=rows(ROW_SUBLANES),
            scratch_shapes=[pltpu.VMEM((D_MODEL, EXPERT_FF), BF16),
                            pltpu.VMEM((D_MODEL, EXPERT_FF), BF16),
                            pltpu.VMEM((EXPERT_FF, D_MODEL), BF16)],
        ),
        out_shape=jax.ShapeDtypeStruct((nb * MOE_BLK * ROW_SUBLANES, LANES), F32),
        compiler_params=_cparams(("arbitrary",)),
        name="experts",
    )(block_expert, block_new, block_valid, buf, w_gate, w_up, w_down)


def _combine_kernel(d0_ref, d1_ref, d0_next_ref, d1_next_ref, d0_ahead_ref, d1_ahead_ref, x1_ref, gate_ref,
                    gf_ref, eo_ref, o_ref, rows_a, rows_b, rows_c, sems):
    i = pl.program_id(0)
    tt = x1_ref.shape[0]
    bufs = (rows_a, rows_b, rows_c)
    n_buf = len(bufs)

    def start_row(d_refs, slot, r):
        for k, d_ref in enumerate(d_refs):
            pltpu.make_async_copy(eo_ref.at[_row_tile(d_ref[r])], bufs[slot].at[k, _row_tile(r)],
                                  sems.at[slot]).start(priority=k)

    def drain(slot):
        for k in range(TOP_K):
            pltpu.make_async_copy(eo_ref.at[pl.ds(0, tt * ROW_SUBLANES)], bufs[slot].at[k], sems.at[slot]).wait()

    @pl.when(i == 0)
    def _():
        def issue(g, c):
            base = pl.multiple_of(g * ISSUE_GROUP, ISSUE_GROUP)
            for j in range(ISSUE_GROUP):
                start_row((d0_ref, d1_ref), 0, base + j)
                start_row((d0_next_ref, d1_next_ref), 1, base + j)
            return c
        lax.fori_loop(0, tt // ISSUE_GROUP, issue, 0)

    def step(slot):
        ahead = (slot + 2) % n_buf
        drain(slot)
        gf = gf_ref[...]
        for ch in range(tt // COMBINE_CHUNK):
            r0 = ch * COMBINE_CHUNK
            for j in range(COMBINE_CHUNK):
                start_row((d0_ahead_ref, d1_ahead_ref), ahead, r0 + j)
            gate = gate_ref[r0:r0 + COMBINE_CHUNK, :]

            def rows(k):
                return jnp.concatenate(
                    [bufs[slot][k, pl.ds(ROW_SUBLANES * r0 + c, COMBINE_CHUNK, stride=ROW_SUBLANES), :]
                     for c in range(ROW_SUBLANES)], axis=1)

            y = rows(0) * gate[:, 0:1] + rows(1) * gate[:, 1:2]
            o_ref[r0:r0 + COMBINE_CHUNK, :] = _rms(x1_ref[r0:r0 + COMBINE_CHUNK, :] + y, gf)

        @pl.when(i == pl.num_programs(0) - 1)
        def _():
            drain((slot + 1) % n_buf)
            drain(ahead)

    for phase in range(n_buf):
        pl.when(i % n_buf == phase)(functools.partial(step, phase))


def _combine_call(dests, x1, gates, g_final, expert_out):
    t = x1.shape[0]
    tt = ROW_TILE // 2
    last = t // tt - 1
    tile = lambda ahead: pl.BlockSpec((tt,), lambda i: (jnp.minimum(i + ahead, last),), memory_space=pltpu.SMEM)
    buf = pltpu.VMEM((TOP_K, tt * ROW_SUBLANES, LANES), F32)
    return pl.pallas_call(
        _combine_kernel,
        grid=(t // tt,),
        in_specs=[
            tile(0), tile(0), tile(1), tile(1), tile(2), tile(2),
            pl.BlockSpec((tt, D_MODEL), lambda i: (i, 0)),
            pl.BlockSpec((tt, TOP_K), lambda i: (i, 0)),
            pl.BlockSpec((1, D_MODEL), lambda i: (0, 0)),
            pl.BlockSpec(memory_space=pl.ANY),
        ],
        out_specs=pl.BlockSpec((tt, D_MODEL), lambda i: (i, 0)),
        out_shape=jax.ShapeDtypeStruct((t, D_MODEL), F32),
        scratch_shapes=[buf, buf, buf, pltpu.SemaphoreType.DMA((3,))],
        compiler_params=_cparams(("arbitrary",), disable_bounds_checks=True),
        name="combine",
    )(*dests, *dests, *dests, x1, gates, g_final, expert_out)


def _rope_tables(seq):
    half = B_QK_ROPE // 2
    inv_freq = ROPE_THETA ** (-(jnp.arange(half, dtype=F32) / half))
    ang = jnp.arange(seq, dtype=F32)[:, None] * inv_freq[None, :]
    cos, sin = jnp.cos(ang), jnp.sin(ang)
    z = jnp.zeros((seq, B_QK_NOPE), F32)
    z2 = jnp.zeros((seq, B_QK_ROPE), F32)
    return (jnp.concatenate([z, cos, cos, z2], axis=1), jnp.concatenate([z, -sin, sin, z2], axis=1))


def _swap_halves(w):
    half = w.shape[-1] // 2
    return jnp.concatenate([w[..., half:], w[..., :half]], axis=-1)


def _layout_weights(w_in, w_q_up, w_kv_up):
    d = w_in.shape[0]
    w_kr = w_in[:, 3 * A_WIDTH + B_Q_LORA + B_KV_LORA:]
    w_in_l = jnp.concatenate(
        [w_in[:, :A_WIDTH] * (A_HEAD_DIM ** -0.5), w_in[:, A_WIDTH:3 * A_WIDTH + B_Q_LORA + B_KV_LORA],
         jnp.zeros((d, B_QK_NOPE), F32), w_kr, _swap_halves(w_kr)], axis=1).astype(BF16)
    wq = w_q_up.reshape(B_Q_LORA, B_HEADS, B_QK_NOPE + B_QK_ROPE)
    wq_l = jnp.concatenate([wq, _swap_halves(wq[..., B_QK_NOPE:])], axis=-1)
    wq_l = wq_l.reshape(B_Q_LORA, B_HEADS * LANES).astype(BF16)
    wkv = w_kv_up.reshape(B_KV_LORA, B_HEADS, B_QK_NOPE + B_V_DIM)
    wkb = jnp.concatenate([wkv[..., :B_QK_NOPE], jnp.zeros_like(wkv[..., :B_QK_NOPE])], axis=-1)
    wkb = wkb.reshape(B_KV_LORA, B_HEADS * LANES).astype(BF16)
    wvb = wkv[..., B_QK_NOPE:].reshape(B_KV_LORA, B_WIDTH).astype(BF16)
    return w_in_l, wq_l, wkb, wvb


def _block_plan(counts, n_blocks):
    padded = (counts + MOE_BLK - 1) // MOE_BLK * MOE_BLK
    ends = jnp.cumsum(padded)
    starts = ends - padded
    first_row = jnp.arange(n_blocks, dtype=I32) * MOE_BLK
    expert = jnp.minimum(jnp.sum(ends[None, :] <= first_row[:, None], axis=1), N_EXPERTS - 1).astype(I32)
    new = jnp.concatenate([jnp.ones((1,), I32), (expert[1:] != expert[:-1]).astype(I32)])
    valid = jnp.clip((starts + counts)[expert] - first_row, 0, MOE_BLK).astype(I32)
    return starts.astype(I32), expert, new, valid


def kernel(x, g_attn_norm, w_in, rel_bias, g_q_latent, w_q_up, g_kv_latent, w_kv_up, g_out_a, g_out_b, w_out,
           g_ffn_norm, w_router_group, b_router_group, w_router_expert, b_router_expert, w_gate, w_up, w_down,
           g_final):
    batch, seq, d = x.shape
    t = batch * seq
    assert g_attn_norm.shape[0] == 1 and d == D_MODEL and seq % ROW_TILE == 0
    cos_t, sin_t = _rope_tables(seq)
    unit = ROW_TILE // MIX_UNITS
    tri = jnp.triu(jnp.ones((unit, unit), F32), 1).astype(BF16)
    spread = (jnp.arange(2 * LANES)[:, None] % LANES == jnp.arange(A_WIDTH)[None, :] // A_HEAD_DIM).astype(BF16)
    n_blocks = t * TOP_K // MOE_BLK + N_EXPERTS
    x2 = x.reshape(t, d)
    row = lambda v: v.reshape(1, -1)

    w_in_l, wq_l, wkb_l, wvb_l = _layout_weights(w_in[0], w_q_up[0], w_kv_up[0])
    *qkv_a, qb, kb, vb = _proj_call(x2, row(g_attn_norm[0]), w_in_l, row(g_q_latent[0]), wq_l,
                                    row(g_kv_latent[0]), wkb_l, wvb_l, cos_t, sin_t, seq)
    oas, lses = [], []
    for pi, (window, dilation) in enumerate(DILATED_PATTERNS):
        bias = _dilated_bias(rel_bias, seq, dilation, window // (2 * dilation))
        o_p, lse_p = _dilated_call(qkv_a[pi], bias, batch, seq, dilation)
        oas.append(o_p)
        lses.append(lse_p)
    ob = _mla_call(qb, kb, vb, batch, seq)

    pad = ROUTER_ROWS - N_EXPERTS - N_GROUPS
    w_router = jnp.concatenate([w_router_expert[0], w_router_group[0], jnp.zeros((d, pad), F32)], axis=1).T
    b_router = jnp.concatenate([b_router_expert[0], b_router_group[0], jnp.zeros((pad,), F32)])
    x1, hp, idx, gates, cnt = _mix_call(oas, lses, ob, x2, row(g_out_a[0]), row(g_out_b[0]), w_out[0].astype(BF16),
                                        row(g_ffn_norm[0]), w_router, b_router.reshape(-1, 1), tri, spread)
    pstart, block_expert, block_new, block_valid = _block_plan(cnt[:N_EXPERTS, 0], n_blocks)
    pstart_col = jnp.concatenate([pstart, jnp.zeros((ROUTER_ROWS - N_EXPERTS,), I32)]).reshape(-1, 1)
    dest = _dest_call(idx, pstart_col)
    dests = (dest[0], dest[1])
    buf = _dispatch_call(block_valid, dests, hp)
    expert_out = _expert_call(block_expert, block_new, block_valid, buf, w_gate, w_up, w_down)
    return _combine_call(dests, x1, gates[:TOP_K].T, row(g_final), expert_out).reshape(batch, seq, d)
```

```python
import functools
import math

import numpy as np
import jax
import jax.numpy as jnp
from jax import lax
from jax.experimental import pallas as pl
from jax.experimental.pallas import tpu as pltpu

F32 = jnp.float32
BF16 = jnp.bfloat16
I32 = jnp.int32
U32 = jnp.uint32

D_MODEL = 1024
EPS = 1e-6
NEG_INF = -1e30
LANES = 128
ROW_SUBLANES = D_MODEL // LANES
PACKED_SUBLANES = ROW_SUBLANES // 2

A_HEADS = 8
A_HEAD_DIM = 64
A_WIDTH = 512
A_QKV_WIDTH = 3 * A_WIDTH
DILATED_PATTERNS = ((128, 1), (512, 4), (2048, 16))
REL_BUCKETS = 32
REL_MAX_DISTANCE = 1024
A_QB = 128

B_HEADS = 8
B_Q_LORA = 256
B_KV_LORA = 128
B_QK_NOPE = 64
B_QK_ROPE = 32
B_V_DIM = 64
B_WIDTH = 512
ROPE_THETA = 10000.0
B_SCALE = (B_QK_NOPE + B_QK_ROPE) ** -0.5
B_QB = 512
B_SUB = 256
LOG2E = math.log2(math.e)

N_GROUPS = 4
EXPERTS_PER_GROUP = 8
N_EXPERTS = 32
TOP_K = 2
EXPERT_FF = 256
MOE_BLK = 512
ROUTER_ROWS = 40
MIX_UNITS = 2

ROW_TILE = 512
ISSUE_GROUP = 8
COMBINE_CHUNK = 32
PROJ_COLS = 2048

_NT = (((1,), (1,)), ((), ()))


def _cparams(semantics, vmem_mb=48, **kw):
    return pltpu.CompilerParams(dimension_semantics=semantics,
                                vmem_limit_bytes=vmem_mb * 1024 * 1024, **kw)


def _rms(x, g):
    return x * lax.rsqrt(jnp.mean(x * x, axis=-1, keepdims=True) + EPS) * g


def _lane_iota(rows=1):
    return lax.broadcasted_iota(I32, (rows, LANES), 1)


def _row_tile(r, sublanes=ROW_SUBLANES):
    return pl.ds(pl.multiple_of(r * sublanes, sublanes), sublanes)


def _proj_kernel(x_ref, g_ref, win_ref, gq_ref, wq_ref, gkv_ref, wkb_ref, wvb_ref, cos_ref, sin_ref, *refs):
    out1, out4, out16 = refs[:3]
    qb_ref, kb_ref, vb_ref, slab_ref, slab4_ref = refs[3:]
    tm = x_ref.shape[0]
    h = _rms(x_ref[...], g_ref[...]).astype(BF16)
    lo = _lane_iota() < A_HEAD_DIM
    n_slabs = A_QKV_WIDTH // LANES
    parts = []
    for s in range(0, n_slabs, 2):
        part = jnp.dot(h, win_ref[:, LANES * s:LANES * (s + 2)], preferred_element_type=F32)
        if LANES * s < A_WIDTH:
            part = part * LOG2E
        slab_ref[s] = part[:, :LANES]
        slab_ref[s + 1] = part[:, LANES:]
        parts.append(part)
    out1[0, 0] = jnp.concatenate(parts, axis=1).astype(BF16)
    lat = jnp.dot(h, win_ref[:, A_QKV_WIDTH:], preferred_element_type=F32)
    cos = cos_ref[...]
    sin = sin_ref[...]
    n4 = tm // 4

    cq = _rms(lat[:, :B_Q_LORA], gq_ref[...]).astype(BF16)
    q = jnp.dot(cq, wq_ref[...], preferred_element_type=F32)
    q_mul = (cos + jnp.where(lo, 1.0, 0.0)) * (B_SCALE * LOG2E)
    q_rot = sin * (B_SCALE * LOG2E)
    qb_ref[...] = jnp.concatenate(
        [q[:, LANES * hd:LANES * (hd + 1)] * q_mul + pltpu.roll(q[:, LANES * hd:LANES * (hd + 1)], 96, 1) * q_rot
         for hd in range(B_HEADS)], axis=1).astype(BF16)

    ckv = _rms(lat[:, B_Q_LORA:B_Q_LORA + B_KV_LORA], gkv_ref[...]).astype(BF16)
    kr = lat[:, B_Q_LORA + B_KV_LORA:]
    kr = kr * cos + pltpu.roll(kr, 96, 1) * sin
    kn = jnp.dot(ckv, wkb_ref[...], preferred_element_type=F32)
    kb_ref[...] = jnp.concatenate([kn[:, LANES * hd:LANES * (hd + 1)] + kr for hd in range(B_HEADS)],
                                  axis=1).astype(BF16)
    vb_ref[...] = jnp.dot(ckv, wvb_ref[...], preferred_element_type=F32).astype(BF16)

    for c4 in range(4):
        pieces = [slab_ref[s, pl.ds(c4, n4, stride=4), :] for s in range(n_slabs)]
        for s, piece in enumerate(pieces):
            slab4_ref[s, c4] = piece
        out4[0, c4] = jnp.concatenate(pieces, axis=1).astype(BF16)
    for c4 in range(4):
        for j in range(4):
            out16[0, c4 + 4 * j] = jnp.concatenate(
                [slab4_ref[s, c4, pl.ds(j, n4 // 4, stride=4), :] for s in range(n_slabs)], axis=1).astype(BF16)


def _proj_call(x2, g_attn, w_in, g_q, w_q, g_kv, w_kb, w_vb, cos_t, sin_t, seq):
    t = x2.shape[0]
    tm = ROW_TILE
    nseq = seq // tm
    row = lambda i: (i, 0)
    const = lambda i: (0, 0)
    pos = lambda i: (i % nseq, 0)
    out = lambda w: jax.ShapeDtypeStruct((t, w), BF16)
    a_specs, a_shapes = [], []
    for _, r in DILATED_PATTERNS:
        a_specs.append(pl.BlockSpec((1, r, tm // r, A_QKV_WIDTH), lambda i: (i // nseq, 0, i % nseq, 0)))
        a_shapes.append(jax.ShapeDtypeStruct((t // seq, r, seq // r, A_QKV_WIDTH), BF16))
    return pl.pallas_call(
        _proj_kernel,
        grid=(t // tm,),
        in_specs=[
            pl.BlockSpec((tm, D_MODEL), row),
            pl.BlockSpec((1, D_MODEL), const),
            pl.BlockSpec((D_MODEL, PROJ_COLS), const),
            pl.BlockSpec((1, B_Q_LORA), const),
            pl.BlockSpec((B_Q_LORA, B_HEADS * LANES), const),
            pl.BlockSpec((1, B_KV_LORA), const),
            pl.BlockSpec((B_KV_LORA, B_HEADS * LANES), const),
            pl.BlockSpec((B_KV_LORA, B_WIDTH), const),
            pl.BlockSpec((tm, LANES), pos),
            pl.BlockSpec((tm, LANES), pos),
        ],
        out_specs=a_specs + [
            pl.BlockSpec((tm, B_HEADS * LANES), row),
            pl.BlockSpec((tm, B_HEADS * LANES), row),
            pl.BlockSpec((tm, B_WIDTH), row),
        ],
        out_shape=a_shapes + [out(B_HEADS * LANES), out(B_HEADS * LANES), out(B_WIDTH)],
        scratch_shapes=[pltpu.VMEM((A_QKV_WIDTH // LANES, tm, LANES), F32),
                        pltpu.VMEM((A_QKV_WIDTH // LANES, 4, tm // 4, LANES), F32)],
        compiler_params=_cparams(("parallel",)),
        name="proj",
    )(x2, g_attn, w_in, g_q, w_q, g_kv, w_kb, w_vb, cos_t, sin_t)


def _pack_bf16_pair(a, b):
    a_bits = lax.bitcast_convert_type(a.astype(BF16).astype(F32), U32) >> 16
    b_bits = lax.bitcast_convert_type(b.astype(BF16).astype(F32), U32) & jnp.uint32(0xFFFF0000)
    return a_bits | b_bits


def _unpack_bf16_pair(w):
    return (lax.bitcast_convert_type(w << 16, F32), lax.bitcast_convert_type(w & jnp.uint32(0xFFFF0000), F32))


def _dilated_kernel(qkv_ref, bias_ref, o_ref, lse_ref, *, seq_len, dilation, key_width, group):
    nblk = seq_len // A_QB
    lane = _lane_iota()
    lo = lane < A_HEAD_DIM
    first_class = pl.program_id(1) * group
    pairs = A_HEADS // 2

    def block(it, carry):
        c = it // nblk
        if nblk == 1:
            q0, ks, var = 0, 0, 0
        else:
            n = it % nblk
            q0 = pl.multiple_of(n * A_QB, A_QB)
            ks = pl.multiple_of(jnp.clip(q0 - 64, 0, seq_len - key_width), 64)
            var = jnp.where(n == 0, 0, jnp.where(n == nblk - 1, 2, 1))
        rows = pl.ds(q0, A_QB)
        keys = pl.ds(ks, key_width)
        if dilation == 1:
            out_rows = rows
        else:
            out_rows = pl.ds(first_class + c + dilation * q0, A_QB, stride=dilation)
        q_tiles = [qkv_ref[0, c, rows, LANES * p:LANES * (p + 1)] for p in range(pairs)]
        k_tiles = [qkv_ref[0, c, keys, A_WIDTH + LANES * p:A_WIDTH + LANES * (p + 1)] for p in range(pairs)]
        v_tiles = [qkv_ref[0, c, keys, 2 * A_WIDTH + LANES * p:2 * A_WIDTH + LANES * (p + 1)] for p in range(pairs)]
        zero = jnp.zeros((), BF16)
        k_heads = [jnp.where(lo, k_tiles[hd // 2], zero) if hd % 2 == 0 else jnp.where(lo, zero, k_tiles[hd // 2])
                   for hd in range(A_HEADS)]
        scores = [lax.dot_general(q_tiles[hd // 2], k_heads[hd], _NT, preferred_element_type=F32) + bias_ref[var, hd]
                  for hd in range(A_HEADS)]
        maxes = [jnp.max(s, axis=-1, keepdims=True) for s in scores]
        probs = [jnp.exp2(s - m) for s, m in zip(scores, maxes)]
        dens = [jnp.sum(pr, axis=-1, keepdims=True) for pr in probs]
        pvs = [jnp.dot(pr.astype(BF16), v_tiles[hd // 2], preferred_element_type=F32) for hd, pr in enumerate(probs)]
        outs = []
        for p in range(pairs):
            h0, h1 = 2 * p, 2 * p + 1
            outs.append(jnp.where(lo, pvs[h0] * (1.0 / dens[h0]), pvs[h1] * (1.0 / dens[h1])))
        lse = jnp.zeros((A_QB, LANES), F32)
        for hd in range(A_HEADS):
            lse = jnp.where(lane == hd, maxes[hd] + jnp.log2(dens[hd]), lse)
        lse_ref[0, out_rows, :] = lse
        for j in range(pairs // 2):
            o_ref[0, j, out_rows, :] = _pack_bf16_pair(outs[2 * j], outs[2 * j + 1])
        return carry

    lax.fori_loop(0, group * nblk, block, 0, unroll=4)


def _dilated_call(qkv, bias, batch, seq, dilation):
    r = dilation
    sl = seq // r
    kw = min(2 * A_QB, sl)
    group = r
    pairs = A_HEADS // 2
    return pl.pallas_call(
        functools.partial(_dilated_kernel, seq_len=sl, dilation=r, key_width=kw, group=group),
        grid=(batch, r // group),
        in_specs=[pl.BlockSpec((1, group, sl, A_QKV_WIDTH), lambda b, c: (b, c, 0, 0)),
                  pl.BlockSpec(bias.shape, lambda b, c: (0, 0, 0, 0))],
        out_specs=[pl.BlockSpec((1, pairs // 2, seq, LANES), lambda b, c: (b, 0, 0, 0)),
                   pl.BlockSpec((1, seq, LANES), lambda b, c: (b, 0, 0))],
        out_shape=[jax.ShapeDtypeStruct((batch, pairs // 2, seq, LANES), U32),
                   jax.ShapeDtypeStruct((batch, seq, LANES), F32)],
        compiler_params=_cparams(("parallel", "arbitrary")),
        name=f"dilated_r{r}",
    )(qkv, bias)


def _t5_bucket(rel):
    half = REL_BUCKETS // 2
    max_exact = half // 2
    n = np.abs(rel)
    large = max_exact + (np.log(np.maximum(n, 1) / max_exact)
                         / math.log(REL_MAX_DISTANCE / max_exact) * (half - max_exact)).astype(np.int32)
    large = np.minimum(large, half - 1)
    return (np.where(rel > 0, half, 0) + np.where(n < max_exact, n, large)).astype(np.int32)


def _dilated_bias(rel_bias, seq, dilation, half_steps):
    sl = seq // dilation
    kw = min(2 * A_QB, sl)
    offsets = [0] if sl == kw else [0, -half_steps, A_QB - kw]
    rel = np.stack([np.arange(kw)[None, :] + off - np.arange(A_QB)[:, None] for off in offsets])
    valid = np.abs(rel) <= half_steps
    bucket = np.where(valid, _t5_bucket(rel * dilation), REL_BUCKETS).astype(np.int32)
    onehot = (jnp.asarray(bucket)[..., None] == jnp.arange(REL_BUCKETS + 1, dtype=I32)).astype(F32)
    table = jnp.concatenate([rel_bias.astype(F32), jnp.full((1, A_HEADS), NEG_INF, F32)], axis=0)
    return jnp.einsum("vqkb,bh->vhqk", onehot, table * LOG2E, precision=lax.Precision.HIGHEST)


def _mla_kernel(q_ref, k_ref, v_ref, o_ref, v1_ref):
    lo = _lane_iota() < B_V_DIM
    sub = B_QB // B_SUB
    tiles = [slice(0, LANES), slice(LANES, 2 * LANES)]
    v1_ref[:, :LANES] = v_ref[0]
    v1_ref[:, LANES:] = jnp.ones((v_ref.shape[1], LANES), BF16)

    units = [(j, half) for j in range(sub) for half in range(2)]

    def rows(i):
        return [pl.ds(i * B_QB + B_SUB * j, B_SUB) for j in range(sub)]

    def scores(i):
        return [lax.dot_general(q_ref[0, rows(i)[j], tiles[half]], k_ref[0, :, tiles[half]], _NT,
                                preferred_element_type=F32) for j, half in units]

    def finish(i, sc):
        maxes = [jnp.max(s, axis=-1, keepdims=True) for s in sc]
        probs = [jnp.exp2(s - m) for s, m in zip(sc, maxes)]
        pvs = [jnp.dot(pr.astype(BF16), v1_ref[...], preferred_element_type=F32) for pr in probs]
        outs = [pv[:, :LANES] * (1.0 / pv[:, LANES:]) for pv in pvs]
        for j in range(sub):
            o_ref[0, rows(i)[j], :] = jnp.where(lo, outs[2 * j], outs[2 * j + 1]).astype(BF16)

    n_blocks = q_ref.shape[1] // B_QB
    sc = scores(0)
    for i in range(n_blocks):
        sc_next = scores(i + 1) if i + 1 < n_blocks else None
        finish(i, sc)
        sc = sc_next


def _mla_call(qb, kb, vb, batch, seq):
    qb = qb.reshape(batch, seq, B_HEADS * LANES)
    kb = kb.reshape(batch, seq, B_HEADS * LANES)
    vb = vb.reshape(batch, seq, B_WIDTH)
    pair = lambda w: pl.BlockSpec((1, seq, w), lambda b, p: (b, 0, p))
    out = pl.pallas_call(
        _mla_kernel,
        grid=(batch, B_HEADS // 2),
        in_specs=[pair(2 * LANES), pair(2 * LANES), pair(LANES)],
        out_specs=pair(LANES),
        out_shape=jax.ShapeDtypeStruct((batch, seq, B_WIDTH), BF16),
        scratch_shapes=[pltpu.VMEM((seq, 2 * LANES), BF16)],
        compiler_params=_cparams(("parallel", "parallel")),
        name="mla",
    )(qb, kb, vb)
    return out.reshape(batch * seq, B_WIDTH)


def _merge_patterns(o_refs, lse_refs, spread_ref, rows):
    lses = [r[0, rows, :] for r in lse_refs]
    top = functools.reduce(jnp.maximum, lses)
    es = [jnp.exp2(l - top) for l in lses]
    inv = 1.0 / functools.reduce(jnp.add, es)
    spread = spread_ref[...]
    weights = []
    for e in es:
        w = e * inv
        w_hi = w.astype(BF16)
        w_lo = (w - w_hi.astype(F32)).astype(BF16)
        weights.append(jnp.dot(jnp.concatenate([w_hi, w_lo], axis=1), spread, preferred_element_type=F32))
    tiles = []
    for j in range(A_HEADS // 4):
        outs = [_unpack_bf16_pair(r[0, j, rows, :]) for r in o_refs]
        for half in range(2):
            tile = slice(LANES * (2 * j + half), LANES * (2 * j + half + 1))
            tiles.append(functools.reduce(jnp.add, [w[:, tile] * o[half] for w, o in zip(weights, outs)]))
    return jnp.concatenate(tiles, axis=1)


def _mix_kernel(o1_ref, o4_ref, o16_ref, l1_ref, l4_ref, l16_ref, ob_ref, x_ref, ga_ref, gb_ref, wo_ref,
                gf_ref, wr_ref, br_ref, tri_ref, spread_ref, x1_ref, hp_ref, idx_ref, gate_ref, cnt_ref, carry_ref):
    i = pl.program_id(0)
    unit = x_ref.shape[0] // MIX_UNITS
    units = [slice(unit * u, unit * (u + 1)) for u in range(MIX_UNITS)]

    @pl.when(i == 0)
    def _():
        carry_ref[...] = jnp.zeros_like(carry_ref)

    def residual_stream(u):
        rows = units[u]
        oa = _merge_patterns((o1_ref, o4_ref, o16_ref), (l1_ref, l4_ref, l16_ref), spread_ref, rows)
        a = _rms(oa, ga_ref[...]).astype(BF16)
        b = _rms(ob_ref[rows, :].astype(F32), gb_ref[...]).astype(BF16)
        mix = (jnp.dot(a, wo_ref[0:A_WIDTH, :], preferred_element_type=F32)
               + jnp.dot(b, wo_ref[A_WIDTH:, :], preferred_element_type=F32))
        x1 = x_ref[rows, :] + mix
        x1_ref[rows, :] = x1
        h2 = _rms(x1, gf_ref[...])
        half = D_MODEL // 2
        for c in range(PACKED_SUBLANES):
            hp_ref[pl.ds(PACKED_SUBLANES * unit * u + c, unit, stride=PACKED_SUBLANES), :] = _pack_bf16_pair(
                h2[:, LANES * c:LANES * (c + 1)], h2[:, half + LANES * c:half + LANES * (c + 1)])
        return h2

    wr = wr_ref[...]
    wr_hi = wr.astype(BF16)
    wr_lo = (wr - wr_hi.astype(F32)).astype(BF16)
    wr_both = jnp.concatenate([wr_hi, wr_lo], axis=0)

    def logits(h2):
        h_hi = h2.astype(BF16)
        h_lo = (h2 - h_hi.astype(F32)).astype(BF16)
        lg_hi = lax.dot_general(wr_both, h_hi, _NT, preferred_element_type=F32)
        return (lg_hi[:ROUTER_ROWS] + lg_hi[ROUTER_ROWS:]
                + lax.dot_general(wr_hi, h_lo, _NT, preferred_element_type=F32) + br_ref[...])

    def route(lg):
        row = lax.broadcasted_iota(I32, lg.shape, 0)
        is_g = (row >= N_EXPERTS) & (row < N_EXPERTS + N_GROUPS)
        gl = jnp.where(is_g, lg, NEG_INF)
        ge = jnp.exp(gl - jnp.max(gl, axis=0, keepdims=True))
        gp = ge / jnp.sum(ge, axis=0, keepdims=True)
        g_gate = jnp.max(gp, axis=0, keepdims=True)
        g_idx = jnp.min(jnp.where(is_g & (gp == g_gate), row - N_EXPERTS, LANES), axis=0, keepdims=True)
        sel = (row // EXPERTS_PER_GROUP) == g_idx
        el = jnp.where(sel, lg, NEG_INF)
        ee = jnp.exp(el - jnp.max(el, axis=0, keepdims=True))
        ep = jnp.where(sel, ee / jnp.sum(ee, axis=0, keepdims=True), -1.0)
        p1 = jnp.max(ep, axis=0, keepdims=True)
        i1 = jnp.min(jnp.where(ep == p1, row, LANES), axis=0, keepdims=True)
        ep2 = jnp.where(row == i1, -1.0, ep)
        p2 = jnp.max(ep2, axis=0, keepdims=True)
        i2 = jnp.min(jnp.where(sel & (ep2 == p2) & (row != i1), row, LANES), axis=0, keepdims=True)
        den = p1 + p2
        return i1, i2, g_gate * p1 / den, g_gate * p2 / den, row == i1, row == i2

    h2s = [residual_stream(u) for u in range(MIX_UNITS)]
    lgs = [logits(h2) for h2 in h2s]
    routes = [route(lg) for lg in lgs]

    carry = carry_ref[...]
    row8 = lax.broadcasted_iota(I32, (idx_ref.shape[0], unit), 0)
    for u, (i1, i2, g1, g2, hit1, hit2) in enumerate(routes):
        onehot = jnp.where(hit1 | hit2, 1.0, 0.0)
        before = jnp.dot(onehot.astype(BF16), tri_ref[...], preferred_element_type=F32) + carry
        r1 = jnp.sum(jnp.where(hit1, before, 0.0), axis=0, keepdims=True).astype(I32)
        r2 = jnp.sum(jnp.where(hit2, before, 0.0), axis=0, keepdims=True).astype(I32)
        carry = carry + jnp.sum(onehot, axis=1, keepdims=True)
        idx_ref[:, units[u]] = jnp.where(row8 == 0, i1, jnp.where(row8 == 1, i2,
                                         jnp.where(row8 == 2, r1, jnp.where(row8 == 3, r2, 0))))
        gate_ref[:, units[u]] = jnp.where(row8 == 0, g1, jnp.where(row8 == 1, g2, 0.0))
    carry_ref[...] = carry

    @pl.when(i == pl.num_programs(0) - 1)
    def _():
        cnt_ref[...] = jnp.broadcast_to(carry_ref[...], cnt_ref.shape).astype(I32)


def _mix_call(oas, lses, ob, x2, g_a, g_b, w_out, g_ffn, w_router, b_router, tri, spread):
    t = x2.shape[0]
    tm = ROW_TILE
    nseq = oas[0].shape[2] // tm
    row = lambda i: (i, 0)
    const = lambda i: (0, 0)
    slab = lambda n: pl.BlockSpec((1, n, tm, LANES), lambda i: (i // nseq, 0, i % nseq, 0))
    o_slab = slab(A_HEADS // 4)
    lse_slab = pl.BlockSpec((1, tm, LANES), lambda i: (i // nseq, i % nseq, 0))
    return pl.pallas_call(
        _mix_kernel,
        grid=(t // tm,),
        in_specs=[
            o_slab, o_slab, o_slab, lse_slab, lse_slab, lse_slab,
            pl.BlockSpec((tm, B_WIDTH), row),
            pl.BlockSpec((tm, D_MODEL), row),
            pl.BlockSpec((1, A_WIDTH), const),
            pl.BlockSpec((1, B_WIDTH), const),
            pl.BlockSpec((D_MODEL, D_MODEL), const),
            pl.BlockSpec((1, D_MODEL), const),
            pl.BlockSpec((ROUTER_ROWS, D_MODEL), const),
            pl.BlockSpec((ROUTER_ROWS, 1), const),
            pl.BlockSpec((tm // MIX_UNITS, tm // MIX_UNITS), const),
            pl.BlockSpec((2 * LANES, A_WIDTH), const),
        ],
        out_specs=[
            pl.BlockSpec((tm, D_MODEL), row),
            pl.BlockSpec((tm * PACKED_SUBLANES, LANES), row),
            pl.BlockSpec((8, tm), lambda i: (0, i)),
            pl.BlockSpec((8, tm), lambda i: (0, i)),
            pl.BlockSpec((ROUTER_ROWS, LANES), const),
        ],
        out_shape=[
            jax.ShapeDtypeStruct((t, D_MODEL), F32),
            jax.ShapeDtypeStruct((t * PACKED_SUBLANES, LANES), U32),
            jax.ShapeDtypeStruct((8, t), I32),
            jax.ShapeDtypeStruct((8, t), F32),
            jax.ShapeDtypeStruct((ROUTER_ROWS, LANES), I32),
        ],
        scratch_shapes=[pltpu.VMEM((ROUTER_ROWS, 1), F32)],
        compiler_params=_cparams(("arbitrary",)),
        name="mix_router",
    )(*oas, *lses, ob, x2, g_a, g_b, w_out, g_ffn, w_router, b_router, tri, spread)


def _dest_kernel(idx_ref, pstart_ref, dest_ref):
    idx = idx_ref[...]
    row = lax.broadcasted_iota(I32, (ROUTER_ROWS, idx.shape[1]), 0)
    ps = pstart_ref[...]

    def slot(k):
        return jnp.sum(jnp.where(row == idx[k:k + 1, :], ps, 0), axis=0, keepdims=True) + idx[2 + k:3 + k, :]

    row8 = lax.broadcasted_iota(I32, idx.shape, 0)
    dest_ref[...] = jnp.where(row8 == 0, slot(0), jnp.where(row8 == 1, slot(1), 0))


def _dest_call(idx, pstart):
    t = idx.shape[1]
    tm = 4 * ROW_TILE
    return pl.pallas_call(
        _dest_kernel,
        grid=(t // tm,),
        in_specs=[pl.BlockSpec((8, tm), lambda i: (0, i)), pl.BlockSpec((ROUTER_ROWS, 1), lambda i: (0, 0))],
        out_specs=pl.BlockSpec((8, tm), lambda i: (0, i)),
        out_shape=jax.ShapeDtypeStruct((8, t), I32),
        compiler_params=_cparams(("parallel",)),
        name="dest_rows",
    )(idx, pstart)


def _dispatch_kernel(valid_ref, d0_ref, d1_ref, h_ref, buf_ref, zero_ref, sem, pad_sem):
    i = pl.program_id(0)
    tt = h_ref.shape[0] // PACKED_SUBLANES
    n_blocks = valid_ref.shape[0]

    def for_padded_blocks(fn):
        def body(j, c):
            taken = valid_ref[j]
            even = (taken + 1) // 2 * 2

            @pl.when(even < MOE_BLK)
            def _():
                n_pad = pl.multiple_of((MOE_BLK - even) * PACKED_SUBLANES, ROW_SUBLANES)
                first = pl.multiple_of((j * MOE_BLK + even) * PACKED_SUBLANES, ROW_SUBLANES)
                fn(pltpu.make_async_copy(zero_ref.at[pl.ds(0, n_pad)], buf_ref.at[pl.ds(first, n_pad)], pad_sem))

            @pl.when(even != taken)
            def _():
                fn(pltpu.make_async_copy(zero_ref.at[pl.ds(0, PACKED_SUBLANES)],
                                         buf_ref.at[_row_tile(j * MOE_BLK + taken, PACKED_SUBLANES)], pad_sem))
            return c
        lax.fori_loop(0, n_blocks, body, 0)

    @pl.when(i == 0)
    def _():
        zero_ref[...] = jnp.zeros_like(zero_ref)
        for_padded_blocks(lambda cp: cp.start())

    def issue(g, c):
        base = pl.multiple_of(g * ISSUE_GROUP, ISSUE_GROUP)
        for j in range(ISSUE_GROUP):
            for prio, d_ref in enumerate((d0_ref, d1_ref)):
                pltpu.make_async_copy(h_ref.at[_row_tile(base + j, PACKED_SUBLANES)],
                                      buf_ref.at[_row_tile(d_ref[base + j], PACKED_SUBLANES)],
                                      sem).start(priority=prio)
        return c

    lax.fori_loop(0, tt // ISSUE_GROUP, issue, 0)
    for k in range(TOP_K):
        pltpu.make_async_copy(h_ref, buf_ref.at[pl.ds(0, tt * PACKED_SUBLANES)], sem).wait()

    @pl.when(i == pl.num_programs(0) - 1)
    def _():
        for_padded_blocks(lambda cp: cp.wait())


def _dispatch_call(block_valid, dests, hp):
    t = hp.shape[0] // PACKED_SUBLANES
    tt = 8 * ROW_TILE
    n_rows = block_valid.shape[0] * MOE_BLK
    return pl.pallas_call(
        _dispatch_kernel,
        grid_spec=pltpu.PrefetchScalarGridSpec(
            num_scalar_prefetch=1,
            grid=(t // tt,),
            in_specs=[
                pl.BlockSpec((tt,), lambda i, va: (i,), memory_space=pltpu.SMEM),
                pl.BlockSpec((tt,), lambda i, va: (i,), memory_space=pltpu.SMEM),
                pl.BlockSpec((tt * PACKED_SUBLANES, LANES), lambda i, va: (i, 0)),
            ],
            out_specs=pl.BlockSpec(memory_space=pl.ANY),
            scratch_shapes=[pltpu.VMEM((MOE_BLK * PACKED_SUBLANES, LANES), U32),
                            pltpu.SemaphoreType.DMA(()), pltpu.SemaphoreType.DMA(())],
        ),
        out_shape=jax.ShapeDtypeStruct((n_rows * PACKED_SUBLANES, LANES), U32),
        compiler_params=_cparams(("arbitrary",), disable_bounds_checks=True, has_side_effects=True),
        name="dispatch",
    )(block_valid, *dests, hp)


def _expert_kernel(be_ref, new_ref, valid_ref, buf_ref, wg_ref, wu_ref, wd_ref, out_ref, wg_s, wu_s, wd_s):
    j = pl.program_id(0)
    del be_ref

    @pl.when(new_ref[j] == 1)
    def _():
        wg_s[...] = wg_ref[0, 0].astype(BF16)
        wu_s[...] = wu_ref[0, 0].astype(BF16)
        wd_s[...] = wd_ref[0, 0].astype(BF16)

    n_valid = valid_ref[j]

    @pl.when(n_valid > 0)
    def _():
        blk = buf_ref.shape[0] // PACKED_SUBLANES
        words = [_unpack_bf16_pair(buf_ref[pl.ds(c, blk, stride=PACKED_SUBLANES), :]) for c in range(PACKED_SUBLANES)]
        x = jnp.concatenate([w[0] for w in words] + [w[1] for w in words], axis=1).astype(BF16)
        cols = [pl.ds(c, blk, stride=ROW_SUBLANES) for c in range(ROW_SUBLANES)]
        g = jnp.dot(x, wg_s[...], preferred_element_type=F32)
        u = jnp.dot(x, wu_s[...], preferred_element_type=F32)
        hb = (g * jax.nn.sigmoid(g)) * u
        out = jnp.dot(hb.astype(BF16), wd_s[...], preferred_element_type=F32)
        for c, rows in enumerate(cols):
            out_ref[rows, :] = out[:, LANES * c:LANES * (c + 1)]

    @pl.when(n_valid == 0)
    def _():
        out_ref[...] = jnp.zeros_like(out_ref)


def _expert_call(block_expert, block_new, block_valid, buf, w_gate, w_up, w_down):
    nb = buf.shape[0] // (MOE_BLK * PACKED_SUBLANES)
    wsel = lambda j, be, nw, va: (0, be[j], 0, 0)
    rows = lambda sublanes: pl.BlockSpec((MOE_BLK * sublanes, LANES), lambda j, be, nw, va: (j, 0))
    return pl.pallas_call(
        _expert_kernel,
        grid_spec=pltpu.PrefetchScalarGridSpec(
            num_scalar_prefetch=3,
            grid=(nb,),
            in_specs=[
                rows(PACKED_SUBLANES),
                pl.BlockSpec((1, 1, D_MODEL, EXPERT_FF), wsel),
                pl.BlockSpec((1, 1, D_MODEL, EXPERT_FF), wsel),
                pl.BlockSpec((1, 1, EXPERT_FF, D_MODEL), wsel),
            ],
            out_specs=rows(ROW_SUBLANES),
            scratch_shapes=[pltpu.VMEM((D_MODEL, EXPERT_FF), BF16),
                            pltpu.VMEM((D_MODEL, EXPERT_FF), BF16),
                            pltpu.VMEM((EXPERT_FF, D_MODEL), BF16)],
        ),
        out_shape=jax.ShapeDtypeStruct((nb * MOE_BLK * ROW_SUBLANES, LANES), F32),
        compiler_params=_cparams(("arbitrary",)),
        name="experts",
    )(block_expert, block_new, block_valid, buf, w_gate, w_up, w_down)


def _combine_kernel(d0_ref, d1_ref, d0_next_ref, d1_next_ref, d0_ahead_ref, d1_ahead_ref, x1_ref, gate_ref,
                    gf_ref, eo_ref, o_ref, rows_a, rows_b, rows_c, sems):
    i = pl.program_id(0)
    tt = x1_ref.shape[0]
    bufs = (rows_a, rows_b, rows_c)
    n_buf = len(bufs)

    def start_row(d_refs, slot, r):
        for k, d_ref in enumerate(d_refs):
            pltpu.make_async_copy(eo_ref.at[_row_tile(d_ref[r])], bufs[slot].at[k, _row_tile(r)],
                                  sems.at[slot]).start(priority=k)

    def drain(slot):
        for k in range(TOP_K):
            pltpu.make_async_copy(eo_ref.at[pl.ds(0, tt * ROW_SUBLANES)], bufs[slot].at[k], sems.at[slot]).wait()

    @pl.when(i == 0)
    def _():
        def issue(g, c):
            base = pl.multiple_of(g * ISSUE_GROUP, ISSUE_GROUP)
            for j in range(ISSUE_GROUP):
                start_row((d0_ref, d1_ref), 0, base + j)
                start_row((d0_next_ref, d1_next_ref), 1, base + j)
            return c
        lax.fori_loop(0, tt // ISSUE_GROUP, issue, 0)

    def step(slot):
        ahead = (slot + 2) % n_buf
        drain(slot)
        gf = gf_ref[...]
        for ch in range(tt // COMBINE_CHUNK):
            r0 = ch * COMBINE_CHUNK
            for j in range(COMBINE_CHUNK):
                start_row((d0_ahead_ref, d1_ahead_ref), ahead, r0 + j)
            gate = gate_ref[r0:r0 + COMBINE_CHUNK, :]

            def rows(k):
                return jnp.concatenate(
                    [bufs[slot][k, pl.ds(ROW_SUBLANES * r0 + c, COMBINE_CHUNK, stride=ROW_SUBLANES), :]
                     for c in range(ROW_SUBLANES)], axis=1)

            y = rows(0) * gate[:, 0:1] + rows(1) * gate[:, 1:2]
            o_ref[r0:r0 + COMBINE_CHUNK, :] = _rms(x1_ref[r0:r0 + COMBINE_CHUNK, :] + y, gf)

        @pl.when(i == pl.num_programs(0) - 1)
        def _():
            drain((slot + 1) % n_buf)
            drain(ahead)

    for phase in range(n_buf):
        pl.when(i % n_buf == phase)(functools.partial(step, phase))


def _combine_call(dests, x1, gates, g_final, expert_out):
    t = x1.shape[0]
    tt = ROW_TILE // 2
    last = t // tt - 1
    tile = lambda ahead: pl.BlockSpec((tt,), lambda i: (jnp.minimum(i + ahead, last),), memory_space=pltpu.SMEM)
    buf = pltpu.VMEM((TOP_K, tt * ROW_SUBLANES, LANES), F32)
    return pl.pallas_call(
        _combine_kernel,
        grid=(t // tt,),
        in_specs=[
            tile(0), tile(0), tile(1), tile(1), tile(2), tile(2),
            pl.BlockSpec((tt, D_MODEL), lambda i: (i, 0)),
            pl.BlockSpec((tt, TOP_K), lambda i: (i, 0)),
            pl.BlockSpec((1, D_MODEL), lambda i: (0, 0)),
            pl.BlockSpec(memory_space=pl.ANY),
        ],
        out_specs=pl.BlockSpec((tt, D_MODEL), lambda i: (i, 0)),
        out_shape=jax.ShapeDtypeStruct((t, D_MODEL), F32),
        scratch_shapes=[buf, buf, buf, pltpu.SemaphoreType.DMA((3,))],
        compiler_params=_cparams(("arbitrary",), disable_bounds_checks=True),
        name="combine",
    )(*dests, *dests, *dests, x1, gates, g_final, expert_out)


def _rope_tables(seq):
    half = B_QK_ROPE // 2
    inv_freq = ROPE_THETA ** (-(jnp.arange(half, dtype=F32) / half))
    ang = jnp.arange(seq, dtype=F32)[:, None] * inv_freq[None, :]
    cos, sin = jnp.cos(ang), jnp.sin(ang)
    z = jnp.zeros((seq, B_QK_NOPE), F32)
    z2 = jnp.zeros((seq, B_QK_ROPE), F32)
    return (jnp.concatenate([z, cos, cos, z2], axis=1), jnp.concatenate([z, -sin, sin, z2], axis=1))


def _swap_halves(w):
    half = w.shape[-1] // 2
    return jnp.concatenate([w[..., half:], w[..., :half]], axis=-1)


def _layout_weights(w_in, w_q_up, w_kv_up):
    d = w_in.shape[0]
    w_kr = w_in[:, 3 * A_WIDTH + B_Q_LORA + B_KV_LORA:]
    w_in_l = jnp.concatenate(
        [w_in[:, :A_WIDTH] * (A_HEAD_DIM ** -0.5), w_in[:, A_WIDTH:3 * A_WIDTH + B_Q_LORA + B_KV_LORA],
         jnp.zeros((d, B_QK_NOPE), F32), w_kr, _swap_halves(w_kr)], axis=1).astype(BF16)
    wq = w_q_up.reshape(B_Q_LORA, B_HEADS, B_QK_NOPE + B_QK_ROPE)
    wq_l = jnp.concatenate([wq, _swap_halves(wq[..., B_QK_NOPE:])], axis=-1)
    wq_l = wq_l.reshape(B_Q_LORA, B_HEADS * LANES).astype(BF16)
    wkv = w_kv_up.reshape(B_KV_LORA, B_HEADS, B_QK_NOPE + B_V_DIM)
    wkb = jnp.concatenate([wkv[..., :B_QK_NOPE], jnp.zeros_like(wkv[..., :B_QK_NOPE])], axis=-1)
    wkb = wkb.reshape(B_KV_LORA, B_HEADS * LANES).astype(BF16)
    wvb = wkv[..., B_QK_NOPE:].reshape(B_KV_LORA, B_WIDTH).astype(BF16)
    return w_in_l, wq_l, wkb, wvb


def _block_plan(counts, n_blocks):
    padded = (counts + MOE_BLK - 1) // MOE_BLK * MOE_BLK
    ends = jnp.cumsum(padded)
    starts = ends - padded
    first_row = jnp.arange(n_blocks, dtype=I32) * MOE_BLK
    expert = jnp.minimum(jnp.sum(ends[None, :] <= first_row[:, None], axis=1), N_EXPERTS - 1).astype(I32)
    new = jnp.concatenate([jnp.ones((1,), I32), (expert[1:] != expert[:-1]).astype(I32)])
    valid = jnp.clip((starts + counts)[expert] - first_row, 0, MOE_BLK).astype(I32)
    return starts.astype(I32), expert, new, valid


def kernel(x, g_attn_norm, w_in, rel_bias, g_q_latent, w_q_up, g_kv_latent, w_kv_up, g_out_a, g_out_b, w_out,
           g_ffn_norm, w_router_group, b_router_group, w_router_expert, b_router_expert, w_gate, w_up, w_down,
           g_final):
    batch, seq, d = x.shape
    t = batch * seq
    assert g_attn_norm.shape[0] == 1 and d == D_MODEL and seq % ROW_TILE == 0
    cos_t, sin_t = _rope_tables(seq)
    unit = ROW_TILE // MIX_UNITS
    tri = jnp.triu(jnp.ones((unit, unit), F32), 1).astype(BF16)
    spread = (jnp.arange(2 * LANES)[:, None] % LANES == jnp.arange(A_WIDTH)[None, :] // A_HEAD_DIM).astype(BF16)
    n_blocks = t * TOP_K // MOE_BLK + N_EXPERTS
    x2 = x.reshape(t, d)
    row = lambda v: v.reshape(1, -1)

    w_in_l, wq_l, wkb_l, wvb_l = _layout_weights(w_in[0], w_q_up[0], w_kv_up[0])
    *qkv_a, qb, kb, vb = _proj_call(x2, row(g_attn_norm[0]), w_in_l, row(g_q_latent[0]), wq_l,
                                    row(g_kv_latent[0]), wkb_l, wvb_l, cos_t, sin_t, seq)
    oas, lses = [], []
    for pi, (window, dilation) in enumerate(DILATED_PATTERNS):
        bias = _dilated_bias(rel_bias, seq, dilation, window // (2 * dilation))
        o_p, lse_p = _dilated_call(qkv_a[pi], bias, batch, seq, dilation)
        oas.append(o_p)
        lses.append(lse_p)
    ob = _mla_call(qb, kb, vb, batch, seq)

    pad = ROUTER_ROWS - N_EXPERTS - N_GROUPS
    w_router = jnp.concatenate([w_router_expert[0], w_router_group[0], jnp.zeros((d, pad), F32)], axis=1).T
    b_router = jnp.concatenate([b_router_expert[0], b_router_group[0], jnp.zeros((pad,), F32)])
    x1, hp, idx, gates, cnt = _mix_call(oas, lses, ob, x2, row(g_out_a[0]), row(g_out_b[0]), w_out[0].astype(BF16),
                                        row(g_ffn_norm[0]), w_router, b_router.reshape(-1, 1), tri, spread)
    pstart, block_expert, block_new, block_valid = _block_plan(cnt[:N_EXPERTS, 0], n_blocks)
    pstart_col = jnp.concatenate([pstart, jnp.zeros((ROUTER_ROWS - N_EXPERTS,), I32)]).reshape(-1, 1)
    dest = _dest_call(idx, pstart_col)
    dests = (dest[0], dest[1])
    buf = _dispatch_call(block_valid, dests, hp)
    expert_out = _expert_call(block_expert, block_new, block_valid, buf, w_gate, w_up, w_down)
    return _combine_call(dests, x1, gates[:TOP_K].T, row(g_final), expert_out).reshape(batch, seq, d)
```

```python
import functools
import math

import numpy as np
import jax
import jax.numpy as jnp
from jax import lax
from jax.experimental import pallas as pl
from jax.experimental.pallas import tpu as pltpu

F32 = jnp.float32
BF16 = jnp.bfloat16
I32 = jnp.int32
U32 = jnp.uint32

D_MODEL = 1024
EPS = 1e-6
NEG_INF = -1e30
LANES = 128
ROW_SUBLANES = D_MODEL // LANES
PACKED_SUBLANES = ROW_SUBLANES // 2

A_HEADS = 8
A_HEAD_DIM = 64
A_WIDTH = 512
A_QKV_WIDTH = 3 * A_WIDTH
DILATED_PATTERNS = ((128, 1), (512, 4), (2048, 16))
REL_BUCKETS = 32
REL_MAX_DISTANCE = 1024
A_QB = 128

B_HEADS = 8
B_Q_LORA = 256
B_KV_LORA = 128
B_QK_NOPE = 64
B_QK_ROPE = 32
B_V_DIM = 64
B_WIDTH = 512
ROPE_THETA = 10000.0
B_SCALE = (B_QK_NOPE + B_QK_ROPE) ** -0.5
B_QB = 512
B_SUB = 256
LOG2E = math.log2(math.e)
ROPE_SHIFT = LANES - B_QK_ROPE

N_GROUPS = 4
EXPERTS_PER_GROUP = 8
N_EXPERTS = 32
TOP_K = 2
EXPERT_FF = 256
MOE_BLK = 512
ROUTER_ROWS = 40
MIX_UNITS = 2

ROW_TILE = 512
ISSUE_GROUP = 8
COMBINE_CHUNK = 32
PROJ_COLS = 2048

_NT = (((1,), (1,)), ((), ()))


VMEM_LIMIT_MB = 48


def _cparams(semantics, **kw):
    return pltpu.CompilerParams(dimension_semantics=semantics,
                                vmem_limit_bytes=VMEM_LIMIT_MB * 1024 * 1024, **kw)


def _rms(x, g):
    return x * lax.rsqrt(jnp.mean(x * x, axis=-1, keepdims=True) + EPS) * g


def _lane_iota(rows=1):
    return lax.broadcasted_iota(I32, (rows, LANES), 1)


def _row_tile(r, sublanes=ROW_SUBLANES):
    return pl.ds(pl.multiple_of(r * sublanes, sublanes), sublanes)


def _proj_kernel(x_ref, g_ref, win_ref, gq_ref, wq_ref, gkv_ref, wkb_ref, wvb_ref, cos_ref, sin_ref, *refs):
    out1, out4, out16 = refs[:3]
    qb_ref, kb_ref, vb_ref, slab_ref, slab4_ref = refs[3:]
    tm = x_ref.shape[0]
    h = _rms(x_ref[...], g_ref[...]).astype(BF16)
    lo = _lane_iota() < A_HEAD_DIM
    n_slabs = A_QKV_WIDTH // LANES
    parts = []
    for s in range(0, n_slabs, 2):
        part = jnp.dot(h, win_ref[:, LANES * s:LANES * (s + 2)], preferred_element_type=F32)
        if LANES * s < A_WIDTH:
            part = part * LOG2E
        slab_ref[s] = part[:, :LANES]
        slab_ref[s + 1] = part[:, LANES:]
        parts.append(part)
    out1[0, 0] = jnp.concatenate(parts, axis=1).astype(BF16)
    lat = jnp.dot(h, win_ref[:, A_QKV_WIDTH:], preferred_element_type=F32)
    cos = cos_ref[...]
    sin = sin_ref[...]
    n4 = tm // 4

    cq = _rms(lat[:, :B_Q_LORA], gq_ref[...]).astype(BF16)
    q = jnp.dot(cq, wq_ref[...], preferred_element_type=F32)
    q_mul = (cos + jnp.where(lo, 1.0, 0.0)) * (B_SCALE * LOG2E)
    q_rot = sin * (B_SCALE * LOG2E)
    qb_ref[...] = jnp.concatenate(
        [q[:, LANES * hd:LANES * (hd + 1)] * q_mul + pltpu.roll(q[:, LANES * hd:LANES * (hd + 1)], ROPE_SHIFT, 1) * q_rot
         for hd in range(B_HEADS)], axis=1).astype(BF16)

    ckv = _rms(lat[:, B_Q_LORA:B_Q_LORA + B_KV_LORA], gkv_ref[...]).astype(BF16)
    kr = lat[:, B_Q_LORA + B_KV_LORA:]
    kr = kr * cos + pltpu.roll(kr, ROPE_SHIFT, 1) * sin
    kn = jnp.dot(ckv, wkb_ref[...], preferred_element_type=F32)
    kb_ref[...] = jnp.concatenate([kn[:, LANES * hd:LANES * (hd + 1)] + kr for hd in range(B_HEADS)],
                                  axis=1).astype(BF16)
    vb_ref[...] = jnp.dot(ckv, wvb_ref[...], preferred_element_type=F32).astype(BF16)

    for c4 in range(4):
        pieces = [slab_ref[s, pl.ds(c4, n4, stride=4), :] for s in range(n_slabs)]
        for s, piece in enumerate(pieces):
            slab4_ref[s, c4] = piece
        out4[0, c4] = jnp.concatenate(pieces, axis=1).astype(BF16)
    for c4 in range(4):
        for j in range(4):
            out16[0, c4 + 4 * j] = jnp.concatenate(
                [slab4_ref[s, c4, pl.ds(j, n4 // 4, stride=4), :] for s in range(n_slabs)], axis=1).astype(BF16)


def _proj_call(x2, g_attn, w_in, g_q, w_q, g_kv, w_kb, w_vb, cos_t, sin_t, seq):
    t = x2.shape[0]
    tm = ROW_TILE
    nseq = seq // tm
    row = lambda i: (i, 0)
    const = lambda i: (0, 0)
    pos = lambda i: (i % nseq, 0)
    out = lambda w: jax.ShapeDtypeStruct((t, w), BF16)
    a_specs, a_shapes = [], []
    for _, r in DILATED_PATTERNS:
        a_specs.append(pl.BlockSpec((1, r, tm // r, A_QKV_WIDTH), lambda i: (i // nseq, 0, i % nseq, 0)))
        a_shapes.append(jax.ShapeDtypeStruct((t // seq, r, seq // r, A_QKV_WIDTH), BF16))
    return pl.pallas_call(
        _proj_kernel,
        grid=(t // tm,),
        in_specs=[
            pl.BlockSpec((tm, D_MODEL), row),
            pl.BlockSpec((1, D_MODEL), const),
            pl.BlockSpec((D_MODEL, PROJ_COLS), const),
            pl.BlockSpec((1, B_Q_LORA), const),
            pl.BlockSpec((B_Q_LORA, B_HEADS * LANES), const),
            pl.BlockSpec((1, B_KV_LORA), const),
            pl.BlockSpec((B_KV_LORA, B_HEADS * LANES), const),
            pl.BlockSpec((B_KV_LORA, B_WIDTH), const),
            pl.BlockSpec((tm, LANES), pos),
            pl.BlockSpec((tm, LANES), pos),
        ],
        out_specs=a_specs + [
            pl.BlockSpec((tm, B_HEADS * LANES), row),
            pl.BlockSpec((tm, B_HEADS * LANES), row),
            pl.BlockSpec((tm, B_WIDTH), row),
        ],
        out_shape=a_shapes + [out(B_HEADS * LANES), out(B_HEADS * LANES), out(B_WIDTH)],
        scratch_shapes=[pltpu.VMEM((A_QKV_WIDTH // LANES, tm, LANES), F32),
                        pltpu.VMEM((A_QKV_WIDTH // LANES, 4, tm // 4, LANES), F32)],
        compiler_params=_cparams(("parallel",)),
        name="proj",
    )(x2, g_attn, w_in, g_q, w_q, g_kv, w_kb, w_vb, cos_t, sin_t)


def _pack_bf16_pair(a, b):
    a_bits = lax.bitcast_convert_type(a.astype(BF16).astype(F32), U32) >> 16
    b_bits = lax.bitcast_convert_type(b.astype(BF16).astype(F32), U32) & jnp.uint32(0xFFFF0000)
    return a_bits | b_bits


def _unpack_bf16_pair(w):
    return (lax.bitcast_convert_type(w << 16, F32), lax.bitcast_convert_type(w & jnp.uint32(0xFFFF0000), F32))


def _dilated_kernel(qkv_ref, bias_ref, o_ref, lse_ref, *, seq_len, dilation, key_width, group):
    nblk = seq_len // A_QB
    lane = _lane_iota()
    lo = lane < A_HEAD_DIM
    first_class = pl.program_id(1) * group
    pairs = A_HEADS // 2

    def block(it, carry):
        c = it // nblk
        if nblk == 1:
            q0, ks, var = 0, 0, 0
        else:
            n = it % nblk
            q0 = pl.multiple_of(n * A_QB, A_QB)
            ks = pl.multiple_of(jnp.clip(q0 - 64, 0, seq_len - key_width), 64)
            var = jnp.where(n == 0, 0, jnp.where(n == nblk - 1, 2, 1))
        rows = pl.ds(q0, A_QB)
        keys = pl.ds(ks, key_width)
        if dilation == 1:
            out_rows = rows
        else:
            out_rows = pl.ds(first_class + c + dilation * q0, A_QB, stride=dilation)
        q_tiles = [qkv_ref[0, c, rows, LANES * p:LANES * (p + 1)] for p in range(pairs)]
        k_tiles = [qkv_ref[0, c, keys, A_WIDTH + LANES * p:A_WIDTH + LANES * (p + 1)] for p in range(pairs)]
        v_tiles = [qkv_ref[0, c, keys, 2 * A_WIDTH + LANES * p:2 * A_WIDTH + LANES * (p + 1)] for p in range(pairs)]
        zero = jnp.zeros((), BF16)
        k_heads = [jnp.where(lo, k_tiles[hd // 2], zero) if hd % 2 == 0 else jnp.where(lo, zero, k_tiles[hd // 2])
                   for hd in range(A_HEADS)]
        scores = [lax.dot_general(q_tiles[hd // 2], k_heads[hd], _NT, preferred_element_type=F32) + bias_ref[var, hd]
                  for hd in range(A_HEADS)]
        maxes = [jnp.max(s, axis=-1, keepdims=True) for s in scores]
        probs = [jnp.exp2(s - m) for s, m in zip(scores, maxes)]
        dens = [jnp.sum(pr, axis=-1, keepdims=True) for pr in probs]
        pvs = [jnp.dot(pr.astype(BF16), v_tiles[hd // 2], preferred_element_type=F32) for hd, pr in enumerate(probs)]
        outs = []
        for p in range(pairs):
            h0, h1 = 2 * p, 2 * p + 1
            outs.append(jnp.where(lo, pvs[h0] * (1.0 / dens[h0]), pvs[h1] * (1.0 / dens[h1])))
        lse = jnp.zeros((A_QB, LANES), F32)
        for hd in range(A_HEADS):
            lse = jnp.where(lane == hd, maxes[hd] + jnp.log2(dens[hd]), lse)
        lse_ref[0, out_rows, :] = lse
        for j in range(pairs // 2):
            o_ref[0, j, out_rows, :] = _pack_bf16_pair(outs[2 * j], outs[2 * j + 1])
        return carry

    lax.fori_loop(0, group * nblk, block, 0, unroll=4)


def _dilated_call(qkv, bias, batch, seq, dilation):
    r = dilation
    sl = seq // r
    kw = min(2 * A_QB, sl)
    group = r
    pairs = A_HEADS // 2
    return pl.pallas_call(
        functools.partial(_dilated_kernel, seq_len=sl, dilation=r, key_width=kw, group=group),
        grid=(batch, r // group),
        in_specs=[pl.BlockSpec((1, group, sl, A_QKV_WIDTH), lambda b, c: (b, c, 0, 0)),
                  pl.BlockSpec(bias.shape, lambda b, c: (0, 0, 0, 0))],
        out_specs=[pl.BlockSpec((1, pairs // 2, seq, LANES), lambda b, c: (b, 0, 0, 0)),
                   pl.BlockSpec((1, seq, LANES), lambda b, c: (b, 0, 0))],
        out_shape=[jax.ShapeDtypeStruct((batch, pairs // 2, seq, LANES), U32),
                   jax.ShapeDtypeStruct((batch, seq, LANES), F32)],
        compiler_params=_cparams(("parallel", "arbitrary")),
        name=f"dilated_r{r}",
    )(qkv, bias)


def _t5_bucket(rel):
    half = REL_BUCKETS // 2
    max_exact = half // 2
    n = np.abs(rel)
    large = max_exact + (np.log(np.maximum(n, 1) / max_exact)
                         / math.log(REL_MAX_DISTANCE / max_exact) * (half - max_exact)).astype(np.int32)
    large = np.minimum(large, half - 1)
    return (np.where(rel > 0, half, 0) + np.where(n < max_exact, n, large)).astype(np.int32)


def _dilated_bias(rel_bias, seq, dilation, half_steps):
    sl = seq // dilation
    kw = min(2 * A_QB, sl)
    offsets = [0] if sl == kw else [0, -half_steps, A_QB - kw]
    rel = np.stack([np.arange(kw)[None, :] + off - np.arange(A_QB)[:, None] for off in offsets])
    valid = np.abs(rel) <= half_steps
    bucket = np.where(valid, _t5_bucket(rel * dilation), REL_BUCKETS).astype(np.int32)
    onehot = (jnp.asarray(bucket)[..., None] == jnp.arange(REL_BUCKETS + 1, dtype=I32)).astype(F32)
    table = jnp.concatenate([rel_bias.astype(F32), jnp.full((1, A_HEADS), NEG_INF, F32)], axis=0)
    return jnp.einsum("vqkb,bh->vhqk", onehot, table * LOG2E, precision=lax.Precision.HIGHEST)


def _mla_kernel(q_ref, k_ref, v_ref, o_ref, v1_ref):
    lo = _lane_iota() < B_V_DIM
    sub = B_QB // B_SUB
    tiles = [slice(0, LANES), slice(LANES, 2 * LANES)]
    v1_ref[:, :LANES] = v_ref[0]
    v1_ref[:, LANES:] = jnp.ones((v_ref.shape[1], LANES), BF16)

    units = [(j, half) for j in range(sub) for half in range(2)]

    def rows(i):
        return [pl.ds(i * B_QB + B_SUB * j, B_SUB) for j in range(sub)]

    def scores(i):
        return [lax.dot_general(q_ref[0, rows(i)[j], tiles[half]], k_ref[0, :, tiles[half]], _NT,
                                preferred_element_type=F32) for j, half in units]

    def finish(i, sc):
        maxes = [jnp.max(s, axis=-1, keepdims=True) for s in sc]
        probs = [jnp.exp2(s - m) for s, m in zip(sc, maxes)]
        pvs = [jnp.dot(pr.astype(BF16), v1_ref[...], preferred_element_type=F32) for pr in probs]
        outs = [pv[:, :LANES] * (1.0 / pv[:, LANES:]) for pv in pvs]
        for j in range(sub):
            o_ref[0, rows(i)[j], :] = jnp.where(lo, outs[2 * j], outs[2 * j + 1]).astype(BF16)

    n_blocks = q_ref.shape[1] // B_QB
    sc = scores(0)
    for i in range(n_blocks):
        sc_next = scores(i + 1) if i + 1 < n_blocks else None
        finish(i, sc)
        sc = sc_next


def _mla_call(qb, kb, vb, batch, seq):
    qb = qb.reshape(batch, seq, B_HEADS * LANES)
    kb = kb.reshape(batch, seq, B_HEADS * LANES)
    vb = vb.reshape(batch, seq, B_WIDTH)
    pair = lambda w: pl.BlockSpec((1, seq, w), lambda b, p: (b, 0, p))
    out = pl.pallas_call(
        _mla_kernel,
        grid=(batch, B_HEADS // 2),
        in_specs=[pair(2 * LANES), pair(2 * LANES), pair(LANES)],
        out_specs=pair(LANES),
        out_shape=jax.ShapeDtypeStruct((batch, seq, B_WIDTH), BF16),
        scratch_shapes=[pltpu.VMEM((seq, 2 * LANES), BF16)],
        compiler_params=_cparams(("parallel", "parallel")),
        name="mla",
    )(qb, kb, vb)
    return out.reshape(batch * seq, B_WIDTH)


def _merge_patterns(o_refs, lse_refs, spread_ref, rows):
    lses = [r[0, rows, :] for r in lse_refs]
    top = functools.reduce(jnp.maximum, lses)
    es = [jnp.exp2(l - top) for l in lses]
    inv = 1.0 / functools.reduce(jnp.add, es)
    spread = spread_ref[...]
    weights = []
    for e in es:
        w = e * inv
        w_hi = w.astype(BF16)
        w_lo = (w - w_hi.astype(F32)).astype(BF16)
        weights.append(jnp.dot(jnp.concatenate([w_hi, w_lo], axis=1), spread, preferred_element_type=F32))
    tiles = []
    for j in range(A_HEADS // 4):
        outs = [_unpack_bf16_pair(r[0, j, rows, :]) for r in o_refs]
        for half in range(2):
            tile = slice(LANES * (2 * j + half), LANES * (2 * j + half + 1))
            tiles.append(functools.reduce(jnp.add, [w[:, tile] * o[half] for w, o in zip(weights, outs)]))
    return jnp.concatenate(tiles, axis=1)


def _mix_kernel(o1_ref, o4_ref, o16_ref, l1_ref, l4_ref, l16_ref, ob_ref, x_ref, ga_ref, gb_ref, wo_ref,
                gf_ref, wr_ref, br_ref, tri_ref, spread_ref, x1_ref, hp_ref, idx_ref, gate_ref, cnt_ref, carry_ref):
    i = pl.program_id(0)
    unit = x_ref.shape[0] // MIX_UNITS
    units = [slice(unit * u, unit * (u + 1)) for u in range(MIX_UNITS)]

    @pl.when(i == 0)
    def _():
        carry_ref[...] = jnp.zeros_like(carry_ref)

    def residual_stream(u):
        rows = units[u]
        oa = _merge_patterns((o1_ref, o4_ref, o16_ref), (l1_ref, l4_ref, l16_ref), spread_ref, rows)
        a = _rms(oa, ga_ref[...]).astype(BF16)
        b = _rms(ob_ref[rows, :].astype(F32), gb_ref[...]).astype(BF16)
        mix = (jnp.dot(a, wo_ref[0:A_WIDTH, :], preferred_element_type=F32)
               + jnp.dot(b, wo_ref[A_WIDTH:, :], preferred_element_type=F32))
        x1 = x_ref[rows, :] + mix
        x1_ref[rows, :] = x1
        h2 = _rms(x1, gf_ref[...])
        half = D_MODEL // 2
        for c in range(PACKED_SUBLANES):
            hp_ref[pl.ds(PACKED_SUBLANES * unit * u + c, unit, stride=PACKED_SUBLANES), :] = _pack_bf16_pair(
                h2[:, LANES * c:LANES * (c + 1)], h2[:, half + LANES * c:half + LANES * (c + 1)])
        return h2

    wr = wr_ref[...]
    wr_hi = wr.astype(BF16)
    wr_lo = (wr - wr_hi.astype(F32)).astype(BF16)
    wr_both = jnp.concatenate([wr_hi, wr_lo], axis=0)

    def logits(h2):
        h_hi = h2.astype(BF16)
        h_lo = (h2 - h_hi.astype(F32)).astype(BF16)
        lg_hi = lax.dot_general(wr_both, h_hi, _NT, preferred_element_type=F32)
        return (lg_hi[:ROUTER_ROWS] + lg_hi[ROUTER_ROWS:]
                + lax.dot_general(wr_hi, h_lo, _NT, preferred_element_type=F32) + br_ref[...])

    def route(lg):
        row = lax.broadcasted_iota(I32, lg.shape, 0)
        is_g = (row >= N_EXPERTS) & (row < N_EXPERTS + N_GROUPS)
        gl = jnp.where(is_g, lg, NEG_INF)
        ge = jnp.exp(gl - jnp.max(gl, axis=0, keepdims=True))
        gp = ge / jnp.sum(ge, axis=0, keepdims=True)
        g_gate = jnp.max(gp, axis=0, keepdims=True)
        g_idx = jnp.min(jnp.where(is_g & (gp == g_gate), row - N_EXPERTS, LANES), axis=0, keepdims=True)
        sel = (row // EXPERTS_PER_GROUP) == g_idx
        el = jnp.where(sel, lg, NEG_INF)
        ee = jnp.exp(el - jnp.max(el, axis=0, keepdims=True))
        ep = jnp.where(sel, ee / jnp.sum(ee, axis=0, keepdims=True), -1.0)
        p1 = jnp.max(ep, axis=0, keepdims=True)
        i1 = jnp.min(jnp.where(ep == p1, row, LANES), axis=0, keepdims=True)
        ep2 = jnp.where(row == i1, -1.0, ep)
        p2 = jnp.max(ep2, axis=0, keepdims=True)
        i2 = jnp.min(jnp.where(sel & (ep2 == p2) & (row != i1), row, LANES), axis=0, keepdims=True)
        den = p1 + p2
        return i1, i2, g_gate * p1 / den, g_gate * p2 / den, row == i1, row == i2

    h2s = [residual_stream(u) for u in range(MIX_UNITS)]
    lgs = [logits(h2) for h2 in h2s]
    routes = [route(lg) for lg in lgs]

    carry = carry_ref[...]
    row8 = lax.broadcasted_iota(I32, (idx_ref.shape[0], unit), 0)
    for u, (i1, i2, g1, g2, hit1, hit2) in enumerate(routes):
        onehot = jnp.where(hit1 | hit2, 1.0, 0.0)
        before = jnp.dot(onehot.astype(BF16), tri_ref[...], preferred_element_type=F32) + carry
        r1 = jnp.sum(jnp.where(hit1, before, 0.0), axis=0, keepdims=True).astype(I32)
        r2 = jnp.sum(jnp.where(hit2, before, 0.0), axis=0, keepdims=True).astype(I32)
        carry = carry + jnp.sum(onehot, axis=1, keepdims=True)
        idx_ref[:, units[u]] = jnp.where(row8 == 0, i1, jnp.where(row8 == 1, i2,
                                         jnp.where(row8 == 2, r1, jnp.where(row8 == 3, r2, 0))))
        gate_ref[:, units[u]] = jnp.where(row8 == 0, g1, jnp.where(row8 == 1, g2, 0.0))
    carry_ref[...] = carry

    @pl.when(i == pl.num_programs(0) - 1)
    def _():
        cnt_ref[...] = jnp.broadcast_to(carry_ref[...], cnt_ref.shape).astype(I32)


def _mix_call(oas, lses, ob, x2, g_a, g_b, w_out, g_ffn, w_router, b_router, tri, spread):
    t = x2.shape[0]
    tm = ROW_TILE
    nseq = oas[0].shape[2] // tm
    row = lambda i: (i, 0)
    const = lambda i: (0, 0)
    slab = lambda n: pl.BlockSpec((1, n, tm, LANES), lambda i: (i // nseq, 0, i % nseq, 0))
    o_slab = slab(A_HEADS // 4)
    lse_slab = pl.BlockSpec((1, tm, LANES), lambda i: (i // nseq, i % nseq, 0))
    return pl.pallas_call(
        _mix_kernel,
        grid=(t // tm,),
        in_specs=[
            o_slab, o_slab, o_slab, lse_slab, lse_slab, lse_slab,
            pl.BlockSpec((tm, B_WIDTH), row),
            pl.BlockSpec((tm, D_MODEL), row),
            pl.BlockSpec((1, A_WIDTH), const),
            pl.BlockSpec((1, B_WIDTH), const),
            pl.BlockSpec((D_MODEL, D_MODEL), const),
            pl.BlockSpec((1, D_MODEL), const),
            pl.BlockSpec((ROUTER_ROWS, D_MODEL), const),
            pl.BlockSpec((ROUTER_ROWS, 1), const),
            pl.BlockSpec((tm // MIX_UNITS, tm // MIX_UNITS), const),
            pl.BlockSpec((2 * LANES, A_WIDTH), const),
        ],
        out_specs=[
            pl.BlockSpec((tm, D_MODEL), row),
            pl.BlockSpec((tm * PACKED_SUBLANES, LANES), row),
            pl.BlockSpec((8, tm), lambda i: (0, i)),
            pl.BlockSpec((8, tm), lambda i: (0, i)),
            pl.BlockSpec((ROUTER_ROWS, LANES), const),
        ],
        out_shape=[
            jax.ShapeDtypeStruct((t, D_MODEL), F32),
            jax.ShapeDtypeStruct((t * PACKED_SUBLANES, LANES), U32),
            jax.ShapeDtypeStruct((8, t), I32),
            jax.ShapeDtypeStruct((8, t), F32),
            jax.ShapeDtypeStruct((ROUTER_ROWS, LANES), I32),
        ],
        scratch_shapes=[pltpu.VMEM((ROUTER_ROWS, 1), F32)],
        compiler_params=_cparams(("arbitrary",)),
        name="mix_router",
    )(*oas, *lses, ob, x2, g_a, g_b, w_out, g_ffn, w_router, b_router, tri, spread)


def _dest_kernel(idx_ref, pstart_ref, dest_ref):
    idx = idx_ref[...]
    row = lax.broadcasted_iota(I32, (ROUTER_ROWS, idx.shape[1]), 0)
    ps = pstart_ref[...]

    def slot(k):
        return jnp.sum(jnp.where(row == idx[k:k + 1, :], ps, 0), axis=0, keepdims=True) + idx[2 + k:3 + k, :]

    row8 = lax.broadcasted_iota(I32, idx.shape, 0)
    dest_ref[...] = jnp.where(row8 == 0, slot(0), jnp.where(row8 == 1, slot(1), 0))


def _dest_call(idx, pstart):
    t = idx.shape[1]
    tm = 4 * ROW_TILE
    return pl.pallas_call(
        _dest_kernel,
        grid=(t // tm,),
        in_specs=[pl.BlockSpec((8, tm), lambda i: (0, i)), pl.BlockSpec((ROUTER_ROWS, 1), lambda i: (0, 0))],
        out_specs=pl.BlockSpec((8, tm), lambda i: (0, i)),
        out_shape=jax.ShapeDtypeStruct((8, t), I32),
        compiler_params=_cparams(("parallel",)),
        name="dest_rows",
    )(idx, pstart)


def _dispatch_kernel(valid_ref, d0_ref, d1_ref, h_ref, buf_ref, zero_ref, sem, pad_sem):
    i = pl.program_id(0)
    tt = h_ref.shape[0] // PACKED_SUBLANES
    n_blocks = valid_ref.shape[0]

    def for_padded_blocks(fn):
        def body(j, c):
            taken = valid_ref[j]
            even = (taken + 1) // 2 * 2

            @pl.when(even < MOE_BLK)
            def _():
                n_pad = pl.multiple_of((MOE_BLK - even) * PACKED_SUBLANES, ROW_SUBLANES)
                first = pl.multiple_of((j * MOE_BLK + even) * PACKED_SUBLANES, ROW_SUBLANES)
                fn(pltpu.make_async_copy(zero_ref.at[pl.ds(0, n_pad)], buf_ref.at[pl.ds(first, n_pad)], pad_sem))

            @pl.when(even != taken)
            def _():
                fn(pltpu.make_async_copy(zero_ref.at[pl.ds(0, PACKED_SUBLANES)],
                                         buf_ref.at[_row_tile(j * MOE_BLK + taken, PACKED_SUBLANES)], pad_sem))
            return c
        lax.fori_loop(0, n_blocks, body, 0)

    @pl.when(i == 0)
    def _():
        zero_ref[...] = jnp.zeros_like(zero_ref)
        for_padded_blocks(lambda cp: cp.start())

    def issue(g, c):
        base = pl.multiple_of(g * ISSUE_GROUP, ISSUE_GROUP)
        for j in range(ISSUE_GROUP):
            for prio, d_ref in enumerate((d0_ref, d1_ref)):
                pltpu.make_async_copy(h_ref.at[_row_tile(base + j, PACKED_SUBLANES)],
                                      buf_ref.at[_row_tile(d_ref[base + j], PACKED_SUBLANES)],
                                      sem).start(priority=prio)
        return c

    lax.fori_loop(0, tt // ISSUE_GROUP, issue, 0)
    for k in range(TOP_K):
        pltpu.make_async_copy(h_ref, buf_ref.at[pl.ds(0, tt * PACKED_SUBLANES)], sem).wait()

    @pl.when(i == pl.num_programs(0) - 1)
    def _():
        for_padded_blocks(lambda cp: cp.wait())


def _dispatch_call(block_valid, dests, hp):
    t = hp.shape[0] // PACKED_SUBLANES
    tt = 8 * ROW_TILE
    assert t % tt == 0
    n_rows = block_valid.shape[0] * MOE_BLK
    return pl.pallas_call(
        _dispatch_kernel,
        grid_spec=pltpu.PrefetchScalarGridSpec(
            num_scalar_prefetch=1,
            grid=(t // tt,),
            in_specs=[
                pl.BlockSpec((tt,), lambda i, va: (i,), memory_space=pltpu.SMEM),
                pl.BlockSpec((tt,), lambda i, va: (i,), memory_space=pltpu.SMEM),
                pl.BlockSpec((tt * PACKED_SUBLANES, LANES), lambda i, va: (i, 0)),
            ],
            out_specs=pl.BlockSpec(memory_space=pl.ANY),
            scratch_shapes=[pltpu.VMEM((MOE_BLK * PACKED_SUBLANES, LANES), U32),
                            pltpu.SemaphoreType.DMA(()), pltpu.SemaphoreType.DMA(())],
        ),
        out_shape=jax.ShapeDtypeStruct((n_rows * PACKED_SUBLANES, LANES), U32),
        compiler_params=_cparams(("arbitrary",), disable_bounds_checks=True, has_side_effects=True),
        name="dispatch",
    )(block_valid, *dests, hp)


def _expert_kernel(be_ref, new_ref, valid_ref, buf_ref, wg_ref, wu_ref, wd_ref, out_ref, wg_s, wu_s, wd_s):
    j = pl.program_id(0)
    del be_ref

    @pl.when(new_ref[j] == 1)
    def _():
        wg_s[...] = wg_ref[0, 0].astype(BF16)
        wu_s[...] = wu_ref[0, 0].astype(BF16)
        wd_s[...] = wd_ref[0, 0].astype(BF16)

    n_valid = valid_ref[j]

    @pl.when(n_valid > 0)
    def _():
        blk = buf_ref.shape[0] // PACKED_SUBLANES
        words = [_unpack_bf16_pair(buf_ref[pl.ds(c, blk, stride=PACKED_SUBLANES), :]) for c in range(PACKED_SUBLANES)]
        x = jnp.concatenate([w[0] for w in words] + [w[1] for w in words], axis=1).astype(BF16)
        cols = [pl.ds(c, blk, stride=ROW_SUBLANES) for c in range(ROW_SUBLANES)]
        g = jnp.dot(x, wg_s[...], preferred_element_type=F32)
        u = jnp.dot(x, wu_s[...], preferred_element_type=F32)
        hb = (g * jax.nn.sigmoid(g)) * u
        out = jnp.dot(hb.astype(BF16), wd_s[...], preferred_element_type=F32)
        for c, rows in enumerate(cols):
            out_ref[rows, :] = out[:, LANES * c:LANES * (c + 1)]

    @pl.when(n_valid == 0)
    def _():
        out_ref[...] = jnp.zeros_like(out_ref)


def _expert_call(block_expert, block_new, block_valid, buf, w_gate, w_up, w_down):
    nb = buf.shape[0] // (MOE_BLK * PACKED_SUBLANES)
    wsel = lambda j, be, nw, va: (0, be[j], 0, 0)
    rows = lambda sublanes: pl.BlockSpec((MOE_BLK * sublanes, LANES), lambda j, be, nw, va: (j, 0))
    return pl.pallas_call(
        _expert_kernel,
        grid_spec=pltpu.PrefetchScalarGridSpec(
            num_scalar_prefetch=3,
            grid=(nb,),
            in_specs=[
                rows(PACKED_SUBLANES),
                pl.BlockSpec((1, 1, D_MODEL, EXPERT_FF), wsel),
                pl.BlockSpec((1, 1, D_MODEL, EXPERT_FF), wsel),
                pl.BlockSpec((1, 1, EXPERT_FF, D_MODEL), wsel),
            ],
            out_specs=rows(ROW_SUBLANES),
            scratch_shapes=[pltpu.VMEM((D_MODEL, EXPERT_FF), BF16),
                            pltpu.VMEM((D_MODEL, EXPERT_FF), BF16),
                            pltpu.VMEM((EXPERT_FF, D_MODEL), BF16)],
        ),
        out_shape=jax.ShapeDtypeStruct((nb * MOE_BLK * ROW_SUBLANES, LANES), F32),
        compiler_params=_cparams(("arbitrary",)),
        name="experts",
    )(block_expert, block_new, block_valid, buf, w_gate, w_up, w_down)


def _combine_kernel(d0_ref, d1_ref, d0_next_ref, d1_next_ref, d0_ahead_ref, d1_ahead_ref, x1_ref, gate_ref,
                    gf_ref, eo_ref, o_ref, rows_a, rows_b, rows_c, sems):
    i = pl.program_id(0)
    tt = x1_ref.shape[0]
    bufs = (rows_a, rows_b, rows_c)
    n_buf = len(bufs)

    def start_row(d_refs, slot, r):
        for k, d_ref in enumerate(d_refs):
            pltpu.make_async_copy(eo_ref.at[_row_tile(d_ref[r])], bufs[slot].at[k, _row_tile(r)],
                                  sems.at[slot]).start(priority=k)

    def drain(slot):
        for k in range(TOP_K):
            pltpu.make_async_copy(eo_ref.at[pl.ds(0, tt * ROW_SUBLANES)], bufs[slot].at[k], sems.at[slot]).wait()

    @pl.when(i == 0)
    def _():
        def issue(g, c):
            base = pl.multiple_of(g * ISSUE_GROUP, ISSUE_GROUP)
            for j in range(ISSUE_GROUP):
                start_row((d0_ref, d1_ref), 0, base + j)
                start_row((d0_next_ref, d1_next_ref), 1, base + j)
            return c
        lax.fori_loop(0, tt // ISSUE_GROUP, issue, 0)

    def step(slot):
        ahead = (slot + 2) % n_buf
        drain(slot)
        gf = gf_ref[...]
        for ch in range(tt // COMBINE_CHUNK):
            r0 = ch * COMBINE_CHUNK
            for j in range(COMBINE_CHUNK):
                start_row((d0_ahead_ref, d1_ahead_ref), ahead, r0 + j)
            gate = gate_ref[r0:r0 + COMBINE_CHUNK, :]

            def rows(k):
                return jnp.concatenate(
                    [bufs[slot][k, pl.ds(ROW_SUBLANES * r0 + c, COMBINE_CHUNK, stride=ROW_SUBLANES), :]
                     for c in range(ROW_SUBLANES)], axis=1)

            y = rows(0) * gate[:, 0:1] + rows(1) * gate[:, 1:2]
            o_ref[r0:r0 + COMBINE_CHUNK, :] = _rms(x1_ref[r0:r0 + COMBINE_CHUNK, :] + y, gf)

        @pl.when(i == pl.num_programs(0) - 1)
        def _():
            drain((slot + 1) % n_buf)
            drain(ahead)

    for phase in range(n_buf):
        pl.when(i % n_buf == phase)(functools.partial(step, phase))


def _combine_call(dests, x1, gates, g_final, expert_out):
    t = x1.shape[0]
    tt = ROW_TILE // 2
    last = t // tt - 1
    tile = lambda ahead: pl.BlockSpec((tt,), lambda i: (jnp.minimum(i + ahead, last),), memory_space=pltpu.SMEM)
    buf = pltpu.VMEM((TOP_K, tt * ROW_SUBLANES, LANES), F32)
    return pl.pallas_call(
        _combine_kernel,
        grid=(t // tt,),
        in_specs=[
            tile(0), tile(0), tile(1), tile(1), tile(2), tile(2),
            pl.BlockSpec((tt, D_MODEL), lambda i: (i, 0)),
            pl.BlockSpec((tt, TOP_K), lambda i: (i, 0)),
            pl.BlockSpec((1, D_MODEL), lambda i: (0, 0)),
            pl.BlockSpec(memory_space=pl.ANY),
        ],
        out_specs=pl.BlockSpec((tt, D_MODEL), lambda i: (i, 0)),
        out_shape=jax.ShapeDtypeStruct((t, D_MODEL), F32),
        scratch_shapes=[buf, buf, buf, pltpu.SemaphoreType.DMA((3,))],
        compiler_params=_cparams(("arbitrary",), disable_bounds_checks=True),
        name="combine",
    )(*dests, *dests, *dests, x1, gates, g_final, expert_out)


def _rope_tables(seq):
    half = B_QK_ROPE // 2
    inv_freq = ROPE_THETA ** (-(jnp.arange(half, dtype=F32) / half))
    ang = jnp.arange(seq, dtype=F32)[:, None] * inv_freq[None, :]
    cos, sin = jnp.cos(ang), jnp.sin(ang)
    z = jnp.zeros((seq, B_QK_NOPE), F32)
    z2 = jnp.zeros((seq, B_QK_ROPE), F32)
    return (jnp.concatenate([z, cos, cos, z2], axis=1), jnp.concatenate([z, -sin, sin, z2], axis=1))


def _swap_halves(w):
    half = w.shape[-1] // 2
    return jnp.concatenate([w[..., half:], w[..., :half]], axis=-1)


def _layout_weights(w_in, w_q_up, w_kv_up):
    d = w_in.shape[0]
    w_kr = w_in[:, 3 * A_WIDTH + B_Q_LORA + B_KV_LORA:]
    w_in_l = jnp.concatenate(
        [w_in[:, :A_WIDTH] * (A_HEAD_DIM ** -0.5), w_in[:, A_WIDTH:3 * A_WIDTH + B_Q_LORA + B_KV_LORA],
         jnp.zeros((d, B_QK_NOPE), F32), w_kr, _swap_halves(w_kr)], axis=1).astype(BF16)
    wq = w_q_up.reshape(B_Q_LORA, B_HEADS, B_QK_NOPE + B_QK_ROPE)
    wq_l = jnp.concatenate([wq, _swap_halves(wq[..., B_QK_NOPE:])], axis=-1)
    wq_l = wq_l.reshape(B_Q_LORA, B_HEADS * LANES).astype(BF16)
    wkv = w_kv_up.reshape(B_KV_LORA, B_HEADS, B_QK_NOPE + B_V_DIM)
    wkb = jnp.concatenate([wkv[..., :B_QK_NOPE], jnp.zeros_like(wkv[..., :B_QK_NOPE])], axis=-1)
    wkb = wkb.reshape(B_KV_LORA, B_HEADS * LANES).astype(BF16)
    wvb = wkv[..., B_QK_NOPE:].reshape(B_KV_LORA, B_WIDTH).astype(BF16)
    return w_in_l, wq_l, wkb, wvb


def _block_plan(counts, n_blocks):
    padded = (counts + MOE_BLK - 1) // MOE_BLK * MOE_BLK
    ends = jnp.cumsum(padded)
    starts = ends - padded
    first_row = jnp.arange(n_blocks, dtype=I32) * MOE_BLK
    expert = jnp.minimum(jnp.sum(ends[None, :] <= first_row[:, None], axis=1), N_EXPERTS - 1).astype(I32)
    new = jnp.concatenate([jnp.ones((1,), I32), (expert[1:] != expert[:-1]).astype(I32)])
    valid = jnp.clip((starts + counts)[expert] - first_row, 0, MOE_BLK).astype(I32)
    return starts.astype(I32), expert, new, valid


def kernel(x, g_attn_norm, w_in, rel_bias, g_q_latent, w_q_up, g_kv_latent, w_kv_up, g_out_a, g_out_b, w_out,
           g_ffn_norm, w_router_group, b_router_group, w_router_expert, b_router_expert, w_gate, w_up, w_down,
           g_final):
    batch, seq, d = x.shape
    t = batch * seq
    assert g_attn_norm.shape[0] == 1 and d == D_MODEL and seq % ROW_TILE == 0
    cos_t, sin_t = _rope_tables(seq)
    unit = ROW_TILE // MIX_UNITS
    tri = jnp.triu(jnp.ones((unit, unit), F32), 1).astype(BF16)
    spread = (jnp.arange(2 * LANES)[:, None] % LANES == jnp.arange(A_WIDTH)[None, :] // A_HEAD_DIM).astype(BF16)
    n_blocks = t * TOP_K // MOE_BLK + N_EXPERTS
    x2 = x.reshape(t, d)
    row = lambda v: v.reshape(1, -1)

    w_in_l, wq_l, wkb_l, wvb_l = _layout_weights(w_in[0], w_q_up[0], w_kv_up[0])
    *qkv_a, qb, kb, vb = _proj_call(x2, row(g_attn_norm[0]), w_in_l, row(g_q_latent[0]), wq_l,
                                    row(g_kv_latent[0]), wkb_l, wvb_l, cos_t, sin_t, seq)
    oas, lses = [], []
    for pi, (window, dilation) in enumerate(DILATED_PATTERNS):
        bias = _dilated_bias(rel_bias, seq, dilation, window // (2 * dilation))
        o_p, lse_p = _dilated_call(qkv_a[pi], bias, batch, seq, dilation)
        oas.append(o_p)
        lses.append(lse_p)
    ob = _mla_call(qb, kb, vb, batch, seq)

    pad = ROUTER_ROWS - N_EXPERTS - N_GROUPS
    w_router = jnp.concatenate([w_router_expert[0], w_router_group[0], jnp.zeros((d, pad), F32)], axis=1).T
    b_router = jnp.concatenate([b_router_expert[0], b_router_group[0], jnp.zeros((pad,), F32)])
    x1, hp, idx, gates, cnt = _mix_call(oas, lses, ob, x2, row(g_out_a[0]), row(g_out_b[0]), w_out[0].astype(BF16),
                                        row(g_ffn_norm[0]), w_router, b_router.reshape(-1, 1), tri, spread)
    pstart, block_expert, block_new, block_valid = _block_plan(cnt[:N_EXPERTS, 0], n_blocks)
    pstart_col = jnp.concatenate([pstart, jnp.zeros((ROUTER_ROWS - N_EXPERTS,), I32)]).reshape(-1, 1)
    dest = _dest_call(idx, pstart_col)
    dests = (dest[0], dest[1])
    buf = _dispatch_call(block_valid, dests, hp)
    expert_out = _expert_call(block_expert, block_new, block_valid, buf, w_gate, w_up, w_down)
    return _combine_call(dests, x1, gates[:TOP_K].T, row(g_final), expert_out).reshape(batch, seq, d)
```
